```python
import jax, jax.numpy as jnp
from jax import lax
import numpy as np

D_MODEL = 2048
BATCH = 8
SEQ = 4096
DEPTH = 4

N_A_LAYERS = DEPTH // 2
N_B_LAYERS = DEPTH - N_A_LAYERS
D_FF = 5632
GMLP_CHUNK = 128
GMLP_D_GATE = D_MODEL
GMLP_GROUP_WIDTH = 128
GMLP_GROUPS = GMLP_D_GATE // GMLP_GROUP_WIDTH
HEAD_DIM = 128
N_HEADS = D_MODEL // HEAD_DIM
DILATED_GROUPS = ((128, 1), (512, 4), (2048, 16))
N_GROUPS = len(DILATED_GROUPS)
ATTN_BLOCK = 128
REL_WINDOW = 128
EPS = 1e-6

kernel_name = "yoco_gmlp_dilated_alibi_macaron"


def rms_norm(x, g):
    xf = x.astype(jnp.float32)
    y = xf * lax.rsqrt(jnp.mean(xf * xf, axis=-1, keepdims=True) + EPS)
    return (y * g.astype(jnp.float32)).astype(x.dtype)


def swiglu(h, w_gate, w_up, w_down):
    return (jax.nn.silu(h @ w_gate) * (h @ w_up)) @ w_down


def gmlp_mixer(h, w_in, v_norm, w_s, b_s, w_out):
    bsz, seq, _ = h.shape
    z = jax.nn.gelu(h @ w_in)
    u, v = z[..., :GMLP_D_GATE], z[..., GMLP_D_GATE:]
    v = rms_norm(v, v_norm)
    v = v.reshape(bsz, seq // GMLP_CHUNK, GMLP_CHUNK, GMLP_GROUPS, GMLP_GROUP_WIDTH)
    causal = jnp.tril(jnp.ones((GMLP_CHUNK, GMLP_CHUNK), dtype=w_s.dtype))
    ws = w_s * causal[None]
    sv = jnp.einsum('gpq,bnqgc->bnpgc', ws, v) + b_s.T[None, None, :, :, None]
    return (u * sv.reshape(bsz, seq, GMLP_D_GATE)) @ w_out


def dilated_branch(q, k, v, dil, slopes):
    bsz, seq, nh, dh = q.shape
    L = seq // dil
    n = bsz * dil

    def to_sub(t):
        t = t.reshape(bsz, L, dil, nh, dh).transpose(0, 2, 1, 3, 4)
        return t.reshape(n, L, nh, dh)

    def from_sub(t):
        rest = t.shape[2:]
        t = t.reshape((bsz, dil, L) + rest)
        t = jnp.swapaxes(t, 1, 2)
        return t.reshape((bsz, seq) + rest)

    nb = -(-L // ATTN_BLOCK)
    Lp = nb * ATTN_BLOCK
    pad = Lp - L
    qs = jnp.pad(to_sub(q), ((0, 0), (0, pad), (0, 0), (0, 0))).reshape(n, nb, ATTN_BLOCK, nh, dh)

    def band(t):
        tp = jnp.pad(to_sub(t), ((0, 0), (ATTN_BLOCK, pad), (0, 0), (0, 0)))
        prev = tp[:, :Lp].reshape(n, nb, ATTN_BLOCK, nh, dh)
        cur = tp[:, ATTN_BLOCK:].reshape(n, nb, ATTN_BLOCK, nh, dh)
        return jnp.concatenate([prev, cur], axis=2)

    kb, vb = band(k), band(v)
    s = jnp.einsum('nbqhd,nbkhd->nbhqk', qs, kb, preferred_element_type=jnp.float32)
    qi = jnp.arange(ATTN_BLOCK)[:, None]
    kj = jnp.arange(2 * ATTN_BLOCK)[None, :]
    delta = qi + ATTN_BLOCK - kj
    j_abs = jnp.arange(nb)[:, None, None] * ATTN_BLOCK - ATTN_BLOCK + kj[None]
    valid = (delta >= 0)[None] & (delta <= REL_WINDOW)[None] & (j_abs >= 0)
    alibi = -slopes[:, None, None] * (delta * dil).astype(jnp.float32)[None]
    s = jnp.where(valid[None, :, None], s + alibi[None, None], -jnp.inf)
    m = jnp.max(s, axis=-1, keepdims=True)
    p = jnp.exp(s - m)
    l = jnp.sum(p, axis=-1, keepdims=True)
    o = jnp.einsum('nbhqk,nbkhd->nbqhd', p / l, vb.astype(jnp.float32))
    lse = (m + jnp.log(l))[..., 0]
    lse = lse.transpose(0, 1, 3, 2).reshape(n, Lp, nh)[:, :L]
    o = o.reshape(n, Lp, nh, dh)[:, :L]
    return from_sub(o), from_sub(lse)


def dilated_mixer(h, k_sh, v_sh, w_q, q_norm, w_o, slopes):
    bsz, seq, _ = h.shape
    q = (h @ w_q).reshape(bsz, seq, N_GROUPS, N_HEADS, HEAD_DIM)
    q = rms_norm(q, q_norm[:, None, :]) * (HEAD_DIM ** -0.5)
    outs, lses = [], []
    for g, (_, dil) in enumerate(DILATED_GROUPS):
        o, lse = dilated_branch(q[:, :, g], k_sh[:, :, g], v_sh[:, :, g], dil, slopes)
        outs.append(o)
        lses.append(lse)
    wts = jax.nn.softmax(jnp.stack(lses, 0), axis=0)
    o = jnp.sum(wts[..., None] * jnp.stack(outs, 0), axis=0)
    return o.astype(h.dtype).reshape(bsz, seq, N_HEADS * HEAD_DIM) @ w_o


def _fwd_setup_inputs(seed: int = 0) -> dict:
    key = jax.random.key(seed)
    ks = iter(jax.random.split(key, 32))

    def nrm(shape, scale):
        return jax.random.normal(next(ks), shape, dtype=jnp.float32) * scale

    def gain(shape):
        return 1.0 + nrm(shape, 0.02)

    D, F = D_MODEL, D_FF
    qkv_w = N_GROUPS * N_HEADS * HEAD_DIM
    return {
        "x": nrm((BATCH, SEQ, D), 1.0),
        "ffn1_norm": gain((DEPTH, D)),
        "ffn1_w_gate": nrm((DEPTH, D, F), D ** -0.5),
        "ffn1_w_up": nrm((DEPTH, D, F), D ** -0.5),
        "ffn1_w_down": nrm((DEPTH, F, D), F ** -0.5),
        "mix_norm": gain((DEPTH, D)),
        "ffn2_norm": gain((DEPTH, D)),
        "ffn2_w_gate": nrm((DEPTH, D, F), D ** -0.5),
        "ffn2_w_up": nrm((DEPTH, D, F), D ** -0.5),
        "ffn2_w_down": nrm((DEPTH, F, D), F ** -0.5),
        "gmlp_w_in": nrm((N_A_LAYERS, D, 2 * GMLP_D_GATE), D ** -0.5),
        "gmlp_v_norm": gain((N_A_LAYERS, GMLP_D_GATE)),
        "gmlp_w_s": nrm((N_A_LAYERS, GMLP_GROUPS, GMLP_CHUNK, GMLP_CHUNK), GMLP_CHUNK ** -0.5),
        "gmlp_b_s": 1.0 + nrm((N_A_LAYERS, GMLP_GROUPS, GMLP_CHUNK), 0.02),
        "gmlp_w_out": nrm((N_A_LAYERS, GMLP_D_GATE, D), GMLP_D_GATE ** -0.5),
        "kv_norm": gain((D,)),
        "w_kv": nrm((D, 2 * qkv_w), D ** -0.5),
        "k_norm": gain((N_GROUPS, HEAD_DIM)),
        "attn_w_q": nrm((N_B_LAYERS, D, qkv_w), D ** -0.5),
        "attn_q_norm": gain((N_B_LAYERS, N_GROUPS, HEAD_DIM)),
        "attn_w_o": nrm((N_B_LAYERS, N_HEADS * HEAD_DIM, D), (N_HEADS * HEAD_DIM) ** -0.5),
    }


def _fwd_reference(x, ffn1_norm, ffn1_w_gate, ffn1_w_up, ffn1_w_down, mix_norm,
              ffn2_norm, ffn2_w_gate, ffn2_w_up, ffn2_w_down,
              gmlp_w_in, gmlp_v_norm, gmlp_w_s, gmlp_b_s, gmlp_w_out,
              kv_norm, w_kv, k_norm, attn_w_q, attn_q_norm, attn_w_o):
    bsz, seq, _ = x.shape
    slopes = jnp.exp2(-8.0 * jnp.arange(1, N_HEADS + 1, dtype=jnp.float32) / N_HEADS)
    k_sh = v_sh = None
    for l in range(DEPTH):
        x = x + 0.5 * swiglu(rms_norm(x, ffn1_norm[l]), ffn1_w_gate[l], ffn1_w_up[l], ffn1_w_down[l])
        h = rms_norm(x, mix_norm[l])
        if l < N_A_LAYERS:
            x = x + gmlp_mixer(h, gmlp_w_in[l], gmlp_v_norm[l], gmlp_w_s[l], gmlp_b_s[l], gmlp_w_out[l])
        else:
            j = l - N_A_LAYERS
            x = x + dilated_mixer(h, k_sh, v_sh, attn_w_q[j], attn_q_norm[j], attn_w_o[j], slopes)
        x = x + 0.5 * swiglu(rms_norm(x, ffn2_norm[l]), ffn2_w_gate[l], ffn2_w_up[l], ffn2_w_down[l])
        if l == N_A_LAYERS - 1:
            kv = (rms_norm(x, kv_norm) @ w_kv).reshape(bsz, seq, 2, N_GROUPS, N_HEADS, HEAD_DIM)
            k_sh = rms_norm(kv[:, :, 0], k_norm[:, None, :])
            v_sh = kv[:, :, 1]
    return x


import jax as _jax
import jax.numpy as _jnp

TWIN_FORMAT = 'train_step'
FWD_PARAMS = ['x', 'ffn1_norm', 'ffn1_w_gate', 'ffn1_w_up', 'ffn1_w_down', 'mix_norm', 'ffn2_norm', 'ffn2_w_gate', 'ffn2_w_up', 'ffn2_w_down', 'gmlp_w_in', 'gmlp_v_norm', 'gmlp_w_s', 'gmlp_b_s', 'gmlp_w_out', 'kv_norm', 'w_kv', 'k_norm', 'attn_w_q', 'attn_q_norm', 'attn_w_o']
TWIN_WEIGHTS = ['ffn1_norm', 'ffn1_w_gate', 'ffn1_w_up', 'ffn1_w_down', 'mix_norm', 'ffn2_norm', 'ffn2_w_gate', 'ffn2_w_up', 'ffn2_w_down', 'gmlp_w_in', 'gmlp_v_norm', 'gmlp_w_s', 'gmlp_b_s', 'gmlp_w_out', 'kv_norm', 'w_kv', 'k_norm', 'attn_w_q', 'attn_q_norm', 'attn_w_o']
TWIN_DIFF_INPUT = 'x'
TWIN_INPUTS = ['x', 'ffn1_norm', 'ffn1_w_gate', 'ffn1_w_up', 'ffn1_w_down', 'mix_norm', 'ffn2_norm', 'ffn2_w_gate', 'ffn2_w_up', 'ffn2_w_down', 'gmlp_w_in', 'gmlp_v_norm', 'gmlp_w_s', 'gmlp_b_s', 'gmlp_w_out', 'kv_norm', 'w_kv', 'k_norm', 'attn_w_q', 'attn_q_norm', 'attn_w_o', 'loss_target', 'm_ffn1_norm', 'm_ffn1_w_gate', 'm_ffn1_w_up', 'm_ffn1_w_down', 'm_mix_norm', 'm_ffn2_norm', 'm_ffn2_w_gate', 'm_ffn2_w_up', 'm_ffn2_w_down', 'm_gmlp_w_in', 'm_gmlp_v_norm', 'm_gmlp_w_s', 'm_gmlp_b_s', 'm_gmlp_w_out', 'm_kv_norm', 'm_w_kv', 'm_k_norm', 'm_attn_w_q', 'm_attn_q_norm', 'm_attn_w_o', 'v_ffn1_norm', 'v_ffn1_w_gate', 'v_ffn1_w_up', 'v_ffn1_w_down', 'v_mix_norm', 'v_ffn2_norm', 'v_ffn2_w_gate', 'v_ffn2_w_up', 'v_ffn2_w_down', 'v_gmlp_w_in', 'v_gmlp_v_norm', 'v_gmlp_w_s', 'v_gmlp_b_s', 'v_gmlp_w_out', 'v_kv_norm', 'v_w_kv', 'v_k_norm', 'v_attn_w_q', 'v_attn_q_norm', 'v_attn_w_o']
TWIN_OUTPUTS = ['loss', 'grad_x', 'grad_ffn1_norm', 'grad_ffn1_w_gate', 'grad_ffn1_w_up', 'grad_ffn1_w_down', 'grad_mix_norm', 'grad_ffn2_norm', 'grad_ffn2_w_gate', 'grad_ffn2_w_up', 'grad_ffn2_w_down', 'grad_gmlp_w_in', 'grad_gmlp_v_norm', 'grad_gmlp_w_s', 'grad_gmlp_b_s', 'grad_gmlp_w_out', 'grad_kv_norm', 'grad_w_kv', 'grad_k_norm', 'grad_attn_w_q', 'grad_attn_q_norm', 'grad_attn_w_o', 'delta_ffn1_norm', 'delta_ffn1_w_gate', 'delta_ffn1_w_up', 'delta_ffn1_w_down', 'delta_mix_norm', 'delta_ffn2_norm', 'delta_ffn2_w_gate', 'delta_ffn2_w_up', 'delta_ffn2_w_down', 'delta_gmlp_w_in', 'delta_gmlp_v_norm', 'delta_gmlp_w_s', 'delta_gmlp_b_s', 'delta_gmlp_w_out', 'delta_kv_norm', 'delta_w_kv', 'delta_k_norm', 'delta_attn_w_q', 'delta_attn_q_norm', 'delta_attn_w_o', 'new_m_ffn1_norm', 'new_m_ffn1_w_gate', 'new_m_ffn1_w_up', 'new_m_ffn1_w_down', 'new_m_mix_norm', 'new_m_ffn2_norm', 'new_m_ffn2_w_gate', 'new_m_ffn2_w_up', 'new_m_ffn2_w_down', 'new_m_gmlp_w_in', 'new_m_gmlp_v_norm', 'new_m_gmlp_w_s', 'new_m_gmlp_b_s', 'new_m_gmlp_w_out', 'new_m_kv_norm', 'new_m_w_kv', 'new_m_k_norm', 'new_m_attn_w_q', 'new_m_attn_q_norm', 'new_m_attn_w_o', 'new_v_ffn1_norm', 'new_v_ffn1_w_gate', 'new_v_ffn1_w_up', 'new_v_ffn1_w_down', 'new_v_mix_norm', 'new_v_ffn2_norm', 'new_v_ffn2_w_gate', 'new_v_ffn2_w_up', 'new_v_ffn2_w_down', 'new_v_gmlp_w_in', 'new_v_gmlp_v_norm', 'new_v_gmlp_w_s', 'new_v_gmlp_b_s', 'new_v_gmlp_w_out', 'new_v_kv_norm', 'new_v_w_kv', 'new_v_k_norm', 'new_v_attn_w_q', 'new_v_attn_q_norm', 'new_v_attn_w_o']
TWIN_LEAF_KINDS = {'loss': 'loss', 'grad_x': 'grad_x', 'grad_ffn1_norm': 'grad_w', 'grad_ffn1_w_gate': 'grad_w', 'grad_ffn1_w_up': 'grad_w', 'grad_ffn1_w_down': 'grad_w', 'grad_mix_norm': 'grad_w', 'grad_ffn2_norm': 'grad_w', 'grad_ffn2_w_gate': 'grad_w', 'grad_ffn2_w_up': 'grad_w', 'grad_ffn2_w_down': 'grad_w', 'grad_gmlp_w_in': 'grad_w', 'grad_gmlp_v_norm': 'grad_w', 'grad_gmlp_w_s': 'grad_w', 'grad_gmlp_b_s': 'grad_w', 'grad_gmlp_w_out': 'grad_w', 'grad_kv_norm': 'grad_w', 'grad_w_kv': 'grad_w', 'grad_k_norm': 'grad_w', 'grad_attn_w_q': 'grad_w', 'grad_attn_q_norm': 'grad_w', 'grad_attn_w_o': 'grad_w', 'delta_ffn1_norm': 'delta_w', 'delta_ffn1_w_gate': 'delta_w', 'delta_ffn1_w_up': 'delta_w', 'delta_ffn1_w_down': 'delta_w', 'delta_mix_norm': 'delta_w', 'delta_ffn2_norm': 'delta_w', 'delta_ffn2_w_gate': 'delta_w', 'delta_ffn2_w_up': 'delta_w', 'delta_ffn2_w_down': 'delta_w', 'delta_gmlp_w_in': 'delta_w', 'delta_gmlp_v_norm': 'delta_w', 'delta_gmlp_w_s': 'delta_w', 'delta_gmlp_b_s': 'delta_w', 'delta_gmlp_w_out': 'delta_w', 'delta_kv_norm': 'delta_w', 'delta_w_kv': 'delta_w', 'delta_k_norm': 'delta_w', 'delta_attn_w_q': 'delta_w', 'delta_attn_q_norm': 'delta_w', 'delta_attn_w_o': 'delta_w', 'new_m_ffn1_norm': 'new_m', 'new_m_ffn1_w_gate': 'new_m', 'new_m_ffn1_w_up': 'new_m', 'new_m_ffn1_w_down': 'new_m', 'new_m_mix_norm': 'new_m', 'new_m_ffn2_norm': 'new_m', 'new_m_ffn2_w_gate': 'new_m', 'new_m_ffn2_w_up': 'new_m', 'new_m_ffn2_w_down': 'new_m', 'new_m_gmlp_w_in': 'new_m', 'new_m_gmlp_v_norm': 'new_m', 'new_m_gmlp_w_s': 'new_m', 'new_m_gmlp_b_s': 'new_m', 'new_m_gmlp_w_out': 'new_m', 'new_m_kv_norm': 'new_m', 'new_m_w_kv': 'new_m', 'new_m_k_norm': 'new_m', 'new_m_attn_w_q': 'new_m', 'new_m_attn_q_norm': 'new_m', 'new_m_attn_w_o': 'new_m', 'new_v_ffn1_norm': 'new_v', 'new_v_ffn1_w_gate': 'new_v', 'new_v_ffn1_w_up': 'new_v', 'new_v_ffn1_w_down': 'new_v', 'new_v_mix_norm': 'new_v', 'new_v_ffn2_norm': 'new_v', 'new_v_ffn2_w_gate': 'new_v', 'new_v_ffn2_w_up': 'new_v', 'new_v_ffn2_w_down': 'new_v', 'new_v_gmlp_w_in': 'new_v', 'new_v_gmlp_v_norm': 'new_v', 'new_v_gmlp_w_s': 'new_v', 'new_v_gmlp_b_s': 'new_v', 'new_v_gmlp_w_out': 'new_v', 'new_v_kv_norm': 'new_v', 'new_v_w_kv': 'new_v', 'new_v_k_norm': 'new_v', 'new_v_attn_w_q': 'new_v', 'new_v_attn_q_norm': 'new_v', 'new_v_attn_w_o': 'new_v'}


def _forward(args):
    return _fwd_reference(*[args[k] for k in FWD_PARAMS])


def _output_shape():
    def fwd():
        inp = _fwd_setup_inputs(0)
        return _fwd_reference(*[inp[k] for k in FWD_PARAMS])
    out = _jax.eval_shape(fwd)
    return out.shape, out.dtype

N_MICROBATCH = 1
ADAM_LR = 0.001
ADAM_B1 = 0.9
ADAM_B2 = 0.999
ADAM_EPS = 1e-08
ADAM_WD = 0.01
ADAM_STEP = 10
PER_EXAMPLE_BATCH_AXIS = {'x': 0, 'loss_target': 0}
SHARED_INPUTS = []
_WEIGHT_DTYPES = {'ffn1_norm': _jnp.float32, 'ffn1_w_gate': _jnp.float32, 'ffn1_w_up': _jnp.float32, 'ffn1_w_down': _jnp.float32, 'mix_norm': _jnp.float32, 'ffn2_norm': _jnp.float32, 'ffn2_w_gate': _jnp.float32, 'ffn2_w_up': _jnp.float32, 'ffn2_w_down': _jnp.float32, 'gmlp_w_in': _jnp.float32, 'gmlp_v_norm': _jnp.float32, 'gmlp_w_s': _jnp.float32, 'gmlp_b_s': _jnp.float32, 'gmlp_w_out': _jnp.float32, 'kv_norm': _jnp.float32, 'w_kv': _jnp.float32, 'k_norm': _jnp.float32, 'attn_w_q': _jnp.float32, 'attn_q_norm': _jnp.float32, 'attn_w_o': _jnp.float32}
MOMENT_SCALE = {'ffn1_norm': 3.073086e+00, 'ffn1_w_gate': 1.178557e-01, 'ffn1_w_up': 1.462760e-01, 'ffn1_w_down': 2.370721e-01, 'mix_norm': 8.675514e+00, 'ffn2_norm': 3.075444e+00, 'ffn2_w_gate': 1.388225e-01, 'ffn2_w_up': 1.867891e-01, 'ffn2_w_down': 3.018918e-01, 'gmlp_w_in': 1.080591e+00, 'gmlp_v_norm': 3.525204e+00, 'gmlp_w_s': 2.313554e+00, 'gmlp_b_s': 7.395994e+00, 'gmlp_w_out': 3.770046e+00, 'kv_norm': 4.726298e+00, 'w_kv': 9.371758e-01, 'k_norm': 7.187515e+00, 'attn_w_q': 6.876020e-02, 'attn_q_norm': 3.599155e+00, 'attn_w_o': 1.423081e+00}


def _to_microbatches(a, axis):
    t = _jnp.moveaxis(a, axis, 0)
    t = t.reshape((N_MICROBATCH, t.shape[0] // N_MICROBATCH) + t.shape[1:])
    return _jnp.moveaxis(t, 1, axis + 1)


def setup_inputs(seed: int = 0) -> dict:
    inp = _fwd_setup_inputs(seed)
    key = _jax.random.fold_in(_jax.random.key(seed), 7919)
    shape, _ = _output_shape()
    out = dict(inp)
    out["loss_target"] = _jax.random.normal(_jax.random.fold_in(key, 0), shape, _jnp.float32)
    for i, name in enumerate(TWIN_WEIGHTS):
        w = inp[name].astype(_jnp.float32)
        if MOMENT_SCALE is None:
            s = _jnp.sqrt(_jnp.mean(_jnp.square(w)) + 1e-30)
        else:
            s = MOMENT_SCALE[name]
        km, kv = _jax.random.split(_jax.random.fold_in(key, i + 1))
        out[name] = w
        out["m_" + name] = s * _jax.random.normal(km, w.shape, _jnp.float32)
        out["v_" + name] = (s * s) * _jax.random.uniform(kv, w.shape, _jnp.float32, 0.5, 1.5)
    if N_MICROBATCH > 1:
        for name, axis in PER_EXAMPLE_BATCH_AXIS.items():
            out[name] = _to_microbatches(out[name], axis)
    return {'x': out['x'], 'ffn1_norm': out['ffn1_norm'], 'ffn1_w_gate': out['ffn1_w_gate'], 'ffn1_w_up': out['ffn1_w_up'], 'ffn1_w_down': out['ffn1_w_down'], 'mix_norm': out['mix_norm'], 'ffn2_norm': out['ffn2_norm'], 'ffn2_w_gate': out['ffn2_w_gate'], 'ffn2_w_up': out['ffn2_w_up'], 'ffn2_w_down': out['ffn2_w_down'], 'gmlp_w_in': out['gmlp_w_in'], 'gmlp_v_norm': out['gmlp_v_norm'], 'gmlp_w_s': out['gmlp_w_s'], 'gmlp_b_s': out['gmlp_b_s'], 'gmlp_w_out': out['gmlp_w_out'], 'kv_norm': out['kv_norm'], 'w_kv': out['w_kv'], 'k_norm': out['k_norm'], 'attn_w_q': out['attn_w_q'], 'attn_q_norm': out['attn_q_norm'], 'attn_w_o': out['attn_w_o'], 'loss_target': out['loss_target'], 'm_ffn1_norm': out['m_ffn1_norm'], 'm_ffn1_w_gate': out['m_ffn1_w_gate'], 'm_ffn1_w_up': out['m_ffn1_w_up'], 'm_ffn1_w_down': out['m_ffn1_w_down'], 'm_mix_norm': out['m_mix_norm'], 'm_ffn2_norm': out['m_ffn2_norm'], 'm_ffn2_w_gate': out['m_ffn2_w_gate'], 'm_ffn2_w_up': out['m_ffn2_w_up'], 'm_ffn2_w_down': out['m_ffn2_w_down'], 'm_gmlp_w_in': out['m_gmlp_w_in'], 'm_gmlp_v_norm': out['m_gmlp_v_norm'], 'm_gmlp_w_s': out['m_gmlp_w_s'], 'm_gmlp_b_s': out['m_gmlp_b_s'], 'm_gmlp_w_out': out['m_gmlp_w_out'], 'm_kv_norm': out['m_kv_norm'], 'm_w_kv': out['m_w_kv'], 'm_k_norm': out['m_k_norm'], 'm_attn_w_q': out['m_attn_w_q'], 'm_attn_q_norm': out['m_attn_q_norm'], 'm_attn_w_o': out['m_attn_w_o'], 'v_ffn1_norm': out['v_ffn1_norm'], 'v_ffn1_w_gate': out['v_ffn1_w_gate'], 'v_ffn1_w_up': out['v_ffn1_w_up'], 'v_ffn1_w_down': out['v_ffn1_w_down'], 'v_mix_norm': out['v_mix_norm'], 'v_ffn2_norm': out['v_ffn2_norm'], 'v_ffn2_w_gate': out['v_ffn2_w_gate'], 'v_ffn2_w_up': out['v_ffn2_w_up'], 'v_ffn2_w_down': out['v_ffn2_w_down'], 'v_gmlp_w_in': out['v_gmlp_w_in'], 'v_gmlp_v_norm': out['v_gmlp_v_norm'], 'v_gmlp_w_s': out['v_gmlp_w_s'], 'v_gmlp_b_s': out['v_gmlp_b_s'], 'v_gmlp_w_out': out['v_gmlp_w_out'], 'v_kv_norm': out['v_kv_norm'], 'v_w_kv': out['v_w_kv'], 'v_k_norm': out['v_k_norm'], 'v_attn_w_q': out['v_attn_w_q'], 'v_attn_q_norm': out['v_attn_q_norm'], 'v_attn_w_o': out['v_attn_w_o']}


def _loss(weights, diff, rest, loss_target):
    with _jax.named_scope("forward"):
        args = {**rest, TWIN_DIFF_INPUT: diff, **{k: w.astype(_WEIGHT_DTYPES[k]) for k, w in weights.items()}}
        y = _forward(args)
    with _jax.named_scope("loss_head"):
        err = _jnp.square(y.astype(_jnp.float32) - loss_target)
        return 0.5 * _jnp.sum(_jnp.mean(err, axis=-1)) if err.ndim else 0.5 * err


def _adamw(w, g, m, v):
    m = ADAM_B1 * m + (1.0 - ADAM_B1) * g
    v = ADAM_B2 * v + (1.0 - ADAM_B2) * _jnp.square(g)
    m_hat = m / (1.0 - ADAM_B1 ** ADAM_STEP)
    v_hat = v / (1.0 - ADAM_B2 ** ADAM_STEP)
    delta = -ADAM_LR * (m_hat / (_jnp.sqrt(v_hat) + ADAM_EPS) + ADAM_WD * w)
    return delta, m, v


def reference(x, ffn1_norm, ffn1_w_gate, ffn1_w_up, ffn1_w_down, mix_norm, ffn2_norm, ffn2_w_gate, ffn2_w_up, ffn2_w_down, gmlp_w_in, gmlp_v_norm, gmlp_w_s, gmlp_b_s, gmlp_w_out, kv_norm, w_kv, k_norm, attn_w_q, attn_q_norm, attn_w_o, loss_target, m_ffn1_norm, m_ffn1_w_gate, m_ffn1_w_up, m_ffn1_w_down, m_mix_norm, m_ffn2_norm, m_ffn2_w_gate, m_ffn2_w_up, m_ffn2_w_down, m_gmlp_w_in, m_gmlp_v_norm, m_gmlp_w_s, m_gmlp_b_s, m_gmlp_w_out, m_kv_norm, m_w_kv, m_k_norm, m_attn_w_q, m_attn_q_norm, m_attn_w_o, v_ffn1_norm, v_ffn1_w_gate, v_ffn1_w_up, v_ffn1_w_down, v_mix_norm, v_ffn2_norm, v_ffn2_w_gate, v_ffn2_w_up, v_ffn2_w_down, v_gmlp_w_in, v_gmlp_v_norm, v_gmlp_w_s, v_gmlp_b_s, v_gmlp_w_out, v_kv_norm, v_w_kv, v_k_norm, v_attn_w_q, v_attn_q_norm, v_attn_w_o):
    given = dict(x=x, ffn1_norm=ffn1_norm, ffn1_w_gate=ffn1_w_gate, ffn1_w_up=ffn1_w_up, ffn1_w_down=ffn1_w_down, mix_norm=mix_norm, ffn2_norm=ffn2_norm, ffn2_w_gate=ffn2_w_gate, ffn2_w_up=ffn2_w_up, ffn2_w_down=ffn2_w_down, gmlp_w_in=gmlp_w_in, gmlp_v_norm=gmlp_v_norm, gmlp_w_s=gmlp_w_s, gmlp_b_s=gmlp_b_s, gmlp_w_out=gmlp_w_out, kv_norm=kv_norm, w_kv=w_kv, k_norm=k_norm, attn_w_q=attn_w_q, attn_q_norm=attn_q_norm, attn_w_o=attn_w_o, loss_target=loss_target, m_ffn1_norm=m_ffn1_norm, m_ffn1_w_gate=m_ffn1_w_gate, m_ffn1_w_up=m_ffn1_w_up, m_ffn1_w_down=m_ffn1_w_down, m_mix_norm=m_mix_norm, m_ffn2_norm=m_ffn2_norm, m_ffn2_w_gate=m_ffn2_w_gate, m_ffn2_w_up=m_ffn2_w_up, m_ffn2_w_down=m_ffn2_w_down, m_gmlp_w_in=m_gmlp_w_in, m_gmlp_v_norm=m_gmlp_v_norm, m_gmlp_w_s=m_gmlp_w_s, m_gmlp_b_s=m_gmlp_b_s, m_gmlp_w_out=m_gmlp_w_out, m_kv_norm=m_kv_norm, m_w_kv=m_w_kv, m_k_norm=m_k_norm, m_attn_w_q=m_attn_w_q, m_attn_q_norm=m_attn_q_norm, m_attn_w_o=m_attn_w_o, v_ffn1_norm=v_ffn1_norm, v_ffn1_w_gate=v_ffn1_w_gate, v_ffn1_w_up=v_ffn1_w_up, v_ffn1_w_down=v_ffn1_w_down, v_mix_norm=v_mix_norm, v_ffn2_norm=v_ffn2_norm, v_ffn2_w_gate=v_ffn2_w_gate, v_ffn2_w_up=v_ffn2_w_up, v_ffn2_w_down=v_ffn2_w_down, v_gmlp_w_in=v_gmlp_w_in, v_gmlp_v_norm=v_gmlp_v_norm, v_gmlp_w_s=v_gmlp_w_s, v_gmlp_b_s=v_gmlp_b_s, v_gmlp_w_out=v_gmlp_w_out, v_kv_norm=v_kv_norm, v_w_kv=v_w_kv, v_k_norm=v_k_norm, v_attn_w_q=v_attn_w_q, v_attn_q_norm=v_attn_q_norm, v_attn_w_o=v_attn_w_o)
    weights = {n: given[n] for n in TWIN_WEIGHTS}
    shared = {n: given[n] for n in SHARED_INPUTS}
    per_example = {n: given[n] for n in ['x']}
    grad_fn = _jax.value_and_grad(_loss, argnums=(0, 1))

    def one_microbatch(ex, loss_target):
        ex = dict(ex)
        diff = ex.pop(TWIN_DIFF_INPUT)
        return grad_fn(weights, diff, {**shared, **ex}, loss_target)

    if N_MICROBATCH == 1:
        loss, (grad_w, grad_x) = one_microbatch(per_example, given["loss_target"])
    else:
        def body(carry, xs):
            loss_sum, grad_sum = carry
            l_k, (gw_k, gx_k) = one_microbatch(xs[0], xs[1])
            with _jax.named_scope("update"):
                return (loss_sum + l_k, _jax.tree.map(_jnp.add, grad_sum, gw_k)), gx_k

        init = (_jnp.zeros((), _jnp.float32), _jax.tree.map(_jnp.zeros_like, weights))
        (loss, grad_w), grad_x = _jax.lax.scan(body, init, (per_example, given["loss_target"]))
    with _jax.named_scope("update"):
        delta_w, new_m, new_v = {}, {}, {}
        for n in TWIN_WEIGHTS:
            delta_w[n], new_m[n], new_v[n] = _adamw(weights[n], grad_w[n], given["m_" + n], given["v_" + n])
    return (loss, grad_x, *[grad_w[n] for n in TWIN_WEIGHTS], *[delta_w[n] for n in TWIN_WEIGHTS],
            *[new_m[n] for n in TWIN_WEIGHTS], *[new_v[n] for n in TWIN_WEIGHTS])
```

```python
import numpy as np
import jax
import jax.numpy as jnp
from jax import lax
from jax.experimental import pallas as pl
from jax.experimental.pallas import tpu as pltpu

F32 = jnp.float32
BF16 = jnp.bfloat16
SDS = jax.ShapeDtypeStruct

EPS = 1e-6
HEAD_DIM = 128
GMLP_CHUNK = 128
GMLP_GROUP_WIDTH = 128
DILATIONS = (1, 4, 16)
ATTN_BLOCK = 128
N_A_LAYERS = 2
ADAM_LR, ADAM_B1, ADAM_B2, ADAM_EPS, ADAM_WD, ADAM_STEP = 0.001, 0.9, 0.999, 1e-08, 0.01, 10

N_CHIPS = 4
N_DEV = 8
MESH = pl.DeviceIdType.MESH
V7X_VMEM_BYTES = 64 * 2 ** 20
VMEM_CEILING = V7X_VMEM_BYTES - 6 * 2 ** 20
VMEM_BLOCK_BUDGET = 38 * 2 ** 20
LANE = 128
BF16_ROWS = 16
PACK_TILE = 8 * LANE


def _cp(sem=None, vmem=None):
    kw = {}
    if sem is not None:
        kw["dimension_semantics"] = sem
    if vmem is not None:
        kw["vmem_limit_bytes"] = int(min(max(vmem, 16 * 2 ** 20), VMEM_CEILING))
    return pltpu.CompilerParams(**kw)


def _pick(dim, cands):
    for c in cands:
        if c <= dim and dim % c == 0:
            return c
    return dim


def _row_tile(rows, row_bytes, target=2 ** 20):
    t = 1024
    while t > 8 and (t * row_bytes > target or rows % t):
        t //= 2
    return t if rows % t == 0 else rows


def _sigmoid(x):
    return 1.0 / (1.0 + jnp.exp(-x))


_GELU_C = 0.7978845608028654
_GELU_A = 0.044715


def _gelu(x):
    return 0.5 * x * (1.0 + jnp.tanh(_GELU_C * (x + _GELU_A * (x * x * x))))


def _gelu_grad(x):
    t = jnp.tanh(_GELU_C * (x + _GELU_A * (x * x * x)))
    return 0.5 * (1.0 + t) + 0.5 * x * (1.0 - t * t) * (_GELU_C * (1.0 + 3.0 * _GELU_A * x * x))


def _mm_tiles(M, N, K, n_a, n_b, n_acc, io_bytes):
    tks = [K] + [d for d in (4096, 3072, 2816, 2048, 1024, 512, 256, 128) if d < K and K % d == 0]
    tms = [t for t in (1024, 512, 256, 128) if M % t == 0] or [M]
    tns = [t for t in (512, 256, 128) if N % t == 0] or [N]
    best = None
    for tk in tks:
        for tm in tms:
            for tn in tns:
                est = 2 * 2 * (n_a * tm * tk + n_b * tk * tn) + 2 * tm * tn * io_bytes
                est += n_acc * tm * tn * 4 * (2 if tk < K else 1)
                if est <= VMEM_BLOCK_BUDGET:
                    return tm, tn, tk, est
                if best is None or est < best[3]:
                    best = (tm, tn, tk, est)
    return best


def _mm(name, a_list, b_list, terms, epilogue, out_dtypes, *, ta=False, tb=False, extras=()):
    n_acc = 1 + max(t[2] for t in terms)
    a0, b0 = a_list[0], b_list[0]
    (K, M) = a0.shape if ta else a0.shape[::-1]
    N = b0.shape[0] if tb else b0.shape[1]
    io_bytes = sum(jnp.dtype(e.dtype).itemsize for e in extras) + sum(jnp.dtype(d).itemsize for d in out_dtypes)
    tm, tn, tk, est = _mm_tiles(M, N, K, len(a_list), len(b_list), n_acc, io_bytes)
    nk = K // tk
    na, nb, ne, no = len(a_list), len(b_list), len(extras), len(out_dtypes)
    dn = (((0 if ta else 1,), (1 if tb else 0,)), ((), ()))

    def body(*refs):
        a_refs = refs[:na]
        b_refs = refs[na:na + nb]
        e_refs = refs[na + nb:na + nb + ne]
        o_refs = refs[na + nb + ne:na + nb + ne + no]
        acc_refs = refs[na + nb + ne + no:]
        parts = [None] * n_acc
        for ai, bi, qi in terms:
            d = lax.dot_general(a_refs[ai][...], b_refs[bi][...], dn, preferred_element_type=F32)
            parts[qi] = d if parts[qi] is None else parts[qi] + d

        def finish(accs):
            outs = epilogue(accs, [e[...] for e in e_refs])
            for o_ref, o in zip(o_refs, outs):
                o_ref[...] = o.astype(o_ref.dtype)

        if nk == 1:
            finish(parts)
        else:
            k = pl.program_id(2)

            @pl.when(k == 0)
            def _():
                for q in range(n_acc):
                    acc_refs[q][...] = parts[q]

            @pl.when(k > 0)
            def _():
                for q in range(n_acc):
                    acc_refs[q][...] += parts[q]

            @pl.when(k == nk - 1)
            def _():
                finish([acc_refs[q][...] for q in range(n_acc)])

    a_spec = pl.BlockSpec((tk, tm), lambda i, j, k: (k, i)) if ta else pl.BlockSpec((tm, tk), lambda i, j, k: (i, k))
    b_spec = pl.BlockSpec((tn, tk), lambda i, j, k: (j, k)) if tb else pl.BlockSpec((tk, tn), lambda i, j, k: (k, j))
    e_spec = pl.BlockSpec((tm, tn), lambda i, j, k: (i, j))
    outs = pl.pallas_call(
        body, name=name, grid=(M // tm, N // tn, nk),
        in_specs=[a_spec] * na + [b_spec] * nb + [e_spec] * ne,
        out_specs=[e_spec] * no,
        out_shape=[SDS((M, N), d) for d in out_dtypes],
        scratch_shapes=[pltpu.VMEM((tm, tn), F32) for _ in range(n_acc)] if nk > 1 else [],
        compiler_params=_cp(("parallel", "parallel", "arbitrary"), est + 12 * 2 ** 20),
    )(*a_list, *b_list, *extras)
    return outs


def _ep_plain(accs, ex):
    return [accs[0]]


def _ep_swiglu(accs, ex):
    g, u = accs
    return [g, u, g * _sigmoid(g) * u]


def _ep_swiglu_bwd(accs, ex):
    da = accs[0]
    g = ex[0].astype(F32)
    u = ex[1].astype(F32)
    s = _sigmoid(g)
    return [da * u * (s * (1.0 + g * (1.0 - s))), da * (g * s)]


def _ep_gelu(accs, ex):
    return [accs[0], _gelu(accs[0])]


def _ep_residual(scale):
    def ep(accs, ex):
        return [ex[0] + scale * accs[0]]
    return ep


def _rms_fwd(name, x, gamma):
    S, D = x.shape
    tr = _row_tile(S, D * 4)

    def body(x_ref, g_ref, o_ref):
        xv = x_ref[...]
        r = lax.rsqrt(jnp.mean(xv * xv, axis=-1, keepdims=True) + EPS)
        o_ref[...] = (xv * r * g_ref[...]).astype(BF16)

    return pl.pallas_call(
        body, name=name, grid=(S // tr,),
        in_specs=[pl.BlockSpec((tr, D), lambda i: (i, 0)), pl.BlockSpec((1, D), lambda i: (0, 0))],
        out_specs=pl.BlockSpec((tr, D), lambda i: (i, 0)),
        out_shape=SDS((S, D), BF16),
        compiler_params=_cp(("parallel",), 32 * 2 ** 20),
    )(x, gamma.reshape(1, D))


def _rms_bwd(name, x, gamma, dn, dx_in, out_scale):
    S, D = x.shape
    tr = _row_tile(S, D * 4, 2 ** 19)

    def body(x_ref, g_ref, dn_ref, dxi_ref, dxo_ref, dxb_ref, dg_ref):
        i = pl.program_id(0)
        xv = x_ref[...]
        r = lax.rsqrt(jnp.mean(xv * xv, axis=-1, keepdims=True) + EPS)
        xh = xv * r
        dnv = dn_ref[...]
        dxh = dnv * g_ref[...]
        dx = dxi_ref[...] + r * (dxh - xh * jnp.mean(dxh * xh, axis=-1, keepdims=True))
        dxo_ref[...] = dx
        dxb_ref[...] = (out_scale * dx).astype(BF16)
        part = jnp.sum(dnv * xh, axis=0, keepdims=True)

        @pl.when(i == 0)
        def _():
            dg_ref[...] = part

        @pl.when(i > 0)
        def _():
            dg_ref[...] += part

    row = pl.BlockSpec((tr, D), lambda i: (i, 0))
    vec = pl.BlockSpec((1, D), lambda i: (0, 0))
    return pl.pallas_call(
        body, name=name, grid=(S // tr,),
        in_specs=[row, vec, row, row], out_specs=[row, row, vec],
        out_shape=[SDS((S, D), F32), SDS((S, D), BF16), SDS((1, D), F32)],
        compiler_params=_cp(("arbitrary",), 40 * 2 ** 20),
    )(x, gamma.reshape(1, D), dn, dx_in)


def _loss_grad(name, y, t, out_scale):
    S, D = y.shape
    tr = _row_tile(S, D * 4, 2 ** 19)
    inv_d = 1.0 / D

    def body(y_ref, t_ref, dy_ref, dyb_ref, ls_ref):
        i = pl.program_id(0)
        e = y_ref[...] - t_ref[...]
        dy = e * inv_d
        dy_ref[...] = dy
        dyb_ref[...] = (out_scale * dy).astype(BF16)
        part = jnp.sum(e * e, axis=0, keepdims=True)

        @pl.when(i == 0)
        def _():
            ls_ref[...] = part

        @pl.when(i > 0)
        def _():
            ls_ref[...] += part

    row = pl.BlockSpec((tr, D), lambda i: (i, 0))
    vec = pl.BlockSpec((1, D), lambda i: (0, 0))
    return pl.pallas_call(
        body, name=name, grid=(S // tr,),
        in_specs=[row, row], out_specs=[row, row, vec],
        out_shape=[SDS((S, D), F32), SDS((S, D), BF16), SDS((1, D), F32)],
        compiler_params=_cp(("arbitrary",), 32 * 2 ** 20),
    )(y, t)


def _head_norm_fwd(name, raw, part, n_parts, gain_t, scale, with_pass):
    S = raw.shape[0]
    W = raw.shape[1] // n_parts
    nh = W // HEAD_DIM
    tr = _row_tile(S, W * 4, 2 ** 20)

    def body(*refs):
        if with_pass:
            x_ref, p_ref, g_ref, o_ref, po_ref = refs
            po_ref[...] = p_ref[...].astype(BF16)
        else:
            x_ref, g_ref, o_ref = refs
        for h in range(nh):
            sl = slice(h * HEAD_DIM, (h + 1) * HEAD_DIM)
            xv = x_ref[:, sl]
            r = lax.rsqrt(jnp.mean(xv * xv, axis=-1, keepdims=True) + EPS)
            o_ref[:, sl] = (xv * r * g_ref[:, sl] * scale).astype(BF16)

    xspec = pl.BlockSpec((tr, W), lambda i: (i, part))
    ospec = pl.BlockSpec((tr, W), lambda i: (i, 0))
    gspec = pl.BlockSpec((1, W), lambda i: (0, 0))
    if with_pass:
        in_specs = [xspec, pl.BlockSpec((tr, W), lambda i: (i, 1)), gspec]
        args = (raw, raw, gain_t)
        out_specs, out_shape = [ospec, ospec], [SDS((S, W), BF16), SDS((S, W), BF16)]
    else:
        in_specs, args = [xspec, gspec], (raw, gain_t)
        out_specs, out_shape = ospec, SDS((S, W), BF16)
    return pl.pallas_call(
        body, name=name, grid=(S // tr,), in_specs=in_specs, out_specs=out_specs, out_shape=out_shape,
        compiler_params=_cp(("parallel",), 40 * 2 ** 20),
    )(*args)


def _head_norm_bwd(name, raw, part, n_parts, gain_t, scale, dy_groups, pass_groups):
    S = raw.shape[0]
    W = raw.shape[1] // n_parts
    ng = len(dy_groups)
    HD = W // ng
    nhg = HD // HEAD_DIM
    tr = _row_tile(S, W * 4, 2 ** 19)
    n_dy = [len(g) for g in dy_groups]
    n_ps = [len(g) for g in pass_groups] if pass_groups is not None else []
    flat = [a for g in dy_groups for a in g] + ([a for g in pass_groups for a in g] if pass_groups is not None else [])
    out_w = 2 * W if pass_groups is not None else W

    def body(*refs):
        x_ref, g_ref = refs[0], refs[1]
        d_refs = refs[2:2 + len(flat)]
        o_ref, dg_ref = refs[2 + len(flat)], refs[3 + len(flat)]
        i = pl.program_id(0)

        @pl.when(i == 0)
        def _():
            dg_ref[...] = jnp.zeros_like(dg_ref)

        pos = 0
        for gi in range(ng):
            dys = d_refs[pos:pos + n_dy[gi]]
            pos += n_dy[gi]
            for h in range(nhg):
                sl = slice(gi * HD + h * HEAD_DIM, gi * HD + (h + 1) * HEAD_DIM)
                hs = slice(h * HEAD_DIM, (h + 1) * HEAD_DIM)
                dy = dys[0][:, hs]
                for extra in dys[1:]:
                    dy = dy + extra[:, hs]
                xv = x_ref[:, sl]
                r = lax.rsqrt(jnp.mean(xv * xv, axis=-1, keepdims=True) + EPS)
                xh = xv * r
                dxh = dy * (g_ref[:, sl] * scale)
                o_ref[:, sl] = (r * (dxh - xh * jnp.mean(dxh * xh, axis=-1, keepdims=True))).astype(BF16)
                dg_ref[:, sl] += jnp.sum(dy * xh, axis=0, keepdims=True) * scale
        for gi in range(len(n_ps)):
            ps = d_refs[pos:pos + n_ps[gi]]
            pos += n_ps[gi]
            acc = ps[0][...]
            for extra in ps[1:]:
                acc = acc + extra[...]
            o_ref[:, W + gi * HD:W + (gi + 1) * HD] = acc.astype(BF16)

    dspec = pl.BlockSpec((tr, HD), lambda i: (i, 0))
    return pl.pallas_call(
        body, name=name, grid=(S // tr,),
        in_specs=[pl.BlockSpec((tr, W), lambda i: (i, part)), pl.BlockSpec((1, W), lambda i: (0, 0))] + [dspec] * len(flat),
        out_specs=[pl.BlockSpec((tr, out_w), lambda i: (i, 0)), pl.BlockSpec((1, W), lambda i: (0, 0))],
        out_shape=[SDS((S, out_w), BF16), SDS((1, W), F32)],
        compiler_params=_cp(("arbitrary",), 48 * 2 ** 20),
    )(raw, gain_t, *flat)


def _tril_mask():
    r = lax.broadcasted_iota(jnp.int32, (GMLP_CHUNK, GMLP_CHUNK), 0)
    c = lax.broadcasted_iota(jnp.int32, (GMLP_CHUNK, GMLP_CHUNK), 1)
    return r >= c


def _gmlp_gate_fwd(name, z, v_norm, w_s, bias_full):
    S, DG2 = z.shape
    DG = DG2 // 2
    G = DG // GMLP_GROUP_WIDTH
    C = GMLP_CHUNK

    def body(u_ref, v_ref, vn_ref, ws_ref, b_ref, o_ref):
        mask = _tril_mask()
        v = v_ref[...]
        r = lax.rsqrt(jnp.mean(v * v, axis=-1, keepdims=True) + EPS)
        vn = (v * r * vn_ref[...]).astype(BF16)
        for g in range(G):
            sl = slice(g * GMLP_GROUP_WIDTH, (g + 1) * GMLP_GROUP_WIDTH)
            wm = jnp.where(mask, ws_ref[g], 0.0).astype(BF16)
            sv = jnp.dot(wm, vn[:, sl], preferred_element_type=F32) + b_ref[:, sl]
            o_ref[:, sl] = (u_ref[:, sl] * sv).astype(BF16)

    return pl.pallas_call(
        body, name=name, grid=(S // C,),
        in_specs=[pl.BlockSpec((C, DG), lambda i: (i, 0)), pl.BlockSpec((C, DG), lambda i: (i, 1)),
                  pl.BlockSpec((1, DG), lambda i: (0, 0)), pl.BlockSpec((G, C, C), lambda i: (0, 0, 0)),
                  pl.BlockSpec((C, DG), lambda i: (0, 0))],
        out_specs=pl.BlockSpec((C, DG), lambda i: (i, 0)),
        out_shape=SDS((S, DG), BF16),
        compiler_params=_cp(("parallel",), 32 * 2 ** 20),
    )(z, z, v_norm, w_s, bias_full)


def _gmlp_gate_bwd(name, z, zpre, dgated, v_norm, w_s, w_s_t, bias_full):
    S, DG2 = z.shape
    DG = DG2 // 2
    G = DG // GMLP_GROUP_WIDTH
    C = GMLP_CHUNK

    def body(z_ref, zp_ref, dg_ref, vn_ref, ws_ref, wst_ref, b_ref, dz_ref, dws_ref, db_ref, dvn_ref):
        i = pl.program_id(0)
        mask = _tril_mask()
        mask_t = jnp.logical_not(mask) | (lax.broadcasted_iota(jnp.int32, (C, C), 0) == lax.broadcasted_iota(jnp.int32, (C, C), 1))
        u = z_ref[:, :DG]
        v = z_ref[:, DG:]
        r = lax.rsqrt(jnp.mean(v * v, axis=-1, keepdims=True) + EPS)
        vh = v * r
        gain = vn_ref[...]
        vn = (vh * gain).astype(BF16)
        dgt = dg_ref[...]

        @pl.when(i == 0)
        def _():
            dws_ref[...] = jnp.zeros_like(dws_ref)
            db_ref[...] = jnp.zeros_like(db_ref)
            dvn_ref[...] = jnp.zeros_like(dvn_ref)

        dvn_parts = []
        for g in range(G):
            sl = slice(g * GMLP_GROUP_WIDTH, (g + 1) * GMLP_GROUP_WIDTH)
            wm = jnp.where(mask, ws_ref[g], 0.0).astype(BF16)
            wmt = jnp.where(mask_t, wst_ref[g], 0.0).astype(BF16)
            sv = jnp.dot(wm, vn[:, sl], preferred_element_type=F32) + b_ref[:, sl]
            dgs = dgt[:, sl]
            du = dgs * sv
            dsv = dgs * u[:, sl]
            db_ref[:, sl] += dsv
            dsv_b = dsv.astype(BF16)
            dws = lax.dot_general(dsv_b, vn[:, sl], (((1,), (1,)), ((), ())), preferred_element_type=F32)
            dws_ref[g] += jnp.where(mask, dws, 0.0)
            dvn_parts.append(jnp.dot(wmt, dsv_b, preferred_element_type=F32))
            dz_ref[:, sl] = (du * _gelu_grad(zp_ref[:, sl].astype(F32))).astype(BF16)
        dvn_full = jnp.concatenate(dvn_parts, axis=1)
        dvn_ref[...] += jnp.sum(dvn_full * vh, axis=0, keepdims=True)
        dxh = dvn_full * gain
        dv = r * (dxh - vh * jnp.mean(dxh * vh, axis=-1, keepdims=True))
        dz_ref[:, DG:] = (dv * _gelu_grad(zp_ref[:, DG:].astype(F32))).astype(BF16)

    full = pl.BlockSpec((C, DG2), lambda i: (i, 0))
    wspec = pl.BlockSpec((G, C, C), lambda i: (0, 0, 0))
    return pl.pallas_call(
        body, name=name, grid=(S // C,),
        in_specs=[full, full, pl.BlockSpec((C, DG), lambda i: (i, 0)), pl.BlockSpec((1, DG), lambda i: (0, 0)),
                  wspec, wspec, pl.BlockSpec((C, DG), lambda i: (0, 0))],
        out_specs=[full, wspec, pl.BlockSpec((C, DG), lambda i: (0, 0)), pl.BlockSpec((1, DG), lambda i: (0, 0))],
        out_shape=[SDS((S, DG2), BF16), SDS((G, C, C), F32), SDS((C, DG), F32), SDS((1, DG), F32)],
        compiler_params=_cp(("arbitrary",), 40 * 2 ** 20),
    )(z, zpre, dgated, v_norm, w_s, w_s_t, bias_full)


def _alibi_slopes(n_heads):
    return [float(v) for v in np.exp2(np.float32(-8.0) * np.arange(1, n_heads + 1, dtype=np.float32) / np.float32(n_heads))]


def _dil_view(arr, dil):
    S, C = arr.shape
    return arr if dil == 1 else arr.reshape(S // dil, dil * C)


def _dil_spec(dil, HD, ncb, cb, bmap):
    return pl.BlockSpec((ATTN_BLOCK, HD), lambda r, b: (bmap(b), r * ncb + cb))


def _dil_unview(arr, S):
    return arr.reshape(S, arr.size // S)


def _attn_mask(b):
    qi = lax.broadcasted_iota(jnp.int32, (ATTN_BLOCK, 2 * ATTN_BLOCK), 0)
    kj = lax.broadcasted_iota(jnp.int32, (ATTN_BLOCK, 2 * ATTN_BLOCK), 1)
    delta = qi + ATTN_BLOCK - kj
    valid = (delta >= 0) & (delta <= ATTN_BLOCK) & ((kj >= ATTN_BLOCK) | (b > 0))
    return valid, delta.astype(F32)


def _attn_fwd(name, qn, kn, vb, gi, dil):
    S, C = qn.shape
    NG = len(DILATIONS)
    HD = C // NG
    H = HD // HEAD_DIM
    L = S // dil
    nb = L // ATTN_BLOCK
    slopes = _alibi_slopes(H)

    def body(q_ref, kc_ref, kp_ref, vc_ref, vp_ref, o_ref, lse_ref):
        b = pl.program_id(1)
        valid, delta = _attn_mask(b)
        dist = delta * float(dil)
        for h in range(H):
            sl = slice(h * HEAD_DIM, (h + 1) * HEAD_DIM)
            k = jnp.concatenate([kp_ref[:, sl], kc_ref[:, sl]], axis=0)
            v = jnp.concatenate([vp_ref[:, sl], vc_ref[:, sl]], axis=0)
            s = lax.dot_general(q_ref[:, sl], k, (((1,), (1,)), ((), ())), preferred_element_type=F32)
            s = jnp.where(valid, s - slopes[h] * dist, -jnp.inf)
            m = jnp.max(s, axis=-1, keepdims=True)
            p = jnp.exp(s - m)
            l = jnp.sum(p, axis=-1, keepdims=True)
            o = jnp.dot(p.astype(BF16), v, preferred_element_type=F32)
            o_ref[:, sl] = o / l
            lse_ref[:, sl] = jnp.broadcast_to(m + jnp.log(l), (ATTN_BLOCK, HEAD_DIM))

    cur = lambda b: b
    prev = lambda b: jnp.maximum(b - 1, 0)
    qv, kv, vv = _dil_view(qn, dil), _dil_view(kn, dil), _dil_view(vb, dil)
    view_shape = (L, dil * HD)
    o, lse = pl.pallas_call(
        body, name=name, grid=(dil, nb),
        in_specs=[_dil_spec(dil, HD, NG, gi, cur), _dil_spec(dil, HD, NG, gi, cur), _dil_spec(dil, HD, NG, gi, prev),
                  _dil_spec(dil, HD, NG, gi, cur), _dil_spec(dil, HD, NG, gi, prev)],
        out_specs=[_dil_spec(dil, HD, 1, 0, cur), _dil_spec(dil, HD, 1, 0, cur)],
        out_shape=[SDS(view_shape, F32), SDS(view_shape, F32)],
        compiler_params=_cp(("parallel", "parallel"), 32 * 2 ** 20),
    )(qv, kv, kv, vv, vv)
    return _dil_unview(o, S), _dil_unview(lse, S)


def _attn_combine(name, o_list, lse_list, d_o=None):
    S, HD = o_list[0].shape
    H = HD // HEAD_DIM
    ng = len(o_list)
    tr = _row_tile(S, HD * 4, 2 ** 19)

    def body(*refs):
        o_refs = refs[:ng]
        l_refs = refs[ng:2 * ng]
        ls = [r[...] for r in l_refs]
        m = ls[0]
        for t in ls[1:]:
            m = jnp.maximum(m, t)
        es = [jnp.exp(t - m) for t in ls]
        z = es[0]
        for t in es[1:]:
            z = z + t
        o = es[0] * o_refs[0][...]
        for e, r in zip(es[1:], o_refs[1:]):
            o = o + e * r[...]
        o = o / z
        if d_o is None:
            refs[2 * ng][...] = o.astype(BF16)
        else:
            do_ref, lse_ref, dl_ref, dob_ref = refs[2 * ng:]
            dov = do_ref[...]
            lse_ref[...] = m + jnp.log(z)
            dob_ref[...] = dov.astype(BF16)
            prod = dov * o
            for h in range(H):
                sl = slice(h * HEAD_DIM, (h + 1) * HEAD_DIM)
                dl_ref[:, sl] = jnp.broadcast_to(jnp.sum(prod[:, sl], axis=-1, keepdims=True), (tr, HEAD_DIM))

    row = pl.BlockSpec((tr, HD), lambda i: (i, 0))
    if d_o is None:
        return pl.pallas_call(
            body, name=name, grid=(S // tr,), in_specs=[row] * (2 * ng), out_specs=row,
            out_shape=SDS((S, HD), BF16), compiler_params=_cp(("parallel",), 40 * 2 ** 20),
        )(*o_list, *lse_list)
    return pl.pallas_call(
        body, name=name, grid=(S // tr,), in_specs=[row] * (2 * ng + 1), out_specs=[row, row, row],
        out_shape=[SDS((S, HD), F32), SDS((S, HD), F32), SDS((S, HD), BF16)],
        compiler_params=_cp(("parallel",), 48 * 2 ** 20),
    )(*o_list, *lse_list, d_o)


def _attn_bwd(name, qn, kn, vb, dob, lse, delta_rows, gi, dil):
    S, C = qn.shape
    NG = len(DILATIONS)
    HD = C // NG
    H = HD // HEAD_DIM
    L = S // dil
    nb = L // ATTN_BLOCK
    slopes = _alibi_slopes(H)
    B = ATTN_BLOCK

    def body(q_ref, kc_ref, kp_ref, vc_ref, vp_ref, do_ref, lse_ref, dl_ref, dq_ref, dk_ref, dv_ref, ck_ref, cv_ref):
        b = pl.program_id(1)

        @pl.when(b == 0)
        def _():
            ck_ref[...] = jnp.zeros_like(ck_ref)
            cv_ref[...] = jnp.zeros_like(cv_ref)

        @pl.when(b < nb)
        def _():
            valid, delta = _attn_mask(b)
            dist = delta * float(dil)
            for h in range(H):
                sl = slice(h * HEAD_DIM, (h + 1) * HEAD_DIM)
                q = q_ref[:, sl]
                k = jnp.concatenate([kp_ref[:, sl], kc_ref[:, sl]], axis=0)
                v = jnp.concatenate([vp_ref[:, sl], vc_ref[:, sl]], axis=0)
                do = do_ref[:, sl]
                s = lax.dot_general(q, k, (((1,), (1,)), ((), ())), preferred_element_type=F32)
                s = jnp.where(valid, s - slopes[h] * dist, -jnp.inf)
                lse2 = jnp.concatenate([lse_ref[:, sl], lse_ref[:, sl]], axis=1)
                dl2 = jnp.concatenate([dl_ref[:, sl], dl_ref[:, sl]], axis=1)
                p = jnp.exp(s - lse2)
                dp = lax.dot_general(do, v, (((1,), (1,)), ((), ())), preferred_element_type=F32)
                ds = (p * (dp - dl2)).astype(BF16)
                dq_ref[:, sl] = jnp.dot(ds, k, preferred_element_type=F32)
                dk2 = lax.dot_general(ds, q, (((0,), (0,)), ((), ())), preferred_element_type=F32)
                dv2 = lax.dot_general(p.astype(BF16), do, (((0,), (0,)), ((), ())), preferred_element_type=F32)
                dk_ref[:, sl] = ck_ref[:, sl] + dk2[:B]
                dv_ref[:, sl] = cv_ref[:, sl] + dv2[:B]
                ck_ref[:, sl] = dk2[B:]
                cv_ref[:, sl] = dv2[B:]

        @pl.when(b == nb)
        def _():
            dk_ref[...] = ck_ref[...]
            dv_ref[...] = cv_ref[...]

    cur = lambda b: jnp.minimum(b, nb - 1)
    prev = lambda b: jnp.maximum(jnp.minimum(b, nb - 1) - 1, 0)
    late = lambda b: jnp.maximum(b - 1, 0)
    qv, kv, vv = _dil_view(qn, dil), _dil_view(kn, dil), _dil_view(vb, dil)
    dov, lsev, dlv = _dil_view(dob, dil), _dil_view(lse, dil), _dil_view(delta_rows, dil)
    view_shape = (L, dil * HD)
    one = lambda m: _dil_spec(dil, HD, 1, 0, m)
    grp = lambda m: _dil_spec(dil, HD, NG, gi, m)
    dq, dk, dv = pl.pallas_call(
        body, name=name, grid=(dil, nb + 1),
        in_specs=[grp(cur), grp(cur), grp(prev), grp(cur), grp(prev), one(cur), one(cur), one(cur)],
        out_specs=[one(cur), one(late), one(late)],
        out_shape=[SDS(view_shape, F32)] * 3,
        scratch_shapes=[pltpu.VMEM((B, HD), F32), pltpu.VMEM((B, HD), F32)],
        compiler_params=_cp(("arbitrary", "arbitrary"), 40 * 2 ** 20),
    )(qv, kv, kv, vv, vv, dov, lsev, dlv)
    return _dil_unview(dq, S), _dil_unview(dk, S), _dil_unview(dv, S)


def _cast_bf16(name, w):
    L, R, C = w.shape
    tr = _row_tile(R, C * 4)
    spec = pl.BlockSpec((None, tr, C), lambda l, i: (l, i, 0))
    return pl.pallas_call(
        lambda i_ref, o_ref: o_ref.__setitem__(Ellipsis, i_ref[...].astype(BF16)),
        name=name, grid=(L, R // tr), in_specs=[spec], out_specs=spec, out_shape=SDS((L, R, C), BF16),
        compiler_params=_cp(("parallel", "parallel"), 32 * 2 ** 20),
    )(w)


def _add_half(name, dw, land, sa, c_arr):
    hr, hc = land.shape
    tr = _row_tile(hr, hc * 2)
    nrb = hr // tr
    if sa == 1:
        mine = pl.BlockSpec((tr, hc), lambda i, c: (c[0] * nrb + i, 0))
    else:
        mine = pl.BlockSpec((tr, hc), lambda i, c: (i, c[0]))
    other = pl.BlockSpec((tr, hc), lambda i, c: (i, 0))

    def body(c_ref, a_ref, b_ref, o_ref):
        o_ref[...] = (a_ref[...].astype(F32) + b_ref[...].astype(F32)).astype(BF16)

    return pl.pallas_call(
        body, name=name,
        grid_spec=pltpu.PrefetchScalarGridSpec(num_scalar_prefetch=1, grid=(nrb,), in_specs=[mine, other], out_specs=other),
        out_shape=SDS((hr, hc), BF16), compiler_params=_cp(("parallel",), 32 * 2 ** 20),
    )(c_arr, dw, land)


def _sum_slots(name, slots, out_dtype=F32):
    n, R, C = slots.shape
    tr = _row_tile(R, C * n * jnp.dtype(slots.dtype).itemsize, 2 ** 21)

    def body(s_ref, o_ref):
        acc = s_ref[0].astype(F32)
        for k in range(1, n):
            acc = acc + s_ref[k].astype(F32)
        o_ref[...] = acc.astype(out_dtype)

    return pl.pallas_call(
        body, name=name, grid=(R // tr,),
        in_specs=[pl.BlockSpec((n, tr, C), lambda i: (0, i, 0))], out_specs=pl.BlockSpec((tr, C), lambda i: (i, 0)),
        out_shape=SDS((R, C), out_dtype), compiler_params=_cp(("parallel",), 32 * 2 ** 20),
    )(slots)


def _adamw(name, g, w, m, v):
    R, C = g.shape
    tr = _row_tile(R, C * 4, 2 ** 19)
    bc1 = 1.0 - ADAM_B1 ** ADAM_STEP
    bc2 = 1.0 - ADAM_B2 ** ADAM_STEP

    def body(g_ref, w_ref, m_ref, v_ref, go_ref, d_ref, mo_ref, vo_ref):
        gv = g_ref[...]
        mn = ADAM_B1 * m_ref[...] + (1.0 - ADAM_B1) * gv
        vn = ADAM_B2 * v_ref[...] + (1.0 - ADAM_B2) * (gv * gv)
        go_ref[...] = gv
        mo_ref[...] = mn
        vo_ref[...] = vn
        d_ref[...] = -ADAM_LR * ((mn / bc1) / (jnp.sqrt(vn / bc2) + ADAM_EPS) + ADAM_WD * w_ref[...])

    spec = pl.BlockSpec((tr, C), lambda i: (i, 0))
    return pl.pallas_call(
        body, name=name, grid=(R // tr,), in_specs=[spec] * 4, out_specs=[spec] * 4,
        out_shape=[SDS((R, C), F32)] * 4, compiler_params=_cp(("parallel",), 32 * 2 ** 20),
    )(g, w, m, v)


ANY = pl.BlockSpec(memory_space=pl.ANY)


def _coords():
    return lax.axis_index("x"), lax.axis_index("y"), lax.axis_index("c")


def _other_chips(x, y):
    return [((1 - x, y), 2 * (1 - x) + y), ((x, 1 - y), 2 * x + (1 - y)), ((1 - x, 1 - y), 2 * (1 - x) + (1 - y))]


def _win(ref, axis, start, size):
    if not isinstance(start, int):
        start = pl.multiple_of(start, LANE if axis == 1 else BF16_ROWS)
    if axis == 0:
        return ref.at[pl.ds(start, size), :]
    return ref.at[:, pl.ds(start, size)]


def _rcopy(src, dst, ssem, rsem, dev):
    return pltpu.make_async_remote_copy(src_ref=src, dst_ref=dst, send_sem=ssem, recv_sem=rsem,
                                        device_id=dev, device_id_type=MESH)


def _all_gather_mats(name, items):
    n = len(items)
    stacks = []
    metas = []
    for arr, l, sa in items:
        if not any(arr is s for s in stacks):
            stacks.append(arr)
        si = [k for k, s in enumerate(stacks) if s is arr][0]
        r, c = arr.shape[1:]
        metas.append((si, l, sa, r, c))
    ns = len(stacks)
    out_shape = [SDS((4 * r, c) if sa == 0 else (r, 4 * c), BF16) for (_, _, sa, r, c) in metas]

    def body(*refs):
        sh = refs[:ns]
        outs = refs[ns:ns + n]
        loc_sem, s_ici, r_ici, s_fwd, r_fwd = refs[ns + n:]
        x, y, c = _coords()
        jme = 2 * x + y
        chips = _other_chips(x, y)
        sibling = (x, y, 1 - c)
        sends, local = [], []
        for o, (si, l, sa, r, cc) in enumerate(metas):
            ha, hl, sl_ = 1 - sa, (cc if sa == 0 else r) // 2, (r if sa == 0 else cc)
            src = sh[si].at[l]
            mine = _win(outs[o], sa, jme * sl_, sl_)
            loc = pltpu.make_async_copy(src, mine, loc_sem.at[o])
            loc.start()
            local.append(loc)
            for k, (chip, _) in enumerate(chips):
                cp = _rcopy(_win(src, ha, c * hl, hl), _win(mine, ha, c * hl, hl), s_ici.at[o, k], r_ici.at[o, k], (*chip, c))
                cp.start()
                sends.append(cp)
        for o, (si, l, sa, r, cc) in enumerate(metas):
            ha, hl, sl_ = 1 - sa, (cc if sa == 0 else r) // 2, (r if sa == 0 else cc)
            for k, (chip, jk) in enumerate(chips):
                landed = _win(_win(outs[o], sa, jk * sl_, sl_), ha, c * hl, hl)
                _rcopy(landed, landed, s_ici.at[o, k], r_ici.at[o, k], (*chip, c)).wait_recv()
                fwd = _rcopy(landed, landed, s_fwd.at[o, k], r_fwd.at[o, k], sibling)
                fwd.start()
                sends.append(fwd)
        for o, (si, l, sa, r, cc) in enumerate(metas):
            ha, hl, sl_ = 1 - sa, (cc if sa == 0 else r) // 2, (r if sa == 0 else cc)
            for k, (chip, jk) in enumerate(chips):
                got = _win(_win(outs[o], sa, jk * sl_, sl_), ha, (1 - c) * hl, hl)
                _rcopy(got, got, s_fwd.at[o, k], r_fwd.at[o, k], sibling).wait_recv()
        for cp in sends:
            cp.wait_send()
        for cp in local:
            cp.wait()

    return pl.pallas_call(
        body, name=name, in_specs=[ANY] * ns, out_specs=[ANY] * n, out_shape=out_shape,
        scratch_shapes=[pltpu.SemaphoreType.DMA((n,)), pltpu.SemaphoreType.DMA((n, 3)), pltpu.SemaphoreType.DMA((n, 3)),
                        pltpu.SemaphoreType.DMA((n, 3)), pltpu.SemaphoreType.DMA((n, 3))],
        compiler_params=pltpu.CompilerParams(has_side_effects=True),
    )(*stacks)


def _all_gather_vec(name, v):
    Lv, cv = v.shape

    def body(v_ref, o_ref, loc_sem, s_sem, r_sem):
        x, y, c = _coords()
        jme = 2 * x + y
        chips = _other_chips(x, y)
        mine = _win(o_ref, 1, jme * cv, cv)
        loc = pltpu.make_async_copy(v_ref, mine, loc_sem)
        loc.start()
        sends = []
        for k, (chip, _) in enumerate(chips):
            cp = _rcopy(v_ref, mine, s_sem.at[k], r_sem.at[k], (*chip, c))
            cp.start()
            sends.append(cp)
        for k, (chip, jk) in enumerate(chips):
            got = _win(o_ref, 1, jk * cv, cv)
            _rcopy(got, got, s_sem.at[k], r_sem.at[k], (*chip, c)).wait_recv()
        for cp in sends:
            cp.wait_send()
        loc.wait()

    return pl.pallas_call(
        body, name=name, in_specs=[ANY], out_specs=ANY, out_shape=SDS((Lv, 4 * cv), v.dtype),
        scratch_shapes=[pltpu.SemaphoreType.DMA, pltpu.SemaphoreType.DMA((3,)), pltpu.SemaphoreType.DMA((3,))],
        compiler_params=pltpu.CompilerParams(has_side_effects=True),
    )(v)


def _half_shape(shape, sa):
    R, C = shape
    return (R // 2, C) if sa == 1 else (R, C // 2)


def _rs_sibling(name, dws, sas):
    n = len(dws)

    def body(*refs):
        src, land = refs[:n], refs[n:2 * n]
        s_sem, r_sem = refs[2 * n:]
        x, y, c = _coords()
        sibling = (x, y, 1 - c)
        sends = []
        for o in range(n):
            ha = 1 - sas[o]
            hl = dws[o].shape[ha] // 2
            cp = _rcopy(_win(src[o], ha, (1 - c) * hl, hl), land[o], s_sem.at[o], r_sem.at[o], sibling)
            cp.start()
            sends.append(cp)
        for o in range(n):
            _rcopy(land[o], land[o], s_sem.at[o], r_sem.at[o], sibling).wait_recv()
        for cp in sends:
            cp.wait_send()

    return pl.pallas_call(
        body, name=name, in_specs=[ANY] * n, out_specs=[ANY] * n,
        out_shape=[SDS(_half_shape(d.shape, sa), d.dtype) for d, sa in zip(dws, sas)],
        scratch_shapes=[pltpu.SemaphoreType.DMA((n,)), pltpu.SemaphoreType.DMA((n,))],
        compiler_params=pltpu.CompilerParams(has_side_effects=True),
    )(*dws)


def _rs_chips(name, parts, sas):
    n = len(parts)

    def piece_shape(p, sa):
        hr, hc = p.shape
        return (hr // N_CHIPS, hc) if sa == 0 else (hr, hc // N_CHIPS)

    def body(*refs):
        src, slots = refs[:n], refs[n:2 * n]
        loc_sem, s_sem, r_sem = refs[2 * n:]
        x, y, c = _coords()
        jme = 2 * x + y
        chips = _other_chips(x, y)
        sends, local = [], []
        for o in range(n):
            sa = sas[o]
            pl_ = parts[o].shape[sa] // N_CHIPS
            loc = pltpu.make_async_copy(_win(src[o], sa, jme * pl_, pl_), slots[o].at[jme], loc_sem.at[o])
            loc.start()
            local.append(loc)
            for k, (chip, jk) in enumerate(chips):
                cp = _rcopy(_win(src[o], sa, jk * pl_, pl_), slots[o].at[jme], s_sem.at[o, k], r_sem.at[o, k], (*chip, c))
                cp.start()
                sends.append(cp)
        for o in range(n):
            for k, (chip, jk) in enumerate(chips):
                _rcopy(slots[o].at[jk], slots[o].at[jk], s_sem.at[o, k], r_sem.at[o, k], (*chip, c)).wait_recv()
        for cp in sends:
            cp.wait_send()
        for cp in local:
            cp.wait()

    return pl.pallas_call(
        body, name=name, in_specs=[ANY] * n, out_specs=[ANY] * n,
        out_shape=[SDS((N_CHIPS,) + piece_shape(p, sa), p.dtype) for p, sa in zip(parts, sas)],
        scratch_shapes=[pltpu.SemaphoreType.DMA((n,)), pltpu.SemaphoreType.DMA((n, 3)), pltpu.SemaphoreType.DMA((n, 3))],
        compiler_params=pltpu.CompilerParams(has_side_effects=True),
    )(*parts)


def _rs_share(name, halves, sas, layers, buf_idx, bufs):
    n, nbuf = len(halves), len(bufs)

    def body(*refs):
        src = refs[:n]
        out = refs[n + nbuf:n + 2 * nbuf]
        loc_sem, s_sem, r_sem = refs[n + 2 * nbuf:]
        x, y, c = _coords()
        sibling = (x, y, 1 - c)
        sends, local = [], []
        for o in range(n):
            ha = 1 - sas[o]
            hl = halves[o].shape[ha]
            dst = _win(out[buf_idx[o]].at[layers[o]], ha, c * hl, hl)
            loc = pltpu.make_async_copy(src[o], dst, loc_sem.at[o])
            loc.start()
            local.append(loc)
            cp = _rcopy(src[o], dst, s_sem.at[o], r_sem.at[o], sibling)
            cp.start()
            sends.append(cp)
        for o in range(n):
            ha = 1 - sas[o]
            hl = halves[o].shape[ha]
            got = _win(out[buf_idx[o]].at[layers[o]], ha, (1 - c) * hl, hl)
            _rcopy(got, got, s_sem.at[o], r_sem.at[o], sibling).wait_recv()
        for cp in sends:
            cp.wait_send()
        for cp in local:
            cp.wait()

    return pl.pallas_call(
        body, name=name, in_specs=[ANY] * (n + nbuf), out_specs=[ANY] * nbuf,
        out_shape=[SDS(b.shape, b.dtype) for b in bufs],
        input_output_aliases={n + k: k for k in range(nbuf)},
        scratch_shapes=[pltpu.SemaphoreType.DMA((n,)), pltpu.SemaphoreType.DMA((n,)), pltpu.SemaphoreType.DMA((n,))],
        compiler_params=pltpu.CompilerParams(has_side_effects=True),
    )(*halves, *bufs)


def _exchange_all(name, packed):
    R, C = packed.shape

    def body(p_ref, slots, loc_sem, s_sem, r_sem):
        x, y, c = _coords()
        me = 4 * x + 2 * y + c
        loc = pltpu.make_async_copy(p_ref, slots.at[me], loc_sem)
        loc.start()
        peers = []
        for k in range(1, N_DEV):
            px = 1 - x if k & 4 else x
            py = 1 - y if k & 2 else y
            pc = 1 - c if k & 1 else c
            peers.append(((px, py, pc), 4 * px + 2 * py + pc))
        sends = []
        for k, (peer, _) in enumerate(peers):
            cp = _rcopy(p_ref, slots.at[me], s_sem.at[k], r_sem.at[k], peer)
            cp.start()
            sends.append(cp)
        for k, (peer, pid) in enumerate(peers):
            _rcopy(slots.at[pid], slots.at[pid], s_sem.at[k], r_sem.at[k], peer).wait_recv()
        for cp in sends:
            cp.wait_send()
        loc.wait()

    return pl.pallas_call(
        body, name=name, in_specs=[ANY], out_specs=ANY, out_shape=SDS((N_DEV, R, C), packed.dtype),
        scratch_shapes=[pltpu.SemaphoreType.DMA, pltpu.SemaphoreType.DMA((N_DEV - 1,)), pltpu.SemaphoreType.DMA((N_DEV - 1,))],
        compiler_params=pltpu.CompilerParams(has_side_effects=True),
    )(packed)


def _pack(arrays):
    rows = []
    for a in arrays:
        flat = a.reshape(-1).astype(F32)
        pad = (-flat.size) % PACK_TILE
        rows.append(jnp.pad(flat, (0, pad)).reshape(-1, LANE))
    return jnp.concatenate(rows, axis=0)


def _unpack(packed, shapes):
    out, row = [], 0
    for s in shapes:
        size = int(np.prod(s)) if len(s) else 1
        nrows = -(-size // PACK_TILE) * (PACK_TILE // LANE)
        out.append(packed[row:row + nrows].reshape(-1)[:size].reshape(s))
        row += nrows
    return out


BIG_WEIGHTS = {
    "ffn1_w_gate": 1, "ffn1_w_up": 1, "ffn1_w_down": 0, "ffn2_w_gate": 1, "ffn2_w_up": 1, "ffn2_w_down": 0,
    "gmlp_w_in": 1, "gmlp_w_out": 0, "w_kv": 1, "attn_w_q": 1, "attn_w_o": 0,
}
SMALL_WEIGHTS = ("ffn1_norm", "mix_norm", "ffn2_norm", "gmlp_w_s", "gmlp_b_s", "kv_norm", "k_norm", "attn_q_norm")
WEIGHT_ORDER = ("ffn1_norm", "ffn1_w_gate", "ffn1_w_up", "ffn1_w_down", "mix_norm", "ffn2_norm", "ffn2_w_gate",
                "ffn2_w_up", "ffn2_w_down", "gmlp_w_in", "gmlp_v_norm", "gmlp_w_s", "gmlp_b_s", "gmlp_w_out",
                "kv_norm", "w_kv", "k_norm", "attn_w_q", "attn_q_norm", "attn_w_o")


def _ep_all(accs, ex):
    return list(accs)


def _as3d(w):
    return w if w.ndim == 3 else w.reshape((1,) + w.shape)


def kernel(x, ffn1_norm, ffn1_w_gate, ffn1_w_up, ffn1_w_down, mix_norm, ffn2_norm, ffn2_w_gate, ffn2_w_up, ffn2_w_down, gmlp_w_in, gmlp_v_norm, gmlp_w_s, gmlp_b_s, gmlp_w_out, kv_norm, w_kv, k_norm, attn_w_q, attn_q_norm, attn_w_o, loss_target, m_ffn1_norm, m_ffn1_w_gate, m_ffn1_w_up, m_ffn1_w_down, m_mix_norm, m_ffn2_norm, m_ffn2_w_gate, m_ffn2_w_up, m_ffn2_w_down, m_gmlp_w_in, m_gmlp_v_norm, m_gmlp_w_s, m_gmlp_b_s, m_gmlp_w_out, m_kv_norm, m_w_kv, m_k_norm, m_attn_w_q, m_attn_q_norm, m_attn_w_o, v_ffn1_norm, v_ffn1_w_gate, v_ffn1_w_up, v_ffn1_w_down, v_mix_norm, v_ffn2_norm, v_ffn2_w_gate, v_ffn2_w_up, v_ffn2_w_down, v_gmlp_w_in, v_gmlp_v_norm, v_gmlp_w_s, v_gmlp_b_s, v_gmlp_w_out, v_kv_norm, v_w_kv, v_k_norm, v_attn_w_q, v_attn_q_norm, v_attn_w_o):
    P = dict(locals())
    assert x.shape[0] == 1, "one sample per device"
    S, D = x.shape[1], x.shape[2]
    NL = ffn1_norm.shape[0]
    NG = len(DILATIONS)
    HD = attn_w_o.shape[1] * N_CHIPS
    H = HD // HEAD_DIM
    DG = gmlp_w_out.shape[1] * N_CHIPS
    G = DG // GMLP_GROUP_WIDTH
    assert all((S // d) % ATTN_BLOCK == 0 for d in DILATIONS) and S % GMLP_CHUNK == 0
    xs = x.reshape(S, D)
    tgt = loss_target.reshape(S, D)
    c_arr = lax.axis_index("c").astype(jnp.int32).reshape(1)
    chip = 2 * lax.axis_index("x") + lax.axis_index("y")
    kv_layer = N_A_LAYERS - 1

    shard = {n: _cast_bf16("cast_shard", _as3d(P[n])) for n in BIG_WEIGHTS}

    def layer_weights(l):
        names = [("ffn1_w_gate", l), ("ffn1_w_up", l), ("ffn1_w_down", l), ("ffn2_w_gate", l), ("ffn2_w_up", l), ("ffn2_w_down", l)]
        if l < N_A_LAYERS:
            names += [("gmlp_w_in", l), ("gmlp_w_out", l)]
        else:
            names += [("attn_w_q", l - N_A_LAYERS), ("attn_w_o", l - N_A_LAYERS)]
        if l == kv_layer:
            names += [("w_kv", 0)]
        return names

    W = {}
    for l in range(NL):
        names = layer_weights(l)
        mats = _all_gather_mats(f"ag_weights_l{l}", [(shard[n], li, BIG_WEIGHTS[n]) for n, li in names])
        W.update(dict(zip(names, mats)))
    vnorm_full = _all_gather_vec("ag_vnorm", gmlp_v_norm)

    kgain = jnp.tile(k_norm[:, None, :], (1, H, 1)).reshape(1, NG * HD)
    qgain = [jnp.tile(attn_q_norm[j][:, None, :], (1, H, 1)).reshape(1, NG * HD) for j in range(NL - N_A_LAYERS)]
    q_scale = HEAD_DIM ** -0.5
    one = [(0, 0, 0)]

    def ffn_fwd(xc, gamma, wg, wu, wd):
        n = _rms_fwd("ffn_norm", xc, gamma)
        g, u, act = _mm("ffn_up", [n], [wg, wu], [(0, 0, 0), (0, 1, 1)], _ep_swiglu, [BF16] * 3)
        (x2,) = _mm("ffn_down", [act], [wd], one, _ep_residual(0.5), [F32], extras=[xc])
        return x2, (xc, n, g, u, act)

    saved = {}
    xc = xs
    for l in range(NL):
        xc, saved["f1", l] = ffn_fwd(xc, ffn1_norm[l], W["ffn1_w_gate", l], W["ffn1_w_up", l], W["ffn1_w_down", l])
        h = _rms_fwd("mix_norm", xc, mix_norm[l])
        if l < N_A_LAYERS:
            zpre, z = _mm("gmlp_in", [h], [W["gmlp_w_in", l]], one, _ep_gelu, [BF16, F32])
            bias_full = jnp.repeat(gmlp_b_s[l].T, GMLP_GROUP_WIDTH, axis=1)
            gated = _gmlp_gate_fwd("gmlp_gate", z, vnorm_full[l:l + 1], gmlp_w_s[l], bias_full)
            (x2,) = _mm("gmlp_out", [gated], [W["gmlp_w_out", l]], one, _ep_residual(1.0), [F32], extras=[xc])
            saved["mix", l] = (xc, h, zpre, z, gated, bias_full)
        else:
            j = l - N_A_LAYERS
            (q_raw,) = _mm("attn_q", [h], [W["attn_w_q", j]], one, _ep_plain, [F32])
            qn = _head_norm_fwd("q_norm", q_raw, 0, 1, qgain[j], q_scale, False)
            os_, lses = [], []
            for gi, dil in enumerate(DILATIONS):
                o, lse = _attn_fwd(f"attn_fwd_d{dil}", qn, kn, vb, gi, dil)
                os_.append(o)
                lses.append(lse)
            ob = _attn_combine("attn_mix", os_, lses)
            (x2,) = _mm("attn_o", [ob], [W["attn_w_o", j]], one, _ep_residual(1.0), [F32], extras=[xc])
            saved["mix", l] = (xc, h, q_raw, qn, os_, lses, ob)
        xc = x2
        xc, saved["f2", l] = ffn_fwd(xc, ffn2_norm[l], W["ffn2_w_gate", l], W["ffn2_w_up", l], W["ffn2_w_down", l])
        if l == kv_layer:
            kvn = _rms_fwd("kv_norm", xc, kv_norm)
            (kv_raw,) = _mm("kv_proj", [kvn], [W["w_kv", 0]], one, _ep_plain, [F32])
            kn, vb = _head_norm_fwd("k_norm", kv_raw, 0, 2, kgain, 1.0, True)
            saved["kv"] = (xc, kvn, kv_raw)

    dx, dxb, loss_rows = _loss_grad("loss", xc, tgt, 0.5)
    dW = {}
    dsmall = {n: [None] * P[n].shape[0] for n in ("ffn1_norm", "mix_norm", "ffn2_norm")}
    dsmall.update(gmlp_w_s=[None] * N_A_LAYERS, gmlp_b_s=[None] * N_A_LAYERS, gmlp_v_norm=[None] * N_A_LAYERS,
                  attn_q_norm=[None] * (NL - N_A_LAYERS))
    dks = [[] for _ in DILATIONS]
    dvs = [[] for _ in DILATIONS]

    def ffn_bwd(dx, dxb, sv, gamma, wg, wu, wd, key, l, next_scale):
        xin, n, g, u, act = sv
        dg, du = _mm("ffn_dact", [dxb], [wd], one, _ep_swiglu_bwd, [BF16, BF16], tb=True, extras=[g, u])
        (dW[key + "_w_down", l],) = _mm("ffn_dwd", [act], [dxb], one, _ep_plain, [BF16], ta=True)
        dW[key + "_w_gate", l], dW[key + "_w_up", l] = _mm("ffn_dwgu", [n], [dg, du], [(0, 0, 0), (0, 1, 1)], _ep_all, [BF16, BF16], ta=True)
        (dn,) = _mm("ffn_dn", [dg, du], [wg, wu], [(0, 0, 0), (1, 1, 0)], _ep_plain, [F32], tb=True)
        dx, dxb, dsmall[key + "_norm"][l] = _rms_bwd("ffn_norm_bwd", xin, gamma, dn, dx, next_scale)
        return dx, dxb

    for l in reversed(range(NL)):
        if l == kv_layer:
            x_kv, kvn, kv_raw = saved["kv"]
            dkv_raw, dkgain = _head_norm_bwd("k_norm_bwd", kv_raw, 0, 2, kgain, 1.0, dks, dvs)
            (dW["w_kv", 0],) = _mm("kv_dw", [kvn], [dkv_raw], one, _ep_plain, [BF16], ta=True)
            (dkvn,) = _mm("kv_dn", [dkv_raw], [W["w_kv", 0]], one, _ep_plain, [F32], tb=True)
            dx, dxb, dkvnorm = _rms_bwd("kv_norm_bwd", x_kv, kv_norm, dkvn, dx, 0.5)
        dx, dxb = ffn_bwd(dx, dxb, saved["f2", l], ffn2_norm[l], W["ffn2_w_gate", l], W["ffn2_w_up", l], W["ffn2_w_down", l], "ffn2", l, 1.0)
        if l < N_A_LAYERS:
            xin, h, zpre, z, gated, bias_full = saved["mix", l]
            (dW["gmlp_w_out", l],) = _mm("gmlp_dwout", [gated], [dxb], one, _ep_plain, [BF16], ta=True)
            (dgated,) = _mm("gmlp_dgated", [dxb], [W["gmlp_w_out", l]], one, _ep_plain, [F32], tb=True)
            dzpre, dws, dbacc, dvn = _gmlp_gate_bwd("gmlp_gate_bwd", z, zpre, dgated, vnorm_full[l:l + 1], gmlp_w_s[l],
                                                    jnp.swapaxes(gmlp_w_s[l], 1, 2), bias_full)
            (dW["gmlp_w_in", l],) = _mm("gmlp_dwin", [h], [dzpre], one, _ep_plain, [BF16], ta=True)
            (dh,) = _mm("gmlp_dh", [dzpre], [W["gmlp_w_in", l]], one, _ep_plain, [F32], tb=True)
            dsmall["gmlp_w_s"][l] = dws
            dsmall["gmlp_b_s"][l] = dbacc.reshape(GMLP_CHUNK, G, GMLP_GROUP_WIDTH).sum(-1).T
            dsmall["gmlp_v_norm"][l] = dvn.reshape(DG)
        else:
            j = l - N_A_LAYERS
            xin, h, q_raw, qn, os_, lses, ob = saved["mix", l]
            (dW["attn_w_o", j],) = _mm("attn_dwo", [ob], [dxb], one, _ep_plain, [BF16], ta=True)
            (d_ob,) = _mm("attn_dob", [dxb], [W["attn_w_o", j]], one, _ep_plain, [F32], tb=True)
            lse_t, dl_rows, dob = _attn_combine("attn_mix_bwd", os_, lses, d_o=d_ob)
            dqs = []
            for gi, dil in enumerate(DILATIONS):
                dq, dk, dv = _attn_bwd(f"attn_bwd_d{dil}", qn, kn, vb, dob, lse_t, dl_rows, gi, dil)
                dqs.append([dq])
                dks[gi].append(dk)
                dvs[gi].append(dv)
            dq_raw, dqgain = _head_norm_bwd("q_norm_bwd", q_raw, 0, 1, qgain[j], q_scale, dqs, None)
            (dW["attn_w_q", j],) = _mm("attn_dwq", [h], [dq_raw], one, _ep_plain, [BF16], ta=True)
            (dh,) = _mm("attn_dh", [dq_raw], [W["attn_w_q", j]], one, _ep_plain, [F32], tb=True)
            dsmall["attn_q_norm"][j] = dqgain.reshape(NG, H, HEAD_DIM).sum(1)
        dx, dxb, dsmall["mix_norm"][l] = _rms_bwd("mix_norm_bwd", xin, mix_norm[l], dh, dx, 0.5)
        dx, dxb = ffn_bwd(dx, dxb, saved["f1", l], ffn1_norm[l], W["ffn1_w_gate", l], W["ffn1_w_up", l], W["ffn1_w_down", l], "ffn1", l, 0.5)
    grad_x = dx.reshape(x.shape)

    names_big = list(BIG_WEIGHTS)
    gbuf = [lax.empty(_as3d(P[n]).shape, F32) for n in names_big]
    for l in range(NL):
        names = layer_weights(l)
        sas = [BIG_WEIGHTS[n] for n, _ in names]
        dws = [dW[k] for k in names]
        lands = _rs_sibling(f"rs_sibling_l{l}", dws, sas)
        parts = [_add_half("rs_add_half", d, ln, sa, c_arr) for d, ln, sa in zip(dws, lands, sas)]
        slots = _rs_chips(f"rs_chips_l{l}", parts, sas)
        halves = [_sum_slots("rs_sum_chips", s) for s in slots]
        gbuf = _rs_share(f"rs_share_l{l}", halves, sas, [li for _, li in names], [names_big.index(n) for n, _ in names], gbuf)
    big_out = {}
    for n, gb in zip(names_big, gbuf):
        shp = P[n].shape
        two = lambda a: a.reshape(-1, a.shape[-1])
        outs = _adamw("adamw", two(gb), two(P[n]), two(P["m_" + n]), two(P["v_" + n]))
        big_out[n] = [o.reshape(shp) for o in outs]

    loss_part = (0.5 / D) * jnp.sum(loss_rows)
    small_grads = [jnp.stack([g.reshape(P[n].shape[1:]) for g in dsmall[n]]) for n in ("ffn1_norm", "mix_norm", "ffn2_norm", "gmlp_w_s", "gmlp_b_s")]
    small_grads += [dkvnorm.reshape(kv_norm.shape), dkgain.reshape(NG, H, HEAD_DIM).sum(1), jnp.stack(dsmall["attn_q_norm"])]
    vn_grad_full = jnp.stack(dsmall["gmlp_v_norm"])
    packed = _pack(small_grads + [vn_grad_full, loss_part.reshape(1)])
    total = _sum_slots("sum_devices", _exchange_all("exchange_small", packed))
    shapes = [P[n].shape for n in SMALL_WEIGHTS] + [vn_grad_full.shape, (1,)]
    red = _unpack(total, shapes)
    loss = red[-1].reshape(())
    cv = gmlp_v_norm.shape[1]
    vn_grad = lax.dynamic_slice_in_dim(red[-2], chip * cv, cv, axis=1)
    names_small = list(SMALL_WEIGHTS) + ["gmlp_v_norm"]
    g_small = red[:len(SMALL_WEIGHTS)] + [vn_grad]
    outs = _adamw("adamw_small", _pack(g_small), _pack([P[n] for n in names_small]),
                  _pack([P["m_" + n] for n in names_small]), _pack([P["v_" + n] for n in names_small]))
    small_shapes = [P[n].shape for n in names_small]
    small_out = {n: [] for n in names_small}
    for o in outs:
        for n, a in zip(names_small, _unpack(o, small_shapes)):
            small_out[n].append(a)

    res = {**big_out, **small_out}
    return (loss, grad_x, *[res[n][0] for n in WEIGHT_ORDER], *[res[n][1] for n in WEIGHT_ORDER],
            *[res[n][2] for n in WEIGHT_ORDER], *[res[n][3] for n in WEIGHT_ORDER])
```

```python
import numpy as np
import jax
import jax.numpy as jnp
from jax import lax
from jax.experimental import pallas as pl
from jax.experimental.pallas import tpu as pltpu

F32 = jnp.float32
BF16 = jnp.bfloat16
SDS = jax.ShapeDtypeStruct

EPS = 1e-6
HEAD_DIM = 128
GMLP_CHUNK = 128
GMLP_GROUP_WIDTH = 128
DILATIONS = (1, 4, 16)
ATTN_BLOCK = 128
N_A_LAYERS = 2
ADAM_LR, ADAM_B1, ADAM_B2, ADAM_EPS, ADAM_WD, ADAM_STEP = 0.001, 0.9, 0.999, 1e-08, 0.01, 10

N_CHIPS = 4
N_DEV = 8
MESH = pl.DeviceIdType.MESH
V7X_VMEM_BYTES = 64 * 2 ** 20
VMEM_CEILING = V7X_VMEM_BYTES - 6 * 2 ** 20
VMEM_BLOCK_BUDGET = 38 * 2 ** 20
LANE = 128
BF16_ROWS = 16
PACK_TILE = 8 * LANE


def _cp(sem=None, vmem=None):
    kw = {}
    if sem is not None:
        kw["dimension_semantics"] = sem
    if vmem is not None:
        kw["vmem_limit_bytes"] = int(min(max(vmem, 16 * 2 ** 20), VMEM_CEILING))
    return pltpu.CompilerParams(**kw)


def _pick(dim, cands):
    for c in cands:
        if c <= dim and dim % c == 0:
            return c
    return dim


def _row_tile(rows, row_bytes, target=2 ** 20):
    t = 1024
    while t > 8 and (t * row_bytes > target or rows % t):
        t //= 2
    return t if rows % t == 0 else rows


def _sigmoid(x):
    return 1.0 / (1.0 + jnp.exp(-x))


_GELU_C = 0.7978845608028654
_GELU_A = 0.044715


def _gelu(x):
    return 0.5 * x * (1.0 + jnp.tanh(_GELU_C * (x + _GELU_A * (x * x * x))))


def _gelu_grad(x):
    t = jnp.tanh(_GELU_C * (x + _GELU_A * (x * x * x)))
    return 0.5 * (1.0 + t) + 0.5 * x * (1.0 - t * t) * (_GELU_C * (1.0 + 3.0 * _GELU_A * x * x))


def _mm_tiles(M, N, K, n_a, n_b, n_acc, io_bytes):
    tks = [K] + [d for d in (4096, 3072, 2816, 2048, 1024, 512, 256, 128) if d < K and K % d == 0]
    tms = [t for t in (1024, 512, 256, 128) if M % t == 0] or [M]
    tns = [t for t in (512, 256, 128) if N % t == 0] or [N]
    best = None
    for tk in tks:
        for tm in tms:
            for tn in tns:
                est = 2 * 2 * (n_a * tm * tk + n_b * tk * tn) + 2 * tm * tn * io_bytes
                est += n_acc * tm * tn * 4 * (2 if tk < K else 1)
                if est <= VMEM_BLOCK_BUDGET:
                    return tm, tn, tk, est
                if best is None or est < best[3]:
                    best = (tm, tn, tk, est)
    return best


def _mm(name, a_list, b_list, terms, epilogue, out_dtypes, *, ta=False, tb=False, extras=()):
    n_acc = 1 + max(t[2] for t in terms)
    a0, b0 = a_list[0], b_list[0]
    (K, M) = a0.shape if ta else a0.shape[::-1]
    N = b0.shape[0] if tb else b0.shape[1]
    io_bytes = sum(jnp.dtype(e.dtype).itemsize for e in extras) + sum(jnp.dtype(d).itemsize for d in out_dtypes)
    tm, tn, tk, est = _mm_tiles(M, N, K, len(a_list), len(b_list), n_acc, io_bytes)
    nk = K // tk
    na, nb, ne, no = len(a_list), len(b_list), len(extras), len(out_dtypes)
    dn = (((0 if ta else 1,), (1 if tb else 0,)), ((), ()))

    def body(*refs):
        a_refs = refs[:na]
        b_refs = refs[na:na + nb]
        e_refs = refs[na + nb:na + nb + ne]
        o_refs = refs[na + nb + ne:na + nb + ne + no]
        acc_refs = refs[na + nb + ne + no:]
        parts = [None] * n_acc
        for ai, bi, qi in terms:
            d = lax.dot_general(a_refs[ai][...], b_refs[bi][...], dn, preferred_element_type=F32)
            parts[qi] = d if parts[qi] is None else parts[qi] + d

        def finish(accs):
            outs = epilogue(accs, [e[...] for e in e_refs])
            for o_ref, o in zip(o_refs, outs):
                o_ref[...] = o.astype(o_ref.dtype)

        if nk == 1:
            finish(parts)
        else:
            k = pl.program_id(2)

            @pl.when(k == 0)
            def _():
                for q in range(n_acc):
                    acc_refs[q][...] = parts[q]

            @pl.when(k > 0)
            def _():
                for q in range(n_acc):
                    acc_refs[q][...] += parts[q]

            @pl.when(k == nk - 1)
            def _():
                finish([acc_refs[q][...] for q in range(n_acc)])

    a_spec = pl.BlockSpec((tk, tm), lambda i, j, k: (k, i)) if ta else pl.BlockSpec((tm, tk), lambda i, j, k: (i, k))
    b_spec = pl.BlockSpec((tn, tk), lambda i, j, k: (j, k)) if tb else pl.BlockSpec((tk, tn), lambda i, j, k: (k, j))
    e_spec = pl.BlockSpec((tm, tn), lambda i, j, k: (i, j))
    outs = pl.pallas_call(
        body, name=name, grid=(M // tm, N // tn, nk),
        in_specs=[a_spec] * na + [b_spec] * nb + [e_spec] * ne,
        out_specs=[e_spec] * no,
        out_shape=[SDS((M, N), d) for d in out_dtypes],
        scratch_shapes=[pltpu.VMEM((tm, tn), F32) for _ in range(n_acc)] if nk > 1 else [],
        compiler_params=_cp(("parallel", "parallel", "arbitrary"), est + 12 * 2 ** 20),
    )(*a_list, *b_list, *extras)
    return outs


def _ep_plain(accs, ex):
    return [accs[0]]


def _ep_swiglu(accs, ex):
    g, u = accs
    return [g, u, g * _sigmoid(g) * u]


def _ep_swiglu_bwd(accs, ex):
    da = accs[0]
    g = ex[0].astype(F32)
    u = ex[1].astype(F32)
    s = _sigmoid(g)
    return [da * u * (s * (1.0 + g * (1.0 - s))), da * (g * s)]


def _ep_gelu(accs, ex):
    return [accs[0], _gelu(accs[0])]


def _ep_residual(scale):
    def ep(accs, ex):
        return [ex[0] + scale * accs[0]]
    return ep


def _rms_fwd(name, x, gamma):
    S, D = x.shape
    tr = _row_tile(S, D * 4)

    def body(x_ref, g_ref, o_ref):
        xv = x_ref[...]
        r = lax.rsqrt(jnp.mean(xv * xv, axis=-1, keepdims=True) + EPS)
        o_ref[...] = (xv * r * g_ref[...]).astype(BF16)

    return pl.pallas_call(
        body, name=name, grid=(S // tr,),
        in_specs=[pl.BlockSpec((tr, D), lambda i: (i, 0)), pl.BlockSpec((1, D), lambda i: (0, 0))],
        out_specs=pl.BlockSpec((tr, D), lambda i: (i, 0)),
        out_shape=SDS((S, D), BF16),
        compiler_params=_cp(("parallel",), 32 * 2 ** 20),
    )(x, gamma.reshape(1, D))


def _rms_bwd(name, x, gamma, dn, dx_in, out_scale):
    S, D = x.shape
    tr = _row_tile(S, D * 4, 2 ** 19)

    def body(x_ref, g_ref, dn_ref, dxi_ref, dxo_ref, dxb_ref, dg_ref):
        i = pl.program_id(0)
        xv = x_ref[...]
        r = lax.rsqrt(jnp.mean(xv * xv, axis=-1, keepdims=True) + EPS)
        xh = xv * r
        dnv = dn_ref[...]
        dxh = dnv * g_ref[...]
        dx = dxi_ref[...] + r * (dxh - xh * jnp.mean(dxh * xh, axis=-1, keepdims=True))
        dxo_ref[...] = dx
        dxb_ref[...] = (out_scale * dx).astype(BF16)
        part = jnp.sum(dnv * xh, axis=0, keepdims=True)

        @pl.when(i == 0)
        def _():
            dg_ref[...] = part

        @pl.when(i > 0)
        def _():
            dg_ref[...] += part

    row = pl.BlockSpec((tr, D), lambda i: (i, 0))
    vec = pl.BlockSpec((1, D), lambda i: (0, 0))
    return pl.pallas_call(
        body, name=name, grid=(S // tr,),
        in_specs=[row, vec, row, row], out_specs=[row, row, vec],
        out_shape=[SDS((S, D), F32), SDS((S, D), BF16), SDS((1, D), F32)],
        compiler_params=_cp(("arbitrary",), 40 * 2 ** 20),
    )(x, gamma.reshape(1, D), dn, dx_in)


def _loss_grad(name, y, t, out_scale):
    S, D = y.shape
    tr = _row_tile(S, D * 4, 2 ** 19)
    inv_d = 1.0 / D

    def body(y_ref, t_ref, dy_ref, dyb_ref, ls_ref):
        i = pl.program_id(0)
        e = y_ref[...] - t_ref[...]
        dy = e * inv_d
        dy_ref[...] = dy
        dyb_ref[...] = (out_scale * dy).astype(BF16)
        part = jnp.sum(e * e, axis=0, keepdims=True)

        @pl.when(i == 0)
        def _():
            ls_ref[...] = part

        @pl.when(i > 0)
        def _():
            ls_ref[...] += part

    row = pl.BlockSpec((tr, D), lambda i: (i, 0))
    vec = pl.BlockSpec((1, D), lambda i: (0, 0))
    return pl.pallas_call(
        body, name=name, grid=(S // tr,),
        in_specs=[row, row], out_specs=[row, row, vec],
        out_shape=[SDS((S, D), F32), SDS((S, D), BF16), SDS((1, D), F32)],
        compiler_params=_cp(("arbitrary",), 32 * 2 ** 20),
    )(y, t)


def _head_norm_fwd(name, raw, part, n_parts, gain_t, scale, with_pass):
    S = raw.shape[0]
    W = raw.shape[1] // n_parts
    nh = W // HEAD_DIM
    tr = _row_tile(S, W * 4, 2 ** 20)

    def body(*refs):
        if with_pass:
            x_ref, p_ref, g_ref, o_ref, po_ref = refs
            po_ref[...] = p_ref[...].astype(BF16)
        else:
            x_ref, g_ref, o_ref = refs
        for h in range(nh):
            sl = slice(h * HEAD_DIM, (h + 1) * HEAD_DIM)
            xv = x_ref[:, sl]
            r = lax.rsqrt(jnp.mean(xv * xv, axis=-1, keepdims=True) + EPS)
            o_ref[:, sl] = (xv * r * g_ref[:, sl] * scale).astype(BF16)

    xspec = pl.BlockSpec((tr, W), lambda i: (i, part))
    ospec = pl.BlockSpec((tr, W), lambda i: (i, 0))
    gspec = pl.BlockSpec((1, W), lambda i: (0, 0))
    if with_pass:
        in_specs = [xspec, pl.BlockSpec((tr, W), lambda i: (i, 1)), gspec]
        args = (raw, raw, gain_t)
        out_specs, out_shape = [ospec, ospec], [SDS((S, W), BF16), SDS((S, W), BF16)]
    else:
        in_specs, args = [xspec, gspec], (raw, gain_t)
        out_specs, out_shape = ospec, SDS((S, W), BF16)
    return pl.pallas_call(
        body, name=name, grid=(S // tr,), in_specs=in_specs, out_specs=out_specs, out_shape=out_shape,
        compiler_params=_cp(("parallel",), 40 * 2 ** 20),
    )(*args)


def _head_norm_bwd(name, raw, part, n_parts, gain_t, scale, dy_groups, pass_groups):
    S = raw.shape[0]
    W = raw.shape[1] // n_parts
    ng = len(dy_groups)
    HD = W // ng
    nhg = HD // HEAD_DIM
    tr = _row_tile(S, W * 4, 2 ** 19)
    n_dy = [len(g) for g in dy_groups]
    n_ps = [len(g) for g in pass_groups] if pass_groups is not None else []
    flat = [a for g in dy_groups for a in g] + ([a for g in pass_groups for a in g] if pass_groups is not None else [])
    out_w = 2 * W if pass_groups is not None else W

    def body(*refs):
        x_ref, g_ref = refs[0], refs[1]
        d_refs = refs[2:2 + len(flat)]
        o_ref, dg_ref = refs[2 + len(flat)], refs[3 + len(flat)]
        i = pl.program_id(0)

        @pl.when(i == 0)
        def _():
            dg_ref[...] = jnp.zeros_like(dg_ref)

        pos = 0
        for gi in range(ng):
            dys = d_refs[pos:pos + n_dy[gi]]
            pos += n_dy[gi]
            for h in range(nhg):
                sl = slice(gi * HD + h * HEAD_DIM, gi * HD + (h + 1) * HEAD_DIM)
                hs = slice(h * HEAD_DIM, (h + 1) * HEAD_DIM)
                dy = dys[0][:, hs]
                for extra in dys[1:]:
                    dy = dy + extra[:, hs]
                xv = x_ref[:, sl]
                r = lax.rsqrt(jnp.mean(xv * xv, axis=-1, keepdims=True) + EPS)
                xh = xv * r
                dxh = dy * (g_ref[:, sl] * scale)
                o_ref[:, sl] = (r * (dxh - xh * jnp.mean(dxh * xh, axis=-1, keepdims=True))).astype(BF16)
                dg_ref[:, sl] += jnp.sum(dy * xh, axis=0, keepdims=True) * scale
        for gi in range(len(n_ps)):
            ps = d_refs[pos:pos + n_ps[gi]]
            pos += n_ps[gi]
            acc = ps[0][...]
            for extra in ps[1:]:
                acc = acc + extra[...]
            o_ref[:, W + gi * HD:W + (gi + 1) * HD] = acc.astype(BF16)

    dspec = pl.BlockSpec((tr, HD), lambda i: (i, 0))
    return pl.pallas_call(
        body, name=name, grid=(S // tr,),
        in_specs=[pl.BlockSpec((tr, W), lambda i: (i, part)), pl.BlockSpec((1, W), lambda i: (0, 0))] + [dspec] * len(flat),
        out_specs=[pl.BlockSpec((tr, out_w), lambda i: (i, 0)), pl.BlockSpec((1, W), lambda i: (0, 0))],
        out_shape=[SDS((S, out_w), BF16), SDS((1, W), F32)],
        compiler_params=_cp(("arbitrary",), 48 * 2 ** 20),
    )(raw, gain_t, *flat)


def _tril_mask():
    r = lax.broadcasted_iota(jnp.int32, (GMLP_CHUNK, GMLP_CHUNK), 0)
    c = lax.broadcasted_iota(jnp.int32, (GMLP_CHUNK, GMLP_CHUNK), 1)
    return r >= c


def _gmlp_gate_fwd(name, z, v_norm, w_s, bias_full):
    S, DG2 = z.shape
    DG = DG2 // 2
    G = DG // GMLP_GROUP_WIDTH
    C = GMLP_CHUNK

    def body(u_ref, v_ref, vn_ref, ws_ref, b_ref, o_ref):
        mask = _tril_mask()
        v = v_ref[...]
        r = lax.rsqrt(jnp.mean(v * v, axis=-1, keepdims=True) + EPS)
        vn = (v * r * vn_ref[...]).astype(BF16)
        for g in range(G):
            sl = slice(g * GMLP_GROUP_WIDTH, (g + 1) * GMLP_GROUP_WIDTH)
            wm = jnp.where(mask, ws_ref[g], 0.0).astype(BF16)
            sv = jnp.dot(wm, vn[:, sl], preferred_element_type=F32) + b_ref[:, sl]
            o_ref[:, sl] = (u_ref[:, sl] * sv).astype(BF16)

    return pl.pallas_call(
        body, name=name, grid=(S // C,),
        in_specs=[pl.BlockSpec((C, DG), lambda i: (i, 0)), pl.BlockSpec((C, DG), lambda i: (i, 1)),
                  pl.BlockSpec((1, DG), lambda i: (0, 0)), pl.BlockSpec((G, C, C), lambda i: (0, 0, 0)),
                  pl.BlockSpec((C, DG), lambda i: (0, 0))],
        out_specs=pl.BlockSpec((C, DG), lambda i: (i, 0)),
        out_shape=SDS((S, DG), BF16),
        compiler_params=_cp(("parallel",), 32 * 2 ** 20),
    )(z, z, v_norm, w_s, bias_full)


def _gmlp_gate_bwd(name, z, zpre, dgated, v_norm, w_s, w_s_t, bias_full):
    S, DG2 = z.shape
    DG = DG2 // 2
    G = DG // GMLP_GROUP_WIDTH
    C = GMLP_CHUNK

    def body(z_ref, zp_ref, dg_ref, vn_ref, ws_ref, wst_ref, b_ref, dz_ref, dws_ref, db_ref, dvn_ref):
        i = pl.program_id(0)
        mask = _tril_mask()
        mask_t = jnp.logical_not(mask) | (lax.broadcasted_iota(jnp.int32, (C, C), 0) == lax.broadcasted_iota(jnp.int32, (C, C), 1))
        u = z_ref[:, :DG]
        v = z_ref[:, DG:]
        r = lax.rsqrt(jnp.mean(v * v, axis=-1, keepdims=True) + EPS)
        vh = v * r
        gain = vn_ref[...]
        vn = (vh * gain).astype(BF16)
        dgt = dg_ref[...]

        @pl.when(i == 0)
        def _():
            dws_ref[...] = jnp.zeros_like(dws_ref)
            db_ref[...] = jnp.zeros_like(db_ref)
            dvn_ref[...] = jnp.zeros_like(dvn_ref)

        dvn_parts = []
        for g in range(G):
            sl = slice(g * GMLP_GROUP_WIDTH, (g + 1) * GMLP_GROUP_WIDTH)
            wm = jnp.where(mask, ws_ref[g], 0.0).astype(BF16)
            wmt = jnp.where(mask_t, wst_ref[g], 0.0).astype(BF16)
            sv = jnp.dot(wm, vn[:, sl], preferred_element_type=F32) + b_ref[:, sl]
            dgs = dgt[:, sl]
            du = dgs * sv
            dsv = dgs * u[:, sl]
            db_ref[:, sl] += dsv
            dsv_b = dsv.astype(BF16)
            dws = lax.dot_general(dsv_b, vn[:, sl], (((1,), (1,)), ((), ())), preferred_element_type=F32)
            dws_ref[g] += jnp.where(mask, dws, 0.0)
            dvn_parts.append(jnp.dot(wmt, dsv_b, preferred_element_type=F32))
            dz_ref[:, sl] = (du * _gelu_grad(zp_ref[:, sl].astype(F32))).astype(BF16)
        dvn_full = jnp.concatenate(dvn_parts, axis=1)
        dvn_ref[...] += jnp.sum(dvn_full * vh, axis=0, keepdims=True)
        dxh = dvn_full * gain
        dv = r * (dxh - vh * jnp.mean(dxh * vh, axis=-1, keepdims=True))
        dz_ref[:, DG:] = (dv * _gelu_grad(zp_ref[:, DG:].astype(F32))).astype(BF16)

    full = pl.BlockSpec((C, DG2), lambda i: (i, 0))
    wspec = pl.BlockSpec((G, C, C), lambda i: (0, 0, 0))
    return pl.pallas_call(
        body, name=name, grid=(S // C,),
        in_specs=[full, full, pl.BlockSpec((C, DG), lambda i: (i, 0)), pl.BlockSpec((1, DG), lambda i: (0, 0)),
                  wspec, wspec, pl.BlockSpec((C, DG), lambda i: (0, 0))],
        out_specs=[full, wspec, pl.BlockSpec((C, DG), lambda i: (0, 0)), pl.BlockSpec((1, DG), lambda i: (0, 0))],
        out_shape=[SDS((S, DG2), BF16), SDS((G, C, C), F32), SDS((C, DG), F32), SDS((1, DG), F32)],
        compiler_params=_cp(("arbitrary",), 40 * 2 ** 20),
    )(z, zpre, dgated, v_norm, w_s, w_s_t, bias_full)


def _alibi_slopes(n_heads):
    return [float(v) for v in np.exp2(np.float32(-8.0) * np.arange(1, n_heads + 1, dtype=np.float32) / np.float32(n_heads))]


def _dil_view(arr, dil):
    S, C = arr.shape
    return arr if dil == 1 else arr.reshape(S // dil, dil * C)


def _dil_spec(dil, HD, ncb, cb, bmap):
    return pl.BlockSpec((ATTN_BLOCK, HD), lambda r, b: (bmap(b), r * ncb + cb))


def _dil_unview(arr, S):
    return arr.reshape(S, arr.size // S)


def _attn_mask(b):
    qi = lax.broadcasted_iota(jnp.int32, (ATTN_BLOCK, 2 * ATTN_BLOCK), 0)
    kj = lax.broadcasted_iota(jnp.int32, (ATTN_BLOCK, 2 * ATTN_BLOCK), 1)
    delta = qi + ATTN_BLOCK - kj
    valid = (delta >= 0) & (delta <= ATTN_BLOCK) & ((kj >= ATTN_BLOCK) | (b > 0))
    return valid, delta.astype(F32)


def _attn_fwd(name, qn, kn, vb, gi, dil):
    S, C = qn.shape
    NG = len(DILATIONS)
    HD = C // NG
    H = HD // HEAD_DIM
    L = S // dil
    nb = L // ATTN_BLOCK
    slopes = _alibi_slopes(H)

    def body(q_ref, kc_ref, kp_ref, vc_ref, vp_ref, o_ref, lse_ref):
        b = pl.program_id(1)
        valid, delta = _attn_mask(b)
        dist = delta * float(dil)
        for h in range(H):
            sl = slice(h * HEAD_DIM, (h + 1) * HEAD_DIM)
            k = jnp.concatenate([kp_ref[:, sl], kc_ref[:, sl]], axis=0)
            v = jnp.concatenate([vp_ref[:, sl], vc_ref[:, sl]], axis=0)
            s = lax.dot_general(q_ref[:, sl], k, (((1,), (1,)), ((), ())), preferred_element_type=F32)
            s = jnp.where(valid, s - slopes[h] * dist, -jnp.inf)
            m = jnp.max(s, axis=-1, keepdims=True)
            p = jnp.exp(s - m)
            l = jnp.sum(p, axis=-1, keepdims=True)
            o = jnp.dot(p.astype(BF16), v, preferred_element_type=F32)
            o_ref[:, sl] = o / l
            lse_ref[:, sl] = jnp.broadcast_to(m + jnp.log(l), (ATTN_BLOCK, HEAD_DIM))

    cur = lambda b: b
    prev = lambda b: jnp.maximum(b - 1, 0)
    qv, kv, vv = _dil_view(qn, dil), _dil_view(kn, dil), _dil_view(vb, dil)
    view_shape = (L, dil * HD)
    o, lse = pl.pallas_call(
        body, name=name, grid=(dil, nb),
        in_specs=[_dil_spec(dil, HD, NG, gi, cur), _dil_spec(dil, HD, NG, gi, cur), _dil_spec(dil, HD, NG, gi, prev),
                  _dil_spec(dil, HD, NG, gi, cur), _dil_spec(dil, HD, NG, gi, prev)],
        out_specs=[_dil_spec(dil, HD, 1, 0, cur), _dil_spec(dil, HD, 1, 0, cur)],
        out_shape=[SDS(view_shape, F32), SDS(view_shape, F32)],
        compiler_params=_cp(("parallel", "parallel"), 32 * 2 ** 20),
    )(qv, kv, kv, vv, vv)
    return _dil_unview(o, S), _dil_unview(lse, S)


def _attn_combine(name, o_list, lse_list, d_o=None):
    S, HD = o_list[0].shape
    H = HD // HEAD_DIM
    ng = len(o_list)
    tr = _row_tile(S, HD * 4, 2 ** 19)

    def body(*refs):
        o_refs = refs[:ng]
        l_refs = refs[ng:2 * ng]
        ls = [r[...] for r in l_refs]
        m = ls[0]
        for t in ls[1:]:
            m = jnp.maximum(m, t)
        es = [jnp.exp(t - m) for t in ls]
        z = es[0]
        for t in es[1:]:
            z = z + t
        o = es[0] * o_refs[0][...]
        for e, r in zip(es[1:], o_refs[1:]):
            o = o + e * r[...]
        o = o / z
        if d_o is None:
            refs[2 * ng][...] = o.astype(BF16)
        else:
            do_ref, lse_ref, dl_ref, dob_ref = refs[2 * ng:]
            dov = do_ref[...]
            lse_ref[...] = m + jnp.log(z)
            dob_ref[...] = dov.astype(BF16)
            prod = dov * o
            for h in range(H):
                sl = slice(h * HEAD_DIM, (h + 1) * HEAD_DIM)
                dl_ref[:, sl] = jnp.broadcast_to(jnp.sum(prod[:, sl], axis=-1, keepdims=True), (tr, HEAD_DIM))

    row = pl.BlockSpec((tr, HD), lambda i: (i, 0))
    if d_o is None:
        return pl.pallas_call(
            body, name=name, grid=(S // tr,), in_specs=[row] * (2 * ng), out_specs=row,
            out_shape=SDS((S, HD), BF16), compiler_params=_cp(("parallel",), 40 * 2 ** 20),
        )(*o_list, *lse_list)
    return pl.pallas_call(
        body, name=name, grid=(S // tr,), in_specs=[row] * (2 * ng + 1), out_specs=[row, row, row],
        out_shape=[SDS((S, HD), F32), SDS((S, HD), F32), SDS((S, HD), BF16)],
        compiler_params=_cp(("parallel",), 48 * 2 ** 20),
    )(*o_list, *lse_list, d_o)


def _attn_bwd(name, qn, kn, vb, dob, lse, delta_rows, gi, dil):
    S, C = qn.shape
    NG = len(DILATIONS)
    HD = C // NG
    H = HD // HEAD_DIM
    L = S // dil
    nb = L // ATTN_BLOCK
    slopes = _alibi_slopes(H)
    B = ATTN_BLOCK

    def body(q_ref, kc_ref, kp_ref, vc_ref, vp_ref, do_ref, lse_ref, dl_ref, dq_ref, dk_ref, dv_ref, ck_ref, cv_ref):
        b = pl.program_id(1)

        @pl.when(b == 0)
        def _():
            ck_ref[...] = jnp.zeros_like(ck_ref)
            cv_ref[...] = jnp.zeros_like(cv_ref)

        @pl.when(b < nb)
        def _():
            valid, delta = _attn_mask(b)
            dist = delta * float(dil)
            for h in range(H):
                sl = slice(h * HEAD_DIM, (h + 1) * HEAD_DIM)
                q = q_ref[:, sl]
                k = jnp.concatenate([kp_ref[:, sl], kc_ref[:, sl]], axis=0)
                v = jnp.concatenate([vp_ref[:, sl], vc_ref[:, sl]], axis=0)
                do = do_ref[:, sl]
                s = lax.dot_general(q, k, (((1,), (1,)), ((), ())), preferred_element_type=F32)
                s = jnp.where(valid, s - slopes[h] * dist, -jnp.inf)
                lse2 = jnp.concatenate([lse_ref[:, sl], lse_ref[:, sl]], axis=1)
                dl2 = jnp.concatenate([dl_ref[:, sl], dl_ref[:, sl]], axis=1)
                p = jnp.exp(s - lse2)
                dp = lax.dot_general(do, v, (((1,), (1,)), ((), ())), preferred_element_type=F32)
                ds = (p * (dp - dl2)).astype(BF16)
                dq_ref[:, sl] = jnp.dot(ds, k, preferred_element_type=F32)
                dk2 = lax.dot_general(ds, q, (((0,), (0,)), ((), ())), preferred_element_type=F32)
                dv2 = lax.dot_general(p.astype(BF16), do, (((0,), (0,)), ((), ())), preferred_element_type=F32)
                dk_ref[:, sl] = ck_ref[:, sl] + dk2[:B]
                dv_ref[:, sl] = cv_ref[:, sl] + dv2[:B]
                ck_ref[:, sl] = dk2[B:]
                cv_ref[:, sl] = dv2[B:]

        @pl.when(b == nb)
        def _():
            dk_ref[...] = ck_ref[...]
            dv_ref[...] = cv_ref[...]

    cur = lambda b: jnp.minimum(b, nb - 1)
    prev = lambda b: jnp.maximum(jnp.minimum(b, nb - 1) - 1, 0)
    late = lambda b: jnp.maximum(b - 1, 0)
    qv, kv, vv = _dil_view(qn, dil), _dil_view(kn, dil), _dil_view(vb, dil)
    dov, lsev, dlv = _dil_view(dob, dil), _dil_view(lse, dil), _dil_view(delta_rows, dil)
    view_shape = (L, dil * HD)
    one = lambda m: _dil_spec(dil, HD, 1, 0, m)
    grp = lambda m: _dil_spec(dil, HD, NG, gi, m)
    dq, dk, dv = pl.pallas_call(
        body, name=name, grid=(dil, nb + 1),
        in_specs=[grp(cur), grp(cur), grp(prev), grp(cur), grp(prev), one(cur), one(cur), one(cur)],
        out_specs=[one(cur), one(late), one(late)],
        out_shape=[SDS(view_shape, F32)] * 3,
        scratch_shapes=[pltpu.VMEM((B, HD), F32), pltpu.VMEM((B, HD), F32)],
        compiler_params=_cp(("arbitrary", "arbitrary"), 40 * 2 ** 20),
    )(qv, kv, kv, vv, vv, dov, lsev, dlv)
    return _dil_unview(dq, S), _dil_unview(dk, S), _dil_unview(dv, S)


def _cast_into_gathered(name, w, l, sa, chip_arr):
    L, r, c = w.shape
    tr = _row_tile(r, c * 4)
    nrb = r // tr
    src = pl.BlockSpec((None, tr, c), lambda i, chip: (l, i, 0))
    if sa == 0:
        dst, shape = pl.BlockSpec((tr, c), lambda i, chip: (chip[0] * nrb + i, 0)), (N_CHIPS * r, c)
    else:
        dst, shape = pl.BlockSpec((tr, c), lambda i, chip: (i, chip[0])), (r, N_CHIPS * c)

    def body(chip_ref, i_ref, o_ref):
        o_ref[...] = i_ref[...].astype(BF16)

    return pl.pallas_call(
        body, name=name,
        grid_spec=pltpu.PrefetchScalarGridSpec(num_scalar_prefetch=1, grid=(nrb,), in_specs=[src], out_specs=dst),
        out_shape=SDS(shape, BF16), compiler_params=_cp(("parallel",), 32 * 2 ** 20),
    )(chip_arr, w)


def _add_half(name, dw, land, sa, c_arr):
    hr, hc = land.shape
    tr = _row_tile(hr, hc * 2)
    nrb = hr // tr
    if sa == 1:
        mine = pl.BlockSpec((tr, hc), lambda i, c: (c[0] * nrb + i, 0))
    else:
        mine = pl.BlockSpec((tr, hc), lambda i, c: (i, c[0]))
    other = pl.BlockSpec((tr, hc), lambda i, c: (i, 0))

    def body(c_ref, a_ref, b_ref, o_ref):
        o_ref[...] = (a_ref[...].astype(F32) + b_ref[...].astype(F32)).astype(BF16)

    return pl.pallas_call(
        body, name=name,
        grid_spec=pltpu.PrefetchScalarGridSpec(num_scalar_prefetch=1, grid=(nrb,), in_specs=[mine, other], out_specs=other),
        out_shape=SDS((hr, hc), BF16), compiler_params=_cp(("parallel",), 32 * 2 ** 20),
    )(c_arr, dw, land)


def _sum_slots(name, slots, out_dtype=F32):
    n, R, C = slots.shape
    tr = _row_tile(R, C * n * jnp.dtype(slots.dtype).itemsize, 2 ** 21)

    def body(s_ref, o_ref):
        acc = s_ref[0].astype(F32)
        for k in range(1, n):
            acc = acc + s_ref[k].astype(F32)
        o_ref[...] = acc.astype(out_dtype)

    return pl.pallas_call(
        body, name=name, grid=(R // tr,),
        in_specs=[pl.BlockSpec((n, tr, C), lambda i: (0, i, 0))], out_specs=pl.BlockSpec((tr, C), lambda i: (i, 0)),
        out_shape=SDS((R, C), out_dtype), compiler_params=_cp(("parallel",), 32 * 2 ** 20),
    )(slots)


def _sum_into(name, slots, buf, l, sa, c_arr):
    n, pr, pc = slots.shape
    tr = _row_tile(pr, pc * n * jnp.dtype(slots.dtype).itemsize, 2 ** 21)
    nrb = pr // tr
    if sa == 1:
        dst = pl.BlockSpec((None, tr, pc), lambda i, c: (l, c[0] * nrb + i, 0))
    else:
        dst = pl.BlockSpec((None, tr, pc), lambda i, c: (l, i, c[0]))

    def body(c_ref, s_ref, b_ref, o_ref):
        acc = s_ref[0].astype(F32)
        for k in range(1, n):
            acc = acc + s_ref[k].astype(F32)
        o_ref[...] = acc

    return pl.pallas_call(
        body, name=name,
        grid_spec=pltpu.PrefetchScalarGridSpec(
            num_scalar_prefetch=1, grid=(nrb,),
            in_specs=[pl.BlockSpec((n, tr, pc), lambda i, c: (0, i, 0)), ANY], out_specs=dst),
        out_shape=SDS(buf.shape, buf.dtype), input_output_aliases={2: 0},
        compiler_params=_cp(("parallel",), 32 * 2 ** 20),
    )(c_arr, slots, buf)


def _adamw(name, g, w, m, v):
    R, C = g.shape
    tr = _row_tile(R, C * 4, 2 ** 19)
    bc1 = 1.0 - ADAM_B1 ** ADAM_STEP
    bc2 = 1.0 - ADAM_B2 ** ADAM_STEP

    def body(g_ref, w_ref, m_ref, v_ref, go_ref, d_ref, mo_ref, vo_ref):
        gv = g_ref[...]
        mn = ADAM_B1 * m_ref[...] + (1.0 - ADAM_B1) * gv
        vn = ADAM_B2 * v_ref[...] + (1.0 - ADAM_B2) * (gv * gv)
        go_ref[...] = gv
        mo_ref[...] = mn
        vo_ref[...] = vn
        d_ref[...] = -ADAM_LR * ((mn / bc1) / (jnp.sqrt(vn / bc2) + ADAM_EPS) + ADAM_WD * w_ref[...])

    spec = pl.BlockSpec((tr, C), lambda i: (i, 0))
    return pl.pallas_call(
        body, name=name, grid=(R // tr,), in_specs=[spec] * 4, out_specs=[spec] * 4,
        out_shape=[SDS((R, C), F32)] * 4, compiler_params=_cp(("parallel",), 32 * 2 ** 20),
    )(g, w, m, v)


ANY = pl.BlockSpec(memory_space=pl.ANY)


def _coords():
    return lax.axis_index("x"), lax.axis_index("y"), lax.axis_index("c")


def _other_chips(x, y):
    return [((1 - x, y), 2 * (1 - x) + y), ((x, 1 - y), 2 * x + (1 - y)), ((1 - x, 1 - y), 2 * (1 - x) + (1 - y))]


def _win(ref, axis, start, size):
    if not isinstance(start, int):
        start = pl.multiple_of(start, LANE if axis == 1 else BF16_ROWS)
    if axis == 0:
        return ref.at[pl.ds(start, size), :]
    return ref.at[:, pl.ds(start, size)]


def _rcopy(src, dst, ssem, rsem, dev):
    return pltpu.make_async_remote_copy(src_ref=src, dst_ref=dst, send_sem=ssem, recv_sem=rsem,
                                        device_id=dev, device_id_type=MESH)


def _all_gather_mats(name, mats, sas):
    n = len(mats)

    def geometry(o):
        sa = sas[o]
        return sa, 1 - sa, mats[o].shape[sa] // N_CHIPS, mats[o].shape[1 - sa] // 2

    def body(*refs):
        outs = refs[n:2 * n]
        s_ici, r_ici, s_fwd, r_fwd = refs[2 * n:]
        x, y, c = _coords()
        jme = 2 * x + y
        chips = _other_chips(x, y)
        sibling = (x, y, 1 - c)
        sends = []
        for o in range(n):
            sa, ha, wl, hl = geometry(o)
            mine = _win(_win(outs[o], sa, jme * wl, wl), ha, c * hl, hl)
            for k, (chip, _) in enumerate(chips):
                cp = _rcopy(mine, mine, s_ici.at[o, k], r_ici.at[o, k], (*chip, c))
                cp.start()
                sends.append(cp)
        for o in range(n):
            sa, ha, wl, hl = geometry(o)
            for k, (chip, jk) in enumerate(chips):
                landed = _win(_win(outs[o], sa, jk * wl, wl), ha, c * hl, hl)
                _rcopy(landed, landed, s_ici.at[o, k], r_ici.at[o, k], (*chip, c)).wait_recv()
                fwd = _rcopy(landed, landed, s_fwd.at[o, k], r_fwd.at[o, k], sibling)
                fwd.start()
                sends.append(fwd)
        for o in range(n):
            sa, ha, wl, hl = geometry(o)
            for k, (chip, jk) in enumerate(chips):
                got = _win(_win(outs[o], sa, jk * wl, wl), ha, (1 - c) * hl, hl)
                _rcopy(got, got, s_fwd.at[o, k], r_fwd.at[o, k], sibling).wait_recv()
        for cp in sends:
            cp.wait_send()

    return pl.pallas_call(
        body, name=name, in_specs=[ANY] * n, out_specs=[ANY] * n,
        out_shape=[SDS(m.shape, m.dtype) for m in mats], input_output_aliases={k: k for k in range(n)},
        scratch_shapes=[pltpu.SemaphoreType.DMA((n, 3)), pltpu.SemaphoreType.DMA((n, 3)),
                        pltpu.SemaphoreType.DMA((n, 3)), pltpu.SemaphoreType.DMA((n, 3))],
        compiler_params=pltpu.CompilerParams(has_side_effects=True),
    )(*mats)


def _all_gather_vec(name, v):
    Lv, cv = v.shape

    def body(v_ref, o_ref, loc_sem, s_sem, r_sem):
        x, y, c = _coords()
        jme = 2 * x + y
        chips = _other_chips(x, y)
        mine = _win(o_ref, 1, jme * cv, cv)
        loc = pltpu.make_async_copy(v_ref, mine, loc_sem)
        loc.start()
        sends = []
        for k, (chip, _) in enumerate(chips):
            cp = _rcopy(v_ref, mine, s_sem.at[k], r_sem.at[k], (*chip, c))
            cp.start()
            sends.append(cp)
        for k, (chip, jk) in enumerate(chips):
            got = _win(o_ref, 1, jk * cv, cv)
            _rcopy(got, got, s_sem.at[k], r_sem.at[k], (*chip, c)).wait_recv()
        for cp in sends:
            cp.wait_send()
        loc.wait()

    return pl.pallas_call(
        body, name=name, in_specs=[ANY], out_specs=ANY, out_shape=SDS((Lv, 4 * cv), v.dtype),
        scratch_shapes=[pltpu.SemaphoreType.DMA, pltpu.SemaphoreType.DMA((3,)), pltpu.SemaphoreType.DMA((3,))],
        compiler_params=pltpu.CompilerParams(has_side_effects=True),
    )(v)


def _half_shape(shape, sa):
    R, C = shape
    return (R // 2, C) if sa == 1 else (R, C // 2)


def _rs_sibling(name, dws, sas):
    n = len(dws)

    def body(*refs):
        src, land = refs[:n], refs[n:2 * n]
        s_sem, r_sem = refs[2 * n:]
        x, y, c = _coords()
        sibling = (x, y, 1 - c)
        sends = []
        for o in range(n):
            ha = 1 - sas[o]
            hl = dws[o].shape[ha] // 2
            cp = _rcopy(_win(src[o], ha, (1 - c) * hl, hl), land[o], s_sem.at[o], r_sem.at[o], sibling)
            cp.start()
            sends.append(cp)
        for o in range(n):
            _rcopy(land[o], land[o], s_sem.at[o], r_sem.at[o], sibling).wait_recv()
        for cp in sends:
            cp.wait_send()

    return pl.pallas_call(
        body, name=name, in_specs=[ANY] * n, out_specs=[ANY] * n,
        out_shape=[SDS(_half_shape(d.shape, sa), d.dtype) for d, sa in zip(dws, sas)],
        scratch_shapes=[pltpu.SemaphoreType.DMA((n,)), pltpu.SemaphoreType.DMA((n,))],
        compiler_params=pltpu.CompilerParams(has_side_effects=True),
    )(*dws)


def _rs_chips(name, parts, sas):
    n = len(parts)

    def piece_shape(p, sa):
        hr, hc = p.shape
        return (hr // N_CHIPS, hc) if sa == 0 else (hr, hc // N_CHIPS)

    def body(*refs):
        src, slots = refs[:n], refs[n:2 * n]
        loc_sem, s_sem, r_sem = refs[2 * n:]
        x, y, c = _coords()
        jme = 2 * x + y
        chips = _other_chips(x, y)
        sends, local = [], []
        for o in range(n):
            sa = sas[o]
            pl_ = parts[o].shape[sa] // N_CHIPS
            loc = pltpu.make_async_copy(_win(src[o], sa, jme * pl_, pl_), slots[o].at[jme], loc_sem.at[o])
            loc.start()
            local.append(loc)
            for k, (chip, jk) in enumerate(chips):
                cp = _rcopy(_win(src[o], sa, jk * pl_, pl_), slots[o].at[jme], s_sem.at[o, k], r_sem.at[o, k], (*chip, c))
                cp.start()
                sends.append(cp)
        for o in range(n):
            for k, (chip, jk) in enumerate(chips):
                _rcopy(slots[o].at[jk], slots[o].at[jk], s_sem.at[o, k], r_sem.at[o, k], (*chip, c)).wait_recv()
        for cp in sends:
            cp.wait_send()
        for cp in local:
            cp.wait()

    return pl.pallas_call(
        body, name=name, in_specs=[ANY] * n, out_specs=[ANY] * n,
        out_shape=[SDS((N_CHIPS,) + piece_shape(p, sa), p.dtype) for p, sa in zip(parts, sas)],
        scratch_shapes=[pltpu.SemaphoreType.DMA((n,)), pltpu.SemaphoreType.DMA((n, 3)), pltpu.SemaphoreType.DMA((n, 3))],
        compiler_params=pltpu.CompilerParams(has_side_effects=True),
    )(*parts)


def _rs_share(name, sas, layers, buf_idx, bufs):
    n, nbuf = len(sas), len(bufs)

    def body(*refs):
        out = refs[nbuf:2 * nbuf]
        s_sem, r_sem = refs[2 * nbuf:]
        x, y, c = _coords()
        sibling = (x, y, 1 - c)
        sends = []
        for o in range(n):
            ha = 1 - sas[o]
            hl = bufs[buf_idx[o]].shape[1 + ha] // 2
            mine = _win(out[buf_idx[o]].at[layers[o]], ha, c * hl, hl)
            cp = _rcopy(mine, mine, s_sem.at[o], r_sem.at[o], sibling)
            cp.start()
            sends.append(cp)
        for o in range(n):
            ha = 1 - sas[o]
            hl = bufs[buf_idx[o]].shape[1 + ha] // 2
            got = _win(out[buf_idx[o]].at[layers[o]], ha, (1 - c) * hl, hl)
            _rcopy(got, got, s_sem.at[o], r_sem.at[o], sibling).wait_recv()
        for cp in sends:
            cp.wait_send()

    return pl.pallas_call(
        body, name=name, in_specs=[ANY] * nbuf, out_specs=[ANY] * nbuf,
        out_shape=[SDS(b.shape, b.dtype) for b in bufs],
        input_output_aliases={k: k for k in range(nbuf)},
        scratch_shapes=[pltpu.SemaphoreType.DMA((n,)), pltpu.SemaphoreType.DMA((n,))],
        compiler_params=pltpu.CompilerParams(has_side_effects=True),
    )(*bufs)


def _exchange_all(name, packed):
    R, C = packed.shape

    def body(p_ref, slots, loc_sem, s_sem, r_sem):
        x, y, c = _coords()
        me = 4 * x + 2 * y + c
        loc = pltpu.make_async_copy(p_ref, slots.at[me], loc_sem)
        loc.start()
        peers = []
        for k in range(1, N_DEV):
            px = 1 - x if k & 4 else x
            py = 1 - y if k & 2 else y
            pc = 1 - c if k & 1 else c
            peers.append(((px, py, pc), 4 * px + 2 * py + pc))
        sends = []
        for k, (peer, _) in enumerate(peers):
            cp = _rcopy(p_ref, slots.at[me], s_sem.at[k], r_sem.at[k], peer)
            cp.start()
            sends.append(cp)
        for k, (peer, pid) in enumerate(peers):
            _rcopy(slots.at[pid], slots.at[pid], s_sem.at[k], r_sem.at[k], peer).wait_recv()
        for cp in sends:
            cp.wait_send()
        loc.wait()

    return pl.pallas_call(
        body, name=name, in_specs=[ANY], out_specs=ANY, out_shape=SDS((N_DEV, R, C), packed.dtype),
        scratch_shapes=[pltpu.SemaphoreType.DMA, pltpu.SemaphoreType.DMA((N_DEV - 1,)), pltpu.SemaphoreType.DMA((N_DEV - 1,))],
        compiler_params=pltpu.CompilerParams(has_side_effects=True),
    )(packed)


def _pack(arrays):
    rows = []
    for a in arrays:
        flat = a.reshape(-1).astype(F32)
        pad = (-flat.size) % PACK_TILE
        rows.append(jnp.pad(flat, (0, pad)).reshape(-1, LANE))
    return jnp.concatenate(rows, axis=0)


def _unpack(packed, shapes):
    out, row = [], 0
    for s in shapes:
        size = int(np.prod(s)) if len(s) else 1
        nrows = -(-size // PACK_TILE) * (PACK_TILE // LANE)
        out.append(packed[row:row + nrows].reshape(-1)[:size].reshape(s))
        row += nrows
    return out


BIG_WEIGHTS = {
    "ffn1_w_gate": 1, "ffn1_w_up": 1, "ffn1_w_down": 0, "ffn2_w_gate": 1, "ffn2_w_up": 1, "ffn2_w_down": 0,
    "gmlp_w_in": 1, "gmlp_w_out": 0, "w_kv": 1, "attn_w_q": 1, "attn_w_o": 0,
}
SMALL_WEIGHTS = ("ffn1_norm", "mix_norm", "ffn2_norm", "gmlp_w_s", "gmlp_b_s", "kv_norm", "k_norm", "attn_q_norm")
WEIGHT_ORDER = ("ffn1_norm", "ffn1_w_gate", "ffn1_w_up", "ffn1_w_down", "mix_norm", "ffn2_norm", "ffn2_w_gate",
                "ffn2_w_up", "ffn2_w_down", "gmlp_w_in", "gmlp_v_norm", "gmlp_w_s", "gmlp_b_s", "gmlp_w_out",
                "kv_norm", "w_kv", "k_norm", "attn_w_q", "attn_q_norm", "attn_w_o")


def _ep_all(accs, ex):
    return list(accs)


def _as3d(w):
    return w if w.ndim == 3 else w.reshape((1,) + w.shape)


def kernel(x, ffn1_norm, ffn1_w_gate, ffn1_w_up, ffn1_w_down, mix_norm, ffn2_norm, ffn2_w_gate, ffn2_w_up, ffn2_w_down, gmlp_w_in, gmlp_v_norm, gmlp_w_s, gmlp_b_s, gmlp_w_out, kv_norm, w_kv, k_norm, attn_w_q, attn_q_norm, attn_w_o, loss_target, m_ffn1_norm, m_ffn1_w_gate, m_ffn1_w_up, m_ffn1_w_down, m_mix_norm, m_ffn2_norm, m_ffn2_w_gate, m_ffn2_w_up, m_ffn2_w_down, m_gmlp_w_in, m_gmlp_v_norm, m_gmlp_w_s, m_gmlp_b_s, m_gmlp_w_out, m_kv_norm, m_w_kv, m_k_norm, m_attn_w_q, m_attn_q_norm, m_attn_w_o, v_ffn1_norm, v_ffn1_w_gate, v_ffn1_w_up, v_ffn1_w_down, v_mix_norm, v_ffn2_norm, v_ffn2_w_gate, v_ffn2_w_up, v_ffn2_w_down, v_gmlp_w_in, v_gmlp_v_norm, v_gmlp_w_s, v_gmlp_b_s, v_gmlp_w_out, v_kv_norm, v_w_kv, v_k_norm, v_attn_w_q, v_attn_q_norm, v_attn_w_o):
    P = dict(locals())
    assert x.shape[0] == 1, "one sample per device"
    S, D = x.shape[1], x.shape[2]
    NL = ffn1_norm.shape[0]
    NG = len(DILATIONS)
    HD = attn_w_o.shape[1] * N_CHIPS
    H = HD // HEAD_DIM
    DG = gmlp_w_out.shape[1] * N_CHIPS
    G = DG // GMLP_GROUP_WIDTH
    assert all((S // d) % ATTN_BLOCK == 0 for d in DILATIONS) and S % GMLP_CHUNK == 0
    xs = x.reshape(S, D)
    tgt = loss_target.reshape(S, D)
    c_arr = lax.axis_index("c").astype(jnp.int32).reshape(1)
    chip = 2 * lax.axis_index("x") + lax.axis_index("y")
    chip_arr = chip.astype(jnp.int32).reshape(1)
    kv_layer = N_A_LAYERS - 1


    def layer_weights(l):
        names = [("ffn1_w_gate", l), ("ffn1_w_up", l), ("ffn1_w_down", l), ("ffn2_w_gate", l), ("ffn2_w_up", l), ("ffn2_w_down", l)]
        if l < N_A_LAYERS:
            names += [("gmlp_w_in", l), ("gmlp_w_out", l)]
        else:
            names += [("attn_w_q", l - N_A_LAYERS), ("attn_w_o", l - N_A_LAYERS)]
        if l == kv_layer:
            names += [("w_kv", 0)]
        return names

    W = {}
    for l in range(NL):
        names = layer_weights(l)
        sas = [BIG_WEIGHTS[n] for n, _ in names]
        mats = [_cast_into_gathered("cast_shard", _as3d(P[n]), li, sa, chip_arr) for (n, li), sa in zip(names, sas)]
        W.update(dict(zip(names, _all_gather_mats(f"ag_weights_l{l}", mats, sas))))
    vnorm_full = _all_gather_vec("ag_vnorm", gmlp_v_norm)

    kgain = jnp.tile(k_norm[:, None, :], (1, H, 1)).reshape(1, NG * HD)
    qgain = [jnp.tile(attn_q_norm[j][:, None, :], (1, H, 1)).reshape(1, NG * HD) for j in range(NL - N_A_LAYERS)]
    q_scale = HEAD_DIM ** -0.5
    one = [(0, 0, 0)]

    def ffn_fwd(xc, gamma, wg, wu, wd):
        n = _rms_fwd("ffn_norm", xc, gamma)
        g, u, act = _mm("ffn_up", [n], [wg, wu], [(0, 0, 0), (0, 1, 1)], _ep_swiglu, [BF16] * 3)
        (x2,) = _mm("ffn_down", [act], [wd], one, _ep_residual(0.5), [F32], extras=[xc])
        return x2, (xc, n, g, u, act)

    saved = {}
    xc = xs
    for l in range(NL):
        xc, saved["f1", l] = ffn_fwd(xc, ffn1_norm[l], W["ffn1_w_gate", l], W["ffn1_w_up", l], W["ffn1_w_down", l])
        h = _rms_fwd("mix_norm", xc, mix_norm[l])
        if l < N_A_LAYERS:
            zpre, z = _mm("gmlp_in", [h], [W["gmlp_w_in", l]], one, _ep_gelu, [BF16, F32])
            bias_full = jnp.repeat(gmlp_b_s[l].T, GMLP_GROUP_WIDTH, axis=1)
            gated = _gmlp_gate_fwd("gmlp_gate", z, vnorm_full[l:l + 1], gmlp_w_s[l], bias_full)
            (x2,) = _mm("gmlp_out", [gated], [W["gmlp_w_out", l]], one, _ep_residual(1.0), [F32], extras=[xc])
            saved["mix", l] = (xc, h, zpre, z, gated, bias_full)
        else:
            j = l - N_A_LAYERS
            (q_raw,) = _mm("attn_q", [h], [W["attn_w_q", j]], one, _ep_plain, [F32])
            qn = _head_norm_fwd("q_norm", q_raw, 0, 1, qgain[j], q_scale, False)
            os_, lses = [], []
            for gi, dil in enumerate(DILATIONS):
                o, lse = _attn_fwd(f"attn_fwd_d{dil}", qn, kn, vb, gi, dil)
                os_.append(o)
                lses.append(lse)
            ob = _attn_combine("attn_mix", os_, lses)
            (x2,) = _mm("attn_o", [ob], [W["attn_w_o", j]], one, _ep_residual(1.0), [F32], extras=[xc])
            saved["mix", l] = (xc, h, q_raw, qn, os_, lses, ob)
        xc = x2
        xc, saved["f2", l] = ffn_fwd(xc, ffn2_norm[l], W["ffn2_w_gate", l], W["ffn2_w_up", l], W["ffn2_w_down", l])
        if l == kv_layer:
            kvn = _rms_fwd("kv_norm", xc, kv_norm)
            (kv_raw,) = _mm("kv_proj", [kvn], [W["w_kv", 0]], one, _ep_plain, [F32])
            kn, vb = _head_norm_fwd("k_norm", kv_raw, 0, 2, kgain, 1.0, True)
            saved["kv"] = (xc, kvn, kv_raw)

    dx, dxb, loss_rows = _loss_grad("loss", xc, tgt, 0.5)
    dW = {}
    dsmall = {n: [None] * P[n].shape[0] for n in ("ffn1_norm", "mix_norm", "ffn2_norm")}
    dsmall.update(gmlp_w_s=[None] * N_A_LAYERS, gmlp_b_s=[None] * N_A_LAYERS, gmlp_v_norm=[None] * N_A_LAYERS,
                  attn_q_norm=[None] * (NL - N_A_LAYERS))
    dks = [[] for _ in DILATIONS]
    dvs = [[] for _ in DILATIONS]

    def ffn_bwd(dx, dxb, sv, gamma, wg, wu, wd, key, l, next_scale):
        xin, n, g, u, act = sv
        dg, du = _mm("ffn_dact", [dxb], [wd], one, _ep_swiglu_bwd, [BF16, BF16], tb=True, extras=[g, u])
        (dW[key + "_w_down", l],) = _mm("ffn_dwd", [act], [dxb], one, _ep_plain, [BF16], ta=True)
        dW[key + "_w_gate", l], dW[key + "_w_up", l] = _mm("ffn_dwgu", [n], [dg, du], [(0, 0, 0), (0, 1, 1)], _ep_all, [BF16, BF16], ta=True)
        (dn,) = _mm("ffn_dn", [dg, du], [wg, wu], [(0, 0, 0), (1, 1, 0)], _ep_plain, [F32], tb=True)
        dx, dxb, dsmall[key + "_norm"][l] = _rms_bwd("ffn_norm_bwd", xin, gamma, dn, dx, next_scale)
        return dx, dxb

    for l in reversed(range(NL)):
        if l == kv_layer:
            x_kv, kvn, kv_raw = saved["kv"]
            dkv_raw, dkgain = _head_norm_bwd("k_norm_bwd", kv_raw, 0, 2, kgain, 1.0, dks, dvs)
            (dW["w_kv", 0],) = _mm("kv_dw", [kvn], [dkv_raw], one, _ep_plain, [BF16], ta=True)
            (dkvn,) = _mm("kv_dn", [dkv_raw], [W["w_kv", 0]], one, _ep_plain, [F32], tb=True)
            dx, dxb, dkvnorm = _rms_bwd("kv_norm_bwd", x_kv, kv_norm, dkvn, dx, 0.5)
        dx, dxb = ffn_bwd(dx, dxb, saved["f2", l], ffn2_norm[l], W["ffn2_w_gate", l], W["ffn2_w_up", l], W["ffn2_w_down", l], "ffn2", l, 1.0)
        if l < N_A_LAYERS:
            xin, h, zpre, z, gated, bias_full = saved["mix", l]
            (dW["gmlp_w_out", l],) = _mm("gmlp_dwout", [gated], [dxb], one, _ep_plain, [BF16], ta=True)
            (dgated,) = _mm("gmlp_dgated", [dxb], [W["gmlp_w_out", l]], one, _ep_plain, [F32], tb=True)
            dzpre, dws, dbacc, dvn = _gmlp_gate_bwd("gmlp_gate_bwd", z, zpre, dgated, vnorm_full[l:l + 1], gmlp_w_s[l],
                                                    jnp.swapaxes(gmlp_w_s[l], 1, 2), bias_full)
            (dW["gmlp_w_in", l],) = _mm("gmlp_dwin", [h], [dzpre], one, _ep_plain, [BF16], ta=True)
            (dh,) = _mm("gmlp_dh", [dzpre], [W["gmlp_w_in", l]], one, _ep_plain, [F32], tb=True)
            dsmall["gmlp_w_s"][l] = dws
            dsmall["gmlp_b_s"][l] = dbacc.reshape(GMLP_CHUNK, G, GMLP_GROUP_WIDTH).sum(-1).T
            dsmall["gmlp_v_norm"][l] = dvn.reshape(DG)
        else:
            j = l - N_A_LAYERS
            xin, h, q_raw, qn, os_, lses, ob = saved["mix", l]
            (dW["attn_w_o", j],) = _mm("attn_dwo", [ob], [dxb], one, _ep_plain, [BF16], ta=True)
            (d_ob,) = _mm("attn_dob", [dxb], [W["attn_w_o", j]], one, _ep_plain, [F32], tb=True)
            lse_t, dl_rows, dob = _attn_combine("attn_mix_bwd", os_, lses, d_o=d_ob)
            dqs = []
            for gi, dil in enumerate(DILATIONS):
                dq, dk, dv = _attn_bwd(f"attn_bwd_d{dil}", qn, kn, vb, dob, lse_t, dl_rows, gi, dil)
                dqs.append([dq])
                dks[gi].append(dk)
                dvs[gi].append(dv)
            dq_raw, dqgain = _head_norm_bwd("q_norm_bwd", q_raw, 0, 1, qgain[j], q_scale, dqs, None)
            (dW["attn_w_q", j],) = _mm("attn_dwq", [h], [dq_raw], one, _ep_plain, [BF16], ta=True)
            (dh,) = _mm("attn_dh", [dq_raw], [W["attn_w_q", j]], one, _ep_plain, [F32], tb=True)
            dsmall["attn_q_norm"][j] = dqgain.reshape(NG, H, HEAD_DIM).sum(1)
        dx, dxb, dsmall["mix_norm"][l] = _rms_bwd("mix_norm_bwd", xin, mix_norm[l], dh, dx, 0.5)
        dx, dxb = ffn_bwd(dx, dxb, saved["f1", l], ffn1_norm[l], W["ffn1_w_gate", l], W["ffn1_w_up", l], W["ffn1_w_down", l], "ffn1", l, 0.5)
    grad_x = dx.reshape(x.shape)

    names_big = list(BIG_WEIGHTS)
    gbuf = [lax.empty(_as3d(P[n]).shape, F32) for n in names_big]
    for l in range(NL):
        names = layer_weights(l)
        sas = [BIG_WEIGHTS[n] for n, _ in names]
        dws = [dW[k] for k in names]
        lands = _rs_sibling(f"rs_sibling_l{l}", dws, sas)
        parts = [_add_half("rs_add_half", d, ln, sa, c_arr) for d, ln, sa in zip(dws, lands, sas)]
        slots = _rs_chips(f"rs_chips_l{l}", parts, sas)
        for s, sa, (n, li) in zip(slots, sas, names):
            bi = names_big.index(n)
            gbuf[bi] = _sum_into("rs_sum_chips", s, gbuf[bi], li, sa, c_arr)
        gbuf = list(_rs_share(f"rs_share_l{l}", sas, [li for _, li in names], [names_big.index(n) for n, _ in names], gbuf))
    big_out = {}
    for n, gb in zip(names_big, gbuf):
        shp = P[n].shape
        two = lambda a: a.reshape(-1, a.shape[-1])
        outs = _adamw("adamw", two(gb), two(P[n]), two(P["m_" + n]), two(P["v_" + n]))
        big_out[n] = [o.reshape(shp) for o in outs]

    loss_part = (0.5 / D) * jnp.sum(loss_rows)
    small_grads = [jnp.stack([g.reshape(P[n].shape[1:]) for g in dsmall[n]]) for n in ("ffn1_norm", "mix_norm", "ffn2_norm", "gmlp_w_s", "gmlp_b_s")]
    small_grads += [dkvnorm.reshape(kv_norm.shape), dkgain.reshape(NG, H, HEAD_DIM).sum(1), jnp.stack(dsmall["attn_q_norm"])]
    vn_grad_full = jnp.stack(dsmall["gmlp_v_norm"])
    packed = _pack(small_grads + [vn_grad_full, loss_part.reshape(1)])
    total = _sum_slots("sum_devices", _exchange_all("exchange_small", packed))
    shapes = [P[n].shape for n in SMALL_WEIGHTS] + [vn_grad_full.shape, (1,)]
    red = _unpack(total, shapes)
    loss = red[-1].reshape(())
    cv = gmlp_v_norm.shape[1]
    vn_grad = lax.dynamic_slice_in_dim(red[-2], chip * cv, cv, axis=1)
    names_small = list(SMALL_WEIGHTS) + ["gmlp_v_norm"]
    g_small = red[:len(SMALL_WEIGHTS)] + [vn_grad]
    outs = _adamw("adamw_small", _pack(g_small), _pack([P[n] for n in names_small]),
                  _pack([P["m_" + n] for n in names_small]), _pack([P["v_" + n] for n in names_small]))
    small_shapes = [P[n].shape for n in names_small]
    small_out = {n: [] for n in names_small}
    for o in outs:
        for n, a in zip(names_small, _unpack(o, small_shapes)):
            small_out[n].append(a)

    res = {**big_out, **small_out}
    return (loss, grad_x, *[res[n][0] for n in WEIGHT_ORDER], *[res[n][1] for n in WEIGHT_ORDER],
            *[res[n][2] for n in WEIGHT_ORDER], *[res[n][3] for n in WEIGHT_ORDER])
```

```python
import numpy as np
import jax
import jax.numpy as jnp
from jax import lax
from jax.experimental import pallas as pl
from jax.experimental.pallas import tpu as pltpu

F32 = jnp.float32
BF16 = jnp.bfloat16
SDS = jax.ShapeDtypeStruct

EPS = 1e-6
HEAD_DIM = 128
GMLP_CHUNK = 128
GMLP_GROUP_WIDTH = 128
DILATIONS = (1, 4, 16)
ATTN_BLOCK = 128
N_A_LAYERS = 2
ADAM_LR, ADAM_B1, ADAM_B2, ADAM_EPS, ADAM_WD, ADAM_STEP = 0.001, 0.9, 0.999, 1e-08, 0.01, 10

N_CHIPS = 4
N_DEV = 8
MESH = pl.DeviceIdType.MESH
V7X_VMEM_BYTES = 64 * 2 ** 20
VMEM_CEILING = V7X_VMEM_BYTES - 6 * 2 ** 20
VMEM_BLOCK_BUDGET = 38 * 2 ** 20
LANE = 128
BF16_ROWS = 16
PACK_TILE = 8 * LANE


def _cp(sem=None, vmem=None):
    kw = {}
    if sem is not None:
        kw["dimension_semantics"] = sem
    if vmem is not None:
        kw["vmem_limit_bytes"] = int(min(max(vmem, 16 * 2 ** 20), VMEM_CEILING))
    return pltpu.CompilerParams(**kw)


def _pick(dim, cands):
    for c in cands:
        if c <= dim and dim % c == 0:
            return c
    return dim


def _row_tile(rows, row_bytes, target=2 ** 20):
    t = 1024
    while t > 8 and (t * row_bytes > target or rows % t):
        t //= 2
    return t if rows % t == 0 else rows


def _sigmoid(x):
    return 1.0 / (1.0 + jnp.exp(-x))


_GELU_C = 0.7978845608028654
_GELU_A = 0.044715


def _gelu(x):
    return 0.5 * x * (1.0 + jnp.tanh(_GELU_C * (x + _GELU_A * (x * x * x))))


def _gelu_grad(x):
    t = jnp.tanh(_GELU_C * (x + _GELU_A * (x * x * x)))
    return 0.5 * (1.0 + t) + 0.5 * x * (1.0 - t * t) * (_GELU_C * (1.0 + 3.0 * _GELU_A * x * x))


def _mm_tiles(M, N, K, n_a, n_b, n_acc, io_bytes):
    tks = [K] + [d for d in (4096, 3072, 2816, 2048, 1024, 512, 256, 128) if d < K and K % d == 0]
    tms = [t for t in (1024, 512, 256, 128) if M % t == 0] or [M]
    tns = [t for t in (512, 256, 128) if N % t == 0] or [N]
    best = None
    for tk in tks:
        for tm in tms:
            for tn in tns:
                est = 2 * 2 * (n_a * tm * tk + n_b * tk * tn) + 2 * tm * tn * io_bytes
                est += n_acc * tm * tn * 4 * (2 if tk < K else 1)
                if est <= VMEM_BLOCK_BUDGET:
                    return tm, tn, tk, est
                if best is None or est < best[3]:
                    best = (tm, tn, tk, est)
    return best


def _mm(name, a_list, b_list, terms, epilogue, out_dtypes, *, ta=False, tb=False, extras=(), dep=None):
    n_acc = 1 + max(t[2] for t in terms)
    a0, b0 = a_list[0], b_list[0]
    (K, M) = a0.shape if ta else a0.shape[::-1]
    N = b0.shape[0] if tb else b0.shape[1]
    io_bytes = sum(jnp.dtype(e.dtype).itemsize for e in extras) + sum(jnp.dtype(d).itemsize for d in out_dtypes)
    tm, tn, tk, est = _mm_tiles(M, N, K, len(a_list), len(b_list), n_acc, io_bytes)
    nk = K // tk
    na, nb, ne, no = len(a_list), len(b_list), len(extras), len(out_dtypes)
    deps = [] if dep is None else [dep]
    nd = len(deps)
    dn = (((0 if ta else 1,), (1 if tb else 0,)), ((), ()))

    def body(*refs):
        a_refs = refs[:na]
        b_refs = refs[na:na + nb]
        e_refs = refs[na + nb:na + nb + ne]
        o_refs = refs[na + nb + ne + nd:na + nb + ne + nd + no]
        acc_refs = refs[na + nb + ne + nd + no:]
        parts = [None] * n_acc
        for ai, bi, qi in terms:
            d = lax.dot_general(a_refs[ai][...], b_refs[bi][...], dn, preferred_element_type=F32)
            parts[qi] = d if parts[qi] is None else parts[qi] + d

        def finish(accs):
            outs = epilogue(accs, [e[...] for e in e_refs])
            for o_ref, o in zip(o_refs, outs):
                o_ref[...] = o.astype(o_ref.dtype)

        if nk == 1:
            finish(parts)
        else:
            k = pl.program_id(2)

            @pl.when(k == 0)
            def _():
                for q in range(n_acc):
                    acc_refs[q][...] = parts[q]

            @pl.when(k > 0)
            def _():
                for q in range(n_acc):
                    acc_refs[q][...] += parts[q]

            @pl.when(k == nk - 1)
            def _():
                finish([acc_refs[q][...] for q in range(n_acc)])

    a_spec = pl.BlockSpec((tk, tm), lambda i, j, k: (k, i)) if ta else pl.BlockSpec((tm, tk), lambda i, j, k: (i, k))
    b_spec = pl.BlockSpec((tn, tk), lambda i, j, k: (j, k)) if tb else pl.BlockSpec((tk, tn), lambda i, j, k: (k, j))
    e_spec = pl.BlockSpec((tm, tn), lambda i, j, k: (i, j))
    outs = pl.pallas_call(
        body, name=name, grid=(M // tm, N // tn, nk),
        in_specs=[a_spec] * na + [b_spec] * nb + [e_spec] * ne + [pl.BlockSpec((8, LANE), lambda i, j, k: (0, 0))] * nd,
        out_specs=[e_spec] * no,
        out_shape=[SDS((M, N), d) for d in out_dtypes],
        scratch_shapes=[pltpu.VMEM((tm, tn), F32) for _ in range(n_acc)] if nk > 1 else [],
        compiler_params=_cp(("parallel", "parallel", "arbitrary"), est + 12 * 2 ** 20),
    )(*a_list, *b_list, *extras, *deps)
    return outs


def _ep_plain(accs, ex):
    return [accs[0]]


def _ep_swiglu(accs, ex):
    g, u = accs
    return [g, u, g * _sigmoid(g) * u]


def _ep_swiglu_bwd(accs, ex):
    da = accs[0]
    g = ex[0].astype(F32)
    u = ex[1].astype(F32)
    s = _sigmoid(g)
    return [da * u * (s * (1.0 + g * (1.0 - s))), da * (g * s)]


def _ep_gelu(accs, ex):
    return [accs[0], _gelu(accs[0])]


def _ep_residual(scale):
    def ep(accs, ex):
        return [ex[0] + scale * accs[0]]
    return ep


def _rms_fwd(name, x, gamma, dep=None):
    S, D = x.shape
    tr = _row_tile(S, D * 4)
    deps = [] if dep is None else [dep]

    def body(x_ref, g_ref, *rest):
        o_ref = rest[-1]
        xv = x_ref[...]
        r = lax.rsqrt(jnp.mean(xv * xv, axis=-1, keepdims=True) + EPS)
        o_ref[...] = (xv * r * g_ref[...]).astype(BF16)

    return pl.pallas_call(
        body, name=name, grid=(S // tr,),
        in_specs=[pl.BlockSpec((tr, D), lambda i: (i, 0)), pl.BlockSpec((1, D), lambda i: (0, 0))]
        + [pl.BlockSpec((8, LANE), lambda i: (0, 0))] * len(deps),
        out_specs=pl.BlockSpec((tr, D), lambda i: (i, 0)),
        out_shape=SDS((S, D), BF16),
        compiler_params=_cp(("parallel",), 32 * 2 ** 20),
    )(x, gamma.reshape(1, D), *deps)


def _rms_bwd(name, x, gamma, dn, dx_in, out_scale):
    S, D = x.shape
    tr = _row_tile(S, D * 4, 2 ** 19)

    def body(x_ref, g_ref, dn_ref, dxi_ref, dxo_ref, dxb_ref, dg_ref):
        i = pl.program_id(0)
        xv = x_ref[...]
        r = lax.rsqrt(jnp.mean(xv * xv, axis=-1, keepdims=True) + EPS)
        xh = xv * r
        dnv = dn_ref[...]
        dxh = dnv * g_ref[...]
        dx = dxi_ref[...] + r * (dxh - xh * jnp.mean(dxh * xh, axis=-1, keepdims=True))
        dxo_ref[...] = dx
        dxb_ref[...] = (out_scale * dx).astype(BF16)
        part = jnp.sum(dnv * xh, axis=0, keepdims=True)

        @pl.when(i == 0)
        def _():
            dg_ref[...] = part

        @pl.when(i > 0)
        def _():
            dg_ref[...] += part

    row = pl.BlockSpec((tr, D), lambda i: (i, 0))
    vec = pl.BlockSpec((1, D), lambda i: (0, 0))
    return pl.pallas_call(
        body, name=name, grid=(S // tr,),
        in_specs=[row, vec, row, row], out_specs=[row, row, vec],
        out_shape=[SDS((S, D), F32), SDS((S, D), BF16), SDS((1, D), F32)],
        compiler_params=_cp(("arbitrary",), 40 * 2 ** 20),
    )(x, gamma.reshape(1, D), dn, dx_in)


def _loss_grad(name, y, t, out_scale):
    S, D = y.shape
    tr = _row_tile(S, D * 4, 2 ** 19)
    inv_d = 1.0 / D

    def body(y_ref, t_ref, dy_ref, dyb_ref, ls_ref):
        i = pl.program_id(0)
        e = y_ref[...] - t_ref[...]
        dy = e * inv_d
        dy_ref[...] = dy
        dyb_ref[...] = (out_scale * dy).astype(BF16)
        part = jnp.sum(e * e, axis=0, keepdims=True)

        @pl.when(i == 0)
        def _():
            ls_ref[...] = part

        @pl.when(i > 0)
        def _():
            ls_ref[...] += part

    row = pl.BlockSpec((tr, D), lambda i: (i, 0))
    vec = pl.BlockSpec((1, D), lambda i: (0, 0))
    return pl.pallas_call(
        body, name=name, grid=(S // tr,),
        in_specs=[row, row], out_specs=[row, row, vec],
        out_shape=[SDS((S, D), F32), SDS((S, D), BF16), SDS((1, D), F32)],
        compiler_params=_cp(("arbitrary",), 32 * 2 ** 20),
    )(y, t)


def _head_norm_fwd(name, raw, part, n_parts, gain_t, scale, with_pass):
    S = raw.shape[0]
    W = raw.shape[1] // n_parts
    nh = W // HEAD_DIM
    tr = _row_tile(S, W * 4, 2 ** 20)

    def body(*refs):
        if with_pass:
            x_ref, p_ref, g_ref, o_ref, po_ref = refs
            po_ref[...] = p_ref[...].astype(BF16)
        else:
            x_ref, g_ref, o_ref = refs
        for h in range(nh):
            sl = slice(h * HEAD_DIM, (h + 1) * HEAD_DIM)
            xv = x_ref[:, sl]
            r = lax.rsqrt(jnp.mean(xv * xv, axis=-1, keepdims=True) + EPS)
            o_ref[:, sl] = (xv * r * g_ref[:, sl] * scale).astype(BF16)

    xspec = pl.BlockSpec((tr, W), lambda i: (i, part))
    ospec = pl.BlockSpec((tr, W), lambda i: (i, 0))
    gspec = pl.BlockSpec((1, W), lambda i: (0, 0))
    if with_pass:
        in_specs = [xspec, pl.BlockSpec((tr, W), lambda i: (i, 1)), gspec]
        args = (raw, raw, gain_t)
        out_specs, out_shape = [ospec, ospec], [SDS((S, W), BF16), SDS((S, W), BF16)]
    else:
        in_specs, args = [xspec, gspec], (raw, gain_t)
        out_specs, out_shape = ospec, SDS((S, W), BF16)
    return pl.pallas_call(
        body, name=name, grid=(S // tr,), in_specs=in_specs, out_specs=out_specs, out_shape=out_shape,
        compiler_params=_cp(("parallel",), 40 * 2 ** 20),
    )(*args)


def _head_norm_bwd(name, raw, part, n_parts, gain_t, scale, dy_groups, pass_groups):
    S = raw.shape[0]
    W = raw.shape[1] // n_parts
    ng = len(dy_groups)
    HD = W // ng
    nhg = HD // HEAD_DIM
    tr = _row_tile(S, W * 4, 2 ** 19)
    n_dy = [len(g) for g in dy_groups]
    n_ps = [len(g) for g in pass_groups] if pass_groups is not None else []
    flat = [a for g in dy_groups for a in g] + ([a for g in pass_groups for a in g] if pass_groups is not None else [])
    out_w = 2 * W if pass_groups is not None else W

    def body(*refs):
        x_ref, g_ref = refs[0], refs[1]
        d_refs = refs[2:2 + len(flat)]
        o_ref, dg_ref = refs[2 + len(flat)], refs[3 + len(flat)]
        i = pl.program_id(0)

        @pl.when(i == 0)
        def _():
            dg_ref[...] = jnp.zeros_like(dg_ref)

        pos = 0
        for gi in range(ng):
            dys = d_refs[pos:pos + n_dy[gi]]
            pos += n_dy[gi]
            for h in range(nhg):
                sl = slice(gi * HD + h * HEAD_DIM, gi * HD + (h + 1) * HEAD_DIM)
                hs = slice(h * HEAD_DIM, (h + 1) * HEAD_DIM)
                dy = dys[0][:, hs]
                for extra in dys[1:]:
                    dy = dy + extra[:, hs]
                xv = x_ref[:, sl]
                r = lax.rsqrt(jnp.mean(xv * xv, axis=-1, keepdims=True) + EPS)
                xh = xv * r
                dxh = dy * (g_ref[:, sl] * scale)
                o_ref[:, sl] = (r * (dxh - xh * jnp.mean(dxh * xh, axis=-1, keepdims=True))).astype(BF16)
                dg_ref[:, sl] += jnp.sum(dy * xh, axis=0, keepdims=True) * scale
        for gi in range(len(n_ps)):
            ps = d_refs[pos:pos + n_ps[gi]]
            pos += n_ps[gi]
            acc = ps[0][...]
            for extra in ps[1:]:
                acc = acc + extra[...]
            o_ref[:, W + gi * HD:W + (gi + 1) * HD] = acc.astype(BF16)

    dspec = pl.BlockSpec((tr, HD), lambda i: (i, 0))
    return pl.pallas_call(
        body, name=name, grid=(S // tr,),
        in_specs=[pl.BlockSpec((tr, W), lambda i: (i, part)), pl.BlockSpec((1, W), lambda i: (0, 0))] + [dspec] * len(flat),
        out_specs=[pl.BlockSpec((tr, out_w), lambda i: (i, 0)), pl.BlockSpec((1, W), lambda i: (0, 0))],
        out_shape=[SDS((S, out_w), BF16), SDS((1, W), F32)],
        compiler_params=_cp(("arbitrary",), 48 * 2 ** 20),
    )(raw, gain_t, *flat)


def _tril_mask():
    r = lax.broadcasted_iota(jnp.int32, (GMLP_CHUNK, GMLP_CHUNK), 0)
    c = lax.broadcasted_iota(jnp.int32, (GMLP_CHUNK, GMLP_CHUNK), 1)
    return r >= c


def _gmlp_gate_fwd(name, z, v_norm, w_s, bias_full):
    S, DG2 = z.shape
    DG = DG2 // 2
    G = DG // GMLP_GROUP_WIDTH
    C = GMLP_CHUNK

    def body(u_ref, v_ref, vn_ref, ws_ref, b_ref, o_ref):
        mask = _tril_mask()
        v = v_ref[...]
        r = lax.rsqrt(jnp.mean(v * v, axis=-1, keepdims=True) + EPS)
        vn = (v * r * vn_ref[...]).astype(BF16)
        for g in range(G):
            sl = slice(g * GMLP_GROUP_WIDTH, (g + 1) * GMLP_GROUP_WIDTH)
            wm = jnp.where(mask, ws_ref[g], 0.0).astype(BF16)
            sv = jnp.dot(wm, vn[:, sl], preferred_element_type=F32) + b_ref[:, sl]
            o_ref[:, sl] = (u_ref[:, sl] * sv).astype(BF16)

    return pl.pallas_call(
        body, name=name, grid=(S // C,),
        in_specs=[pl.BlockSpec((C, DG), lambda i: (i, 0)), pl.BlockSpec((C, DG), lambda i: (i, 1)),
                  pl.BlockSpec((1, DG), lambda i: (0, 0)), pl.BlockSpec((G, C, C), lambda i: (0, 0, 0)),
                  pl.BlockSpec((C, DG), lambda i: (0, 0))],
        out_specs=pl.BlockSpec((C, DG), lambda i: (i, 0)),
        out_shape=SDS((S, DG), BF16),
        compiler_params=_cp(("parallel",), 32 * 2 ** 20),
    )(z, z, v_norm, w_s, bias_full)


def _gmlp_gate_bwd(name, z, zpre, dgated, v_norm, w_s, w_s_t, bias_full):
    S, DG2 = z.shape
    DG = DG2 // 2
    G = DG // GMLP_GROUP_WIDTH
    C = GMLP_CHUNK

    def body(z_ref, zp_ref, dg_ref, vn_ref, ws_ref, wst_ref, b_ref, dz_ref, dws_ref, db_ref, dvn_ref):
        i = pl.program_id(0)
        mask = _tril_mask()
        mask_t = jnp.logical_not(mask) | (lax.broadcasted_iota(jnp.int32, (C, C), 0) == lax.broadcasted_iota(jnp.int32, (C, C), 1))
        u = z_ref[:, :DG]
        v = z_ref[:, DG:]
        r = lax.rsqrt(jnp.mean(v * v, axis=-1, keepdims=True) + EPS)
        vh = v * r
        gain = vn_ref[...]
        vn = (vh * gain).astype(BF16)
        dgt = dg_ref[...]

        @pl.when(i == 0)
        def _():
            dws_ref[...] = jnp.zeros_like(dws_ref)
            db_ref[...] = jnp.zeros_like(db_ref)
            dvn_ref[...] = jnp.zeros_like(dvn_ref)

        dvn_parts = []
        for g in range(G):
            sl = slice(g * GMLP_GROUP_WIDTH, (g + 1) * GMLP_GROUP_WIDTH)
            wm = jnp.where(mask, ws_ref[g], 0.0).astype(BF16)
            wmt = jnp.where(mask_t, wst_ref[g], 0.0).astype(BF16)
            sv = jnp.dot(wm, vn[:, sl], preferred_element_type=F32) + b_ref[:, sl]
            dgs = dgt[:, sl]
            du = dgs * sv
            dsv = dgs * u[:, sl]
            db_ref[:, sl] += dsv
            dsv_b = dsv.astype(BF16)
            dws = lax.dot_general(dsv_b, vn[:, sl], (((1,), (1,)), ((), ())), preferred_element_type=F32)
            dws_ref[g] += jnp.where(mask, dws, 0.0)
            dvn_parts.append(jnp.dot(wmt, dsv_b, preferred_element_type=F32))
            dz_ref[:, sl] = (du * _gelu_grad(zp_ref[:, sl].astype(F32))).astype(BF16)
        dvn_full = jnp.concatenate(dvn_parts, axis=1)
        dvn_ref[...] += jnp.sum(dvn_full * vh, axis=0, keepdims=True)
        dxh = dvn_full * gain
        dv = r * (dxh - vh * jnp.mean(dxh * vh, axis=-1, keepdims=True))
        dz_ref[:, DG:] = (dv * _gelu_grad(zp_ref[:, DG:].astype(F32))).astype(BF16)

    full = pl.BlockSpec((C, DG2), lambda i: (i, 0))
    wspec = pl.BlockSpec((G, C, C), lambda i: (0, 0, 0))
    return pl.pallas_call(
        body, name=name, grid=(S // C,),
        in_specs=[full, full, pl.BlockSpec((C, DG), lambda i: (i, 0)), pl.BlockSpec((1, DG), lambda i: (0, 0)),
                  wspec, wspec, pl.BlockSpec((C, DG), lambda i: (0, 0))],
        out_specs=[full, wspec, pl.BlockSpec((C, DG), lambda i: (0, 0)), pl.BlockSpec((1, DG), lambda i: (0, 0))],
        out_shape=[SDS((S, DG2), BF16), SDS((G, C, C), F32), SDS((C, DG), F32), SDS((1, DG), F32)],
        compiler_params=_cp(("arbitrary",), 40 * 2 ** 20),
    )(z, zpre, dgated, v_norm, w_s, w_s_t, bias_full)


def _alibi_slopes(n_heads):
    return [float(v) for v in np.exp2(np.float32(-8.0) * np.arange(1, n_heads + 1, dtype=np.float32) / np.float32(n_heads))]


def _dil_view(arr, dil):
    S, C = arr.shape
    return arr if dil == 1 else arr.reshape(S // dil, dil * C)


def _dil_spec(dil, HD, ncb, cb, bmap):
    return pl.BlockSpec((ATTN_BLOCK, HD), lambda r, b: (bmap(b), r * ncb + cb))


def _dil_unview(arr, S):
    return arr.reshape(S, arr.size // S)


def _attn_mask(b):
    qi = lax.broadcasted_iota(jnp.int32, (ATTN_BLOCK, 2 * ATTN_BLOCK), 0)
    kj = lax.broadcasted_iota(jnp.int32, (ATTN_BLOCK, 2 * ATTN_BLOCK), 1)
    delta = qi + ATTN_BLOCK - kj
    valid = (delta >= 0) & (delta <= ATTN_BLOCK) & ((kj >= ATTN_BLOCK) | (b > 0))
    return valid, delta.astype(F32)


def _attn_fwd(name, qn, kn, vb, gi, dil):
    S, C = qn.shape
    NG = len(DILATIONS)
    HD = C // NG
    H = HD // HEAD_DIM
    L = S // dil
    nb = L // ATTN_BLOCK
    slopes = _alibi_slopes(H)

    def body(q_ref, kc_ref, kp_ref, vc_ref, vp_ref, o_ref, lse_ref):
        b = pl.program_id(1)
        valid, delta = _attn_mask(b)
        dist = delta * float(dil)
        for h in range(H):
            sl = slice(h * HEAD_DIM, (h + 1) * HEAD_DIM)
            k = jnp.concatenate([kp_ref[:, sl], kc_ref[:, sl]], axis=0)
            v = jnp.concatenate([vp_ref[:, sl], vc_ref[:, sl]], axis=0)
            s = lax.dot_general(q_ref[:, sl], k, (((1,), (1,)), ((), ())), preferred_element_type=F32)
            s = jnp.where(valid, s - slopes[h] * dist, -jnp.inf)
            m = jnp.max(s, axis=-1, keepdims=True)
            p = jnp.exp(s - m)
            l = jnp.sum(p, axis=-1, keepdims=True)
            o = jnp.dot(p.astype(BF16), v, preferred_element_type=F32)
            o_ref[:, sl] = o / l
            lse_ref[:, sl] = jnp.broadcast_to(m + jnp.log(l), (ATTN_BLOCK, HEAD_DIM))

    cur = lambda b: b
    prev = lambda b: jnp.maximum(b - 1, 0)
    qv, kv, vv = _dil_view(qn, dil), _dil_view(kn, dil), _dil_view(vb, dil)
    view_shape = (L, dil * HD)
    o, lse = pl.pallas_call(
        body, name=name, grid=(dil, nb),
        in_specs=[_dil_spec(dil, HD, NG, gi, cur), _dil_spec(dil, HD, NG, gi, cur), _dil_spec(dil, HD, NG, gi, prev),
                  _dil_spec(dil, HD, NG, gi, cur), _dil_spec(dil, HD, NG, gi, prev)],
        out_specs=[_dil_spec(dil, HD, 1, 0, cur), _dil_spec(dil, HD, 1, 0, cur)],
        out_shape=[SDS(view_shape, F32), SDS(view_shape, F32)],
        compiler_params=_cp(("parallel", "parallel"), 32 * 2 ** 20),
    )(qv, kv, kv, vv, vv)
    return _dil_unview(o, S), _dil_unview(lse, S)


def _attn_combine(name, o_list, lse_list, d_o=None):
    S, HD = o_list[0].shape
    H = HD // HEAD_DIM
    ng = len(o_list)
    tr = _row_tile(S, HD * 4, 2 ** 19)

    def body(*refs):
        o_refs = refs[:ng]
        l_refs = refs[ng:2 * ng]
        ls = [r[...] for r in l_refs]
        m = ls[0]
        for t in ls[1:]:
            m = jnp.maximum(m, t)
        es = [jnp.exp(t - m) for t in ls]
        z = es[0]
        for t in es[1:]:
            z = z + t
        o = es[0] * o_refs[0][...]
        for e, r in zip(es[1:], o_refs[1:]):
            o = o + e * r[...]
        o = o / z
        if d_o is None:
            refs[2 * ng][...] = o.astype(BF16)
        else:
            do_ref, lse_ref, dl_ref, dob_ref = refs[2 * ng:]
            dov = do_ref[...]
            lse_ref[...] = m + jnp.log(z)
            dob_ref[...] = dov.astype(BF16)
            prod = dov * o
            for h in range(H):
                sl = slice(h * HEAD_DIM, (h + 1) * HEAD_DIM)
                dl_ref[:, sl] = jnp.broadcast_to(jnp.sum(prod[:, sl], axis=-1, keepdims=True), (tr, HEAD_DIM))

    row = pl.BlockSpec((tr, HD), lambda i: (i, 0))
    if d_o is None:
        return pl.pallas_call(
            body, name=name, grid=(S // tr,), in_specs=[row] * (2 * ng), out_specs=row,
            out_shape=SDS((S, HD), BF16), compiler_params=_cp(("parallel",), 40 * 2 ** 20),
        )(*o_list, *lse_list)
    return pl.pallas_call(
        body, name=name, grid=(S // tr,), in_specs=[row] * (2 * ng + 1), out_specs=[row, row, row],
        out_shape=[SDS((S, HD), F32), SDS((S, HD), F32), SDS((S, HD), BF16)],
        compiler_params=_cp(("parallel",), 48 * 2 ** 20),
    )(*o_list, *lse_list, d_o)


def _attn_bwd(name, qn, kn, vb, dob, lse, delta_rows, gi, dil):
    S, C = qn.shape
    NG = len(DILATIONS)
    HD = C // NG
    H = HD // HEAD_DIM
    L = S // dil
    nb = L // ATTN_BLOCK
    slopes = _alibi_slopes(H)
    B = ATTN_BLOCK

    def body(q_ref, kc_ref, kp_ref, vc_ref, vp_ref, do_ref, lse_ref, dl_ref, dq_ref, dk_ref, dv_ref, ck_ref, cv_ref):
        b = pl.program_id(1)

        @pl.when(b == 0)
        def _():
            ck_ref[...] = jnp.zeros_like(ck_ref)
            cv_ref[...] = jnp.zeros_like(cv_ref)

        @pl.when(b < nb)
        def _():
            valid, delta = _attn_mask(b)
            dist = delta * float(dil)
            for h in range(H):
                sl = slice(h * HEAD_DIM, (h + 1) * HEAD_DIM)
                q = q_ref[:, sl]
                k = jnp.concatenate([kp_ref[:, sl], kc_ref[:, sl]], axis=0)
                v = jnp.concatenate([vp_ref[:, sl], vc_ref[:, sl]], axis=0)
                do = do_ref[:, sl]
                s = lax.dot_general(q, k, (((1,), (1,)), ((), ())), preferred_element_type=F32)
                s = jnp.where(valid, s - slopes[h] * dist, -jnp.inf)
                lse2 = jnp.concatenate([lse_ref[:, sl], lse_ref[:, sl]], axis=1)
                dl2 = jnp.concatenate([dl_ref[:, sl], dl_ref[:, sl]], axis=1)
                p = jnp.exp(s - lse2)
                dp = lax.dot_general(do, v, (((1,), (1,)), ((), ())), preferred_element_type=F32)
                ds = (p * (dp - dl2)).astype(BF16)
                dq_ref[:, sl] = jnp.dot(ds, k, preferred_element_type=F32)
                dk2 = lax.dot_general(ds, q, (((0,), (0,)), ((), ())), preferred_element_type=F32)
                dv2 = lax.dot_general(p.astype(BF16), do, (((0,), (0,)), ((), ())), preferred_element_type=F32)
                dk_ref[:, sl] = ck_ref[:, sl] + dk2[:B]
                dv_ref[:, sl] = cv_ref[:, sl] + dv2[:B]
                ck_ref[:, sl] = dk2[B:]
                cv_ref[:, sl] = dv2[B:]

        @pl.when(b == nb)
        def _():
            dk_ref[...] = ck_ref[...]
            dv_ref[...] = cv_ref[...]

    cur = lambda b: jnp.minimum(b, nb - 1)
    prev = lambda b: jnp.maximum(jnp.minimum(b, nb - 1) - 1, 0)
    late = lambda b: jnp.maximum(b - 1, 0)
    qv, kv, vv = _dil_view(qn, dil), _dil_view(kn, dil), _dil_view(vb, dil)
    dov, lsev, dlv = _dil_view(dob, dil), _dil_view(lse, dil), _dil_view(delta_rows, dil)
    view_shape = (L, dil * HD)
    one = lambda m: _dil_spec(dil, HD, 1, 0, m)
    grp = lambda m: _dil_spec(dil, HD, NG, gi, m)
    dq, dk, dv = pl.pallas_call(
        body, name=name, grid=(dil, nb + 1),
        in_specs=[grp(cur), grp(cur), grp(prev), grp(cur), grp(prev), one(cur), one(cur), one(cur)],
        out_specs=[one(cur), one(late), one(late)],
        out_shape=[SDS(view_shape, F32)] * 3,
        scratch_shapes=[pltpu.VMEM((B, HD), F32), pltpu.VMEM((B, HD), F32)],
        compiler_params=_cp(("arbitrary", "arbitrary"), 40 * 2 ** 20),
    )(qv, kv, kv, vv, vv, dov, lsev, dlv)
    return _dil_unview(dq, S), _dil_unview(dk, S), _dil_unview(dv, S)


def _cast_into_gathered(name, w, l, sa, chip_arr):
    L, r, c = w.shape
    tr = _row_tile(r, c * 4)
    nrb = r // tr
    src = pl.BlockSpec((None, tr, c), lambda i, chip: (l, i, 0))
    if sa == 0:
        dst, shape = pl.BlockSpec((tr, c), lambda i, chip: (chip[0] * nrb + i, 0)), (N_CHIPS * r, c)
    else:
        dst, shape = pl.BlockSpec((tr, c), lambda i, chip: (i, chip[0])), (r, N_CHIPS * c)

    def body(chip_ref, i_ref, o_ref):
        o_ref[...] = i_ref[...].astype(BF16)

    return pl.pallas_call(
        body, name=name,
        grid_spec=pltpu.PrefetchScalarGridSpec(num_scalar_prefetch=1, grid=(nrb,), in_specs=[src], out_specs=dst),
        out_shape=SDS(shape, BF16), compiler_params=_cp(("parallel",), 32 * 2 ** 20),
    )(chip_arr, w)


def _add_half(name, dw, land, sa, c_arr):
    hr, hc = land.shape
    tr = _row_tile(hr, hc * 2)
    nrb = hr // tr
    if sa == 1:
        mine = pl.BlockSpec((tr, hc), lambda i, c: (c[0] * nrb + i, 0))
    else:
        mine = pl.BlockSpec((tr, hc), lambda i, c: (i, c[0]))
    other = pl.BlockSpec((tr, hc), lambda i, c: (i, 0))

    def body(c_ref, a_ref, b_ref, o_ref):
        o_ref[...] = (a_ref[...].astype(F32) + b_ref[...].astype(F32)).astype(BF16)

    return pl.pallas_call(
        body, name=name,
        grid_spec=pltpu.PrefetchScalarGridSpec(num_scalar_prefetch=1, grid=(nrb,), in_specs=[mine, other], out_specs=other),
        out_shape=SDS((hr, hc), BF16), compiler_params=_cp(("parallel",), 32 * 2 ** 20),
    )(c_arr, dw, land)


def _sum_slots(name, slots, out_dtype=F32):
    n, R, C = slots.shape
    tr = _row_tile(R, C * n * jnp.dtype(slots.dtype).itemsize, 2 ** 21)

    def body(s_ref, o_ref):
        acc = s_ref[0].astype(F32)
        for k in range(1, n):
            acc = acc + s_ref[k].astype(F32)
        o_ref[...] = acc.astype(out_dtype)

    return pl.pallas_call(
        body, name=name, grid=(R // tr,),
        in_specs=[pl.BlockSpec((n, tr, C), lambda i: (0, i, 0))], out_specs=pl.BlockSpec((tr, C), lambda i: (i, 0)),
        out_shape=SDS((R, C), out_dtype), compiler_params=_cp(("parallel",), 32 * 2 ** 20),
    )(slots)


def _sum_into(name, part, slots, buf, l, sa, where):
    n, pr, pc = slots.shape
    tr = _row_tile(pr, pc * (n + 1) * jnp.dtype(slots.dtype).itemsize, 2 ** 21)
    nrb = pr // tr
    if sa == 1:
        dst = pl.BlockSpec((None, tr, pc), lambda i, c, j: (l, c[0] * nrb + i, 0))
        own = pl.BlockSpec((tr, pc), lambda i, c, j: (i, j[0]))
    else:
        dst = pl.BlockSpec((None, tr, pc), lambda i, c, j: (l, i, c[0]))
        own = pl.BlockSpec((tr, pc), lambda i, c, j: (j[0] * nrb + i, 0))

    def body(c_ref, j_ref, p_ref, s_ref, b_ref, o_ref):
        acc = p_ref[...].astype(F32)
        for k in range(n):
            acc = acc + s_ref[k].astype(F32)
        o_ref[...] = acc

    return pl.pallas_call(
        body, name=name,
        grid_spec=pltpu.PrefetchScalarGridSpec(
            num_scalar_prefetch=2, grid=(nrb,),
            in_specs=[own, pl.BlockSpec((n, tr, pc), lambda i, c, j: (0, i, 0)), ANY], out_specs=dst),
        out_shape=SDS(buf.shape, buf.dtype), input_output_aliases={4: 0},
        compiler_params=_cp(("parallel",), 32 * 2 ** 20),
    )(where[0], where[1], part, slots, buf)


def _adamw(name, g, w, m, v):
    R, C = g.shape
    tr = _row_tile(R, C * 4, 2 ** 19)
    bc1 = 1.0 - ADAM_B1 ** ADAM_STEP
    bc2 = 1.0 - ADAM_B2 ** ADAM_STEP

    def body(g_ref, w_ref, m_ref, v_ref, go_ref, d_ref, mo_ref, vo_ref):
        gv = g_ref[...]
        mn = ADAM_B1 * m_ref[...] + (1.0 - ADAM_B1) * gv
        vn = ADAM_B2 * v_ref[...] + (1.0 - ADAM_B2) * (gv * gv)
        go_ref[...] = gv
        mo_ref[...] = mn
        vo_ref[...] = vn
        d_ref[...] = -ADAM_LR * ((mn / bc1) / (jnp.sqrt(vn / bc2) + ADAM_EPS) + ADAM_WD * w_ref[...])

    spec = pl.BlockSpec((tr, C), lambda i: (i, 0))
    return pl.pallas_call(
        body, name=name, grid=(R // tr,), in_specs=[spec] * 4, out_specs=[spec] * 4,
        out_shape=[SDS((R, C), F32)] * 4, compiler_params=_cp(("parallel",), 32 * 2 ** 20),
    )(g, w, m, v)


ANY = pl.BlockSpec(memory_space=pl.ANY)


def _coords():
    return lax.axis_index("x"), lax.axis_index("y"), lax.axis_index("c")


def _other_chips(x, y):
    return [((1 - x, y), 2 * (1 - x) + y), ((x, 1 - y), 2 * x + (1 - y)), ((1 - x, 1 - y), 2 * (1 - x) + (1 - y))]


def _win(ref, axis, start, size):
    if not isinstance(start, int):
        start = pl.multiple_of(start, LANE if axis == 1 else BF16_ROWS)
    if axis == 0:
        return ref.at[pl.ds(start, size), :]
    return ref.at[:, pl.ds(start, size)]


def _rcopy(src, dst, ssem, rsem, dev):
    return pltpu.make_async_remote_copy(src_ref=src, dst_ref=dst, send_sem=ssem, recv_sem=rsem,
                                        device_id=dev, device_id_type=MESH)


HBM = pl.BlockSpec(memory_space=pltpu.HBM)
SEM = pl.BlockSpec(memory_space=pltpu.SEMAPHORE)
TOKEN = pl.BlockSpec(memory_space=pltpu.VMEM)
EFFECT = pltpu.SideEffectType.DATAFLOW_SIDE_EFFECTING


def _in_hbm(a):
    return pltpu.with_memory_space_constraint(a, pltpu.HBM)


def _ag_geometry(mats, sas, o):
    sa = sas[o]
    return sa, 1 - sa, mats[o].shape[sa] // N_CHIPS, mats[o].shape[1 - sa] // 2


def _ag_ici_copy(mats, sas, refs, o, k, sems, x, y, c):
    sa, ha, wl, hl = _ag_geometry(mats, sas, o)
    chip, jk = _other_chips(x, y)[k]
    mine = _win(_win(refs[o], sa, (2 * x + y) * wl, wl), ha, c * hl, hl)
    landed = _win(_win(refs[o], sa, jk * wl, wl), ha, c * hl, hl)
    return mine, landed, (*chip, c)


def _ag_start(name, mats, sas, after):
    n = len(mats)

    def body(*refs):
        ins = refs[:n]
        s_sem, r_sem = refs[n + 1], refs[n + 2]
        token = refs[2 * n + 3]
        x, y, c = _coords()
        for o in range(n):
            for k in range(3):
                mine, _, dev = _ag_ici_copy(mats, sas, ins, o, k, None, x, y, c)
                _rcopy(mine, mine, s_sem.at[3 * o + k], r_sem.at[3 * o + k], dev).start()
        token[...] = jnp.zeros_like(token)

    out = pl.pallas_call(
        body, name=name,
        out_shape=(pltpu.SemaphoreType.DMA((3 * n,)), pltpu.SemaphoreType.DMA((3 * n,)),
                   *[pltpu.HBM(m.shape, m.dtype) for m in mats], SDS((8, LANE), F32)),
        in_specs=[HBM] * n + [ANY], out_specs=(SEM, SEM, *[HBM] * n, TOKEN),
        input_output_aliases={k: 2 + k for k in range(n)},
        compiler_params=pltpu.CompilerParams(has_side_effects=EFFECT),
    )(*[_in_hbm(m) for m in mats], after)
    return out[0], out[1], list(out[2:2 + n]), out[2 + n]


def _ag_wait(name, mats, sas, s_sem, r_sem, after):
    n = len(mats)

    def body(*refs):
        ins = refs[:n]
        s_ref, r_ref = refs[n], refs[n + 1]
        x, y, c = _coords()
        for o in range(n):
            for k in range(3):
                mine, landed, dev = _ag_ici_copy(mats, sas, ins, o, k, None, x, y, c)
                cp = _rcopy(mine, landed, s_ref.at[3 * o + k], r_ref.at[3 * o + k], dev)
                cp.wait_send()
                cp.wait_recv()

    return list(pl.pallas_call(
        body, name=name, out_shape=[pltpu.HBM(m.shape, m.dtype) for m in mats],
        in_specs=[HBM] * n + [SEM, SEM, ANY], out_specs=[HBM] * n,
        input_output_aliases={k: k for k in range(n)},
        compiler_params=pltpu.CompilerParams(has_side_effects=EFFECT),
    )(*mats, s_sem, r_sem, after))


def _ag_forward(name, mats, sas):
    n = len(mats)

    def body(*refs):
        outs = refs[n:2 * n]
        s_fwd, r_fwd = refs[2 * n:]
        x, y, c = _coords()
        sibling = (x, y, 1 - c)
        sends = []
        for o in range(n):
            sa, ha, wl, hl = _ag_geometry(mats, sas, o)
            for k, (chip, jk) in enumerate(_other_chips(x, y)):
                landed = _win(_win(outs[o], sa, jk * wl, wl), ha, c * hl, hl)
                fwd = _rcopy(landed, landed, s_fwd.at[3 * o + k], r_fwd.at[3 * o + k], sibling)
                fwd.start()
                sends.append(fwd)
        for o in range(n):
            sa, ha, wl, hl = _ag_geometry(mats, sas, o)
            for k, (chip, jk) in enumerate(_other_chips(x, y)):
                got = _win(_win(outs[o], sa, jk * wl, wl), ha, (1 - c) * hl, hl)
                _rcopy(got, got, s_fwd.at[3 * o + k], r_fwd.at[3 * o + k], sibling).wait_recv()
        for cp in sends:
            cp.wait_send()

    return list(pl.pallas_call(
        body, name=name, in_specs=[ANY] * n, out_specs=[ANY] * n,
        out_shape=[SDS(m.shape, m.dtype) for m in mats], input_output_aliases={k: k for k in range(n)},
        scratch_shapes=[pltpu.SemaphoreType.DMA((3 * n,)), pltpu.SemaphoreType.DMA((3 * n,))],
        compiler_params=pltpu.CompilerParams(has_side_effects=True),
    )(*mats))


def _all_gather_vec(name, v):
    Lv, cv = v.shape

    def body(v_ref, o_ref, loc_sem, s_sem, r_sem):
        x, y, c = _coords()
        jme = 2 * x + y
        chips = _other_chips(x, y)
        mine = _win(o_ref, 1, jme * cv, cv)
        loc = pltpu.make_async_copy(v_ref, mine, loc_sem)
        loc.start()
        sends = []
        for k, (chip, _) in enumerate(chips):
            cp = _rcopy(v_ref, mine, s_sem.at[k], r_sem.at[k], (*chip, c))
            cp.start()
            sends.append(cp)
        for k, (chip, jk) in enumerate(chips):
            got = _win(o_ref, 1, jk * cv, cv)
            _rcopy(got, got, s_sem.at[k], r_sem.at[k], (*chip, c)).wait_recv()
        for cp in sends:
            cp.wait_send()
        loc.wait()

    return pl.pallas_call(
        body, name=name, in_specs=[ANY], out_specs=ANY, out_shape=SDS((Lv, 4 * cv), v.dtype),
        scratch_shapes=[pltpu.SemaphoreType.DMA, pltpu.SemaphoreType.DMA((3,)), pltpu.SemaphoreType.DMA((3,))],
        compiler_params=pltpu.CompilerParams(has_side_effects=True),
    )(v)


def _half_shape(shape, sa):
    R, C = shape
    return (R // 2, C) if sa == 1 else (R, C // 2)


def _rs_sibling(name, dws, sas):
    n = len(dws)

    def body(*refs):
        src, land = refs[:n], refs[n:2 * n]
        s_sem, r_sem = refs[2 * n:]
        x, y, c = _coords()
        sibling = (x, y, 1 - c)
        sends = []
        for o in range(n):
            ha = 1 - sas[o]
            hl = dws[o].shape[ha] // 2
            cp = _rcopy(_win(src[o], ha, (1 - c) * hl, hl), land[o], s_sem.at[o], r_sem.at[o], sibling)
            cp.start()
            sends.append(cp)
        for o in range(n):
            _rcopy(land[o], land[o], s_sem.at[o], r_sem.at[o], sibling).wait_recv()
        for cp in sends:
            cp.wait_send()

    return pl.pallas_call(
        body, name=name, in_specs=[ANY] * n, out_specs=[ANY] * n,
        out_shape=[SDS(_half_shape(d.shape, sa), d.dtype) for d, sa in zip(dws, sas)],
        scratch_shapes=[pltpu.SemaphoreType.DMA((n,)), pltpu.SemaphoreType.DMA((n,))],
        compiler_params=pltpu.CompilerParams(has_side_effects=True),
    )(*dws)


def _rs_piece_shape(p, sa):
    hr, hc = p.shape
    return (hr // N_CHIPS, hc) if sa == 0 else (hr, hc // N_CHIPS)


def _rs_ici_copy(parts, sas, p_refs, slot_refs, o, k, x, y, c):
    sa = sas[o]
    pl_ = parts[o].shape[sa] // N_CHIPS
    chip, jk = _other_chips(x, y)[k]
    return _win(p_refs[o], sa, jk * pl_, pl_), slot_refs[o].at[k], (*chip, c)


def _rs_start(name, parts, sas, after):
    n = len(parts)

    def body(*refs):
        ins = refs[:n]
        s_sem, r_sem = refs[n + 1], refs[n + 2]
        slots = refs[2 * n + 3:3 * n + 3]
        token = refs[3 * n + 3]
        x, y, c = _coords()
        for o in range(n):
            for k in range(3):
                src, dst, dev = _rs_ici_copy(parts, sas, ins, slots, o, k, x, y, c)
                _rcopy(src, dst, s_sem.at[3 * o + k], r_sem.at[3 * o + k], dev).start()
        token[...] = jnp.zeros_like(token)

    out = pl.pallas_call(
        body, name=name,
        out_shape=(pltpu.SemaphoreType.DMA((3 * n,)), pltpu.SemaphoreType.DMA((3 * n,)),
                   *[pltpu.HBM(p.shape, p.dtype) for p in parts],
                   *[pltpu.HBM((3,) + _rs_piece_shape(p, sa), p.dtype) for p, sa in zip(parts, sas)],
                   SDS((8, LANE), F32)),
        in_specs=[HBM] * n + [ANY], out_specs=(SEM, SEM, *[HBM] * (2 * n), TOKEN),
        input_output_aliases={k: 2 + k for k in range(n)},
        compiler_params=pltpu.CompilerParams(has_side_effects=EFFECT),
    )(*[_in_hbm(p) for p in parts], after)
    return out[0], out[1], list(out[2:2 + n]), list(out[2 + n:2 + 2 * n]), out[2 + 2 * n]


def _rs_wait(name, parts, slots, sas, s_sem, r_sem, after):
    n = len(parts)

    def body(*refs):
        p_refs, slot_refs = refs[:n], refs[n:2 * n]
        s_ref, r_ref = refs[2 * n], refs[2 * n + 1]
        x, y, c = _coords()
        for o in range(n):
            for k in range(3):
                src, dst, dev = _rs_ici_copy(parts, sas, p_refs, slot_refs, o, k, x, y, c)
                cp = _rcopy(src, dst, s_ref.at[3 * o + k], r_ref.at[3 * o + k], dev)
                cp.wait_send()
                cp.wait_recv()

    out = pl.pallas_call(
        body, name=name,
        out_shape=[pltpu.HBM(a.shape, a.dtype) for a in (*parts, *slots)],
        in_specs=[HBM] * (2 * n) + [SEM, SEM, ANY], out_specs=[HBM] * (2 * n),
        input_output_aliases={k: k for k in range(2 * n)},
        compiler_params=pltpu.CompilerParams(has_side_effects=EFFECT),
    )(*parts, *slots, s_sem, r_sem, after)
    return list(out[:n]), list(out[n:])


def _rs_share(name, sas, layers, buf_idx, bufs):
    n, nbuf = len(sas), len(bufs)

    def body(*refs):
        out = refs[nbuf:2 * nbuf]
        s_sem, r_sem = refs[2 * nbuf:]
        x, y, c = _coords()
        sibling = (x, y, 1 - c)
        sends = []
        for o in range(n):
            ha = 1 - sas[o]
            hl = bufs[buf_idx[o]].shape[1 + ha] // 2
            mine = _win(out[buf_idx[o]].at[layers[o]], ha, c * hl, hl)
            cp = _rcopy(mine, mine, s_sem.at[o], r_sem.at[o], sibling)
            cp.start()
            sends.append(cp)
        for o in range(n):
            ha = 1 - sas[o]
            hl = bufs[buf_idx[o]].shape[1 + ha] // 2
            got = _win(out[buf_idx[o]].at[layers[o]], ha, (1 - c) * hl, hl)
            _rcopy(got, got, s_sem.at[o], r_sem.at[o], sibling).wait_recv()
        for cp in sends:
            cp.wait_send()

    return pl.pallas_call(
        body, name=name, in_specs=[ANY] * nbuf, out_specs=[ANY] * nbuf,
        out_shape=[SDS(b.shape, b.dtype) for b in bufs],
        input_output_aliases={k: k for k in range(nbuf)},
        scratch_shapes=[pltpu.SemaphoreType.DMA((n,)), pltpu.SemaphoreType.DMA((n,))],
        compiler_params=pltpu.CompilerParams(has_side_effects=True),
    )(*bufs)


def _exchange_all(name, packed):
    R, C = packed.shape

    def body(p_ref, slots, loc_sem, s_sem, r_sem):
        x, y, c = _coords()
        me = 4 * x + 2 * y + c
        loc = pltpu.make_async_copy(p_ref, slots.at[me], loc_sem)
        loc.start()
        peers = []
        for k in range(1, N_DEV):
            px = 1 - x if k & 4 else x
            py = 1 - y if k & 2 else y
            pc = 1 - c if k & 1 else c
            peers.append(((px, py, pc), 4 * px + 2 * py + pc))
        sends = []
        for k, (peer, _) in enumerate(peers):
            cp = _rcopy(p_ref, slots.at[me], s_sem.at[k], r_sem.at[k], peer)
            cp.start()
            sends.append(cp)
        for k, (peer, pid) in enumerate(peers):
            _rcopy(slots.at[pid], slots.at[pid], s_sem.at[k], r_sem.at[k], peer).wait_recv()
        for cp in sends:
            cp.wait_send()
        loc.wait()

    return pl.pallas_call(
        body, name=name, in_specs=[ANY], out_specs=ANY, out_shape=SDS((N_DEV, R, C), packed.dtype),
        scratch_shapes=[pltpu.SemaphoreType.DMA, pltpu.SemaphoreType.DMA((N_DEV - 1,)), pltpu.SemaphoreType.DMA((N_DEV - 1,))],
        compiler_params=pltpu.CompilerParams(has_side_effects=True),
    )(packed)


def _pack(arrays):
    rows = []
    for a in arrays:
        flat = a.reshape(-1).astype(F32)
        pad = (-flat.size) % PACK_TILE
        rows.append(jnp.pad(flat, (0, pad)).reshape(-1, LANE))
    return jnp.concatenate(rows, axis=0)


def _unpack(packed, shapes):
    out, row = [], 0
    for s in shapes:
        size = int(np.prod(s)) if len(s) else 1
        nrows = -(-size // PACK_TILE) * (PACK_TILE // LANE)
        out.append(packed[row:row + nrows].reshape(-1)[:size].reshape(s))
        row += nrows
    return out


BIG_WEIGHTS = {
    "ffn1_w_gate": 1, "ffn1_w_up": 1, "ffn1_w_down": 0, "ffn2_w_gate": 1, "ffn2_w_up": 1, "ffn2_w_down": 0,
    "gmlp_w_in": 1, "gmlp_w_out": 0, "w_kv": 1, "attn_w_q": 1, "attn_w_o": 0,
}
SMALL_WEIGHTS = ("ffn1_norm", "mix_norm", "ffn2_norm", "gmlp_w_s", "gmlp_b_s", "kv_norm", "k_norm", "attn_q_norm")
WEIGHT_ORDER = ("ffn1_norm", "ffn1_w_gate", "ffn1_w_up", "ffn1_w_down", "mix_norm", "ffn2_norm", "ffn2_w_gate",
                "ffn2_w_up", "ffn2_w_down", "gmlp_w_in", "gmlp_v_norm", "gmlp_w_s", "gmlp_b_s", "gmlp_w_out",
                "kv_norm", "w_kv", "k_norm", "attn_w_q", "attn_q_norm", "attn_w_o")


def _ep_all(accs, ex):
    return list(accs)


def _as3d(w):
    return w if w.ndim == 3 else w.reshape((1,) + w.shape)


def kernel(x, ffn1_norm, ffn1_w_gate, ffn1_w_up, ffn1_w_down, mix_norm, ffn2_norm, ffn2_w_gate, ffn2_w_up, ffn2_w_down, gmlp_w_in, gmlp_v_norm, gmlp_w_s, gmlp_b_s, gmlp_w_out, kv_norm, w_kv, k_norm, attn_w_q, attn_q_norm, attn_w_o, loss_target, m_ffn1_norm, m_ffn1_w_gate, m_ffn1_w_up, m_ffn1_w_down, m_mix_norm, m_ffn2_norm, m_ffn2_w_gate, m_ffn2_w_up, m_ffn2_w_down, m_gmlp_w_in, m_gmlp_v_norm, m_gmlp_w_s, m_gmlp_b_s, m_gmlp_w_out, m_kv_norm, m_w_kv, m_k_norm, m_attn_w_q, m_attn_q_norm, m_attn_w_o, v_ffn1_norm, v_ffn1_w_gate, v_ffn1_w_up, v_ffn1_w_down, v_mix_norm, v_ffn2_norm, v_ffn2_w_gate, v_ffn2_w_up, v_ffn2_w_down, v_gmlp_w_in, v_gmlp_v_norm, v_gmlp_w_s, v_gmlp_b_s, v_gmlp_w_out, v_kv_norm, v_w_kv, v_k_norm, v_attn_w_q, v_attn_q_norm, v_attn_w_o):
    P = dict(locals())
    assert x.shape[0] == 1, "one sample per device"
    S, D = x.shape[1], x.shape[2]
    NL = ffn1_norm.shape[0]
    NG = len(DILATIONS)
    HD = attn_w_o.shape[1] * N_CHIPS
    H = HD // HEAD_DIM
    DG = gmlp_w_out.shape[1] * N_CHIPS
    G = DG // GMLP_GROUP_WIDTH
    assert all((S // d) % ATTN_BLOCK == 0 for d in DILATIONS) and S % GMLP_CHUNK == 0
    xs = x.reshape(S, D)
    tgt = loss_target.reshape(S, D)
    c_arr = lax.axis_index("c").astype(jnp.int32).reshape(1)
    chip = 2 * lax.axis_index("x") + lax.axis_index("y")
    chip_arr = chip.astype(jnp.int32).reshape(1)
    kv_layer = N_A_LAYERS - 1


    def layer_weights(l):
        names = [("ffn1_w_gate", l), ("ffn1_w_up", l), ("ffn1_w_down", l), ("ffn2_w_gate", l), ("ffn2_w_up", l), ("ffn2_w_down", l)]
        if l < N_A_LAYERS:
            names += [("gmlp_w_in", l), ("gmlp_w_out", l)]
        else:
            names += [("attn_w_q", l - N_A_LAYERS), ("attn_w_o", l - N_A_LAYERS)]
        if l == kv_layer:
            names += [("w_kv", 0)]
        return names

    W = {}
    ag_open = {}
    vnorm_full = _all_gather_vec("ag_vnorm", gmlp_v_norm)

    def ag_begin(l, after):
        names = layer_weights(l)
        sas = [BIG_WEIGHTS[n] for n, _ in names]
        mats = [_cast_into_gathered("cast_shard", _as3d(P[n]), li, sa, chip_arr) for (n, li), sa in zip(names, sas)]
        s_sem, r_sem, mats, token = _ag_start(f"ag_start_l{l}", mats, sas, after)
        ag_open[l] = (names, sas, s_sem, r_sem, mats)
        return token

    def ag_finish(l, after):
        names, sas, s_sem, r_sem, mats = ag_open.pop(l)
        mats = _ag_wait(f"ag_wait_l{l}", mats, sas, s_sem, r_sem, after)
        W.update(dict(zip(names, _ag_forward(f"ag_forward_l{l}", mats, sas))))

    ag_token = ag_begin(0, vnorm_full)
    if NL > 1:
        ag_token = ag_begin(1, ag_token)
    ag_finish(0, ag_token)

    kgain = jnp.tile(k_norm[:, None, :], (1, H, 1)).reshape(1, NG * HD)
    qgain = [jnp.tile(attn_q_norm[j][:, None, :], (1, H, 1)).reshape(1, NG * HD) for j in range(NL - N_A_LAYERS)]
    q_scale = HEAD_DIM ** -0.5
    one = [(0, 0, 0)]

    def ffn_fwd(xc, gamma, wg, wu, wd, dep=None):
        n = _rms_fwd("ffn_norm", xc, gamma, dep)
        g, u, act = _mm("ffn_up", [n], [wg, wu], [(0, 0, 0), (0, 1, 1)], _ep_swiglu, [BF16] * 3)
        (x2,) = _mm("ffn_down", [act], [wd], one, _ep_residual(0.5), [F32], extras=[xc])
        return x2, (xc, n, g, u, act)

    saved = {}
    xc = xs
    for l in range(NL):
        if l > 0:
            ag_finish(l, xc)
        if l + 2 < NL:
            ag_token = ag_begin(l + 2, ag_token)
        xc, saved["f1", l] = ffn_fwd(xc, ffn1_norm[l], W["ffn1_w_gate", l], W["ffn1_w_up", l], W["ffn1_w_down", l], ag_token)
        h = _rms_fwd("mix_norm", xc, mix_norm[l])
        if l < N_A_LAYERS:
            zpre, z = _mm("gmlp_in", [h], [W["gmlp_w_in", l]], one, _ep_gelu, [BF16, F32])
            bias_full = jnp.repeat(gmlp_b_s[l].T, GMLP_GROUP_WIDTH, axis=1)
            gated = _gmlp_gate_fwd("gmlp_gate", z, vnorm_full[l:l + 1], gmlp_w_s[l], bias_full)
            (x2,) = _mm("gmlp_out", [gated], [W["gmlp_w_out", l]], one, _ep_residual(1.0), [F32], extras=[xc])
            saved["mix", l] = (xc, h, zpre, z, gated, bias_full)
        else:
            j = l - N_A_LAYERS
            (q_raw,) = _mm("attn_q", [h], [W["attn_w_q", j]], one, _ep_plain, [F32])
            qn = _head_norm_fwd("q_norm", q_raw, 0, 1, qgain[j], q_scale, False)
            os_, lses = [], []
            for gi, dil in enumerate(DILATIONS):
                o, lse = _attn_fwd(f"attn_fwd_d{dil}", qn, kn, vb, gi, dil)
                os_.append(o)
                lses.append(lse)
            ob = _attn_combine("attn_mix", os_, lses)
            (x2,) = _mm("attn_o", [ob], [W["attn_w_o", j]], one, _ep_residual(1.0), [F32], extras=[xc])
            saved["mix", l] = (xc, h, q_raw, qn, os_, lses, ob)
        xc = x2
        xc, saved["f2", l] = ffn_fwd(xc, ffn2_norm[l], W["ffn2_w_gate", l], W["ffn2_w_up", l], W["ffn2_w_down", l])
        if l == kv_layer:
            kvn = _rms_fwd("kv_norm", xc, kv_norm)
            (kv_raw,) = _mm("kv_proj", [kvn], [W["w_kv", 0]], one, _ep_plain, [F32])
            kn, vb = _head_norm_fwd("k_norm", kv_raw, 0, 2, kgain, 1.0, True)
            saved["kv"] = (xc, kvn, kv_raw)

    dx, dxb, loss_rows = _loss_grad("loss", xc, tgt, 0.5)
    dW = {}
    dsmall = {n: [None] * P[n].shape[0] for n in ("ffn1_norm", "mix_norm", "ffn2_norm")}
    dsmall.update(gmlp_w_s=[None] * N_A_LAYERS, gmlp_b_s=[None] * N_A_LAYERS, gmlp_v_norm=[None] * N_A_LAYERS,
                  attn_q_norm=[None] * (NL - N_A_LAYERS))
    dks = [[] for _ in DILATIONS]
    dvs = [[] for _ in DILATIONS]

    names_big = list(BIG_WEIGHTS)
    gbuf = [lax.empty(_as3d(P[n]).shape, F32) for n in names_big]
    where = (c_arr, chip_arr)
    rs_open = {}

    def rs_begin(l, after):
        names = layer_weights(l)
        sas = [BIG_WEIGHTS[n] for n, _ in names]
        dws = [dW[k] for k in names]
        lands = _rs_sibling(f"rs_sibling_l{l}", dws, sas)
        parts = [_add_half("rs_add_half", d, ln, sa, c_arr) for d, ln, sa in zip(dws, lands, sas)]
        s_sem, r_sem, parts, slots, token = _rs_start(f"rs_start_l{l}", parts, sas, after)
        rs_open[l] = (names, sas, s_sem, r_sem, parts, slots)
        return token

    def rs_finish(l, after):
        names, sas, s_sem, r_sem, parts, slots = rs_open.pop(l)
        parts, slots = _rs_wait(f"rs_wait_l{l}", parts, slots, sas, s_sem, r_sem, after)
        for p, s, sa, (n, li) in zip(parts, slots, sas, names):
            bi = names_big.index(n)
            gbuf[bi] = _sum_into("rs_sum_chips", p, s, gbuf[bi], li, sa, where)
        gbuf[:] = _rs_share(f"rs_share_l{l}", sas, [li for _, li in names], [names_big.index(n) for n, _ in names], gbuf)

    def ffn_bwd(dx, dxb, sv, gamma, wg, wu, wd, key, l, next_scale, dep=None):
        xin, n, g, u, act = sv
        dg, du = _mm("ffn_dact", [dxb], [wd], one, _ep_swiglu_bwd, [BF16, BF16], tb=True, extras=[g, u], dep=dep)
        (dW[key + "_w_down", l],) = _mm("ffn_dwd", [act], [dxb], one, _ep_plain, [BF16], ta=True)
        dW[key + "_w_gate", l], dW[key + "_w_up", l] = _mm("ffn_dwgu", [n], [dg, du], [(0, 0, 0), (0, 1, 1)], _ep_all, [BF16, BF16], ta=True)
        (dn,) = _mm("ffn_dn", [dg, du], [wg, wu], [(0, 0, 0), (1, 1, 0)], _ep_plain, [F32], tb=True)
        dx, dxb, dsmall[key + "_norm"][l] = _rms_bwd("ffn_norm_bwd", xin, gamma, dn, dx, next_scale)
        return dx, dxb

    rs_token = None
    for l in reversed(range(NL)):
        if l == kv_layer:
            x_kv, kvn, kv_raw = saved["kv"]
            dkv_raw, dkgain = _head_norm_bwd("k_norm_bwd", kv_raw, 0, 2, kgain, 1.0, dks, dvs)
            (dW["w_kv", 0],) = _mm("kv_dw", [kvn], [dkv_raw], one, _ep_plain, [BF16], ta=True, dep=rs_token)
            (dkvn,) = _mm("kv_dn", [dkv_raw], [W["w_kv", 0]], one, _ep_plain, [F32], tb=True)
            dx, dxb, dkvnorm = _rms_bwd("kv_norm_bwd", x_kv, kv_norm, dkvn, dx, 0.5)
        dx, dxb = ffn_bwd(dx, dxb, saved["f2", l], ffn2_norm[l], W["ffn2_w_gate", l], W["ffn2_w_up", l], W["ffn2_w_down", l], "ffn2", l, 1.0, rs_token)
        if l < N_A_LAYERS:
            xin, h, zpre, z, gated, bias_full = saved["mix", l]
            (dW["gmlp_w_out", l],) = _mm("gmlp_dwout", [gated], [dxb], one, _ep_plain, [BF16], ta=True)
            (dgated,) = _mm("gmlp_dgated", [dxb], [W["gmlp_w_out", l]], one, _ep_plain, [F32], tb=True)
            dzpre, dws, dbacc, dvn = _gmlp_gate_bwd("gmlp_gate_bwd", z, zpre, dgated, vnorm_full[l:l + 1], gmlp_w_s[l],
                                                    jnp.swapaxes(gmlp_w_s[l], 1, 2), bias_full)
            (dW["gmlp_w_in", l],) = _mm("gmlp_dwin", [h], [dzpre], one, _ep_plain, [BF16], ta=True)
            (dh,) = _mm("gmlp_dh", [dzpre], [W["gmlp_w_in", l]], one, _ep_plain, [F32], tb=True)
            dsmall["gmlp_w_s"][l] = dws
            dsmall["gmlp_b_s"][l] = dbacc.reshape(GMLP_CHUNK, G, GMLP_GROUP_WIDTH).sum(-1).T
            dsmall["gmlp_v_norm"][l] = dvn.reshape(DG)
        else:
            j = l - N_A_LAYERS
            xin, h, q_raw, qn, os_, lses, ob = saved["mix", l]
            (dW["attn_w_o", j],) = _mm("attn_dwo", [ob], [dxb], one, _ep_plain, [BF16], ta=True)
            (d_ob,) = _mm("attn_dob", [dxb], [W["attn_w_o", j]], one, _ep_plain, [F32], tb=True)
            lse_t, dl_rows, dob = _attn_combine("attn_mix_bwd", os_, lses, d_o=d_ob)
            dqs = []
            for gi, dil in enumerate(DILATIONS):
                dq, dk, dv = _attn_bwd(f"attn_bwd_d{dil}", qn, kn, vb, dob, lse_t, dl_rows, gi, dil)
                dqs.append([dq])
                dks[gi].append(dk)
                dvs[gi].append(dv)
            dq_raw, dqgain = _head_norm_bwd("q_norm_bwd", q_raw, 0, 1, qgain[j], q_scale, dqs, None)
            (dW["attn_w_q", j],) = _mm("attn_dwq", [h], [dq_raw], one, _ep_plain, [BF16], ta=True)
            (dh,) = _mm("attn_dh", [dq_raw], [W["attn_w_q", j]], one, _ep_plain, [F32], tb=True)
            dsmall["attn_q_norm"][j] = dqgain.reshape(NG, H, HEAD_DIM).sum(1)
        dx, dxb, dsmall["mix_norm"][l] = _rms_bwd("mix_norm_bwd", xin, mix_norm[l], dh, dx, 0.5)
        dx, dxb = ffn_bwd(dx, dxb, saved["f1", l], ffn1_norm[l], W["ffn1_w_gate", l], W["ffn1_w_up", l], W["ffn1_w_down", l], "ffn1", l, 0.5)
        rs_token = rs_begin(l, dx)
        if l + 1 < NL:
            rs_finish(l + 1, rs_token)
    rs_finish(0, rs_token)
    grad_x = dx.reshape(x.shape)
    big_out = {}
    for n, gb in zip(names_big, gbuf):
        shp = P[n].shape
        two = lambda a: a.reshape(-1, a.shape[-1])
        outs = _adamw("adamw", two(gb), two(P[n]), two(P["m_" + n]), two(P["v_" + n]))
        big_out[n] = [o.reshape(shp) for o in outs]

    loss_part = (0.5 / D) * jnp.sum(loss_rows)
    small_grads = [jnp.stack([g.reshape(P[n].shape[1:]) for g in dsmall[n]]) for n in ("ffn1_norm", "mix_norm", "ffn2_norm", "gmlp_w_s", "gmlp_b_s")]
    small_grads += [dkvnorm.reshape(kv_norm.shape), dkgain.reshape(NG, H, HEAD_DIM).sum(1), jnp.stack(dsmall["attn_q_norm"])]
    vn_grad_full = jnp.stack(dsmall["gmlp_v_norm"])
    packed = _pack(small_grads + [vn_grad_full, loss_part.reshape(1)])
    total = _sum_slots("sum_devices", _exchange_all("exchange_small", packed))
    shapes = [P[n].shape for n in SMALL_WEIGHTS] + [vn_grad_full.shape, (1,)]
    red = _unpack(total, shapes)
    loss = red[-1].reshape(())
    cv = gmlp_v_norm.shape[1]
    vn_grad = lax.dynamic_slice_in_dim(red[-2], chip * cv, cv, axis=1)
    names_small = list(SMALL_WEIGHTS) + ["gmlp_v_norm"]
    g_small = red[:len(SMALL_WEIGHTS)] + [vn_grad]
    outs = _adamw("adamw_small", _pack(g_small), _pack([P[n] for n in names_small]),
                  _pack([P["m_" + n] for n in names_small]), _pack([P["v_" + n] for n in names_small]))
    small_shapes = [P[n].shape for n in names_small]
    small_out = {n: [] for n in names_small}
    for o in outs:
        for n, a in zip(names_small, _unpack(o, small_shapes)):
            small_out[n].append(a)

    res = {**big_out, **small_out}
    return (loss, grad_x, *[res[n][0] for n in WEIGHT_ORDER], *[res[n][1] for n in WEIGHT_ORDER],
            *[res[n][2] for n in WEIGHT_ORDER], *[res[n][3] for n in WEIGHT_ORDER])
```

```python
import numpy as np
import jax
import jax.numpy as jnp
from jax import lax
from jax.experimental import pallas as pl
from jax.experimental.pallas import tpu as pltpu

F32 = jnp.float32
BF16 = jnp.bfloat16
SDS = jax.ShapeDtypeStruct

EPS = 1e-6
HEAD_DIM = 128
GMLP_CHUNK = 128
GMLP_GROUP_WIDTH = 128
DILATIONS = (1, 4, 16)
ATTN_BLOCK = 128
N_A_LAYERS = 2
ADAM_LR, ADAM_B1, ADAM_B2, ADAM_EPS, ADAM_WD, ADAM_STEP = 0.001, 0.9, 0.999, 1e-08, 0.01, 10

N_CHIPS = 4
N_DEV = 8
MESH = pl.DeviceIdType.MESH
V7X_VMEM_BYTES = 64 * 2 ** 20
VMEM_CEILING = V7X_VMEM_BYTES - 6 * 2 ** 20
VMEM_BLOCK_BUDGET = 38 * 2 ** 20
LANE = 128
BF16_ROWS = 16
PACK_TILE = 8 * LANE


def _cp(sem=None, vmem=None):
    kw = {}
    if sem is not None:
        kw["dimension_semantics"] = sem
    if vmem is not None:
        kw["vmem_limit_bytes"] = int(min(max(vmem, 16 * 2 ** 20), VMEM_CEILING))
    return pltpu.CompilerParams(**kw)


def _pick(dim, cands):
    for c in cands:
        if c <= dim and dim % c == 0:
            return c
    return dim


def _row_tile(rows, row_bytes, target=2 ** 20):
    t = 1024
    while t > 8 and (t * row_bytes > target or rows % t):
        t //= 2
    return t if rows % t == 0 else rows


def _sigmoid(x):
    return 1.0 / (1.0 + jnp.exp(-x))


_GELU_C = 0.7978845608028654
_GELU_A = 0.044715


def _gelu(x):
    return 0.5 * x * (1.0 + jnp.tanh(_GELU_C * (x + _GELU_A * (x * x * x))))


def _gelu_grad(x):
    t = jnp.tanh(_GELU_C * (x + _GELU_A * (x * x * x)))
    return 0.5 * (1.0 + t) + 0.5 * x * (1.0 - t * t) * (_GELU_C * (1.0 + 3.0 * _GELU_A * x * x))


def _mm_tiles(M, N, K, n_a, n_b, n_acc, io_bytes):
    tks = [K] + [d for d in (4096, 3072, 2816, 2048, 1024, 512, 256, 128) if d < K and K % d == 0]
    tms = [t for t in (1024, 512, 256, 128) if M % t == 0] or [M]
    tns = [t for t in (512, 256, 128) if N % t == 0] or [N]
    best = None
    for tk in tks:
        for tm in tms:
            for tn in tns:
                est = 2 * 2 * (n_a * tm * tk + n_b * tk * tn) + 2 * tm * tn * io_bytes
                est += n_acc * tm * tn * 4 * (2 if tk < K else 1)
                if est <= VMEM_BLOCK_BUDGET:
                    return tm, tn, tk, est
                if best is None or est < best[3]:
                    best = (tm, tn, tk, est)
    return best


def _mm(name, a_list, b_list, terms, epilogue, out_dtypes, *, ta=False, tb=False, extras=(), dep=None):
    n_acc = 1 + max(t[2] for t in terms)
    a0, b0 = a_list[0], b_list[0]
    (K, M) = a0.shape if ta else a0.shape[::-1]
    N = b0.shape[0] if tb else b0.shape[1]
    io_bytes = sum(jnp.dtype(e.dtype).itemsize for e in extras) + sum(jnp.dtype(d).itemsize for d in out_dtypes)
    tm, tn, tk, est = _mm_tiles(M, N, K, len(a_list), len(b_list), n_acc, io_bytes)
    nk = K // tk
    na, nb, ne, no = len(a_list), len(b_list), len(extras), len(out_dtypes)
    deps = [] if dep is None else [dep]
    nd = len(deps)
    dn = (((0 if ta else 1,), (1 if tb else 0,)), ((), ()))

    def body(*refs):
        a_refs = refs[:na]
        b_refs = refs[na:na + nb]
        e_refs = refs[na + nb:na + nb + ne]
        o_refs = refs[na + nb + ne + nd:na + nb + ne + nd + no]
        acc_refs = refs[na + nb + ne + nd + no:]
        parts = [None] * n_acc
        for ai, bi, qi in terms:
            d = lax.dot_general(a_refs[ai][...], b_refs[bi][...], dn, preferred_element_type=F32)
            parts[qi] = d if parts[qi] is None else parts[qi] + d

        def finish(accs):
            outs = epilogue(accs, [e[...] for e in e_refs])
            for o_ref, o in zip(o_refs, outs):
                o_ref[...] = o.astype(o_ref.dtype)

        if nk == 1:
            finish(parts)
        else:
            k = pl.program_id(2)

            @pl.when(k == 0)
            def _():
                for q in range(n_acc):
                    acc_refs[q][...] = parts[q]

            @pl.when(k > 0)
            def _():
                for q in range(n_acc):
                    acc_refs[q][...] += parts[q]

            @pl.when(k == nk - 1)
            def _():
                finish([acc_refs[q][...] for q in range(n_acc)])

    a_spec = pl.BlockSpec((tk, tm), lambda i, j, k: (k, i)) if ta else pl.BlockSpec((tm, tk), lambda i, j, k: (i, k))
    b_spec = pl.BlockSpec((tn, tk), lambda i, j, k: (j, k)) if tb else pl.BlockSpec((tk, tn), lambda i, j, k: (k, j))
    e_spec = pl.BlockSpec((tm, tn), lambda i, j, k: (i, j))
    outs = pl.pallas_call(
        body, name=name, grid=(M // tm, N // tn, nk),
        in_specs=[a_spec] * na + [b_spec] * nb + [e_spec] * ne + [pl.BlockSpec((8, LANE), lambda i, j, k: (0, 0))] * nd,
        out_specs=[e_spec] * no,
        out_shape=[SDS((M, N), d) for d in out_dtypes],
        scratch_shapes=[pltpu.VMEM((tm, tn), F32) for _ in range(n_acc)] if nk > 1 else [],
        compiler_params=_cp(("parallel", "parallel", "arbitrary"), est + 12 * 2 ** 20),
    )(*a_list, *b_list, *extras, *deps)
    return outs


def _ep_plain(accs, ex):
    return [accs[0]]


def _ep_swiglu(accs, ex):
    g, u = accs
    s = _sigmoid(g)
    sg = g * s
    return [u * (s + sg * (1.0 - s)), sg, sg * u]


def _ep_swiglu_bwd(accs, ex):
    da = accs[0]
    return [da * ex[0].astype(F32), da * ex[1].astype(F32)]


def _ep_gelu(accs, ex):
    return [_gelu_grad(accs[0]), _gelu(accs[0])]


def _ep_residual(scale):
    def ep(accs, ex):
        return [ex[0] + scale * accs[0]]
    return ep


def _rms_fwd(name, x, gamma, dep=None):
    S, D = x.shape
    tr = _row_tile(S, D * 4)
    deps = [] if dep is None else [dep]

    def body(x_ref, g_ref, *rest):
        o_ref = rest[-1]
        xv = x_ref[...]
        r = lax.rsqrt(jnp.mean(xv * xv, axis=-1, keepdims=True) + EPS)
        o_ref[...] = (xv * r * g_ref[...]).astype(BF16)

    return pl.pallas_call(
        body, name=name, grid=(S // tr,),
        in_specs=[pl.BlockSpec((tr, D), lambda i: (i, 0)), pl.BlockSpec((1, D), lambda i: (0, 0))]
        + [pl.BlockSpec((8, LANE), lambda i: (0, 0))] * len(deps),
        out_specs=pl.BlockSpec((tr, D), lambda i: (i, 0)),
        out_shape=SDS((S, D), BF16),
        compiler_params=_cp(("parallel",), 32 * 2 ** 20),
    )(x, gamma.reshape(1, D), *deps)


def _rms_bwd(name, x, gamma, dn, dx_in, out_scale):
    S, D = x.shape
    tr = _row_tile(S, D * 4, 2 ** 19)

    def body(x_ref, g_ref, dn_ref, dxi_ref, dxo_ref, dxb_ref, dg_ref):
        i = pl.program_id(0)
        xv = x_ref[...]
        r = lax.rsqrt(jnp.mean(xv * xv, axis=-1, keepdims=True) + EPS)
        xh = xv * r
        dnv = dn_ref[...]
        dxh = dnv * g_ref[...]
        dx = dxi_ref[...] + r * (dxh - xh * jnp.mean(dxh * xh, axis=-1, keepdims=True))
        dxo_ref[...] = dx
        dxb_ref[...] = (out_scale * dx).astype(BF16)
        part = jnp.sum(dnv * xh, axis=0, keepdims=True)

        @pl.when(i == 0)
        def _():
            dg_ref[...] = part

        @pl.when(i > 0)
        def _():
            dg_ref[...] += part

    row = pl.BlockSpec((tr, D), lambda i: (i, 0))
    vec = pl.BlockSpec((1, D), lambda i: (0, 0))
    return pl.pallas_call(
        body, name=name, grid=(S // tr,),
        in_specs=[row, vec, row, row], out_specs=[row, row, vec],
        out_shape=[SDS((S, D), F32), SDS((S, D), BF16), SDS((1, D), F32)],
        compiler_params=_cp(("arbitrary",), 40 * 2 ** 20),
    )(x, gamma.reshape(1, D), dn, dx_in)


def _loss_grad(name, y, t, out_scale):
    S, D = y.shape
    tr = _row_tile(S, D * 4, 2 ** 19)
    inv_d = 1.0 / D

    def body(y_ref, t_ref, dy_ref, dyb_ref, ls_ref):
        i = pl.program_id(0)
        e = y_ref[...] - t_ref[...]
        dy = e * inv_d
        dy_ref[...] = dy
        dyb_ref[...] = (out_scale * dy).astype(BF16)
        part = jnp.sum(e * e, axis=0, keepdims=True)

        @pl.when(i == 0)
        def _():
            ls_ref[...] = part

        @pl.when(i > 0)
        def _():
            ls_ref[...] += part

    row = pl.BlockSpec((tr, D), lambda i: (i, 0))
    vec = pl.BlockSpec((1, D), lambda i: (0, 0))
    return pl.pallas_call(
        body, name=name, grid=(S // tr,),
        in_specs=[row, row], out_specs=[row, row, vec],
        out_shape=[SDS((S, D), F32), SDS((S, D), BF16), SDS((1, D), F32)],
        compiler_params=_cp(("arbitrary",), 32 * 2 ** 20),
    )(y, t)


def _head_mean(v):
    avg = jnp.full((HEAD_DIM, HEAD_DIM), 1.0 / HEAD_DIM, F32)
    return jnp.dot(v, avg, preferred_element_type=F32, precision=lax.Precision.HIGHEST)


def _head_norm_fwd(name, raw, part, n_parts, gain_t, scale, with_pass):
    S = raw.shape[0]
    W = raw.shape[1] // n_parts
    nh = W // HEAD_DIM
    tr = _row_tile(S, W * (4 + 2) * (2 if with_pass else 1), 4 * 2 ** 20)

    def body(*refs):
        if with_pass:
            x_ref, p_ref, g_ref, o_ref, po_ref = refs
            po_ref[...] = p_ref[...].astype(BF16)
        else:
            x_ref, g_ref, o_ref = refs
        for h in range(nh):
            sl = slice(h * HEAD_DIM, (h + 1) * HEAD_DIM)
            xv = x_ref[:, sl]
            r = lax.rsqrt(_head_mean(xv * xv) + EPS)
            o_ref[:, sl] = (xv * r * g_ref[:, sl] * scale).astype(BF16)

    xspec = pl.BlockSpec((tr, W), lambda i: (i, part))
    ospec = pl.BlockSpec((tr, W), lambda i: (i, 0))
    gspec = pl.BlockSpec((1, W), lambda i: (0, 0))
    if with_pass:
        in_specs = [xspec, pl.BlockSpec((tr, W), lambda i: (i, 1)), gspec]
        args = (raw, raw, gain_t)
        out_specs, out_shape = [ospec, ospec], [SDS((S, W), BF16), SDS((S, W), BF16)]
    else:
        in_specs, args = [xspec, gspec], (raw, gain_t)
        out_specs, out_shape = ospec, SDS((S, W), BF16)
    return pl.pallas_call(
        body, name=name, grid=(S // tr,), in_specs=in_specs, out_specs=out_specs, out_shape=out_shape,
        compiler_params=_cp(("parallel",), 40 * 2 ** 20),
    )(*args)


def _head_norm_bwd(name, raw, part, n_parts, gain_t, scale, dy_groups, pass_groups):
    S = raw.shape[0]
    W = raw.shape[1] // n_parts
    ng = len(dy_groups)
    HD = W // ng
    nhg = HD // HEAD_DIM
    n_dy = [len(g) for g in dy_groups]
    n_ps = [len(g) for g in pass_groups] if pass_groups is not None else []
    flat = [a for g in dy_groups for a in g] + ([a for g in pass_groups for a in g] if pass_groups is not None else [])
    out_w = 2 * W if pass_groups is not None else W
    tr = _row_tile(S, W * 4 + len(flat) * HD * 4 + out_w * 2, 8 * 2 ** 20)

    def body(*refs):
        x_ref, g_ref = refs[0], refs[1]
        d_refs = refs[2:2 + len(flat)]
        o_ref, dg_ref = refs[2 + len(flat)], refs[3 + len(flat)]
        i = pl.program_id(0)

        @pl.when(i == 0)
        def _():
            dg_ref[...] = jnp.zeros_like(dg_ref)

        pos = 0
        for gi in range(ng):
            dys = d_refs[pos:pos + n_dy[gi]]
            pos += n_dy[gi]
            for h in range(nhg):
                sl = slice(gi * HD + h * HEAD_DIM, gi * HD + (h + 1) * HEAD_DIM)
                hs = slice(h * HEAD_DIM, (h + 1) * HEAD_DIM)
                dy = dys[0][:, hs]
                for extra in dys[1:]:
                    dy = dy + extra[:, hs]
                xv = x_ref[:, sl]
                r = lax.rsqrt(_head_mean(xv * xv) + EPS)
                xh = xv * r
                dxh = dy * (g_ref[:, sl] * scale)
                o_ref[:, sl] = (r * (dxh - xh * _head_mean(dxh * xh))).astype(BF16)
                dg_ref[:, sl] += jnp.sum(dy * xh, axis=0, keepdims=True) * scale
        for gi in range(len(n_ps)):
            ps = d_refs[pos:pos + n_ps[gi]]
            pos += n_ps[gi]
            acc = ps[0][...]
            for extra in ps[1:]:
                acc = acc + extra[...]
            o_ref[:, W + gi * HD:W + (gi + 1) * HD] = acc.astype(BF16)

    dspec = pl.BlockSpec((tr, HD), lambda i: (i, 0))
    return pl.pallas_call(
        body, name=name, grid=(S // tr,),
        in_specs=[pl.BlockSpec((tr, W), lambda i: (i, part)), pl.BlockSpec((1, W), lambda i: (0, 0))] + [dspec] * len(flat),
        out_specs=[pl.BlockSpec((tr, out_w), lambda i: (i, 0)), pl.BlockSpec((1, W), lambda i: (0, 0))],
        out_shape=[SDS((S, out_w), BF16), SDS((1, W), F32)],
        compiler_params=_cp(("arbitrary",), 48 * 2 ** 20),
    )(raw, gain_t, *flat)


def _tril_mask():
    r = lax.broadcasted_iota(jnp.int32, (GMLP_CHUNK, GMLP_CHUNK), 0)
    c = lax.broadcasted_iota(jnp.int32, (GMLP_CHUNK, GMLP_CHUNK), 1)
    return r >= c


def _gmlp_gate_fwd(name, z, v_norm, w_s, bias_full):
    S, DG2 = z.shape
    DG = DG2 // 2
    G = DG // GMLP_GROUP_WIDTH
    C = GMLP_CHUNK

    def body(u_ref, v_ref, vn_ref, ws_ref, b_ref, o_ref):
        mask = _tril_mask()
        v = v_ref[...]
        r = lax.rsqrt(jnp.mean(v * v, axis=-1, keepdims=True) + EPS)
        vn = (v * r * vn_ref[...]).astype(BF16)
        for g in range(G):
            sl = slice(g * GMLP_GROUP_WIDTH, (g + 1) * GMLP_GROUP_WIDTH)
            wm = jnp.where(mask, ws_ref[g], 0.0).astype(BF16)
            sv = jnp.dot(wm, vn[:, sl], preferred_element_type=F32) + b_ref[:, sl]
            o_ref[:, sl] = (u_ref[:, sl] * sv).astype(BF16)

    return pl.pallas_call(
        body, name=name, grid=(S // C,),
        in_specs=[pl.BlockSpec((C, DG), lambda i: (i, 0)), pl.BlockSpec((C, DG), lambda i: (i, 1)),
                  pl.BlockSpec((1, DG), lambda i: (0, 0)), pl.BlockSpec((G, C, C), lambda i: (0, 0, 0)),
                  pl.BlockSpec((C, DG), lambda i: (0, 0))],
        out_specs=pl.BlockSpec((C, DG), lambda i: (i, 0)),
        out_shape=SDS((S, DG), BF16),
        compiler_params=_cp(("parallel",), 32 * 2 ** 20),
    )(z, z, v_norm, w_s, bias_full)


def _gmlp_gate_bwd(name, z, zpre, dgated, v_norm, w_s, w_s_t, bias_full):
    S, DG2 = z.shape
    DG = DG2 // 2
    G = DG // GMLP_GROUP_WIDTH
    C = GMLP_CHUNK

    def body(z_ref, zp_ref, dg_ref, vn_ref, ws_ref, wst_ref, b_ref, dz_ref, dws_ref, db_ref, dvn_ref):
        i = pl.program_id(0)
        mask = _tril_mask()
        mask_t = jnp.logical_not(mask) | (lax.broadcasted_iota(jnp.int32, (C, C), 0) == lax.broadcasted_iota(jnp.int32, (C, C), 1))
        u = z_ref[:, :DG]
        v = z_ref[:, DG:]
        r = lax.rsqrt(jnp.mean(v * v, axis=-1, keepdims=True) + EPS)
        vh = v * r
        gain = vn_ref[...]
        vn = (vh * gain).astype(BF16)
        dgt = dg_ref[...]

        @pl.when(i == 0)
        def _():
            dws_ref[...] = jnp.zeros_like(dws_ref)
            db_ref[...] = jnp.zeros_like(db_ref)
            dvn_ref[...] = jnp.zeros_like(dvn_ref)

        dvn_parts = []
        for g in range(G):
            sl = slice(g * GMLP_GROUP_WIDTH, (g + 1) * GMLP_GROUP_WIDTH)
            wm = jnp.where(mask, ws_ref[g], 0.0).astype(BF16)
            wmt = jnp.where(mask_t, wst_ref[g], 0.0).astype(BF16)
            sv = jnp.dot(wm, vn[:, sl], preferred_element_type=F32) + b_ref[:, sl]
            dgs = dgt[:, sl]
            du = dgs * sv
            dsv = dgs * u[:, sl]
            db_ref[:, sl] += dsv
            dsv_b = dsv.astype(BF16)
            dws = lax.dot_general(dsv_b, vn[:, sl], (((1,), (1,)), ((), ())), preferred_element_type=F32)
            dws_ref[g] += jnp.where(mask, dws, 0.0)
            dvn_parts.append(jnp.dot(wmt, dsv_b, preferred_element_type=F32))
            dz_ref[:, sl] = (du * zp_ref[:, sl].astype(F32)).astype(BF16)
        dvn_full = jnp.concatenate(dvn_parts, axis=1)
        dvn_ref[...] += jnp.sum(dvn_full * vh, axis=0, keepdims=True)
        dxh = dvn_full * gain
        dv = r * (dxh - vh * jnp.mean(dxh * vh, axis=-1, keepdims=True))
        dz_ref[:, DG:] = (dv * zp_ref[:, DG:].astype(F32)).astype(BF16)

    full = pl.BlockSpec((C, DG2), lambda i: (i, 0))
    wspec = pl.BlockSpec((G, C, C), lambda i: (0, 0, 0))
    return pl.pallas_call(
        body, name=name, grid=(S // C,),
        in_specs=[full, full, pl.BlockSpec((C, DG), lambda i: (i, 0)), pl.BlockSpec((1, DG), lambda i: (0, 0)),
                  wspec, wspec, pl.BlockSpec((C, DG), lambda i: (0, 0))],
        out_specs=[full, wspec, pl.BlockSpec((C, DG), lambda i: (0, 0)), pl.BlockSpec((1, DG), lambda i: (0, 0))],
        out_shape=[SDS((S, DG2), BF16), SDS((G, C, C), F32), SDS((C, DG), F32), SDS((1, DG), F32)],
        compiler_params=_cp(("arbitrary",), 40 * 2 ** 20),
    )(z, zpre, dgated, v_norm, w_s, w_s_t, bias_full)


def _alibi_slopes(n_heads):
    return [float(v) for v in np.exp2(np.float32(-8.0) * np.arange(1, n_heads + 1, dtype=np.float32) / np.float32(n_heads))]


def _dil_view(arr, dil):
    S, C = arr.shape
    return arr if dil == 1 else arr.reshape(S // dil, dil * C)


def _dil_spec(dil, HD, ncb, cb, bmap):
    return pl.BlockSpec((ATTN_BLOCK, HD), lambda r, b: (bmap(b), r * ncb + cb))


def _group_view(arr, gi, dil, HD):
    if dil == 1:
        return arr, arr.shape[1] // HD, gi
    return _dil_view(arr[:, gi * HD:(gi + 1) * HD], dil), 1, 0


def _dil_unview(arr, S):
    return arr.reshape(S, arr.size // S)


def _attn_mask(b):
    qi = lax.broadcasted_iota(jnp.int32, (ATTN_BLOCK, 2 * ATTN_BLOCK), 0)
    kj = lax.broadcasted_iota(jnp.int32, (ATTN_BLOCK, 2 * ATTN_BLOCK), 1)
    delta = qi + ATTN_BLOCK - kj
    valid = (delta >= 0) & (delta <= ATTN_BLOCK) & ((kj >= ATTN_BLOCK) | (b > 0))
    return valid, delta.astype(F32)


def _attn_fwd(name, qn, kn, vb, gi, dil):
    S, C = qn.shape
    NG = len(DILATIONS)
    HD = C // NG
    H = HD // HEAD_DIM
    L = S // dil
    nb = L // ATTN_BLOCK
    slopes = _alibi_slopes(H)

    def body(q_ref, kc_ref, kp_ref, vc_ref, vp_ref, o_ref, lse_ref):
        b = pl.program_id(1)
        valid, delta = _attn_mask(b)
        dist = delta * float(dil)
        for h in range(H):
            sl = slice(h * HEAD_DIM, (h + 1) * HEAD_DIM)
            k = jnp.concatenate([kp_ref[:, sl], kc_ref[:, sl]], axis=0)
            v = jnp.concatenate([vp_ref[:, sl], vc_ref[:, sl]], axis=0)
            s = lax.dot_general(q_ref[:, sl], k, (((1,), (1,)), ((), ())), preferred_element_type=F32)
            s = jnp.where(valid, s - slopes[h] * dist, -jnp.inf)
            m = jnp.max(s, axis=-1, keepdims=True)
            p = jnp.exp(s - m)
            l = jnp.sum(p, axis=-1, keepdims=True)
            o = jnp.dot(p.astype(BF16), v, preferred_element_type=F32)
            o_ref[:, sl] = o / l
            lse_ref[:, sl] = jnp.broadcast_to(m + jnp.log(l), (ATTN_BLOCK, HEAD_DIM))

    cur = lambda b: b
    prev = lambda b: jnp.maximum(b - 1, 0)
    (qv, ncb, cb), (kv, _, _), (vv, _, _) = (_group_view(a, gi, dil, HD) for a in (qn, kn, vb))
    view_shape = (L, dil * HD)
    o, lse = pl.pallas_call(
        body, name=name, grid=(dil, nb),
        in_specs=[_dil_spec(dil, HD, ncb, cb, cur), _dil_spec(dil, HD, ncb, cb, cur), _dil_spec(dil, HD, ncb, cb, prev),
                  _dil_spec(dil, HD, ncb, cb, cur), _dil_spec(dil, HD, ncb, cb, prev)],
        out_specs=[_dil_spec(dil, HD, 1, 0, cur), _dil_spec(dil, HD, 1, 0, cur)],
        out_shape=[SDS(view_shape, F32), SDS(view_shape, F32)],
        compiler_params=_cp(("parallel", "parallel"), 32 * 2 ** 20),
    )(qv, kv, kv, vv, vv)
    return _dil_unview(o, S), _dil_unview(lse, S)


def _attn_combine(name, o_list, lse_list, d_o=None):
    S, HD = o_list[0].shape
    H = HD // HEAD_DIM
    ng = len(o_list)
    tr = _row_tile(S, HD * 4, 2 ** 19)

    def body(*refs):
        o_refs = refs[:ng]
        l_refs = refs[ng:2 * ng]
        ls = [r[...] for r in l_refs]
        m = ls[0]
        for t in ls[1:]:
            m = jnp.maximum(m, t)
        es = [jnp.exp(t - m) for t in ls]
        z = es[0]
        for t in es[1:]:
            z = z + t
        o = es[0] * o_refs[0][...]
        for e, r in zip(es[1:], o_refs[1:]):
            o = o + e * r[...]
        o = o / z
        if d_o is None:
            refs[2 * ng][...] = o.astype(BF16)
        else:
            do_ref, lse_ref, dl_ref, dob_ref = refs[2 * ng:]
            dov = do_ref[...]
            lse_ref[...] = m + jnp.log(z)
            dob_ref[...] = dov.astype(BF16)
            prod = dov * o
            for h in range(H):
                sl = slice(h * HEAD_DIM, (h + 1) * HEAD_DIM)
                dl_ref[:, sl] = jnp.broadcast_to(jnp.sum(prod[:, sl], axis=-1, keepdims=True), (tr, HEAD_DIM))

    row = pl.BlockSpec((tr, HD), lambda i: (i, 0))
    if d_o is None:
        return pl.pallas_call(
            body, name=name, grid=(S // tr,), in_specs=[row] * (2 * ng), out_specs=row,
            out_shape=SDS((S, HD), BF16), compiler_params=_cp(("parallel",), 40 * 2 ** 20),
        )(*o_list, *lse_list)
    return pl.pallas_call(
        body, name=name, grid=(S // tr,), in_specs=[row] * (2 * ng + 1), out_specs=[row, row, row],
        out_shape=[SDS((S, HD), F32), SDS((S, HD), F32), SDS((S, HD), BF16)],
        compiler_params=_cp(("parallel",), 48 * 2 ** 20),
    )(*o_list, *lse_list, d_o)


def _attn_bwd(name, qn, kn, vb, dob, lse, delta_rows, gi, dil):
    S, C = qn.shape
    NG = len(DILATIONS)
    HD = C // NG
    H = HD // HEAD_DIM
    L = S // dil
    nb = L // ATTN_BLOCK
    slopes = _alibi_slopes(H)
    B = ATTN_BLOCK

    def body(q_ref, kc_ref, kp_ref, vc_ref, vp_ref, do_ref, lse_ref, dl_ref, dq_ref, dk_ref, dv_ref, ck_ref, cv_ref):
        b = pl.program_id(1)

        @pl.when(b == 0)
        def _():
            ck_ref[...] = jnp.zeros_like(ck_ref)
            cv_ref[...] = jnp.zeros_like(cv_ref)

        @pl.when(b < nb)
        def _():
            valid, delta = _attn_mask(b)
            dist = delta * float(dil)
            for h in range(H):
                sl = slice(h * HEAD_DIM, (h + 1) * HEAD_DIM)
                q = q_ref[:, sl]
                k = jnp.concatenate([kp_ref[:, sl], kc_ref[:, sl]], axis=0)
                v = jnp.concatenate([vp_ref[:, sl], vc_ref[:, sl]], axis=0)
                do = do_ref[:, sl]
                s = lax.dot_general(q, k, (((1,), (1,)), ((), ())), preferred_element_type=F32)
                s = jnp.where(valid, s - slopes[h] * dist, -jnp.inf)
                lse2 = jnp.concatenate([lse_ref[:, sl], lse_ref[:, sl]], axis=1)
                dl2 = jnp.concatenate([dl_ref[:, sl], dl_ref[:, sl]], axis=1)
                p = jnp.exp(s - lse2)
                dp = lax.dot_general(do, v, (((1,), (1,)), ((), ())), preferred_element_type=F32)
                ds = (p * (dp - dl2)).astype(BF16)
                dq_ref[:, sl] = jnp.dot(ds, k, preferred_element_type=F32)
                dk2 = lax.dot_general(ds, q, (((0,), (0,)), ((), ())), preferred_element_type=F32)
                dv2 = lax.dot_general(p.astype(BF16), do, (((0,), (0,)), ((), ())), preferred_element_type=F32)
                dk_ref[:, sl] = ck_ref[:, sl] + dk2[:B]
                dv_ref[:, sl] = cv_ref[:, sl] + dv2[:B]
                ck_ref[:, sl] = dk2[B:]
                cv_ref[:, sl] = dv2[B:]

        @pl.when(b == nb)
        def _():
            dk_ref[...] = ck_ref[...]
            dv_ref[...] = cv_ref[...]

    cur = lambda b: jnp.minimum(b, nb - 1)
    prev = lambda b: jnp.maximum(jnp.minimum(b, nb - 1) - 1, 0)
    late = lambda b: jnp.maximum(b - 1, 0)
    (qv, ncb, cb), (kv, _, _), (vv, _, _) = (_group_view(a, gi, dil, HD) for a in (qn, kn, vb))
    dov, lsev, dlv = _dil_view(dob, dil), _dil_view(lse, dil), _dil_view(delta_rows, dil)
    view_shape = (L, dil * HD)
    one = lambda m: _dil_spec(dil, HD, 1, 0, m)
    grp = lambda m: _dil_spec(dil, HD, ncb, cb, m)
    dq, dk, dv = pl.pallas_call(
        body, name=name, grid=(dil, nb + 1),
        in_specs=[grp(cur), grp(cur), grp(prev), grp(cur), grp(prev), one(cur), one(cur), one(cur)],
        out_specs=[one(cur), one(late), one(late)],
        out_shape=[SDS(view_shape, F32)] * 3,
        scratch_shapes=[pltpu.VMEM((B, HD), F32), pltpu.VMEM((B, HD), F32)],
        compiler_params=_cp(("arbitrary", "arbitrary"), 40 * 2 ** 20),
    )(qv, kv, kv, vv, vv, dov, lsev, dlv)
    return _dil_unview(dq, S), _dil_unview(dk, S), _dil_unview(dv, S)


def _cast_into_gathered(name, w, l, sa, chip_arr):
    L, r, c = w.shape
    tr = _row_tile(r, c * 4)
    nrb = r // tr
    src = pl.BlockSpec((None, tr, c), lambda i, chip: (l, i, 0))
    if sa == 0:
        dst, shape = pl.BlockSpec((tr, c), lambda i, chip: (chip[0] * nrb + i, 0)), (N_CHIPS * r, c)
    else:
        dst, shape = pl.BlockSpec((tr, c), lambda i, chip: (i, chip[0])), (r, N_CHIPS * c)

    def body(chip_ref, i_ref, o_ref):
        o_ref[...] = i_ref[...].astype(BF16)

    return pl.pallas_call(
        body, name=name,
        grid_spec=pltpu.PrefetchScalarGridSpec(num_scalar_prefetch=1, grid=(nrb,), in_specs=[src], out_specs=dst),
        out_shape=SDS(shape, BF16), compiler_params=_cp(("parallel",), 32 * 2 ** 20),
    )(chip_arr, w)


def _add_half(name, dw, land, sa, c_arr):
    hr, hc = land.shape
    tr = _row_tile(hr, hc * 2)
    nrb = hr // tr
    if sa == 1:
        mine = pl.BlockSpec((tr, hc), lambda i, c: (c[0] * nrb + i, 0))
    else:
        mine = pl.BlockSpec((tr, hc), lambda i, c: (i, c[0]))
    other = pl.BlockSpec((tr, hc), lambda i, c: (i, 0))

    def body(c_ref, a_ref, b_ref, o_ref):
        o_ref[...] = (a_ref[...].astype(F32) + b_ref[...].astype(F32)).astype(BF16)

    return pl.pallas_call(
        body, name=name,
        grid_spec=pltpu.PrefetchScalarGridSpec(num_scalar_prefetch=1, grid=(nrb,), in_specs=[mine, other], out_specs=other),
        out_shape=SDS((hr, hc), BF16), compiler_params=_cp(("parallel",), 32 * 2 ** 20),
    )(c_arr, dw, land)


def _sum_slots(name, slots, out_dtype=F32):
    n, R, C = slots.shape
    tr = _row_tile(R, C * n * jnp.dtype(slots.dtype).itemsize, 2 ** 21)

    def body(s_ref, o_ref):
        acc = s_ref[0].astype(F32)
        for k in range(1, n):
            acc = acc + s_ref[k].astype(F32)
        o_ref[...] = acc.astype(out_dtype)

    return pl.pallas_call(
        body, name=name, grid=(R // tr,),
        in_specs=[pl.BlockSpec((n, tr, C), lambda i: (0, i, 0))], out_specs=pl.BlockSpec((tr, C), lambda i: (i, 0)),
        out_shape=SDS((R, C), out_dtype), compiler_params=_cp(("parallel",), 32 * 2 ** 20),
    )(slots)


def _sum_into(name, part, slots, buf, l, sa, where):
    n, pr, pc = slots.shape
    tr = _row_tile(pr, pc * (n + 1) * jnp.dtype(slots.dtype).itemsize, 2 ** 21)
    nrb = pr // tr
    if sa == 1:
        dst = pl.BlockSpec((None, tr, pc), lambda i, c, j: (l, c[0] * nrb + i, 0))
        own = pl.BlockSpec((tr, pc), lambda i, c, j: (i, j[0]))
    else:
        dst = pl.BlockSpec((None, tr, pc), lambda i, c, j: (l, i, c[0]))
        own = pl.BlockSpec((tr, pc), lambda i, c, j: (j[0] * nrb + i, 0))

    def body(c_ref, j_ref, p_ref, s_ref, b_ref, o_ref):
        acc = p_ref[...].astype(F32)
        for k in range(n):
            acc = acc + s_ref[k].astype(F32)
        o_ref[...] = acc

    return pl.pallas_call(
        body, name=name,
        grid_spec=pltpu.PrefetchScalarGridSpec(
            num_scalar_prefetch=2, grid=(nrb,),
            in_specs=[own, pl.BlockSpec((n, tr, pc), lambda i, c, j: (0, i, 0)), ANY], out_specs=dst),
        out_shape=SDS(buf.shape, buf.dtype), input_output_aliases={4: 0},
        compiler_params=_cp(("parallel",), 32 * 2 ** 20),
    )(where[0], where[1], part, slots, buf)


def _adamw_body(g_ref, w_ref, m_ref, v_ref, go_ref, d_ref, mo_ref, vo_ref):
    bc1 = 1.0 - ADAM_B1 ** ADAM_STEP
    bc2 = 1.0 - ADAM_B2 ** ADAM_STEP
    gv = g_ref[...]
    mn = ADAM_B1 * m_ref[...] + (1.0 - ADAM_B1) * gv
    vn = ADAM_B2 * v_ref[...] + (1.0 - ADAM_B2) * (gv * gv)
    go_ref[...] = gv
    mo_ref[...] = mn
    vo_ref[...] = vn
    d_ref[...] = -ADAM_LR * ((mn / bc1) / (jnp.sqrt(vn / bc2) + ADAM_EPS) + ADAM_WD * w_ref[...])


def _adamw_layer(name, g, w, m, v, l, outs):
    L, r, c = g.shape
    tr = _row_tile(r, c * 4, 2 ** 19)
    spec = pl.BlockSpec((None, tr, c), lambda i: (l, i, 0))

    def body(g_ref, w_ref, m_ref, v_ref, a0, a1, a2, a3, go_ref, d_ref, mo_ref, vo_ref):
        _adamw_body(g_ref, w_ref, m_ref, v_ref, go_ref, d_ref, mo_ref, vo_ref)

    return list(pl.pallas_call(
        body, name=name, grid=(r // tr,), in_specs=[spec] * 4 + [ANY] * 4, out_specs=[spec] * 4,
        out_shape=[SDS((L, r, c), F32)] * 4, input_output_aliases={4 + k: k for k in range(4)},
        compiler_params=_cp(("parallel",), 32 * 2 ** 20),
    )(g, w, m, v, *outs))


def _adamw(name, g, w, m, v):
    R, C = g.shape
    tr = _row_tile(R, C * 4, 2 ** 19)

    def body(g_ref, w_ref, m_ref, v_ref, go_ref, d_ref, mo_ref, vo_ref):
        _adamw_body(g_ref, w_ref, m_ref, v_ref, go_ref, d_ref, mo_ref, vo_ref)

    spec = pl.BlockSpec((tr, C), lambda i: (i, 0))
    return pl.pallas_call(
        body, name=name, grid=(R // tr,), in_specs=[spec] * 4, out_specs=[spec] * 4,
        out_shape=[SDS((R, C), F32)] * 4, compiler_params=_cp(("parallel",), 32 * 2 ** 20),
    )(g, w, m, v)


ANY = pl.BlockSpec(memory_space=pl.ANY)


def _coords():
    return lax.axis_index("x"), lax.axis_index("y"), lax.axis_index("c")


def _other_chips(x, y):
    return [((1 - x, y), 2 * (1 - x) + y), ((x, 1 - y), 2 * x + (1 - y)), ((1 - x, 1 - y), 2 * (1 - x) + (1 - y))]


def _win(ref, axis, start, size):
    if not isinstance(start, int):
        start = pl.multiple_of(start, LANE if axis == 1 else BF16_ROWS)
    if axis == 0:
        return ref.at[pl.ds(start, size), :]
    return ref.at[:, pl.ds(start, size)]


def _rcopy(src, dst, ssem, rsem, dev):
    return pltpu.make_async_remote_copy(src_ref=src, dst_ref=dst, send_sem=ssem, recv_sem=rsem,
                                        device_id=dev, device_id_type=MESH)


HBM = pl.BlockSpec(memory_space=pltpu.HBM)
SEM = pl.BlockSpec(memory_space=pltpu.SEMAPHORE)
TOKEN = pl.BlockSpec(memory_space=pltpu.VMEM)
EFFECT = pltpu.SideEffectType.DATAFLOW_SIDE_EFFECTING


def _in_hbm(a):
    return pltpu.with_memory_space_constraint(a, pltpu.HBM)


def _ag_geometry(mats, sas, o):
    sa = sas[o]
    return sa, 1 - sa, mats[o].shape[sa] // N_CHIPS, mats[o].shape[1 - sa] // 2


def _ag_ici_copy(mats, sas, refs, o, k, sems, x, y, c):
    sa, ha, wl, hl = _ag_geometry(mats, sas, o)
    chip, jk = _other_chips(x, y)[k]
    mine = _win(_win(refs[o], sa, (2 * x + y) * wl, wl), ha, c * hl, hl)
    landed = _win(_win(refs[o], sa, jk * wl, wl), ha, c * hl, hl)
    return mine, landed, (*chip, c)


def _ag_start(name, mats, sas, after):
    n = len(mats)

    def body(*refs):
        ins = refs[:n]
        s_sem, r_sem = refs[n + 1], refs[n + 2]
        token = refs[2 * n + 3]
        x, y, c = _coords()
        for o in range(n):
            for k in range(3):
                mine, _, dev = _ag_ici_copy(mats, sas, ins, o, k, None, x, y, c)
                _rcopy(mine, mine, s_sem.at[3 * o + k], r_sem.at[3 * o + k], dev).start()
        token[...] = jnp.zeros_like(token)

    out = pl.pallas_call(
        body, name=name,
        out_shape=(pltpu.SemaphoreType.DMA((3 * n,)), pltpu.SemaphoreType.DMA((3 * n,)),
                   *[pltpu.HBM(m.shape, m.dtype) for m in mats], SDS((8, LANE), F32)),
        in_specs=[HBM] * n + [ANY], out_specs=(SEM, SEM, *[HBM] * n, TOKEN),
        input_output_aliases={k: 2 + k for k in range(n)},
        compiler_params=pltpu.CompilerParams(has_side_effects=EFFECT),
    )(*[_in_hbm(m) for m in mats], after)
    return out[0], out[1], list(out[2:2 + n]), out[2 + n]


def _ag_wait(name, mats, sas, s_sem, r_sem, after):
    n = len(mats)

    def body(*refs):
        ins = refs[:n]
        s_ref, r_ref = refs[n], refs[n + 1]
        x, y, c = _coords()
        for o in range(n):
            for k in range(3):
                mine, landed, dev = _ag_ici_copy(mats, sas, ins, o, k, None, x, y, c)
                cp = _rcopy(mine, landed, s_ref.at[3 * o + k], r_ref.at[3 * o + k], dev)
                cp.wait_send()
                cp.wait_recv()

    after = list(after) if isinstance(after, (list, tuple)) else [after]
    return list(pl.pallas_call(
        body, name=name, out_shape=[pltpu.HBM(m.shape, m.dtype) for m in mats],
        in_specs=[HBM] * n + [SEM, SEM] + [ANY] * len(after), out_specs=[HBM] * n,
        input_output_aliases={k: k for k in range(n)},
        compiler_params=pltpu.CompilerParams(has_side_effects=EFFECT),
    )(*mats, s_sem, r_sem, *after))


def _ag_forward(name, mats, sas):
    n = len(mats)

    def body(*refs):
        outs = refs[n:2 * n]
        s_fwd, r_fwd = refs[2 * n:]
        x, y, c = _coords()
        sibling = (x, y, 1 - c)
        sends = []
        for o in range(n):
            sa, ha, wl, hl = _ag_geometry(mats, sas, o)
            for k, (chip, jk) in enumerate(_other_chips(x, y)):
                landed = _win(_win(outs[o], sa, jk * wl, wl), ha, c * hl, hl)
                fwd = _rcopy(landed, landed, s_fwd.at[3 * o + k], r_fwd.at[3 * o + k], sibling)
                fwd.start()
                sends.append(fwd)
        for o in range(n):
            sa, ha, wl, hl = _ag_geometry(mats, sas, o)
            for k, (chip, jk) in enumerate(_other_chips(x, y)):
                got = _win(_win(outs[o], sa, jk * wl, wl), ha, (1 - c) * hl, hl)
                _rcopy(got, got, s_fwd.at[3 * o + k], r_fwd.at[3 * o + k], sibling).wait_recv()
        for cp in sends:
            cp.wait_send()

    return list(pl.pallas_call(
        body, name=name, in_specs=[ANY] * n, out_specs=[ANY] * n,
        out_shape=[SDS(m.shape, m.dtype) for m in mats], input_output_aliases={k: k for k in range(n)},
        scratch_shapes=[pltpu.SemaphoreType.DMA((3 * n,)), pltpu.SemaphoreType.DMA((3 * n,))],
        compiler_params=pltpu.CompilerParams(has_side_effects=True),
    )(*mats))


def _all_gather_vec(name, v):
    Lv, cv = v.shape

    def body(v_ref, o_ref, loc_sem, s_sem, r_sem):
        x, y, c = _coords()
        jme = 2 * x + y
        chips = _other_chips(x, y)
        mine = _win(o_ref, 1, jme * cv, cv)
        loc = pltpu.make_async_copy(v_ref, mine, loc_sem)
        loc.start()
        sends = []
        for k, (chip, _) in enumerate(chips):
            cp = _rcopy(v_ref, mine, s_sem.at[k], r_sem.at[k], (*chip, c))
            cp.start()
            sends.append(cp)
        for k, (chip, jk) in enumerate(chips):
            got = _win(o_ref, 1, jk * cv, cv)
            _rcopy(got, got, s_sem.at[k], r_sem.at[k], (*chip, c)).wait_recv()
        for cp in sends:
            cp.wait_send()
        loc.wait()

    return pl.pallas_call(
        body, name=name, in_specs=[ANY], out_specs=ANY, out_shape=SDS((Lv, 4 * cv), v.dtype),
        scratch_shapes=[pltpu.SemaphoreType.DMA, pltpu.SemaphoreType.DMA((3,)), pltpu.SemaphoreType.DMA((3,))],
        compiler_params=pltpu.CompilerParams(has_side_effects=True),
    )(v)


def _half_shape(shape, sa):
    R, C = shape
    return (R // 2, C) if sa == 1 else (R, C // 2)


def _rs_sibling(name, dws, sas):
    n = len(dws)

    def body(*refs):
        src, land = refs[:n], refs[n:2 * n]
        s_sem, r_sem = refs[2 * n:]
        x, y, c = _coords()
        sibling = (x, y, 1 - c)
        sends = []
        for o in range(n):
            ha = 1 - sas[o]
            hl = dws[o].shape[ha] // 2
            cp = _rcopy(_win(src[o], ha, (1 - c) * hl, hl), land[o], s_sem.at[o], r_sem.at[o], sibling)
            cp.start()
            sends.append(cp)
        for o in range(n):
            _rcopy(land[o], land[o], s_sem.at[o], r_sem.at[o], sibling).wait_recv()
        for cp in sends:
            cp.wait_send()

    return pl.pallas_call(
        body, name=name, in_specs=[ANY] * n, out_specs=[ANY] * n,
        out_shape=[SDS(_half_shape(d.shape, sa), d.dtype) for d, sa in zip(dws, sas)],
        scratch_shapes=[pltpu.SemaphoreType.DMA((n,)), pltpu.SemaphoreType.DMA((n,))],
        compiler_params=pltpu.CompilerParams(has_side_effects=True),
    )(*dws)


def _rs_piece_shape(p, sa):
    hr, hc = p.shape
    return (hr // N_CHIPS, hc) if sa == 0 else (hr, hc // N_CHIPS)


def _rs_ici_copy(parts, sas, p_refs, slot_refs, o, k, x, y, c):
    sa = sas[o]
    pl_ = parts[o].shape[sa] // N_CHIPS
    chip, jk = _other_chips(x, y)[k]
    return _win(p_refs[o], sa, jk * pl_, pl_), slot_refs[o].at[k], (*chip, c)


def _rs_start(name, parts, sas, after):
    n = len(parts)

    def body(*refs):
        ins = refs[:n]
        s_sem, r_sem = refs[n + 1], refs[n + 2]
        slots = refs[2 * n + 3:3 * n + 3]
        token = refs[3 * n + 3]
        x, y, c = _coords()
        for o in range(n):
            for k in range(3):
                src, dst, dev = _rs_ici_copy(parts, sas, ins, slots, o, k, x, y, c)
                _rcopy(src, dst, s_sem.at[3 * o + k], r_sem.at[3 * o + k], dev).start()
        token[...] = jnp.zeros_like(token)

    out = pl.pallas_call(
        body, name=name,
        out_shape=(pltpu.SemaphoreType.DMA((3 * n,)), pltpu.SemaphoreType.DMA((3 * n,)),
                   *[pltpu.HBM(p.shape, p.dtype) for p in parts],
                   *[pltpu.HBM((3,) + _rs_piece_shape(p, sa), p.dtype) for p, sa in zip(parts, sas)],
                   SDS((8, LANE), F32)),
        in_specs=[HBM] * n + [ANY], out_specs=(SEM, SEM, *[HBM] * (2 * n), TOKEN),
        input_output_aliases={k: 2 + k for k in range(n)},
        compiler_params=pltpu.CompilerParams(has_side_effects=EFFECT),
    )(*[_in_hbm(p) for p in parts], after)
    return out[0], out[1], list(out[2:2 + n]), list(out[2 + n:2 + 2 * n]), out[2 + 2 * n]


def _rs_wait(name, parts, slots, sas, s_sem, r_sem, after):
    n = len(parts)

    def body(*refs):
        p_refs, slot_refs = refs[:n], refs[n:2 * n]
        s_ref, r_ref = refs[2 * n], refs[2 * n + 1]
        x, y, c = _coords()
        for o in range(n):
            for k in range(3):
                src, dst, dev = _rs_ici_copy(parts, sas, p_refs, slot_refs, o, k, x, y, c)
                cp = _rcopy(src, dst, s_ref.at[3 * o + k], r_ref.at[3 * o + k], dev)
                cp.wait_send()
                cp.wait_recv()

    after = list(after) if isinstance(after, (list, tuple)) else [after]
    out = pl.pallas_call(
        body, name=name,
        out_shape=[pltpu.HBM(a.shape, a.dtype) for a in (*parts, *slots)],
        in_specs=[HBM] * (2 * n) + [SEM, SEM] + [ANY] * len(after), out_specs=[HBM] * (2 * n),
        input_output_aliases={k: k for k in range(2 * n)},
        compiler_params=pltpu.CompilerParams(has_side_effects=EFFECT),
    )(*parts, *slots, s_sem, r_sem, *after)
    return list(out[:n]), list(out[n:])


def _rs_share(name, sas, layers, buf_idx, bufs):
    n, nbuf = len(sas), len(bufs)

    def body(*refs):
        out = refs[nbuf:2 * nbuf]
        s_sem, r_sem = refs[2 * nbuf:]
        x, y, c = _coords()
        sibling = (x, y, 1 - c)
        sends = []
        for o in range(n):
            ha = 1 - sas[o]
            hl = bufs[buf_idx[o]].shape[1 + ha] // 2
            mine = _win(out[buf_idx[o]].at[layers[o]], ha, c * hl, hl)
            cp = _rcopy(mine, mine, s_sem.at[o], r_sem.at[o], sibling)
            cp.start()
            sends.append(cp)
        for o in range(n):
            ha = 1 - sas[o]
            hl = bufs[buf_idx[o]].shape[1 + ha] // 2
            got = _win(out[buf_idx[o]].at[layers[o]], ha, (1 - c) * hl, hl)
            _rcopy(got, got, s_sem.at[o], r_sem.at[o], sibling).wait_recv()
        for cp in sends:
            cp.wait_send()

    return pl.pallas_call(
        body, name=name, in_specs=[ANY] * nbuf, out_specs=[ANY] * nbuf,
        out_shape=[SDS(b.shape, b.dtype) for b in bufs],
        input_output_aliases={k: k for k in range(nbuf)},
        scratch_shapes=[pltpu.SemaphoreType.DMA((n,)), pltpu.SemaphoreType.DMA((n,))],
        compiler_params=pltpu.CompilerParams(has_side_effects=True),
    )(*bufs)


def _exchange_all(name, packed):
    R, C = packed.shape

    def body(p_ref, slots, loc_sem, s_sem, r_sem):
        x, y, c = _coords()
        me = 4 * x + 2 * y + c
        loc = pltpu.make_async_copy(p_ref, slots.at[me], loc_sem)
        loc.start()
        peers = []
        for k in range(1, N_DEV):
            px = 1 - x if k & 4 else x
            py = 1 - y if k & 2 else y
            pc = 1 - c if k & 1 else c
            peers.append(((px, py, pc), 4 * px + 2 * py + pc))
        sends = []
        for k, (peer, _) in enumerate(peers):
            cp = _rcopy(p_ref, slots.at[me], s_sem.at[k], r_sem.at[k], peer)
            cp.start()
            sends.append(cp)
        for k, (peer, pid) in enumerate(peers):
            _rcopy(slots.at[pid], slots.at[pid], s_sem.at[k], r_sem.at[k], peer).wait_recv()
        for cp in sends:
            cp.wait_send()
        loc.wait()

    return pl.pallas_call(
        body, name=name, in_specs=[ANY], out_specs=ANY, out_shape=SDS((N_DEV, R, C), packed.dtype),
        scratch_shapes=[pltpu.SemaphoreType.DMA, pltpu.SemaphoreType.DMA((N_DEV - 1,)), pltpu.SemaphoreType.DMA((N_DEV - 1,))],
        compiler_params=pltpu.CompilerParams(has_side_effects=True),
    )(packed)


def _pack(arrays):
    rows = []
    for a in arrays:
        flat = a.reshape(-1).astype(F32)
        pad = (-flat.size) % PACK_TILE
        rows.append(jnp.pad(flat, (0, pad)).reshape(-1, LANE))
    return jnp.concatenate(rows, axis=0)


def _unpack(packed, shapes):
    out, row = [], 0
    for s in shapes:
        size = int(np.prod(s)) if len(s) else 1
        nrows = -(-size // PACK_TILE) * (PACK_TILE // LANE)
        out.append(packed[row:row + nrows].reshape(-1)[:size].reshape(s))
        row += nrows
    return out


BIG_WEIGHTS = {
    "ffn1_w_gate": 1, "ffn1_w_up": 1, "ffn1_w_down": 0, "ffn2_w_gate": 1, "ffn2_w_up": 1, "ffn2_w_down": 0,
    "gmlp_w_in": 1, "gmlp_w_out": 0, "w_kv": 1, "attn_w_q": 1, "attn_w_o": 0,
}
SMALL_WEIGHTS = ("ffn1_norm", "mix_norm", "ffn2_norm", "gmlp_w_s", "gmlp_b_s", "kv_norm", "k_norm", "attn_q_norm")
WEIGHT_ORDER = ("ffn1_norm", "ffn1_w_gate", "ffn1_w_up", "ffn1_w_down", "mix_norm", "ffn2_norm", "ffn2_w_gate",
                "ffn2_w_up", "ffn2_w_down", "gmlp_w_in", "gmlp_v_norm", "gmlp_w_s", "gmlp_b_s", "gmlp_w_out",
                "kv_norm", "w_kv", "k_norm", "attn_w_q", "attn_q_norm", "attn_w_o")


def _ep_all(accs, ex):
    return list(accs)


def _as3d(w):
    return w if w.ndim == 3 else w.reshape((1,) + w.shape)


def kernel(x, ffn1_norm, ffn1_w_gate, ffn1_w_up, ffn1_w_down, mix_norm, ffn2_norm, ffn2_w_gate, ffn2_w_up, ffn2_w_down, gmlp_w_in, gmlp_v_norm, gmlp_w_s, gmlp_b_s, gmlp_w_out, kv_norm, w_kv, k_norm, attn_w_q, attn_q_norm, attn_w_o, loss_target, m_ffn1_norm, m_ffn1_w_gate, m_ffn1_w_up, m_ffn1_w_down, m_mix_norm, m_ffn2_norm, m_ffn2_w_gate, m_ffn2_w_up, m_ffn2_w_down, m_gmlp_w_in, m_gmlp_v_norm, m_gmlp_w_s, m_gmlp_b_s, m_gmlp_w_out, m_kv_norm, m_w_kv, m_k_norm, m_attn_w_q, m_attn_q_norm, m_attn_w_o, v_ffn1_norm, v_ffn1_w_gate, v_ffn1_w_up, v_ffn1_w_down, v_mix_norm, v_ffn2_norm, v_ffn2_w_gate, v_ffn2_w_up, v_ffn2_w_down, v_gmlp_w_in, v_gmlp_v_norm, v_gmlp_w_s, v_gmlp_b_s, v_gmlp_w_out, v_kv_norm, v_w_kv, v_k_norm, v_attn_w_q, v_attn_q_norm, v_attn_w_o):
    P = dict(locals())
    assert x.shape[0] == 1, "one sample per device"
    S, D = x.shape[1], x.shape[2]
    NL = ffn1_norm.shape[0]
    NG = len(DILATIONS)
    HD = attn_w_o.shape[1] * N_CHIPS
    H = HD // HEAD_DIM
    DG = gmlp_w_out.shape[1] * N_CHIPS
    G = DG // GMLP_GROUP_WIDTH
    assert all((S // d) % ATTN_BLOCK == 0 for d in DILATIONS) and S % GMLP_CHUNK == 0
    xs = x.reshape(S, D)
    tgt = loss_target.reshape(S, D)
    c_arr = lax.axis_index("c").astype(jnp.int32).reshape(1)
    chip = 2 * lax.axis_index("x") + lax.axis_index("y")
    chip_arr = chip.astype(jnp.int32).reshape(1)
    kv_layer = N_A_LAYERS - 1


    def layer_weights(l):
        names = [("ffn1_w_gate", l), ("ffn1_w_up", l), ("ffn1_w_down", l), ("ffn2_w_gate", l), ("ffn2_w_up", l), ("ffn2_w_down", l)]
        if l < N_A_LAYERS:
            names += [("gmlp_w_in", l), ("gmlp_w_out", l)]
        else:
            names += [("attn_w_q", l - N_A_LAYERS), ("attn_w_o", l - N_A_LAYERS)]
        if l == kv_layer:
            names += [("w_kv", 0)]
        return names

    W = {}
    ag_open = {}

    def cast_layer(l):
        return [_cast_into_gathered("cast_shard", _as3d(P[n]), li, BIG_WEIGHTS[n], chip_arr) for n, li in layer_weights(l)]

    def ag_begin(l, after, mats):
        names = layer_weights(l)
        sas = [BIG_WEIGHTS[n] for n, _ in names]
        s_sem, r_sem, mats, token = _ag_start(f"ag_start_l{l}", mats, sas, after)
        ag_open[l] = (names, sas, s_sem, r_sem, mats)
        return token

    def ag_finish(l, after):
        names, sas, s_sem, r_sem, mats = ag_open.pop(l)
        mats = _ag_wait(f"ag_wait_l{l}", mats, sas, s_sem, r_sem, after)
        W.update(dict(zip(names, _ag_forward(f"ag_forward_l{l}", mats, sas))))

    ag_token = ag_begin(0, xs, cast_layer(0))
    if NL > 1:
        ag_token = ag_begin(1, ag_token, cast_layer(1))
    cast_ahead = {l: cast_layer(l) for l in range(2, NL)}
    vnorm_full = _all_gather_vec("ag_vnorm", gmlp_v_norm)
    ag_finish(0, [ag_token, vnorm_full] + [m for l in cast_ahead for m in cast_ahead[l]])

    kgain = jnp.tile(k_norm[:, None, :], (1, H, 1)).reshape(1, NG * HD)
    qgain = [jnp.tile(attn_q_norm[j][:, None, :], (1, H, 1)).reshape(1, NG * HD) for j in range(NL - N_A_LAYERS)]
    q_scale = HEAD_DIM ** -0.5
    one = [(0, 0, 0)]

    def ffn_fwd(xc, gamma, wg, wu, wd, dep=None):
        n = _rms_fwd("ffn_norm", xc, gamma, dep)
        g, u, act = _mm("ffn_up", [n], [wg, wu], [(0, 0, 0), (0, 1, 1)], _ep_swiglu, [BF16] * 3)
        (x2,) = _mm("ffn_down", [act], [wd], one, _ep_residual(0.5), [F32], extras=[xc])
        return x2, (xc, n, g, u, act)

    saved = {}
    xc = xs
    for l in range(NL):
        if l > 0:
            ag_finish(l, xc)
        if l + 2 < NL:
            ag_token = ag_begin(l + 2, ag_token, cast_ahead.pop(l + 2))
        xc, saved["f1", l] = ffn_fwd(xc, ffn1_norm[l], W["ffn1_w_gate", l], W["ffn1_w_up", l], W["ffn1_w_down", l], ag_token)
        h = _rms_fwd("mix_norm", xc, mix_norm[l])
        if l < N_A_LAYERS:
            zpre, z = _mm("gmlp_in", [h], [W["gmlp_w_in", l]], one, _ep_gelu, [BF16, F32])
            bias_full = jnp.repeat(gmlp_b_s[l].T, GMLP_GROUP_WIDTH, axis=1)
            gated = _gmlp_gate_fwd("gmlp_gate", z, vnorm_full[l:l + 1], gmlp_w_s[l], bias_full)
            (x2,) = _mm("gmlp_out", [gated], [W["gmlp_w_out", l]], one, _ep_residual(1.0), [F32], extras=[xc])
            saved["mix", l] = (xc, h, zpre, z, gated, bias_full)
        else:
            j = l - N_A_LAYERS
            (q_raw,) = _mm("attn_q", [h], [W["attn_w_q", j]], one, _ep_plain, [F32])
            qn = _head_norm_fwd("q_norm", q_raw, 0, 1, qgain[j], q_scale, False)
            os_, lses = [], []
            for gi, dil in enumerate(DILATIONS):
                o, lse = _attn_fwd(f"attn_fwd_d{dil}", qn, kn, vb, gi, dil)
                os_.append(o)
                lses.append(lse)
            ob = _attn_combine("attn_mix", os_, lses)
            (x2,) = _mm("attn_o", [ob], [W["attn_w_o", j]], one, _ep_residual(1.0), [F32], extras=[xc])
            saved["mix", l] = (xc, h, q_raw, qn, os_, lses, ob)
        xc = x2
        xc, saved["f2", l] = ffn_fwd(xc, ffn2_norm[l], W["ffn2_w_gate", l], W["ffn2_w_up", l], W["ffn2_w_down", l])
        if l == kv_layer:
            kvn = _rms_fwd("kv_norm", xc, kv_norm)
            (kv_raw,) = _mm("kv_proj", [kvn], [W["w_kv", 0]], one, _ep_plain, [F32])
            kn, vb = _head_norm_fwd("k_norm", kv_raw, 0, 2, kgain, 1.0, True)
            saved["kv"] = (xc, kvn, kv_raw)

    dx, dxb, loss_rows = _loss_grad("loss", xc, tgt, 0.5)
    dW = {}
    dsmall = {n: [None] * P[n].shape[0] for n in ("ffn1_norm", "mix_norm", "ffn2_norm")}
    dsmall.update(gmlp_w_s=[None] * N_A_LAYERS, gmlp_b_s=[None] * N_A_LAYERS, gmlp_v_norm=[None] * N_A_LAYERS,
                  attn_q_norm=[None] * (NL - N_A_LAYERS))
    dks = [[] for _ in DILATIONS]
    dvs = [[] for _ in DILATIONS]

    names_big = list(BIG_WEIGHTS)
    gbuf = [lax.empty(_as3d(P[n]).shape, F32) for n in names_big]
    where = (c_arr, chip_arr)
    rs_open = {}

    def rs_begin(l, after):
        names = layer_weights(l)
        sas = [BIG_WEIGHTS[n] for n, _ in names]
        dws = [dW[k] for k in names]
        lands = _rs_sibling(f"rs_sibling_l{l}", dws, sas)
        parts = [_add_half("rs_add_half", d, ln, sa, c_arr) for d, ln, sa in zip(dws, lands, sas)]
        s_sem, r_sem, parts, slots, token = _rs_start(f"rs_start_l{l}", parts, sas, after)
        rs_open[l] = (names, sas, s_sem, r_sem, parts, slots)
        return token

    def rs_finish(l, after):
        names, sas, s_sem, r_sem, parts, slots = rs_open.pop(l)
        parts, slots = _rs_wait(f"rs_wait_l{l}", parts, slots, sas, s_sem, r_sem, after)
        for p, s, sa, (n, li) in zip(parts, slots, sas, names):
            bi = names_big.index(n)
            gbuf[bi] = _sum_into("rs_sum_chips", p, s, gbuf[bi], li, sa, where)
        gbuf[:] = _rs_share(f"rs_share_l{l}", sas, [li for _, li in names], [names_big.index(n) for n, _ in names], gbuf)

    def ffn_bwd(dx, dxb, sv, gamma, wg, wu, wd, key, l, next_scale, dep=None):
        xin, n, g, u, act = sv
        dg, du = _mm("ffn_dact", [dxb], [wd], one, _ep_swiglu_bwd, [BF16, BF16], tb=True, extras=[g, u], dep=dep)
        (dW[key + "_w_down", l],) = _mm("ffn_dwd", [act], [dxb], one, _ep_plain, [BF16], ta=True)
        dW[key + "_w_gate", l], dW[key + "_w_up", l] = _mm("ffn_dwgu", [n], [dg, du], [(0, 0, 0), (0, 1, 1)], _ep_all, [BF16, BF16], ta=True)
        (dn,) = _mm("ffn_dn", [dg, du], [wg, wu], [(0, 0, 0), (1, 1, 0)], _ep_plain, [F32], tb=True)
        dx, dxb, dsmall[key + "_norm"][l] = _rms_bwd("ffn_norm_bwd", xin, gamma, dn, dx, next_scale)
        return dx, dxb

    rs_token = None
    for l in reversed(range(NL)):
        if l == kv_layer:
            x_kv, kvn, kv_raw = saved["kv"]
            dkv_raw, dkgain = _head_norm_bwd("k_norm_bwd", kv_raw, 0, 2, kgain, 1.0, dks, dvs)
            (dW["w_kv", 0],) = _mm("kv_dw", [kvn], [dkv_raw], one, _ep_plain, [BF16], ta=True, dep=rs_token)
            (dkvn,) = _mm("kv_dn", [dkv_raw], [W["w_kv", 0]], one, _ep_plain, [F32], tb=True)
            dx, dxb, dkvnorm = _rms_bwd("kv_norm_bwd", x_kv, kv_norm, dkvn, dx, 0.5)
        dx, dxb = ffn_bwd(dx, dxb, saved["f2", l], ffn2_norm[l], W["ffn2_w_gate", l], W["ffn2_w_up", l], W["ffn2_w_down", l], "ffn2", l, 1.0, rs_token)
        if l < N_A_LAYERS:
            xin, h, zpre, z, gated, bias_full = saved["mix", l]
            (dW["gmlp_w_out", l],) = _mm("gmlp_dwout", [gated], [dxb], one, _ep_plain, [BF16], ta=True)
            (dgated,) = _mm("gmlp_dgated", [dxb], [W["gmlp_w_out", l]], one, _ep_plain, [F32], tb=True)
            dzpre, dws, dbacc, dvn = _gmlp_gate_bwd("gmlp_gate_bwd", z, zpre, dgated, vnorm_full[l:l + 1], gmlp_w_s[l],
                                                    jnp.swapaxes(gmlp_w_s[l], 1, 2), bias_full)
            (dW["gmlp_w_in", l],) = _mm("gmlp_dwin", [h], [dzpre], one, _ep_plain, [BF16], ta=True)
            (dh,) = _mm("gmlp_dh", [dzpre], [W["gmlp_w_in", l]], one, _ep_plain, [F32], tb=True)
            dsmall["gmlp_w_s"][l] = dws
            dsmall["gmlp_b_s"][l] = dbacc.reshape(GMLP_CHUNK, G, GMLP_GROUP_WIDTH).sum(-1).T
            dsmall["gmlp_v_norm"][l] = dvn.reshape(DG)
        else:
            j = l - N_A_LAYERS
            xin, h, q_raw, qn, os_, lses, ob = saved["mix", l]
            (dW["attn_w_o", j],) = _mm("attn_dwo", [ob], [dxb], one, _ep_plain, [BF16], ta=True)
            (d_ob,) = _mm("attn_dob", [dxb], [W["attn_w_o", j]], one, _ep_plain, [F32], tb=True)
            lse_t, dl_rows, dob = _attn_combine("attn_mix_bwd", os_, lses, d_o=d_ob)
            dqs = []
            for gi, dil in enumerate(DILATIONS):
                dq, dk, dv = _attn_bwd(f"attn_bwd_d{dil}", qn, kn, vb, dob, lse_t, dl_rows, gi, dil)
                dqs.append([dq])
                dks[gi].append(dk)
                dvs[gi].append(dv)
            dq_raw, dqgain = _head_norm_bwd("q_norm_bwd", q_raw, 0, 1, qgain[j], q_scale, dqs, None)
            (dW["attn_w_q", j],) = _mm("attn_dwq", [h], [dq_raw], one, _ep_plain, [BF16], ta=True)
            (dh,) = _mm("attn_dh", [dq_raw], [W["attn_w_q", j]], one, _ep_plain, [F32], tb=True)
            dsmall["attn_q_norm"][j] = dqgain.reshape(NG, H, HEAD_DIM).sum(1)
        dx, dxb, dsmall["mix_norm"][l] = _rms_bwd("mix_norm_bwd", xin, mix_norm[l], dh, dx, 0.5)
        dx, dxb = ffn_bwd(dx, dxb, saved["f1", l], ffn1_norm[l], W["ffn1_w_gate", l], W["ffn1_w_up", l], W["ffn1_w_down", l], "ffn1", l, 0.5)
        rs_token = rs_begin(l, dx)
        if l + 1 < NL:
            rs_finish(l + 1, rs_token)
    grad_x = dx.reshape(x.shape)

    adam_out = {n: [lax.empty(_as3d(P[n]).shape, F32) for _ in range(4)] for n in names_big}

    def adam_layer(l):
        for n, li in layer_weights(l):
            adam_out[n] = _adamw_layer("adamw", gbuf[names_big.index(n)], _as3d(P[n]), _as3d(P["m_" + n]),
                                       _as3d(P["v_" + n]), li, adam_out[n])

    for l in reversed(range(1, NL)):
        adam_layer(l)
    rs_finish(0, [rs_token] + [adam_out[n][0] for n in names_big])
    adam_layer(0)
    big_out = {n: [o.reshape(P[n].shape) for o in adam_out[n]] for n in names_big}

    loss_part = (0.5 / D) * jnp.sum(loss_rows)
    small_grads = [jnp.stack([g.reshape(P[n].shape[1:]) for g in dsmall[n]]) for n in ("ffn1_norm", "mix_norm", "ffn2_norm", "gmlp_w_s", "gmlp_b_s")]
    small_grads += [dkvnorm.reshape(kv_norm.shape), dkgain.reshape(NG, H, HEAD_DIM).sum(1), jnp.stack(dsmall["attn_q_norm"])]
    vn_grad_full = jnp.stack(dsmall["gmlp_v_norm"])
    packed = _pack(small_grads + [vn_grad_full, loss_part.reshape(1)])
    total = _sum_slots("sum_devices", _exchange_all("exchange_small", packed))
    shapes = [P[n].shape for n in SMALL_WEIGHTS] + [vn_grad_full.shape, (1,)]
    red = _unpack(total, shapes)
    loss = red[-1].reshape(())
    cv = gmlp_v_norm.shape[1]
    vn_grad = lax.dynamic_slice_in_dim(red[-2], chip * cv, cv, axis=1)
    names_small = list(SMALL_WEIGHTS) + ["gmlp_v_norm"]
    g_small = red[:len(SMALL_WEIGHTS)] + [vn_grad]
    outs = _adamw("adamw_small", _pack(g_small), _pack([P[n] for n in names_small]),
                  _pack([P["m_" + n] for n in names_small]), _pack([P["v_" + n] for n in names_small]))
    small_shapes = [P[n].shape for n in names_small]
    small_out = {n: [] for n in names_small}
    for o in outs:
        for n, a in zip(names_small, _unpack(o, small_shapes)):
            small_out[n].append(a)

    res = {**big_out, **small_out}
    return (loss, grad_x, *[res[n][0] for n in WEIGHT_ORDER], *[res[n][1] for n in WEIGHT_ORDER],
            *[res[n][2] for n in WEIGHT_ORDER], *[res[n][3] for n in WEIGHT_ORDER])
```

```python
import numpy as np
import jax
import jax.numpy as jnp
from jax import lax
from jax.experimental import pallas as pl
from jax.experimental.pallas import tpu as pltpu

F32 = jnp.float32
BF16 = jnp.bfloat16
SDS = jax.ShapeDtypeStruct

EPS = 1e-6
HEAD_DIM = 128
GMLP_CHUNK = 128
GMLP_GROUP_WIDTH = 128
DILATIONS = (1, 4, 16)
ATTN_BLOCK = 128
N_A_LAYERS = 2
ADAM_LR, ADAM_B1, ADAM_B2, ADAM_EPS, ADAM_WD, ADAM_STEP = 0.001, 0.9, 0.999, 1e-08, 0.01, 10

N_CHIPS = 4
N_DEV = 8
MESH = pl.DeviceIdType.MESH
V7X_VMEM_BYTES = 64 * 2 ** 20
VMEM_CEILING = V7X_VMEM_BYTES - 6 * 2 ** 20
VMEM_BLOCK_BUDGET = 38 * 2 ** 20
LANE = 128
BF16_ROWS = 16
PACK_TILE = 8 * LANE


def _cp(sem=None, vmem=None):
    kw = {}
    if sem is not None:
        kw["dimension_semantics"] = sem
    if vmem is not None:
        kw["vmem_limit_bytes"] = int(min(max(vmem, 16 * 2 ** 20), VMEM_CEILING))
    return pltpu.CompilerParams(**kw)


def _pick(dim, cands):
    for c in cands:
        if c <= dim and dim % c == 0:
            return c
    return dim


def _row_tile(rows, row_bytes, target=2 ** 20):
    t = 1024
    while t > 8 and (t * row_bytes > target or rows % t):
        t //= 2
    return t if rows % t == 0 else rows


def _sigmoid(x):
    return 1.0 / (1.0 + jnp.exp(-x))


_GELU_C = 0.7978845608028654
_GELU_A = 0.044715


def _gelu(x):
    return 0.5 * x * (1.0 + jnp.tanh(_GELU_C * (x + _GELU_A * (x * x * x))))


def _gelu_grad(x):
    t = jnp.tanh(_GELU_C * (x + _GELU_A * (x * x * x)))
    return 0.5 * (1.0 + t) + 0.5 * x * (1.0 - t * t) * (_GELU_C * (1.0 + 3.0 * _GELU_A * x * x))


def _mm_tiles(M, N, K, n_a, n_b, n_acc, io_bytes):
    tks = [K] + [d for d in (4096, 3072, 2816, 2048, 1024, 512, 256, 128) if d < K and K % d == 0]
    tms = [t for t in (1024, 512, 256, 128) if M % t == 0] or [M]
    tns = [t for t in (512, 256, 128) if N % t == 0] or [N]
    best = None
    for tk in tks:
        for tm in tms:
            for tn in tns:
                est = 2 * 2 * (n_a * tm * tk + n_b * tk * tn) + 2 * tm * tn * io_bytes
                est += n_acc * tm * tn * 4 * (2 if tk < K else 1)
                if est <= VMEM_BLOCK_BUDGET:
                    return tm, tn, tk, est
                if best is None or est < best[3]:
                    best = (tm, tn, tk, est)
    return best


def _mm(name, a_list, b_list, terms, epilogue, out_dtypes, *, ta=False, tb=False, extras=(), dep=None):
    n_acc = 1 + max(t[2] for t in terms)
    a0, b0 = a_list[0], b_list[0]
    (K, M) = a0.shape if ta else a0.shape[::-1]
    N = b0.shape[0] if tb else b0.shape[1]
    io_bytes = sum(jnp.dtype(e.dtype).itemsize for e in extras) + sum(jnp.dtype(d).itemsize for d in out_dtypes)
    tm, tn, tk, est = _mm_tiles(M, N, K, len(a_list), len(b_list), n_acc, io_bytes)
    nk = K // tk
    na, nb, ne, no = len(a_list), len(b_list), len(extras), len(out_dtypes)
    deps = [] if dep is None else [dep]
    nd = len(deps)
    dn = (((0 if ta else 1,), (1 if tb else 0,)), ((), ()))

    def body(*refs):
        a_refs = refs[:na]
        b_refs = refs[na:na + nb]
        e_refs = refs[na + nb:na + nb + ne]
        o_refs = refs[na + nb + ne + nd:na + nb + ne + nd + no]
        acc_refs = refs[na + nb + ne + nd + no:]
        parts = [None] * n_acc
        for ai, bi, qi in terms:
            d = lax.dot_general(a_refs[ai][...], b_refs[bi][...], dn, preferred_element_type=F32)
            parts[qi] = d if parts[qi] is None else parts[qi] + d

        def finish(accs):
            outs = epilogue(accs, [e[...] for e in e_refs])
            for o_ref, o in zip(o_refs, outs):
                o_ref[...] = o.astype(o_ref.dtype)

        if nk == 1:
            finish(parts)
        else:
            k = pl.program_id(2)

            @pl.when(k == 0)
            def _():
                for q in range(n_acc):
                    acc_refs[q][...] = parts[q]

            @pl.when(k > 0)
            def _():
                for q in range(n_acc):
                    acc_refs[q][...] += parts[q]

            @pl.when(k == nk - 1)
            def _():
                finish([acc_refs[q][...] for q in range(n_acc)])

    a_spec = pl.BlockSpec((tk, tm), lambda i, j, k: (k, i)) if ta else pl.BlockSpec((tm, tk), lambda i, j, k: (i, k))
    b_spec = pl.BlockSpec((tn, tk), lambda i, j, k: (j, k)) if tb else pl.BlockSpec((tk, tn), lambda i, j, k: (k, j))
    e_spec = pl.BlockSpec((tm, tn), lambda i, j, k: (i, j))
    outs = pl.pallas_call(
        body, name=name, grid=(M // tm, N // tn, nk),
        in_specs=[a_spec] * na + [b_spec] * nb + [e_spec] * ne + [pl.BlockSpec((8, LANE), lambda i, j, k: (0, 0))] * nd,
        out_specs=[e_spec] * no,
        out_shape=[SDS((M, N), d) for d in out_dtypes],
        scratch_shapes=[pltpu.VMEM((tm, tn), F32) for _ in range(n_acc)] if nk > 1 else [],
        compiler_params=_cp(("parallel", "parallel", "arbitrary"), est + 12 * 2 ** 20),
    )(*a_list, *b_list, *extras, *deps)
    return outs


def _ep_plain(accs, ex):
    return [accs[0]]


def _ep_swiglu(accs, ex):
    g, u = accs
    s = _sigmoid(g)
    sg = g * s
    return [u * (s + sg * (1.0 - s)), sg, sg * u]


def _ep_swiglu_bwd(accs, ex):
    da = accs[0]
    return [da * ex[0].astype(F32), da * ex[1].astype(F32)]


def _ep_gelu(accs, ex):
    return [_gelu_grad(accs[0]), _gelu(accs[0])]


def _ep_residual(scale):
    def ep(accs, ex):
        return [ex[0] + scale * accs[0]]
    return ep


def _rms_fwd(name, x, gamma, dep=None):
    S, D = x.shape
    tr = _row_tile(S, D * 4)
    deps = [] if dep is None else [dep]

    def body(x_ref, g_ref, *rest):
        o_ref = rest[-1]
        xv = x_ref[...]
        r = lax.rsqrt(jnp.mean(xv * xv, axis=-1, keepdims=True) + EPS)
        o_ref[...] = (xv * r * g_ref[...]).astype(BF16)

    return pl.pallas_call(
        body, name=name, grid=(S // tr,),
        in_specs=[pl.BlockSpec((tr, D), lambda i: (i, 0)), pl.BlockSpec((1, D), lambda i: (0, 0))]
        + [pl.BlockSpec((8, LANE), lambda i: (0, 0))] * len(deps),
        out_specs=pl.BlockSpec((tr, D), lambda i: (i, 0)),
        out_shape=SDS((S, D), BF16),
        compiler_params=_cp(("parallel",), 32 * 2 ** 20),
    )(x, gamma.reshape(1, D), *deps)


def _rms_bwd(name, x, gamma, dn, dx_in, out_scale):
    S, D = x.shape
    tr = _row_tile(S, D * 4, 2 ** 19)

    def body(x_ref, g_ref, dn_ref, dxi_ref, dxo_ref, dxb_ref, dg_ref):
        i = pl.program_id(0)
        xv = x_ref[...]
        r = lax.rsqrt(jnp.mean(xv * xv, axis=-1, keepdims=True) + EPS)
        xh = xv * r
        dnv = dn_ref[...]
        dxh = dnv * g_ref[...]
        dx = dxi_ref[...] + r * (dxh - xh * jnp.mean(dxh * xh, axis=-1, keepdims=True))
        dxo_ref[...] = dx
        dxb_ref[...] = (out_scale * dx).astype(BF16)
        part = jnp.sum(dnv * xh, axis=0, keepdims=True)

        @pl.when(i == 0)
        def _():
            dg_ref[...] = part

        @pl.when(i > 0)
        def _():
            dg_ref[...] += part

    row = pl.BlockSpec((tr, D), lambda i: (i, 0))
    vec = pl.BlockSpec((1, D), lambda i: (0, 0))
    return pl.pallas_call(
        body, name=name, grid=(S // tr,),
        in_specs=[row, vec, row, row], out_specs=[row, row, vec],
        out_shape=[SDS((S, D), F32), SDS((S, D), BF16), SDS((1, D), F32)],
        compiler_params=_cp(("arbitrary",), 40 * 2 ** 20),
    )(x, gamma.reshape(1, D), dn, dx_in)


def _loss_grad(name, y, t, out_scale):
    S, D = y.shape
    tr = _row_tile(S, D * 4, 2 ** 19)
    inv_d = 1.0 / D

    def body(y_ref, t_ref, dy_ref, dyb_ref, ls_ref):
        i = pl.program_id(0)
        e = y_ref[...] - t_ref[...]
        dy = e * inv_d
        dy_ref[...] = dy
        dyb_ref[...] = (out_scale * dy).astype(BF16)
        part = jnp.sum(e * e, axis=0, keepdims=True)

        @pl.when(i == 0)
        def _():
            ls_ref[...] = part

        @pl.when(i > 0)
        def _():
            ls_ref[...] += part

    row = pl.BlockSpec((tr, D), lambda i: (i, 0))
    vec = pl.BlockSpec((1, D), lambda i: (0, 0))
    return pl.pallas_call(
        body, name=name, grid=(S // tr,),
        in_specs=[row, row], out_specs=[row, row, vec],
        out_shape=[SDS((S, D), F32), SDS((S, D), BF16), SDS((1, D), F32)],
        compiler_params=_cp(("arbitrary",), 32 * 2 ** 20),
    )(y, t)


def _head_mean(v):
    return jnp.mean(v, axis=-1, keepdims=True)


def _head_norm_fwd(name, raw, part, n_parts, gain_t, scale, with_pass):
    S = raw.shape[0]
    W = raw.shape[1] // n_parts
    nh = W // HEAD_DIM
    tr = _row_tile(S, W * (4 + 2) * (2 if with_pass else 1), 4 * 2 ** 20)

    def body(*refs):
        if with_pass:
            x_ref, p_ref, g_ref, o_ref, po_ref = refs
            po_ref[...] = p_ref[...].astype(BF16)
        else:
            x_ref, g_ref, o_ref = refs
        for h in range(nh):
            sl = slice(h * HEAD_DIM, (h + 1) * HEAD_DIM)
            xv = x_ref[:, sl]
            r = lax.rsqrt(_head_mean(xv * xv) + EPS)
            o_ref[:, sl] = (xv * r * g_ref[:, sl] * scale).astype(BF16)

    xspec = pl.BlockSpec((tr, W), lambda i: (i, part))
    ospec = pl.BlockSpec((tr, W), lambda i: (i, 0))
    gspec = pl.BlockSpec((1, W), lambda i: (0, 0))
    if with_pass:
        in_specs = [xspec, pl.BlockSpec((tr, W), lambda i: (i, 1)), gspec]
        args = (raw, raw, gain_t)
        out_specs, out_shape = [ospec, ospec], [SDS((S, W), BF16), SDS((S, W), BF16)]
    else:
        in_specs, args = [xspec, gspec], (raw, gain_t)
        out_specs, out_shape = ospec, SDS((S, W), BF16)
    return pl.pallas_call(
        body, name=name, grid=(S // tr,), in_specs=in_specs, out_specs=out_specs, out_shape=out_shape,
        compiler_params=_cp(("parallel",), 40 * 2 ** 20),
    )(*args)


def _head_norm_bwd(name, raw, part, n_parts, gain_t, scale, dy_groups, pass_groups):
    S = raw.shape[0]
    W = raw.shape[1] // n_parts
    ng = len(dy_groups)
    HD = W // ng
    nhg = HD // HEAD_DIM
    n_dy = [len(g) for g in dy_groups]
    n_ps = [len(g) for g in pass_groups] if pass_groups is not None else []
    flat = [a for g in dy_groups for a in g] + ([a for g in pass_groups for a in g] if pass_groups is not None else [])
    out_w = 2 * W if pass_groups is not None else W
    tr = _row_tile(S, W * 4 + sum(HD * jnp.dtype(a.dtype).itemsize for a in flat) + out_w * 2, 8 * 2 ** 20)

    def body(*refs):
        x_ref, g_ref = refs[0], refs[1]
        d_refs = refs[2:2 + len(flat)]
        o_ref, dg_ref = refs[2 + len(flat)], refs[3 + len(flat)]
        i = pl.program_id(0)

        @pl.when(i == 0)
        def _():
            dg_ref[...] = jnp.zeros_like(dg_ref)

        pos = 0
        for gi in range(ng):
            dys = d_refs[pos:pos + n_dy[gi]]
            pos += n_dy[gi]
            for h in range(nhg):
                sl = slice(gi * HD + h * HEAD_DIM, gi * HD + (h + 1) * HEAD_DIM)
                hs = slice(h * HEAD_DIM, (h + 1) * HEAD_DIM)
                dy = dys[0][:, hs].astype(F32)
                for extra in dys[1:]:
                    dy = dy + extra[:, hs].astype(F32)
                xv = x_ref[:, sl]
                r = lax.rsqrt(_head_mean(xv * xv) + EPS)
                xh = xv * r
                dxh = dy * (g_ref[:, sl] * scale)
                o_ref[:, sl] = (r * (dxh - xh * _head_mean(dxh * xh))).astype(BF16)
                dg_ref[:, sl] += jnp.sum(dy * xh, axis=0, keepdims=True) * scale
        for gi in range(len(n_ps)):
            ps = d_refs[pos:pos + n_ps[gi]]
            pos += n_ps[gi]
            acc = ps[0][...].astype(F32)
            for extra in ps[1:]:
                acc = acc + extra[...].astype(F32)
            o_ref[:, W + gi * HD:W + (gi + 1) * HD] = acc.astype(BF16)

    dspec = pl.BlockSpec((tr, HD), lambda i: (i, 0))
    return pl.pallas_call(
        body, name=name, grid=(S // tr,),
        in_specs=[pl.BlockSpec((tr, W), lambda i: (i, part)), pl.BlockSpec((1, W), lambda i: (0, 0))] + [dspec] * len(flat),
        out_specs=[pl.BlockSpec((tr, out_w), lambda i: (i, 0)), pl.BlockSpec((1, W), lambda i: (0, 0))],
        out_shape=[SDS((S, out_w), BF16), SDS((1, W), F32)],
        compiler_params=_cp(("arbitrary",), 48 * 2 ** 20),
    )(raw, gain_t, *flat)


def _tril_mask():
    r = lax.broadcasted_iota(jnp.int32, (GMLP_CHUNK, GMLP_CHUNK), 0)
    c = lax.broadcasted_iota(jnp.int32, (GMLP_CHUNK, GMLP_CHUNK), 1)
    return r >= c


def _gmlp_gate_fwd(name, z, v_norm, w_s, bias_full):
    S, DG2 = z.shape
    DG = DG2 // 2
    G = DG // GMLP_GROUP_WIDTH
    C = GMLP_CHUNK

    def body(u_ref, v_ref, vn_ref, ws_ref, b_ref, o_ref):
        mask = _tril_mask()
        v = v_ref[...]
        r = lax.rsqrt(jnp.mean(v * v, axis=-1, keepdims=True) + EPS)
        vn = (v * r * vn_ref[...]).astype(BF16)
        for g in range(G):
            sl = slice(g * GMLP_GROUP_WIDTH, (g + 1) * GMLP_GROUP_WIDTH)
            wm = jnp.where(mask, ws_ref[g], 0.0).astype(BF16)
            sv = jnp.dot(wm, vn[:, sl], preferred_element_type=F32) + b_ref[:, sl]
            o_ref[:, sl] = (u_ref[:, sl] * sv).astype(BF16)

    return pl.pallas_call(
        body, name=name, grid=(S // C,),
        in_specs=[pl.BlockSpec((C, DG), lambda i: (i, 0)), pl.BlockSpec((C, DG), lambda i: (i, 1)),
                  pl.BlockSpec((1, DG), lambda i: (0, 0)), pl.BlockSpec((G, C, C), lambda i: (0, 0, 0)),
                  pl.BlockSpec((C, DG), lambda i: (0, 0))],
        out_specs=pl.BlockSpec((C, DG), lambda i: (i, 0)),
        out_shape=SDS((S, DG), BF16),
        compiler_params=_cp(("parallel",), 32 * 2 ** 20),
    )(z, z, v_norm, w_s, bias_full)


def _gmlp_gate_bwd(name, z, zpre, dgated, v_norm, w_s, w_s_t, bias_full):
    S, DG2 = z.shape
    DG = DG2 // 2
    G = DG // GMLP_GROUP_WIDTH
    C = GMLP_CHUNK

    def body(z_ref, zp_ref, dg_ref, vn_ref, ws_ref, wst_ref, b_ref, dz_ref, dws_ref, db_ref, dvn_ref):
        i = pl.program_id(0)
        mask = _tril_mask()
        mask_t = jnp.logical_not(mask) | (lax.broadcasted_iota(jnp.int32, (C, C), 0) == lax.broadcasted_iota(jnp.int32, (C, C), 1))
        u = z_ref[:, :DG]
        v = z_ref[:, DG:]
        r = lax.rsqrt(jnp.mean(v * v, axis=-1, keepdims=True) + EPS)
        vh = v * r
        gain = vn_ref[...]
        vn = (vh * gain).astype(BF16)
        dgt = dg_ref[...]

        @pl.when(i == 0)
        def _():
            dws_ref[...] = jnp.zeros_like(dws_ref)
            db_ref[...] = jnp.zeros_like(db_ref)
            dvn_ref[...] = jnp.zeros_like(dvn_ref)

        dvn_parts = []
        for g in range(G):
            sl = slice(g * GMLP_GROUP_WIDTH, (g + 1) * GMLP_GROUP_WIDTH)
            wm = jnp.where(mask, ws_ref[g], 0.0).astype(BF16)
            wmt = jnp.where(mask_t, wst_ref[g], 0.0).astype(BF16)
            sv = jnp.dot(wm, vn[:, sl], preferred_element_type=F32) + b_ref[:, sl]
            dgs = dgt[:, sl]
            du = dgs * sv
            dsv = dgs * u[:, sl]
            db_ref[:, sl] += dsv
            dsv_b = dsv.astype(BF16)
            dws = lax.dot_general(dsv_b, vn[:, sl], (((1,), (1,)), ((), ())), preferred_element_type=F32)
            dws_ref[g] += jnp.where(mask, dws, 0.0)
            dvn_parts.append(jnp.dot(wmt, dsv_b, preferred_element_type=F32))
            dz_ref[:, sl] = (du * zp_ref[:, sl].astype(F32)).astype(BF16)
        dvn_full = jnp.concatenate(dvn_parts, axis=1)
        dvn_ref[...] += jnp.sum(dvn_full * vh, axis=0, keepdims=True)
        dxh = dvn_full * gain
        dv = r * (dxh - vh * jnp.mean(dxh * vh, axis=-1, keepdims=True))
        dz_ref[:, DG:] = (dv * zp_ref[:, DG:].astype(F32)).astype(BF16)

    full = pl.BlockSpec((C, DG2), lambda i: (i, 0))
    wspec = pl.BlockSpec((G, C, C), lambda i: (0, 0, 0))
    return pl.pallas_call(
        body, name=name, grid=(S // C,),
        in_specs=[full, full, pl.BlockSpec((C, DG), lambda i: (i, 0)), pl.BlockSpec((1, DG), lambda i: (0, 0)),
                  wspec, wspec, pl.BlockSpec((C, DG), lambda i: (0, 0))],
        out_specs=[full, wspec, pl.BlockSpec((C, DG), lambda i: (0, 0)), pl.BlockSpec((1, DG), lambda i: (0, 0))],
        out_shape=[SDS((S, DG2), BF16), SDS((G, C, C), F32), SDS((C, DG), F32), SDS((1, DG), F32)],
        compiler_params=_cp(("arbitrary",), 40 * 2 ** 20),
    )(z, zpre, dgated, v_norm, w_s, w_s_t, bias_full)


def _alibi_slopes(n_heads):
    return [float(v) for v in np.exp2(np.float32(-8.0) * np.arange(1, n_heads + 1, dtype=np.float32) / np.float32(n_heads))]


def _dil_view(arr, dil):
    S, C = arr.shape
    return arr if dil == 1 else arr.reshape(S // dil, dil * C)


def _dil_spec(dil, HD, ncb, cb, bmap):
    return pl.BlockSpec((ATTN_BLOCK, HD), lambda r, b: (bmap(b), r * ncb + cb))


def _group_view(arr, gi, dil, HD):
    if dil == 1:
        return arr, arr.shape[1] // HD, gi
    return _dil_view(arr[:, gi * HD:(gi + 1) * HD], dil), 1, 0


def _dil_unview(arr, S):
    return arr.reshape(S, arr.size // S)


def _attn_mask(b):
    qi = lax.broadcasted_iota(jnp.int32, (ATTN_BLOCK, 2 * ATTN_BLOCK), 0)
    kj = lax.broadcasted_iota(jnp.int32, (ATTN_BLOCK, 2 * ATTN_BLOCK), 1)
    delta = qi + ATTN_BLOCK - kj
    valid = (delta >= 0) & (delta <= ATTN_BLOCK) & ((kj >= ATTN_BLOCK) | (b > 0))
    return valid, delta.astype(F32)


def _attn_fwd(name, qn, kn, vb, gi, dil):
    S, C = qn.shape
    NG = len(DILATIONS)
    HD = C // NG
    H = HD // HEAD_DIM
    L = S // dil
    nb = L // ATTN_BLOCK
    slopes = _alibi_slopes(H)

    def body(q_ref, kc_ref, kp_ref, vc_ref, vp_ref, o_ref, lse_ref):
        b = pl.program_id(1)
        valid, delta = _attn_mask(b)
        dist = delta * float(dil)
        for h in range(H):
            sl = slice(h * HEAD_DIM, (h + 1) * HEAD_DIM)
            k = jnp.concatenate([kp_ref[:, sl], kc_ref[:, sl]], axis=0)
            v = jnp.concatenate([vp_ref[:, sl], vc_ref[:, sl]], axis=0)
            s = lax.dot_general(q_ref[:, sl], k, (((1,), (1,)), ((), ())), preferred_element_type=F32)
            s = jnp.where(valid, s - slopes[h] * dist, -jnp.inf)
            m = jnp.max(s, axis=-1, keepdims=True)
            p = jnp.exp(s - m)
            l = jnp.sum(p, axis=-1, keepdims=True)
            o = jnp.dot(p.astype(BF16), v, preferred_element_type=F32)
            o_ref[:, sl] = o / l
            lse_ref[:, sl] = jnp.broadcast_to(m + jnp.log(l), (ATTN_BLOCK, HEAD_DIM))

    cur = lambda b: b
    prev = lambda b: jnp.maximum(b - 1, 0)
    (qv, ncb, cb), (kv, _, _), (vv, _, _) = (_group_view(a, gi, dil, HD) for a in (qn, kn, vb))
    view_shape = (L, dil * HD)
    o, lse = pl.pallas_call(
        body, name=name, grid=(dil, nb),
        in_specs=[_dil_spec(dil, HD, ncb, cb, cur), _dil_spec(dil, HD, ncb, cb, cur), _dil_spec(dil, HD, ncb, cb, prev),
                  _dil_spec(dil, HD, ncb, cb, cur), _dil_spec(dil, HD, ncb, cb, prev)],
        out_specs=[_dil_spec(dil, HD, 1, 0, cur), _dil_spec(dil, HD, 1, 0, cur)],
        out_shape=[SDS(view_shape, F32), SDS(view_shape, F32)],
        compiler_params=_cp(("parallel", "parallel"), 32 * 2 ** 20),
    )(qv, kv, kv, vv, vv)
    return _dil_unview(o, S), _dil_unview(lse, S)


def _attn_combine(name, o_list, lse_list, d_o=None):
    S, HD = o_list[0].shape
    H = HD // HEAD_DIM
    ng = len(o_list)
    tr = _row_tile(S, HD * 4, 2 ** 19)

    def body(*refs):
        o_refs = refs[:ng]
        l_refs = refs[ng:2 * ng]
        ls = [r[...] for r in l_refs]
        m = ls[0]
        for t in ls[1:]:
            m = jnp.maximum(m, t)
        es = [jnp.exp(t - m) for t in ls]
        z = es[0]
        for t in es[1:]:
            z = z + t
        o = es[0] * o_refs[0][...]
        for e, r in zip(es[1:], o_refs[1:]):
            o = o + e * r[...]
        o = o / z
        if d_o is None:
            refs[2 * ng][...] = o.astype(BF16)
        else:
            do_ref, lse_ref, dl_ref, dob_ref = refs[2 * ng:]
            dov = do_ref[...]
            lse_ref[...] = m + jnp.log(z)
            dob_ref[...] = dov.astype(BF16)
            prod = dov * o
            for h in range(H):
                sl = slice(h * HEAD_DIM, (h + 1) * HEAD_DIM)
                dl_ref[:, sl] = jnp.broadcast_to(jnp.sum(prod[:, sl], axis=-1, keepdims=True), (tr, HEAD_DIM))

    row = pl.BlockSpec((tr, HD), lambda i: (i, 0))
    if d_o is None:
        return pl.pallas_call(
            body, name=name, grid=(S // tr,), in_specs=[row] * (2 * ng), out_specs=row,
            out_shape=SDS((S, HD), BF16), compiler_params=_cp(("parallel",), 40 * 2 ** 20),
        )(*o_list, *lse_list)
    return pl.pallas_call(
        body, name=name, grid=(S // tr,), in_specs=[row] * (2 * ng + 1), out_specs=[row, row, row],
        out_shape=[SDS((S, HD), F32), SDS((S, HD), F32), SDS((S, HD), BF16)],
        compiler_params=_cp(("parallel",), 48 * 2 ** 20),
    )(*o_list, *lse_list, d_o)


def _attn_bwd(name, qn, kn, vb, dob, lse, delta_rows, gi, dil):
    S, C = qn.shape
    NG = len(DILATIONS)
    HD = C // NG
    H = HD // HEAD_DIM
    L = S // dil
    nb = L // ATTN_BLOCK
    slopes = _alibi_slopes(H)
    B = ATTN_BLOCK

    def body(q_ref, kc_ref, kp_ref, vc_ref, vp_ref, do_ref, lse_ref, dl_ref, dq_ref, dk_ref, dv_ref, ck_ref, cv_ref):
        b = pl.program_id(1)

        @pl.when(b == 0)
        def _():
            ck_ref[...] = jnp.zeros_like(ck_ref)
            cv_ref[...] = jnp.zeros_like(cv_ref)

        @pl.when(b < nb)
        def _():
            valid, delta = _attn_mask(b)
            dist = delta * float(dil)
            for h in range(H):
                sl = slice(h * HEAD_DIM, (h + 1) * HEAD_DIM)
                q = q_ref[:, sl]
                k = jnp.concatenate([kp_ref[:, sl], kc_ref[:, sl]], axis=0)
                v = jnp.concatenate([vp_ref[:, sl], vc_ref[:, sl]], axis=0)
                do = do_ref[:, sl]
                s = lax.dot_general(q, k, (((1,), (1,)), ((), ())), preferred_element_type=F32)
                s = jnp.where(valid, s - slopes[h] * dist, -jnp.inf)
                lse2 = jnp.concatenate([lse_ref[:, sl], lse_ref[:, sl]], axis=1)
                dl2 = jnp.concatenate([dl_ref[:, sl], dl_ref[:, sl]], axis=1)
                p = jnp.exp(s - lse2)
                dp = lax.dot_general(do, v, (((1,), (1,)), ((), ())), preferred_element_type=F32)
                ds = (p * (dp - dl2)).astype(BF16)
                dq_ref[:, sl] = jnp.dot(ds, k, preferred_element_type=F32).astype(BF16)
                dk2 = lax.dot_general(ds, q, (((0,), (0,)), ((), ())), preferred_element_type=F32)
                dv2 = lax.dot_general(p.astype(BF16), do, (((0,), (0,)), ((), ())), preferred_element_type=F32)
                dk_ref[:, sl] = (ck_ref[:, sl] + dk2[:B]).astype(BF16)
                dv_ref[:, sl] = (cv_ref[:, sl] + dv2[:B]).astype(BF16)
                ck_ref[:, sl] = dk2[B:]
                cv_ref[:, sl] = dv2[B:]

        @pl.when(b == nb)
        def _():
            dk_ref[...] = ck_ref[...].astype(BF16)
            dv_ref[...] = cv_ref[...].astype(BF16)

    cur = lambda b: jnp.minimum(b, nb - 1)
    prev = lambda b: jnp.maximum(jnp.minimum(b, nb - 1) - 1, 0)
    late = lambda b: jnp.maximum(b - 1, 0)
    (qv, ncb, cb), (kv, _, _), (vv, _, _) = (_group_view(a, gi, dil, HD) for a in (qn, kn, vb))
    dov, lsev, dlv = _dil_view(dob, dil), _dil_view(lse, dil), _dil_view(delta_rows, dil)
    view_shape = (L, dil * HD)
    one = lambda m: _dil_spec(dil, HD, 1, 0, m)
    grp = lambda m: _dil_spec(dil, HD, ncb, cb, m)
    dq, dk, dv = pl.pallas_call(
        body, name=name, grid=(dil, nb + 1),
        in_specs=[grp(cur), grp(cur), grp(prev), grp(cur), grp(prev), one(cur), one(cur), one(cur)],
        out_specs=[one(cur), one(late), one(late)],
        out_shape=[SDS(view_shape, BF16)] * 3,
        scratch_shapes=[pltpu.VMEM((B, HD), F32), pltpu.VMEM((B, HD), F32)],
        compiler_params=_cp(("arbitrary", "arbitrary"), 40 * 2 ** 20),
    )(qv, kv, kv, vv, vv, dov, lsev, dlv)
    return _dil_unview(dq, S), _dil_unview(dk, S), _dil_unview(dv, S)


def _cast_into_gathered(name, w, l, sa, chip_arr):
    L, r, c = w.shape
    tr = _row_tile(r, c * 4)
    nrb = r // tr
    src = pl.BlockSpec((None, tr, c), lambda i, chip: (l, i, 0))
    if sa == 0:
        dst, shape = pl.BlockSpec((tr, c), lambda i, chip: (chip[0] * nrb + i, 0)), (N_CHIPS * r, c)
    else:
        dst, shape = pl.BlockSpec((tr, c), lambda i, chip: (i, chip[0])), (r, N_CHIPS * c)

    def body(chip_ref, i_ref, o_ref):
        o_ref[...] = i_ref[...].astype(BF16)

    return pl.pallas_call(
        body, name=name,
        grid_spec=pltpu.PrefetchScalarGridSpec(num_scalar_prefetch=1, grid=(nrb,), in_specs=[src], out_specs=dst),
        out_shape=SDS(shape, BF16), compiler_params=_cp(("parallel",), 32 * 2 ** 20),
    )(chip_arr, w)


def _add_half(name, dw, land, sa, c_arr):
    hr, hc = land.shape
    tr = _row_tile(hr, hc * 2)
    nrb = hr // tr
    if sa == 1:
        mine = pl.BlockSpec((tr, hc), lambda i, c: (c[0] * nrb + i, 0))
    else:
        mine = pl.BlockSpec((tr, hc), lambda i, c: (i, c[0]))
    other = pl.BlockSpec((tr, hc), lambda i, c: (i, 0))

    def body(c_ref, a_ref, b_ref, o_ref):
        o_ref[...] = (a_ref[...].astype(F32) + b_ref[...].astype(F32)).astype(BF16)

    return pl.pallas_call(
        body, name=name,
        grid_spec=pltpu.PrefetchScalarGridSpec(num_scalar_prefetch=1, grid=(nrb,), in_specs=[mine, other], out_specs=other),
        out_shape=SDS((hr, hc), BF16), compiler_params=_cp(("parallel",), 32 * 2 ** 20),
    )(c_arr, dw, land)


def _sum_slots(name, slots, out_dtype=F32):
    n, R, C = slots.shape
    tr = _row_tile(R, C * n * jnp.dtype(slots.dtype).itemsize, 2 ** 21)

    def body(s_ref, o_ref):
        acc = s_ref[0].astype(F32)
        for k in range(1, n):
            acc = acc + s_ref[k].astype(F32)
        o_ref[...] = acc.astype(out_dtype)

    return pl.pallas_call(
        body, name=name, grid=(R // tr,),
        in_specs=[pl.BlockSpec((n, tr, C), lambda i: (0, i, 0))], out_specs=pl.BlockSpec((tr, C), lambda i: (i, 0)),
        out_shape=SDS((R, C), out_dtype), compiler_params=_cp(("parallel",), 32 * 2 ** 20),
    )(slots)


def _sum_into(name, part, slots, buf, l, sa, where):
    n, pr, pc = slots.shape
    tr = _row_tile(pr, pc * (n + 1) * jnp.dtype(slots.dtype).itemsize, 2 ** 21)
    nrb = pr // tr
    if sa == 1:
        dst = pl.BlockSpec((None, tr, pc), lambda i, c, j: (l, c[0] * nrb + i, 0))
        own = pl.BlockSpec((tr, pc), lambda i, c, j: (i, j[0]))
    else:
        dst = pl.BlockSpec((None, tr, pc), lambda i, c, j: (l, i, c[0]))
        own = pl.BlockSpec((tr, pc), lambda i, c, j: (j[0] * nrb + i, 0))

    def body(c_ref, j_ref, p_ref, s_ref, b_ref, o_ref):
        acc = p_ref[...].astype(F32)
        for k in range(n):
            acc = acc + s_ref[k].astype(F32)
        o_ref[...] = acc

    return pl.pallas_call(
        body, name=name,
        grid_spec=pltpu.PrefetchScalarGridSpec(
            num_scalar_prefetch=2, grid=(nrb,),
            in_specs=[own, pl.BlockSpec((n, tr, pc), lambda i, c, j: (0, i, 0)), ANY], out_specs=dst),
        out_shape=SDS(buf.shape, buf.dtype), input_output_aliases={4: 0},
        compiler_params=_cp(("parallel",), 32 * 2 ** 20),
    )(where[0], where[1], part, slots, buf)


def _adamw_body(g_ref, w_ref, m_ref, v_ref, go_ref, d_ref, mo_ref, vo_ref):
    bc1 = 1.0 - ADAM_B1 ** ADAM_STEP
    bc2 = 1.0 - ADAM_B2 ** ADAM_STEP
    gv = g_ref[...]
    mn = ADAM_B1 * m_ref[...] + (1.0 - ADAM_B1) * gv
    vn = ADAM_B2 * v_ref[...] + (1.0 - ADAM_B2) * (gv * gv)
    go_ref[...] = gv
    mo_ref[...] = mn
    vo_ref[...] = vn
    d_ref[...] = -ADAM_LR * ((mn / bc1) / (jnp.sqrt(vn / bc2) + ADAM_EPS) + ADAM_WD * w_ref[...])


def _adamw_layer(name, g, w, m, v, l, outs):
    L, r, c = g.shape
    tr = _row_tile(r, c * 4, 2 ** 19)
    spec = pl.BlockSpec((None, tr, c), lambda i: (l, i, 0))

    def body(g_ref, w_ref, m_ref, v_ref, a0, a1, a2, a3, go_ref, d_ref, mo_ref, vo_ref):
        _adamw_body(g_ref, w_ref, m_ref, v_ref, go_ref, d_ref, mo_ref, vo_ref)

    return list(pl.pallas_call(
        body, name=name, grid=(r // tr,), in_specs=[spec] * 4 + [ANY] * 4, out_specs=[spec] * 4,
        out_shape=[SDS((L, r, c), F32)] * 4, input_output_aliases={4 + k: k for k in range(4)},
        compiler_params=_cp(("parallel",), 32 * 2 ** 20),
    )(g, w, m, v, *outs))


def _adamw(name, g, w, m, v):
    R, C = g.shape
    tr = _row_tile(R, C * 4, 2 ** 19)

    def body(g_ref, w_ref, m_ref, v_ref, go_ref, d_ref, mo_ref, vo_ref):
        _adamw_body(g_ref, w_ref, m_ref, v_ref, go_ref, d_ref, mo_ref, vo_ref)

    spec = pl.BlockSpec((tr, C), lambda i: (i, 0))
    return pl.pallas_call(
        body, name=name, grid=(R // tr,), in_specs=[spec] * 4, out_specs=[spec] * 4,
        out_shape=[SDS((R, C), F32)] * 4, compiler_params=_cp(("parallel",), 32 * 2 ** 20),
    )(g, w, m, v)


ANY = pl.BlockSpec(memory_space=pl.ANY)


def _coords():
    return lax.axis_index("x"), lax.axis_index("y"), lax.axis_index("c")


def _other_chips(x, y):
    return [((1 - x, y), 2 * (1 - x) + y), ((x, 1 - y), 2 * x + (1 - y)), ((1 - x, 1 - y), 2 * (1 - x) + (1 - y))]


def _win(ref, axis, start, size):
    if not isinstance(start, int):
        start = pl.multiple_of(start, LANE if axis == 1 else BF16_ROWS)
    if axis == 0:
        return ref.at[pl.ds(start, size), :]
    return ref.at[:, pl.ds(start, size)]


def _rcopy(src, dst, ssem, rsem, dev):
    return pltpu.make_async_remote_copy(src_ref=src, dst_ref=dst, send_sem=ssem, recv_sem=rsem,
                                        device_id=dev, device_id_type=MESH)


HBM = pl.BlockSpec(memory_space=pltpu.HBM)
SEM = pl.BlockSpec(memory_space=pltpu.SEMAPHORE)
TOKEN = pl.BlockSpec(memory_space=pltpu.VMEM)
EFFECT = pltpu.SideEffectType.DATAFLOW_SIDE_EFFECTING


def _in_hbm(a):
    return pltpu.with_memory_space_constraint(a, pltpu.HBM)


def _ag_geometry(mats, sas, o):
    sa = sas[o]
    return sa, 1 - sa, mats[o].shape[sa] // N_CHIPS, mats[o].shape[1 - sa] // 2


def _ag_ici_copy(mats, sas, refs, o, k, sems, x, y, c):
    sa, ha, wl, hl = _ag_geometry(mats, sas, o)
    chip, jk = _other_chips(x, y)[k]
    mine = _win(_win(refs[o], sa, (2 * x + y) * wl, wl), ha, c * hl, hl)
    landed = _win(_win(refs[o], sa, jk * wl, wl), ha, c * hl, hl)
    return mine, landed, (*chip, c)


def _ag_start(name, mats, sas, after):
    n = len(mats)

    def body(*refs):
        ins = refs[:n]
        s_sem, r_sem = refs[n + 1], refs[n + 2]
        token = refs[2 * n + 3]
        x, y, c = _coords()
        for o in range(n):
            for k in range(3):
                mine, _, dev = _ag_ici_copy(mats, sas, ins, o, k, None, x, y, c)
                _rcopy(mine, mine, s_sem.at[3 * o + k], r_sem.at[3 * o + k], dev).start()
        token[...] = jnp.zeros_like(token)

    out = pl.pallas_call(
        body, name=name,
        out_shape=(pltpu.SemaphoreType.DMA((3 * n,)), pltpu.SemaphoreType.DMA((3 * n,)),
                   *[pltpu.HBM(m.shape, m.dtype) for m in mats], SDS((8, LANE), F32)),
        in_specs=[HBM] * n + [ANY], out_specs=(SEM, SEM, *[HBM] * n, TOKEN),
        input_output_aliases={k: 2 + k for k in range(n)},
        compiler_params=pltpu.CompilerParams(has_side_effects=EFFECT),
    )(*[_in_hbm(m) for m in mats], after)
    return out[0], out[1], list(out[2:2 + n]), out[2 + n]


def _ag_wait(name, mats, sas, s_sem, r_sem, after):
    n = len(mats)

    def body(*refs):
        ins = refs[:n]
        s_ref, r_ref = refs[n], refs[n + 1]
        x, y, c = _coords()
        for o in range(n):
            for k in range(3):
                mine, landed, dev = _ag_ici_copy(mats, sas, ins, o, k, None, x, y, c)
                cp = _rcopy(mine, landed, s_ref.at[3 * o + k], r_ref.at[3 * o + k], dev)
                cp.wait_send()
                cp.wait_recv()

    after = list(after) if isinstance(after, (list, tuple)) else [after]
    return list(pl.pallas_call(
        body, name=name, out_shape=[pltpu.HBM(m.shape, m.dtype) for m in mats],
        in_specs=[HBM] * n + [SEM, SEM] + [ANY] * len(after), out_specs=[HBM] * n,
        input_output_aliases={k: k for k in range(n)},
        compiler_params=pltpu.CompilerParams(has_side_effects=EFFECT),
    )(*mats, s_sem, r_sem, *after))


def _ag_forward(name, mats, sas):
    n = len(mats)

    def body(*refs):
        outs = refs[n:2 * n]
        s_fwd, r_fwd = refs[2 * n:]
        x, y, c = _coords()
        sibling = (x, y, 1 - c)
        sends = []
        for o in range(n):
            sa, ha, wl, hl = _ag_geometry(mats, sas, o)
            for k, (chip, jk) in enumerate(_other_chips(x, y)):
                landed = _win(_win(outs[o], sa, jk * wl, wl), ha, c * hl, hl)
                fwd = _rcopy(landed, landed, s_fwd.at[3 * o + k], r_fwd.at[3 * o + k], sibling)
                fwd.start()
                sends.append(fwd)
        for o in range(n):
            sa, ha, wl, hl = _ag_geometry(mats, sas, o)
            for k, (chip, jk) in enumerate(_other_chips(x, y)):
                got = _win(_win(outs[o], sa, jk * wl, wl), ha, (1 - c) * hl, hl)
                _rcopy(got, got, s_fwd.at[3 * o + k], r_fwd.at[3 * o + k], sibling).wait_recv()
        for cp in sends:
            cp.wait_send()

    return list(pl.pallas_call(
        body, name=name, in_specs=[ANY] * n, out_specs=[ANY] * n,
        out_shape=[SDS(m.shape, m.dtype) for m in mats], input_output_aliases={k: k for k in range(n)},
        scratch_shapes=[pltpu.SemaphoreType.DMA((3 * n,)), pltpu.SemaphoreType.DMA((3 * n,))],
        compiler_params=pltpu.CompilerParams(has_side_effects=True),
    )(*mats))


def _all_gather_vec(name, v):
    Lv, cv = v.shape

    def body(v_ref, o_ref, loc_sem, s_sem, r_sem):
        x, y, c = _coords()
        jme = 2 * x + y
        chips = _other_chips(x, y)
        mine = _win(o_ref, 1, jme * cv, cv)
        loc = pltpu.make_async_copy(v_ref, mine, loc_sem)
        loc.start()
        sends = []
        for k, (chip, _) in enumerate(chips):
            cp = _rcopy(v_ref, mine, s_sem.at[k], r_sem.at[k], (*chip, c))
            cp.start()
            sends.append(cp)
        for k, (chip, jk) in enumerate(chips):
            got = _win(o_ref, 1, jk * cv, cv)
            _rcopy(got, got, s_sem.at[k], r_sem.at[k], (*chip, c)).wait_recv()
        for cp in sends:
            cp.wait_send()
        loc.wait()

    return pl.pallas_call(
        body, name=name, in_specs=[ANY], out_specs=ANY, out_shape=SDS((Lv, 4 * cv), v.dtype),
        scratch_shapes=[pltpu.SemaphoreType.DMA, pltpu.SemaphoreType.DMA((3,)), pltpu.SemaphoreType.DMA((3,))],
        compiler_params=pltpu.CompilerParams(has_side_effects=True),
    )(v)


def _half_shape(shape, sa):
    R, C = shape
    return (R // 2, C) if sa == 1 else (R, C // 2)


def _rs_sibling_copy(dws, sas, d_refs, land_refs, o, x, y, c):
    ha = 1 - sas[o]
    hl = dws[o].shape[ha] // 2
    return _win(d_refs[o], ha, (1 - c) * hl, hl), land_refs[o], (x, y, 1 - c)


def _rs_sibling_start(name, dws, sas, after):
    n = len(dws)

    def body(*refs):
        ins = refs[:n]
        s_sem, r_sem = refs[n + 1], refs[n + 2]
        lands = refs[2 * n + 3:3 * n + 3]
        token = refs[3 * n + 3]
        x, y, c = _coords()
        for o in range(n):
            src, dst, dev = _rs_sibling_copy(dws, sas, ins, lands, o, x, y, c)
            _rcopy(src, dst, s_sem.at[o], r_sem.at[o], dev).start()
        token[...] = jnp.zeros_like(token)

    out = pl.pallas_call(
        body, name=name,
        out_shape=(pltpu.SemaphoreType.DMA((n,)), pltpu.SemaphoreType.DMA((n,)),
                   *[pltpu.HBM(d.shape, d.dtype) for d in dws],
                   *[pltpu.HBM(_half_shape(d.shape, sa), d.dtype) for d, sa in zip(dws, sas)],
                   SDS((8, LANE), F32)),
        in_specs=[HBM] * n + [ANY], out_specs=(SEM, SEM, *[HBM] * (2 * n), TOKEN),
        input_output_aliases={k: 2 + k for k in range(n)},
        compiler_params=pltpu.CompilerParams(has_side_effects=EFFECT),
    )(*[_in_hbm(d) for d in dws], after)
    return out[0], out[1], list(out[2:2 + n]), list(out[2 + n:2 + 2 * n]), out[2 + 2 * n]


def _rs_sibling_wait(name, dws, lands, sas, s_sem, r_sem, after):
    n = len(dws)

    def body(*refs):
        d_refs, land_refs = refs[:n], refs[n:2 * n]
        s_ref, r_ref = refs[2 * n], refs[2 * n + 1]
        x, y, c = _coords()
        for o in range(n):
            src, dst, dev = _rs_sibling_copy(dws, sas, d_refs, land_refs, o, x, y, c)
            cp = _rcopy(src, dst, s_ref.at[o], r_ref.at[o], dev)
            cp.wait_send()
            cp.wait_recv()

    after = list(after) if isinstance(after, (list, tuple)) else [after]
    out = pl.pallas_call(
        body, name=name,
        out_shape=[pltpu.HBM(a.shape, a.dtype) for a in (*dws, *lands)],
        in_specs=[HBM] * (2 * n) + [SEM, SEM] + [ANY] * len(after), out_specs=[HBM] * (2 * n),
        input_output_aliases={k: k for k in range(2 * n)},
        compiler_params=pltpu.CompilerParams(has_side_effects=EFFECT),
    )(*dws, *lands, s_sem, r_sem, *after)
    return list(out[:n]), list(out[n:])


def _rs_piece_shape(p, sa):
    hr, hc = p.shape
    return (hr // N_CHIPS, hc) if sa == 0 else (hr, hc // N_CHIPS)


def _rs_ici_copy(parts, sas, p_refs, slot_refs, o, k, x, y, c):
    sa = sas[o]
    pl_ = parts[o].shape[sa] // N_CHIPS
    chip, jk = _other_chips(x, y)[k]
    return _win(p_refs[o], sa, jk * pl_, pl_), slot_refs[o].at[k], (*chip, c)


def _rs_start(name, parts, sas, after):
    n = len(parts)

    def body(*refs):
        ins = refs[:n]
        s_sem, r_sem = refs[n + 1], refs[n + 2]
        slots = refs[2 * n + 3:3 * n + 3]
        token = refs[3 * n + 3]
        x, y, c = _coords()
        for o in range(n):
            for k in range(3):
                src, dst, dev = _rs_ici_copy(parts, sas, ins, slots, o, k, x, y, c)
                _rcopy(src, dst, s_sem.at[3 * o + k], r_sem.at[3 * o + k], dev).start()
        token[...] = jnp.zeros_like(token)

    out = pl.pallas_call(
        body, name=name,
        out_shape=(pltpu.SemaphoreType.DMA((3 * n,)), pltpu.SemaphoreType.DMA((3 * n,)),
                   *[pltpu.HBM(p.shape, p.dtype) for p in parts],
                   *[pltpu.HBM((3,) + _rs_piece_shape(p, sa), p.dtype) for p, sa in zip(parts, sas)],
                   SDS((8, LANE), F32)),
        in_specs=[HBM] * n + [ANY], out_specs=(SEM, SEM, *[HBM] * (2 * n), TOKEN),
        input_output_aliases={k: 2 + k for k in range(n)},
        compiler_params=pltpu.CompilerParams(has_side_effects=EFFECT),
    )(*[_in_hbm(p) for p in parts], after)
    return out[0], out[1], list(out[2:2 + n]), list(out[2 + n:2 + 2 * n]), out[2 + 2 * n]


def _rs_wait(name, parts, slots, sas, s_sem, r_sem, after):
    n = len(parts)

    def body(*refs):
        p_refs, slot_refs = refs[:n], refs[n:2 * n]
        s_ref, r_ref = refs[2 * n], refs[2 * n + 1]
        x, y, c = _coords()
        for o in range(n):
            for k in range(3):
                src, dst, dev = _rs_ici_copy(parts, sas, p_refs, slot_refs, o, k, x, y, c)
                cp = _rcopy(src, dst, s_ref.at[3 * o + k], r_ref.at[3 * o + k], dev)
                cp.wait_send()
                cp.wait_recv()

    after = list(after) if isinstance(after, (list, tuple)) else [after]
    out = pl.pallas_call(
        body, name=name,
        out_shape=[pltpu.HBM(a.shape, a.dtype) for a in (*parts, *slots)],
        in_specs=[HBM] * (2 * n) + [SEM, SEM] + [ANY] * len(after), out_specs=[HBM] * (2 * n),
        input_output_aliases={k: k for k in range(2 * n)},
        compiler_params=pltpu.CompilerParams(has_side_effects=EFFECT),
    )(*parts, *slots, s_sem, r_sem, *after)
    return list(out[:n]), list(out[n:])


def _rs_share(name, sas, layers, buf_idx, bufs):
    n, nbuf = len(sas), len(bufs)

    def body(*refs):
        out = refs[nbuf:2 * nbuf]
        s_sem, r_sem = refs[2 * nbuf:]
        x, y, c = _coords()
        sibling = (x, y, 1 - c)
        sends = []
        for o in range(n):
            ha = 1 - sas[o]
            hl = bufs[buf_idx[o]].shape[1 + ha] // 2
            mine = _win(out[buf_idx[o]].at[layers[o]], ha, c * hl, hl)
            cp = _rcopy(mine, mine, s_sem.at[o], r_sem.at[o], sibling)
            cp.start()
            sends.append(cp)
        for o in range(n):
            ha = 1 - sas[o]
            hl = bufs[buf_idx[o]].shape[1 + ha] // 2
            got = _win(out[buf_idx[o]].at[layers[o]], ha, (1 - c) * hl, hl)
            _rcopy(got, got, s_sem.at[o], r_sem.at[o], sibling).wait_recv()
        for cp in sends:
            cp.wait_send()

    return pl.pallas_call(
        body, name=name, in_specs=[ANY] * nbuf, out_specs=[ANY] * nbuf,
        out_shape=[SDS(b.shape, b.dtype) for b in bufs],
        input_output_aliases={k: k for k in range(nbuf)},
        scratch_shapes=[pltpu.SemaphoreType.DMA((n,)), pltpu.SemaphoreType.DMA((n,))],
        compiler_params=pltpu.CompilerParams(has_side_effects=True),
    )(*bufs)


def _exchange_all(name, packed):
    R, C = packed.shape

    def body(p_ref, slots, loc_sem, s_sem, r_sem):
        x, y, c = _coords()
        me = 4 * x + 2 * y + c
        loc = pltpu.make_async_copy(p_ref, slots.at[me], loc_sem)
        loc.start()
        peers = []
        for k in range(1, N_DEV):
            px = 1 - x if k & 4 else x
            py = 1 - y if k & 2 else y
            pc = 1 - c if k & 1 else c
            peers.append(((px, py, pc), 4 * px + 2 * py + pc))
        sends = []
        for k, (peer, _) in enumerate(peers):
            cp = _rcopy(p_ref, slots.at[me], s_sem.at[k], r_sem.at[k], peer)
            cp.start()
            sends.append(cp)
        for k, (peer, pid) in enumerate(peers):
            _rcopy(slots.at[pid], slots.at[pid], s_sem.at[k], r_sem.at[k], peer).wait_recv()
        for cp in sends:
            cp.wait_send()
        loc.wait()

    return pl.pallas_call(
        body, name=name, in_specs=[ANY], out_specs=ANY, out_shape=SDS((N_DEV, R, C), packed.dtype),
        scratch_shapes=[pltpu.SemaphoreType.DMA, pltpu.SemaphoreType.DMA((N_DEV - 1,)), pltpu.SemaphoreType.DMA((N_DEV - 1,))],
        compiler_params=pltpu.CompilerParams(has_side_effects=True),
    )(packed)


def _pack(arrays):
    rows = []
    for a in arrays:
        flat = a.reshape(-1).astype(F32)
        pad = (-flat.size) % PACK_TILE
        rows.append(jnp.pad(flat, (0, pad)).reshape(-1, LANE))
    return jnp.concatenate(rows, axis=0)


def _unpack(packed, shapes):
    out, row = [], 0
    for s in shapes:
        size = int(np.prod(s)) if len(s) else 1
        nrows = -(-size // PACK_TILE) * (PACK_TILE // LANE)
        out.append(packed[row:row + nrows].reshape(-1)[:size].reshape(s))
        row += nrows
    return out


BIG_WEIGHTS = {
    "ffn1_w_gate": 1, "ffn1_w_up": 1, "ffn1_w_down": 0, "ffn2_w_gate": 1, "ffn2_w_up": 1, "ffn2_w_down": 0,
    "gmlp_w_in": 1, "gmlp_w_out": 0, "w_kv": 1, "attn_w_q": 1, "attn_w_o": 0,
}
SMALL_WEIGHTS = ("ffn1_norm", "mix_norm", "ffn2_norm", "gmlp_w_s", "gmlp_b_s", "kv_norm", "k_norm", "attn_q_norm")
WEIGHT_ORDER = ("ffn1_norm", "ffn1_w_gate", "ffn1_w_up", "ffn1_w_down", "mix_norm", "ffn2_norm", "ffn2_w_gate",
                "ffn2_w_up", "ffn2_w_down", "gmlp_w_in", "gmlp_v_norm", "gmlp_w_s", "gmlp_b_s", "gmlp_w_out",
                "kv_norm", "w_kv", "k_norm", "attn_w_q", "attn_q_norm", "attn_w_o")


def _ep_all(accs, ex):
    return list(accs)


def _as3d(w):
    return w if w.ndim == 3 else w.reshape((1,) + w.shape)


def kernel(x, ffn1_norm, ffn1_w_gate, ffn1_w_up, ffn1_w_down, mix_norm, ffn2_norm, ffn2_w_gate, ffn2_w_up, ffn2_w_down, gmlp_w_in, gmlp_v_norm, gmlp_w_s, gmlp_b_s, gmlp_w_out, kv_norm, w_kv, k_norm, attn_w_q, attn_q_norm, attn_w_o, loss_target, m_ffn1_norm, m_ffn1_w_gate, m_ffn1_w_up, m_ffn1_w_down, m_mix_norm, m_ffn2_norm, m_ffn2_w_gate, m_ffn2_w_up, m_ffn2_w_down, m_gmlp_w_in, m_gmlp_v_norm, m_gmlp_w_s, m_gmlp_b_s, m_gmlp_w_out, m_kv_norm, m_w_kv, m_k_norm, m_attn_w_q, m_attn_q_norm, m_attn_w_o, v_ffn1_norm, v_ffn1_w_gate, v_ffn1_w_up, v_ffn1_w_down, v_mix_norm, v_ffn2_norm, v_ffn2_w_gate, v_ffn2_w_up, v_ffn2_w_down, v_gmlp_w_in, v_gmlp_v_norm, v_gmlp_w_s, v_gmlp_b_s, v_gmlp_w_out, v_kv_norm, v_w_kv, v_k_norm, v_attn_w_q, v_attn_q_norm, v_attn_w_o):
    P = dict(locals())
    assert x.shape[0] == 1, "one sample per device"
    S, D = x.shape[1], x.shape[2]
    NL = ffn1_norm.shape[0]
    NG = len(DILATIONS)
    HD = attn_w_o.shape[1] * N_CHIPS
    H = HD // HEAD_DIM
    DG = gmlp_w_out.shape[1] * N_CHIPS
    G = DG // GMLP_GROUP_WIDTH
    assert all((S // d) % ATTN_BLOCK == 0 for d in DILATIONS) and S % GMLP_CHUNK == 0
    xs = x.reshape(S, D)
    tgt = loss_target.reshape(S, D)
    c_arr = lax.axis_index("c").astype(jnp.int32).reshape(1)
    chip = 2 * lax.axis_index("x") + lax.axis_index("y")
    chip_arr = chip.astype(jnp.int32).reshape(1)
    kv_layer = N_A_LAYERS - 1


    def layer_weights(l):
        names = [("ffn1_w_gate", l), ("ffn1_w_up", l), ("ffn1_w_down", l), ("ffn2_w_gate", l), ("ffn2_w_up", l), ("ffn2_w_down", l)]
        if l < N_A_LAYERS:
            names += [("gmlp_w_in", l), ("gmlp_w_out", l)]
        else:
            names += [("attn_w_q", l - N_A_LAYERS), ("attn_w_o", l - N_A_LAYERS)]
        if l == kv_layer:
            names += [("w_kv", 0)]
        return names

    W = {}
    ag_open = {}

    def cast_layer(l):
        return [_cast_into_gathered("cast_shard", _as3d(P[n]), li, BIG_WEIGHTS[n], chip_arr) for n, li in layer_weights(l)]

    def ag_begin(l, after, mats):
        names = layer_weights(l)
        sas = [BIG_WEIGHTS[n] for n, _ in names]
        s_sem, r_sem, mats, token = _ag_start(f"ag_start_l{l}", mats, sas, after)
        ag_open[l] = (names, sas, s_sem, r_sem, mats)
        return token

    def ag_finish(l, after):
        names, sas, s_sem, r_sem, mats = ag_open.pop(l)
        mats = _ag_wait(f"ag_wait_l{l}", mats, sas, s_sem, r_sem, after)
        W.update(dict(zip(names, _ag_forward(f"ag_forward_l{l}", mats, sas))))

    vnorm_full = _all_gather_vec("ag_vnorm", gmlp_v_norm)
    ag_token = ag_begin(0, vnorm_full, cast_layer(0))
    if NL > 1:
        ag_token = ag_begin(1, ag_token, cast_layer(1))
    cast_ahead = {l: cast_layer(l) for l in range(2, NL)}
    ag_finish(0, [ag_token] + [m for l in cast_ahead for m in cast_ahead[l]])

    kgain = jnp.tile(k_norm[:, None, :], (1, H, 1)).reshape(1, NG * HD)
    qgain = [jnp.tile(attn_q_norm[j][:, None, :], (1, H, 1)).reshape(1, NG * HD) for j in range(NL - N_A_LAYERS)]
    q_scale = HEAD_DIM ** -0.5
    one = [(0, 0, 0)]

    def ffn_fwd(xc, gamma, wg, wu, wd, dep=None):
        n = _rms_fwd("ffn_norm", xc, gamma, dep)
        g, u, act = _mm("ffn_up", [n], [wg, wu], [(0, 0, 0), (0, 1, 1)], _ep_swiglu, [BF16] * 3)
        (x2,) = _mm("ffn_down", [act], [wd], one, _ep_residual(0.5), [F32], extras=[xc])
        return x2, (xc, n, g, u, act)

    saved = {}
    xc = xs
    for l in range(NL):
        if l > 0:
            ag_finish(l, xc)
        if l + 2 < NL:
            ag_token = ag_begin(l + 2, ag_token, cast_ahead.pop(l + 2))
        xc, saved["f1", l] = ffn_fwd(xc, ffn1_norm[l], W["ffn1_w_gate", l], W["ffn1_w_up", l], W["ffn1_w_down", l], ag_token)
        h = _rms_fwd("mix_norm", xc, mix_norm[l])
        if l < N_A_LAYERS:
            zpre, z = _mm("gmlp_in", [h], [W["gmlp_w_in", l]], one, _ep_gelu, [BF16, F32])
            bias_full = jnp.repeat(gmlp_b_s[l].T, GMLP_GROUP_WIDTH, axis=1)
            gated = _gmlp_gate_fwd("gmlp_gate", z, vnorm_full[l:l + 1], gmlp_w_s[l], bias_full)
            (x2,) = _mm("gmlp_out", [gated], [W["gmlp_w_out", l]], one, _ep_residual(1.0), [F32], extras=[xc])
            saved["mix", l] = (xc, h, zpre, z, gated, bias_full)
        else:
            j = l - N_A_LAYERS
            (q_raw,) = _mm("attn_q", [h], [W["attn_w_q", j]], one, _ep_plain, [F32])
            qn = _head_norm_fwd("q_norm", q_raw, 0, 1, qgain[j], q_scale, False)
            os_, lses = [], []
            for gi, dil in enumerate(DILATIONS):
                o, lse = _attn_fwd(f"attn_fwd_d{dil}", qn, kn, vb, gi, dil)
                os_.append(o)
                lses.append(lse)
            ob = _attn_combine("attn_mix", os_, lses)
            (x2,) = _mm("attn_o", [ob], [W["attn_w_o", j]], one, _ep_residual(1.0), [F32], extras=[xc])
            saved["mix", l] = (xc, h, q_raw, qn, os_, lses, ob)
        xc = x2
        xc, saved["f2", l] = ffn_fwd(xc, ffn2_norm[l], W["ffn2_w_gate", l], W["ffn2_w_up", l], W["ffn2_w_down", l])
        if l == kv_layer:
            kvn = _rms_fwd("kv_norm", xc, kv_norm)
            (kv_raw,) = _mm("kv_proj", [kvn], [W["w_kv", 0]], one, _ep_plain, [F32])
            kn, vb = _head_norm_fwd("k_norm", kv_raw, 0, 2, kgain, 1.0, True)
            saved["kv"] = (xc, kvn, kv_raw)

    dx, dxb, loss_rows = _loss_grad("loss", xc, tgt, 0.5)
    dW = {}
    dsmall = {n: [None] * P[n].shape[0] for n in ("ffn1_norm", "mix_norm", "ffn2_norm")}
    dsmall.update(gmlp_w_s=[None] * N_A_LAYERS, gmlp_b_s=[None] * N_A_LAYERS, gmlp_v_norm=[None] * N_A_LAYERS,
                  attn_q_norm=[None] * (NL - N_A_LAYERS))
    dks = [[] for _ in DILATIONS]
    dvs = [[] for _ in DILATIONS]

    names_big = list(BIG_WEIGHTS)
    gbuf = [lax.empty(_as3d(P[n]).shape, F32) for n in names_big]
    where = (c_arr, chip_arr)
    sib_open = {}
    rs_open = {}

    def sib_begin(l, after):
        names = layer_weights(l)
        sas = [BIG_WEIGHTS[n] for n, _ in names]
        s_sem, r_sem, dws, lands, token = _rs_sibling_start(f"rs_sibling_start_l{l}", [dW[k] for k in names], sas, after)
        sib_open[l] = (names, sas, s_sem, r_sem, dws, lands)
        return token

    def rs_begin(l, after):
        names, sas, s_sem, r_sem, dws, lands = sib_open.pop(l)
        dws, lands = _rs_sibling_wait(f"rs_sibling_wait_l{l}", dws, lands, sas, s_sem, r_sem, after)
        parts = [_add_half("rs_add_half", d, ln, sa, c_arr) for d, ln, sa in zip(dws, lands, sas)]
        s_sem, r_sem, parts, slots, token = _rs_start(f"rs_start_l{l}", parts, sas, dws[0])
        rs_open[l] = (names, sas, s_sem, r_sem, parts, slots)
        return token

    def rs_finish(l, after):
        names, sas, s_sem, r_sem, parts, slots = rs_open.pop(l)
        parts, slots = _rs_wait(f"rs_wait_l{l}", parts, slots, sas, s_sem, r_sem, after)
        for p, s, sa, (n, li) in zip(parts, slots, sas, names):
            bi = names_big.index(n)
            gbuf[bi] = _sum_into("rs_sum_chips", p, s, gbuf[bi], li, sa, where)
        gbuf[:] = _rs_share(f"rs_share_l{l}", sas, [li for _, li in names], [names_big.index(n) for n, _ in names], gbuf)

    def ffn_bwd(dx, dxb, sv, gamma, wg, wu, wd, key, l, next_scale, dep=None):
        xin, n, g, u, act = sv
        dg, du = _mm("ffn_dact", [dxb], [wd], one, _ep_swiglu_bwd, [BF16, BF16], tb=True, extras=[g, u], dep=dep)
        (dW[key + "_w_down", l],) = _mm("ffn_dwd", [act], [dxb], one, _ep_plain, [BF16], ta=True)
        dW[key + "_w_gate", l], dW[key + "_w_up", l] = _mm("ffn_dwgu", [n], [dg, du], [(0, 0, 0), (0, 1, 1)], _ep_all, [BF16, BF16], ta=True)
        (dn,) = _mm("ffn_dn", [dg, du], [wg, wu], [(0, 0, 0), (1, 1, 0)], _ep_plain, [F32], tb=True)
        dx, dxb, dsmall[key + "_norm"][l] = _rms_bwd("ffn_norm_bwd", xin, gamma, dn, dx, next_scale)
        return dx, dxb

    dep = None
    for l in reversed(range(NL)):
        if l == kv_layer:
            x_kv, kvn, kv_raw = saved["kv"]
            dkv_raw, dkgain = _head_norm_bwd("k_norm_bwd", kv_raw, 0, 2, kgain, 1.0, dks, dvs)
            (dW["w_kv", 0],) = _mm("kv_dw", [kvn], [dkv_raw], one, _ep_plain, [BF16], ta=True, dep=dep)
            (dkvn,) = _mm("kv_dn", [dkv_raw], [W["w_kv", 0]], one, _ep_plain, [F32], tb=True)
            dx, dxb, dkvnorm = _rms_bwd("kv_norm_bwd", x_kv, kv_norm, dkvn, dx, 0.5)
        dx, dxb = ffn_bwd(dx, dxb, saved["f2", l], ffn2_norm[l], W["ffn2_w_gate", l], W["ffn2_w_up", l], W["ffn2_w_down", l], "ffn2", l, 1.0, dep)
        if l + 1 < NL:
            dep = rs_begin(l + 1, dx)
        if l < N_A_LAYERS:
            xin, h, zpre, z, gated, bias_full = saved["mix", l]
            (dW["gmlp_w_out", l],) = _mm("gmlp_dwout", [gated], [dxb], one, _ep_plain, [BF16], ta=True, dep=dep)
            (dgated,) = _mm("gmlp_dgated", [dxb], [W["gmlp_w_out", l]], one, _ep_plain, [F32], tb=True)
            dzpre, dws, dbacc, dvn = _gmlp_gate_bwd("gmlp_gate_bwd", z, zpre, dgated, vnorm_full[l:l + 1], gmlp_w_s[l],
                                                    jnp.swapaxes(gmlp_w_s[l], 1, 2), bias_full)
            (dW["gmlp_w_in", l],) = _mm("gmlp_dwin", [h], [dzpre], one, _ep_plain, [BF16], ta=True)
            (dh,) = _mm("gmlp_dh", [dzpre], [W["gmlp_w_in", l]], one, _ep_plain, [F32], tb=True)
            dsmall["gmlp_w_s"][l] = dws
            dsmall["gmlp_b_s"][l] = dbacc.reshape(GMLP_CHUNK, G, GMLP_GROUP_WIDTH).sum(-1).T
            dsmall["gmlp_v_norm"][l] = dvn.reshape(DG)
        else:
            j = l - N_A_LAYERS
            xin, h, q_raw, qn, os_, lses, ob = saved["mix", l]
            (dW["attn_w_o", j],) = _mm("attn_dwo", [ob], [dxb], one, _ep_plain, [BF16], ta=True, dep=dep)
            (d_ob,) = _mm("attn_dob", [dxb], [W["attn_w_o", j]], one, _ep_plain, [F32], tb=True)
            lse_t, dl_rows, dob = _attn_combine("attn_mix_bwd", os_, lses, d_o=d_ob)
            dqs = []
            for gi, dil in enumerate(DILATIONS):
                dq, dk, dv = _attn_bwd(f"attn_bwd_d{dil}", qn, kn, vb, dob, lse_t, dl_rows, gi, dil)
                dqs.append([dq])
                dks[gi].append(dk)
                dvs[gi].append(dv)
            dq_raw, dqgain = _head_norm_bwd("q_norm_bwd", q_raw, 0, 1, qgain[j], q_scale, dqs, None)
            (dW["attn_w_q", j],) = _mm("attn_dwq", [h], [dq_raw], one, _ep_plain, [BF16], ta=True)
            (dh,) = _mm("attn_dh", [dq_raw], [W["attn_w_q", j]], one, _ep_plain, [F32], tb=True)
            dsmall["attn_q_norm"][j] = dqgain.reshape(NG, H, HEAD_DIM).sum(1)
        dx, dxb, dsmall["mix_norm"][l] = _rms_bwd("mix_norm_bwd", xin, mix_norm[l], dh, dx, 0.5)
        dx, dxb = ffn_bwd(dx, dxb, saved["f1", l], ffn1_norm[l], W["ffn1_w_gate", l], W["ffn1_w_up", l], W["ffn1_w_down", l], "ffn1", l, 0.5)
        dep = sib_begin(l, dx)
        if l + 2 < NL:
            rs_finish(l + 2, dep)
    grad_x = dx.reshape(x.shape)

    adam_out = {n: [lax.empty(_as3d(P[n]).shape, F32) for _ in range(4)] for n in names_big}

    def adam_layer(l):
        for n, li in layer_weights(l):
            adam_out[n] = _adamw_layer("adamw", gbuf[names_big.index(n)], _as3d(P[n]), _as3d(P["m_" + n]),
                                       _as3d(P["v_" + n]), li, adam_out[n])

    def updated():
        return [adam_out[n][0] for n in names_big]

    reduced = list(reversed(range(2, NL)))
    if reduced:
        adam_layer(reduced.pop(0))
    dep = rs_begin(0, [dep] + updated())
    if reduced:
        adam_layer(reduced.pop(0))
    if NL > 1:
        rs_finish(1, [dep] + updated())
        for l in reduced:
            adam_layer(l)
        adam_layer(1)
    rs_finish(0, [dep] + updated())
    adam_layer(0)
    big_out = {n: [o.reshape(P[n].shape) for o in adam_out[n]] for n in names_big}

    loss_part = (0.5 / D) * jnp.sum(loss_rows)
    small_grads = [jnp.stack([g.reshape(P[n].shape[1:]) for g in dsmall[n]]) for n in ("ffn1_norm", "mix_norm", "ffn2_norm", "gmlp_w_s", "gmlp_b_s")]
    small_grads += [dkvnorm.reshape(kv_norm.shape), dkgain.reshape(NG, H, HEAD_DIM).sum(1), jnp.stack(dsmall["attn_q_norm"])]
    vn_grad_full = jnp.stack(dsmall["gmlp_v_norm"])
    packed = _pack(small_grads + [vn_grad_full, loss_part.reshape(1)])
    total = _sum_slots("sum_devices", _exchange_all("exchange_small", packed))
    shapes = [P[n].shape for n in SMALL_WEIGHTS] + [vn_grad_full.shape, (1,)]
    red = _unpack(total, shapes)
    loss = red[-1].reshape(())
    cv = gmlp_v_norm.shape[1]
    vn_grad = lax.dynamic_slice_in_dim(red[-2], chip * cv, cv, axis=1)
    names_small = list(SMALL_WEIGHTS) + ["gmlp_v_norm"]
    g_small = red[:len(SMALL_WEIGHTS)] + [vn_grad]
    outs = _adamw("adamw_small", _pack(g_small), _pack([P[n] for n in names_small]),
                  _pack([P["m_" + n] for n in names_small]), _pack([P["v_" + n] for n in names_small]))
    small_shapes = [P[n].shape for n in names_small]
    small_out = {n: [] for n in names_small}
    for o in outs:
        for n, a in zip(names_small, _unpack(o, small_shapes)):
            small_out[n].append(a)

    res = {**big_out, **small_out}
    return (loss, grad_x, *[res[n][0] for n in WEIGHT_ORDER], *[res[n][1] for n in WEIGHT_ORDER],
            *[res[n][2] for n in WEIGHT_ORDER], *[res[n][3] for n in WEIGHT_ORDER])
```

```python
import numpy as np
import jax
import jax.numpy as jnp
from jax import lax
from jax.experimental import pallas as pl
from jax.experimental.pallas import tpu as pltpu

F32 = jnp.float32
BF16 = jnp.bfloat16
SDS = jax.ShapeDtypeStruct

EPS = 1e-6
HEAD_DIM = 128
GMLP_CHUNK = 128
GMLP_GROUP_WIDTH = 128
DILATIONS = (1, 4, 16)
ATTN_BLOCK = 128
N_A_LAYERS = 2
ADAM_LR, ADAM_B1, ADAM_B2, ADAM_EPS, ADAM_WD, ADAM_STEP = 0.001, 0.9, 0.999, 1e-08, 0.01, 10

N_CHIPS = 4
N_DEV = 8
MESH = pl.DeviceIdType.MESH
V7X_VMEM_BYTES = 64 * 2 ** 20
VMEM_CEILING = V7X_VMEM_BYTES - 6 * 2 ** 20
VMEM_BLOCK_BUDGET = 38 * 2 ** 20
LANE = 128
BF16_ROWS = 16
PACK_TILE = 8 * LANE


def _cp(sem=None, vmem=None):
    kw = {}
    if sem is not None:
        kw["dimension_semantics"] = sem
    if vmem is not None:
        kw["vmem_limit_bytes"] = int(min(max(vmem, 16 * 2 ** 20), VMEM_CEILING))
    return pltpu.CompilerParams(**kw)


def _pick(dim, cands):
    for c in cands:
        if c <= dim and dim % c == 0:
            return c
    return dim


def _row_tile(rows, row_bytes, target=2 ** 20):
    t = 1024
    while t > 8 and (t * row_bytes > target or rows % t):
        t //= 2
    return t if rows % t == 0 else rows


def _sigmoid(x):
    return 1.0 / (1.0 + jnp.exp(-x))


_GELU_C = 0.7978845608028654
_GELU_A = 0.044715


def _gelu(x):
    return 0.5 * x * (1.0 + jnp.tanh(_GELU_C * (x + _GELU_A * (x * x * x))))


def _gelu_grad(x):
    t = jnp.tanh(_GELU_C * (x + _GELU_A * (x * x * x)))
    return 0.5 * (1.0 + t) + 0.5 * x * (1.0 - t * t) * (_GELU_C * (1.0 + 3.0 * _GELU_A * x * x))


def _mm_tiles(M, N, K, n_a, n_b, n_acc, io_bytes):
    tks = [K] + [d for d in (4096, 3072, 2816, 2048, 1024, 512, 256, 128) if d < K and K % d == 0]
    tms = [t for t in (1024, 512, 256, 128) if M % t == 0] or [M]
    tns = [t for t in (512, 256, 128) if N % t == 0] or [N]
    best = None
    for tk in tks:
        for tm in tms:
            for tn in tns:
                est = 2 * 2 * (n_a * tm * tk + n_b * tk * tn) + 2 * tm * tn * io_bytes
                est += n_acc * tm * tn * 4 * (2 if tk < K else 1)
                if est <= VMEM_BLOCK_BUDGET:
                    return tm, tn, tk, est
                if best is None or est < best[3]:
                    best = (tm, tn, tk, est)
    return best


def _mm(name, a_list, b_list, terms, epilogue, out_dtypes, *, ta=False, tb=False, extras=(), dep=None):
    n_acc = 1 + max(t[2] for t in terms)
    a0, b0 = a_list[0], b_list[0]
    (K, M) = a0.shape if ta else a0.shape[::-1]
    N = b0.shape[0] if tb else b0.shape[1]
    io_bytes = sum(jnp.dtype(e.dtype).itemsize for e in extras) + sum(jnp.dtype(d).itemsize for d in out_dtypes)
    tm, tn, tk, est = _mm_tiles(M, N, K, len(a_list), len(b_list), n_acc, io_bytes)
    nk = K // tk
    na, nb, ne, no = len(a_list), len(b_list), len(extras), len(out_dtypes)
    deps = [] if dep is None else [dep]
    nd = len(deps)
    dn = (((0 if ta else 1,), (1 if tb else 0,)), ((), ()))

    def body(*refs):
        a_refs = refs[:na]
        b_refs = refs[na:na + nb]
        e_refs = refs[na + nb:na + nb + ne]
        o_refs = refs[na + nb + ne + nd:na + nb + ne + nd + no]
        acc_refs = refs[na + nb + ne + nd + no:]
        parts = [None] * n_acc
        for ai, bi, qi in terms:
            d = lax.dot_general(a_refs[ai][...], b_refs[bi][...], dn, preferred_element_type=F32)
            parts[qi] = d if parts[qi] is None else parts[qi] + d

        def finish(accs):
            outs = epilogue(accs, [e[...] for e in e_refs])
            for o_ref, o in zip(o_refs, outs):
                o_ref[...] = o.astype(o_ref.dtype)

        if nk == 1:
            finish(parts)
        else:
            k = pl.program_id(2)

            @pl.when(k == 0)
            def _():
                for q in range(n_acc):
                    acc_refs[q][...] = parts[q]

            @pl.when(k > 0)
            def _():
                for q in range(n_acc):
                    acc_refs[q][...] += parts[q]

            @pl.when(k == nk - 1)
            def _():
                finish([acc_refs[q][...] for q in range(n_acc)])

    a_spec = pl.BlockSpec((tk, tm), lambda i, j, k: (k, i)) if ta else pl.BlockSpec((tm, tk), lambda i, j, k: (i, k))
    b_spec = pl.BlockSpec((tn, tk), lambda i, j, k: (j, k)) if tb else pl.BlockSpec((tk, tn), lambda i, j, k: (k, j))
    e_spec = pl.BlockSpec((tm, tn), lambda i, j, k: (i, j))
    outs = pl.pallas_call(
        body, name=name, grid=(M // tm, N // tn, nk),
        in_specs=[a_spec] * na + [b_spec] * nb + [e_spec] * ne + [pl.BlockSpec((8, LANE), lambda i, j, k: (0, 0))] * nd,
        out_specs=[e_spec] * no,
        out_shape=[SDS((M, N), d) for d in out_dtypes],
        scratch_shapes=[pltpu.VMEM((tm, tn), F32) for _ in range(n_acc)] if nk > 1 else [],
        compiler_params=_cp(("parallel", "parallel", "arbitrary"), est + 12 * 2 ** 20),
    )(*a_list, *b_list, *extras, *deps)
    return outs


def _ep_plain(accs, ex):
    return [accs[0]]


def _ep_swiglu(accs, ex):
    g, u = accs
    s = _sigmoid(g)
    sg = g * s
    return [u * (s + sg * (1.0 - s)), sg, sg * u]


def _ep_swiglu_bwd(accs, ex):
    da = accs[0]
    return [da * ex[0].astype(F32), da * ex[1].astype(F32)]


def _ep_gelu(accs, ex):
    return [_gelu_grad(accs[0]), _gelu(accs[0])]


def _ep_residual(scale):
    def ep(accs, ex):
        return [ex[0] + scale * accs[0]]
    return ep


def _rms_fwd(name, x, gamma, dep=None):
    S, D = x.shape
    tr = _row_tile(S, D * 4)
    deps = [] if dep is None else [dep]

    def body(x_ref, g_ref, *rest):
        o_ref = rest[-1]
        xv = x_ref[...]
        r = lax.rsqrt(jnp.mean(xv * xv, axis=-1, keepdims=True) + EPS)
        o_ref[...] = (xv * r * g_ref[...]).astype(BF16)

    return pl.pallas_call(
        body, name=name, grid=(S // tr,),
        in_specs=[pl.BlockSpec((tr, D), lambda i: (i, 0)), pl.BlockSpec((1, D), lambda i: (0, 0))]
        + [pl.BlockSpec((8, LANE), lambda i: (0, 0))] * len(deps),
        out_specs=pl.BlockSpec((tr, D), lambda i: (i, 0)),
        out_shape=SDS((S, D), BF16),
        compiler_params=_cp(("parallel",), 32 * 2 ** 20),
    )(x, gamma.reshape(1, D), *deps)


def _rms_bwd(name, x, gamma, dn, dx_in, out_scale):
    S, D = x.shape
    tr = _row_tile(S, D * 4, 2 ** 19)

    def body(x_ref, g_ref, dn_ref, dxi_ref, dxo_ref, dxb_ref, dg_ref):
        i = pl.program_id(0)
        xv = x_ref[...]
        r = lax.rsqrt(jnp.mean(xv * xv, axis=-1, keepdims=True) + EPS)
        xh = xv * r
        dnv = dn_ref[...]
        dxh = dnv * g_ref[...]
        dx = dxi_ref[...] + r * (dxh - xh * jnp.mean(dxh * xh, axis=-1, keepdims=True))
        dxo_ref[...] = dx
        dxb_ref[...] = (out_scale * dx).astype(BF16)
        part = jnp.sum(dnv * xh, axis=0, keepdims=True)

        @pl.when(i == 0)
        def _():
            dg_ref[...] = part

        @pl.when(i > 0)
        def _():
            dg_ref[...] += part

    row = pl.BlockSpec((tr, D), lambda i: (i, 0))
    vec = pl.BlockSpec((1, D), lambda i: (0, 0))
    return pl.pallas_call(
        body, name=name, grid=(S // tr,),
        in_specs=[row, vec, row, row], out_specs=[row, row, vec],
        out_shape=[SDS((S, D), F32), SDS((S, D), BF16), SDS((1, D), F32)],
        compiler_params=_cp(("arbitrary",), 40 * 2 ** 20),
    )(x, gamma.reshape(1, D), dn, dx_in)


def _loss_grad(name, y, t, out_scale):
    S, D = y.shape
    tr = _row_tile(S, D * 4, 2 ** 19)
    inv_d = 1.0 / D

    def body(y_ref, t_ref, dy_ref, dyb_ref, ls_ref):
        i = pl.program_id(0)
        e = y_ref[...] - t_ref[...]
        dy = e * inv_d
        dy_ref[...] = dy
        dyb_ref[...] = (out_scale * dy).astype(BF16)
        part = jnp.sum(e * e, axis=0, keepdims=True)

        @pl.when(i == 0)
        def _():
            ls_ref[...] = part

        @pl.when(i > 0)
        def _():
            ls_ref[...] += part

    row = pl.BlockSpec((tr, D), lambda i: (i, 0))
    vec = pl.BlockSpec((1, D), lambda i: (0, 0))
    return pl.pallas_call(
        body, name=name, grid=(S // tr,),
        in_specs=[row, row], out_specs=[row, row, vec],
        out_shape=[SDS((S, D), F32), SDS((S, D), BF16), SDS((1, D), F32)],
        compiler_params=_cp(("arbitrary",), 32 * 2 ** 20),
    )(y, t)


def _head_mean(v):
    return jnp.mean(v, axis=-1, keepdims=True)


def _head_norm_fwd(name, raw, part, n_parts, gain_t, scale, with_pass):
    S = raw.shape[0]
    W = raw.shape[1] // n_parts
    nh = W // HEAD_DIM
    tr = _row_tile(S, W * (4 + 2) * (2 if with_pass else 1), 4 * 2 ** 20)

    def body(*refs):
        if with_pass:
            x_ref, p_ref, g_ref, o_ref, po_ref = refs
            po_ref[...] = p_ref[...].astype(BF16)
        else:
            x_ref, g_ref, o_ref = refs
        for h in range(nh):
            sl = slice(h * HEAD_DIM, (h + 1) * HEAD_DIM)
            xv = x_ref[:, sl]
            r = lax.rsqrt(_head_mean(xv * xv) + EPS)
            o_ref[:, sl] = (xv * r * g_ref[:, sl] * scale).astype(BF16)

    xspec = pl.BlockSpec((tr, W), lambda i: (i, part))
    ospec = pl.BlockSpec((tr, W), lambda i: (i, 0))
    gspec = pl.BlockSpec((1, W), lambda i: (0, 0))
    if with_pass:
        in_specs = [xspec, pl.BlockSpec((tr, W), lambda i: (i, 1)), gspec]
        args = (raw, raw, gain_t)
        out_specs, out_shape = [ospec, ospec], [SDS((S, W), BF16), SDS((S, W), BF16)]
    else:
        in_specs, args = [xspec, gspec], (raw, gain_t)
        out_specs, out_shape = ospec, SDS((S, W), BF16)
    return pl.pallas_call(
        body, name=name, grid=(S // tr,), in_specs=in_specs, out_specs=out_specs, out_shape=out_shape,
        compiler_params=_cp(("parallel",), 40 * 2 ** 20),
    )(*args)


def _head_norm_bwd(name, raw, part, n_parts, gain_t, scale, dy_groups, pass_groups):
    S = raw.shape[0]
    W = raw.shape[1] // n_parts
    ng = len(dy_groups)
    HD = W // ng
    nhg = HD // HEAD_DIM
    n_dy = [len(g) for g in dy_groups]
    n_ps = [len(g) for g in pass_groups] if pass_groups is not None else []
    flat = [a for g in dy_groups for a in g] + ([a for g in pass_groups for a in g] if pass_groups is not None else [])
    out_w = 2 * W if pass_groups is not None else W
    tr = _row_tile(S, W * 4 + sum(HD * jnp.dtype(a.dtype).itemsize for a in flat) + out_w * 2, 8 * 2 ** 20)

    def body(*refs):
        x_ref, g_ref = refs[0], refs[1]
        d_refs = refs[2:2 + len(flat)]
        o_ref, dg_ref = refs[2 + len(flat)], refs[3 + len(flat)]
        i = pl.program_id(0)

        @pl.when(i == 0)
        def _():
            dg_ref[...] = jnp.zeros_like(dg_ref)

        pos = 0
        for gi in range(ng):
            dys = d_refs[pos:pos + n_dy[gi]]
            pos += n_dy[gi]
            for h in range(nhg):
                sl = slice(gi * HD + h * HEAD_DIM, gi * HD + (h + 1) * HEAD_DIM)
                hs = slice(h * HEAD_DIM, (h + 1) * HEAD_DIM)
                dy = dys[0][:, hs].astype(F32)
                for extra in dys[1:]:
                    dy = dy + extra[:, hs].astype(F32)
                xv = x_ref[:, sl]
                r = lax.rsqrt(_head_mean(xv * xv) + EPS)
                xh = xv * r
                dxh = dy * (g_ref[:, sl] * scale)
                o_ref[:, sl] = (r * (dxh - xh * _head_mean(dxh * xh))).astype(BF16)
                dg_ref[:, sl] += jnp.sum(dy * xh, axis=0, keepdims=True) * scale
        for gi in range(len(n_ps)):
            ps = d_refs[pos:pos + n_ps[gi]]
            pos += n_ps[gi]
            acc = ps[0][...].astype(F32)
            for extra in ps[1:]:
                acc = acc + extra[...].astype(F32)
            o_ref[:, W + gi * HD:W + (gi + 1) * HD] = acc.astype(BF16)

    dspec = pl.BlockSpec((tr, HD), lambda i: (i, 0))
    return pl.pallas_call(
        body, name=name, grid=(S // tr,),
        in_specs=[pl.BlockSpec((tr, W), lambda i: (i, part)), pl.BlockSpec((1, W), lambda i: (0, 0))] + [dspec] * len(flat),
        out_specs=[pl.BlockSpec((tr, out_w), lambda i: (i, 0)), pl.BlockSpec((1, W), lambda i: (0, 0))],
        out_shape=[SDS((S, out_w), BF16), SDS((1, W), F32)],
        compiler_params=_cp(("arbitrary",), 48 * 2 ** 20),
    )(raw, gain_t, *flat)


def _tril_mask():
    r = lax.broadcasted_iota(jnp.int32, (GMLP_CHUNK, GMLP_CHUNK), 0)
    c = lax.broadcasted_iota(jnp.int32, (GMLP_CHUNK, GMLP_CHUNK), 1)
    return r >= c


def _gmlp_gate_fwd(name, z, v_norm, w_s, bias_full):
    S, DG2 = z.shape
    DG = DG2 // 2
    G = DG // GMLP_GROUP_WIDTH
    C = GMLP_CHUNK

    def body(u_ref, v_ref, vn_ref, ws_ref, b_ref, o_ref):
        mask = _tril_mask()
        v = v_ref[...]
        r = lax.rsqrt(jnp.mean(v * v, axis=-1, keepdims=True) + EPS)
        vn = (v * r * vn_ref[...]).astype(BF16)
        for g in range(G):
            sl = slice(g * GMLP_GROUP_WIDTH, (g + 1) * GMLP_GROUP_WIDTH)
            wm = jnp.where(mask, ws_ref[g], 0.0).astype(BF16)
            sv = jnp.dot(wm, vn[:, sl], preferred_element_type=F32) + b_ref[:, sl]
            o_ref[:, sl] = (u_ref[:, sl] * sv).astype(BF16)

    return pl.pallas_call(
        body, name=name, grid=(S // C,),
        in_specs=[pl.BlockSpec((C, DG), lambda i: (i, 0)), pl.BlockSpec((C, DG), lambda i: (i, 1)),
                  pl.BlockSpec((1, DG), lambda i: (0, 0)), pl.BlockSpec((G, C, C), lambda i: (0, 0, 0)),
                  pl.BlockSpec((C, DG), lambda i: (0, 0))],
        out_specs=pl.BlockSpec((C, DG), lambda i: (i, 0)),
        out_shape=SDS((S, DG), BF16),
        compiler_params=_cp(("parallel",), 32 * 2 ** 20),
    )(z, z, v_norm, w_s, bias_full)


def _gmlp_gate_bwd(name, z, zpre, dgated, v_norm, w_s, w_s_t, bias_full):
    S, DG2 = z.shape
    DG = DG2 // 2
    G = DG // GMLP_GROUP_WIDTH
    C = GMLP_CHUNK

    def body(z_ref, zp_ref, dg_ref, vn_ref, ws_ref, wst_ref, b_ref, dz_ref, dws_ref, db_ref, dvn_ref):
        i = pl.program_id(0)
        mask = _tril_mask()
        mask_t = jnp.logical_not(mask) | (lax.broadcasted_iota(jnp.int32, (C, C), 0) == lax.broadcasted_iota(jnp.int32, (C, C), 1))
        u = z_ref[:, :DG]
        v = z_ref[:, DG:]
        r = lax.rsqrt(jnp.mean(v * v, axis=-1, keepdims=True) + EPS)
        vh = v * r
        gain = vn_ref[...]
        vn = (vh * gain).astype(BF16)
        dgt = dg_ref[...]

        @pl.when(i == 0)
        def _():
            dws_ref[...] = jnp.zeros_like(dws_ref)
            db_ref[...] = jnp.zeros_like(db_ref)
            dvn_ref[...] = jnp.zeros_like(dvn_ref)

        dvn_parts = []
        for g in range(G):
            sl = slice(g * GMLP_GROUP_WIDTH, (g + 1) * GMLP_GROUP_WIDTH)
            wm = jnp.where(mask, ws_ref[g], 0.0).astype(BF16)
            wmt = jnp.where(mask_t, wst_ref[g], 0.0).astype(BF16)
            sv = jnp.dot(wm, vn[:, sl], preferred_element_type=F32) + b_ref[:, sl]
            dgs = dgt[:, sl]
            du = dgs * sv
            dsv = dgs * u[:, sl]
            db_ref[:, sl] += dsv
            dsv_b = dsv.astype(BF16)
            dws = lax.dot_general(dsv_b, vn[:, sl], (((1,), (1,)), ((), ())), preferred_element_type=F32)
            dws_ref[g] += jnp.where(mask, dws, 0.0)
            dvn_parts.append(jnp.dot(wmt, dsv_b, preferred_element_type=F32))
            dz_ref[:, sl] = (du * zp_ref[:, sl].astype(F32)).astype(BF16)
        dvn_full = jnp.concatenate(dvn_parts, axis=1)
        dvn_ref[...] += jnp.sum(dvn_full * vh, axis=0, keepdims=True)
        dxh = dvn_full * gain
        dv = r * (dxh - vh * jnp.mean(dxh * vh, axis=-1, keepdims=True))
        dz_ref[:, DG:] = (dv * zp_ref[:, DG:].astype(F32)).astype(BF16)

    full = pl.BlockSpec((C, DG2), lambda i: (i, 0))
    wspec = pl.BlockSpec((G, C, C), lambda i: (0, 0, 0))
    return pl.pallas_call(
        body, name=name, grid=(S // C,),
        in_specs=[full, full, pl.BlockSpec((C, DG), lambda i: (i, 0)), pl.BlockSpec((1, DG), lambda i: (0, 0)),
                  wspec, wspec, pl.BlockSpec((C, DG), lambda i: (0, 0))],
        out_specs=[full, wspec, pl.BlockSpec((C, DG), lambda i: (0, 0)), pl.BlockSpec((1, DG), lambda i: (0, 0))],
        out_shape=[SDS((S, DG2), BF16), SDS((G, C, C), F32), SDS((C, DG), F32), SDS((1, DG), F32)],
        compiler_params=_cp(("arbitrary",), 40 * 2 ** 20),
    )(z, zpre, dgated, v_norm, w_s, w_s_t, bias_full)


def _alibi_slopes(n_heads):
    return [float(v) for v in np.exp2(np.float32(-8.0) * np.arange(1, n_heads + 1, dtype=np.float32) / np.float32(n_heads))]


def _dil_view(arr, dil):
    S, C = arr.shape
    return arr if dil == 1 else arr.reshape(S // dil, dil * C)


def _dil_spec(dil, HD, ncb, cb, bmap):
    return pl.BlockSpec((ATTN_BLOCK, HD), lambda r, b: (bmap(b), r * ncb + cb))


def _group_view(arr, gi, dil, HD):
    if dil == 1:
        return arr, arr.shape[1] // HD, gi
    return _dil_view(arr[:, gi * HD:(gi + 1) * HD], dil), 1, 0


def _dil_unview(arr, S):
    return arr.reshape(S, arr.size // S)


def _attn_mask(b):
    qi = lax.broadcasted_iota(jnp.int32, (ATTN_BLOCK, 2 * ATTN_BLOCK), 0)
    kj = lax.broadcasted_iota(jnp.int32, (ATTN_BLOCK, 2 * ATTN_BLOCK), 1)
    delta = qi + ATTN_BLOCK - kj
    valid = (delta >= 0) & (delta <= ATTN_BLOCK) & ((kj >= ATTN_BLOCK) | (b > 0))
    return valid, delta.astype(F32)


def _attn_fwd(name, qn, kn, vb, gi, dil):
    S, C = qn.shape
    NG = len(DILATIONS)
    HD = C // NG
    H = HD // HEAD_DIM
    L = S // dil
    nb = L // ATTN_BLOCK
    slopes = _alibi_slopes(H)

    def body(q_ref, kc_ref, kp_ref, vc_ref, vp_ref, o_ref, lse_ref):
        b = pl.program_id(1)
        valid, delta = _attn_mask(b)
        dist = delta * float(dil)
        for h in range(H):
            sl = slice(h * HEAD_DIM, (h + 1) * HEAD_DIM)
            k = jnp.concatenate([kp_ref[:, sl], kc_ref[:, sl]], axis=0)
            v = jnp.concatenate([vp_ref[:, sl], vc_ref[:, sl]], axis=0)
            s = lax.dot_general(q_ref[:, sl], k, (((1,), (1,)), ((), ())), preferred_element_type=F32)
            s = jnp.where(valid, s - slopes[h] * dist, -jnp.inf)
            m = jnp.max(s, axis=-1, keepdims=True)
            p = jnp.exp(s - m)
            l = jnp.sum(p, axis=-1, keepdims=True)
            o = jnp.dot(p.astype(BF16), v, preferred_element_type=F32)
            o_ref[:, sl] = o / l
            lse_ref[:, sl] = jnp.broadcast_to(m + jnp.log(l), (ATTN_BLOCK, HEAD_DIM))

    cur = lambda b: b
    prev = lambda b: jnp.maximum(b - 1, 0)
    (qv, ncb, cb), (kv, _, _), (vv, _, _) = (_group_view(a, gi, dil, HD) for a in (qn, kn, vb))
    view_shape = (L, dil * HD)
    o, lse = pl.pallas_call(
        body, name=name, grid=(dil, nb),
        in_specs=[_dil_spec(dil, HD, ncb, cb, cur), _dil_spec(dil, HD, ncb, cb, cur), _dil_spec(dil, HD, ncb, cb, prev),
                  _dil_spec(dil, HD, ncb, cb, cur), _dil_spec(dil, HD, ncb, cb, prev)],
        out_specs=[_dil_spec(dil, HD, 1, 0, cur), _dil_spec(dil, HD, 1, 0, cur)],
        out_shape=[SDS(view_shape, F32), SDS(view_shape, F32)],
        compiler_params=_cp(("parallel", "parallel"), 32 * 2 ** 20),
    )(qv, kv, kv, vv, vv)
    return _dil_unview(o, S), _dil_unview(lse, S)


def _attn_combine(name, o_list, lse_list, d_o=None):
    S, HD = o_list[0].shape
    H = HD // HEAD_DIM
    ng = len(o_list)
    tr = _row_tile(S, HD * 4, 2 ** 19)

    def body(*refs):
        o_refs = refs[:ng]
        l_refs = refs[ng:2 * ng]
        ls = [r[...] for r in l_refs]
        m = ls[0]
        for t in ls[1:]:
            m = jnp.maximum(m, t)
        es = [jnp.exp(t - m) for t in ls]
        z = es[0]
        for t in es[1:]:
            z = z + t
        o = es[0] * o_refs[0][...]
        for e, r in zip(es[1:], o_refs[1:]):
            o = o + e * r[...]
        o = o / z
        if d_o is None:
            refs[2 * ng][...] = o.astype(BF16)
        else:
            do_ref, lse_ref, dl_ref, dob_ref = refs[2 * ng:]
            dov = do_ref[...]
            lse_ref[...] = m + jnp.log(z)
            dob_ref[...] = dov.astype(BF16)
            prod = dov * o
            for h in range(H):
                sl = slice(h * HEAD_DIM, (h + 1) * HEAD_DIM)
                dl_ref[:, sl] = jnp.broadcast_to(jnp.sum(prod[:, sl], axis=-1, keepdims=True), (tr, HEAD_DIM))

    row = pl.BlockSpec((tr, HD), lambda i: (i, 0))
    if d_o is None:
        return pl.pallas_call(
            body, name=name, grid=(S // tr,), in_specs=[row] * (2 * ng), out_specs=row,
            out_shape=SDS((S, HD), BF16), compiler_params=_cp(("parallel",), 40 * 2 ** 20),
        )(*o_list, *lse_list)
    return pl.pallas_call(
        body, name=name, grid=(S // tr,), in_specs=[row] * (2 * ng + 1), out_specs=[row, row, row],
        out_shape=[SDS((S, HD), F32), SDS((S, HD), F32), SDS((S, HD), BF16)],
        compiler_params=_cp(("parallel",), 48 * 2 ** 20),
    )(*o_list, *lse_list, d_o)


def _attn_bwd(name, qn, kn, vb, dob, lse, delta_rows, gi, dil):
    S, C = qn.shape
    NG = len(DILATIONS)
    HD = C // NG
    H = HD // HEAD_DIM
    L = S // dil
    nb = L // ATTN_BLOCK
    slopes = _alibi_slopes(H)
    B = ATTN_BLOCK

    def body(q_ref, kc_ref, kp_ref, vc_ref, vp_ref, do_ref, lse_ref, dl_ref, dq_ref, dk_ref, dv_ref, ck_ref, cv_ref):
        b = pl.program_id(1)

        @pl.when(b == 0)
        def _():
            ck_ref[...] = jnp.zeros_like(ck_ref)
            cv_ref[...] = jnp.zeros_like(cv_ref)

        @pl.when(b < nb)
        def _():
            valid, delta = _attn_mask(b)
            dist = delta * float(dil)
            for h in range(H):
                sl = slice(h * HEAD_DIM, (h + 1) * HEAD_DIM)
                q = q_ref[:, sl]
                k = jnp.concatenate([kp_ref[:, sl], kc_ref[:, sl]], axis=0)
                v = jnp.concatenate([vp_ref[:, sl], vc_ref[:, sl]], axis=0)
                do = do_ref[:, sl]
                s = lax.dot_general(q, k, (((1,), (1,)), ((), ())), preferred_element_type=F32)
                s = jnp.where(valid, s - slopes[h] * dist, -jnp.inf)
                lse2 = jnp.concatenate([lse_ref[:, sl], lse_ref[:, sl]], axis=1)
                dl2 = jnp.concatenate([dl_ref[:, sl], dl_ref[:, sl]], axis=1)
                p = jnp.exp(s - lse2)
                dp = lax.dot_general(do, v, (((1,), (1,)), ((), ())), preferred_element_type=F32)
                ds = (p * (dp - dl2)).astype(BF16)
                dq_ref[:, sl] = jnp.dot(ds, k, preferred_element_type=F32).astype(BF16)
                dk2 = lax.dot_general(ds, q, (((0,), (0,)), ((), ())), preferred_element_type=F32)
                dv2 = lax.dot_general(p.astype(BF16), do, (((0,), (0,)), ((), ())), preferred_element_type=F32)
                dk_ref[:, sl] = (ck_ref[:, sl] + dk2[:B]).astype(BF16)
                dv_ref[:, sl] = (cv_ref[:, sl] + dv2[:B]).astype(BF16)
                ck_ref[:, sl] = dk2[B:]
                cv_ref[:, sl] = dv2[B:]

        @pl.when(b == nb)
        def _():
            dk_ref[...] = ck_ref[...].astype(BF16)
            dv_ref[...] = cv_ref[...].astype(BF16)

    cur = lambda b: jnp.minimum(b, nb - 1)
    prev = lambda b: jnp.maximum(jnp.minimum(b, nb - 1) - 1, 0)
    late = lambda b: jnp.maximum(b - 1, 0)
    (qv, ncb, cb), (kv, _, _), (vv, _, _) = (_group_view(a, gi, dil, HD) for a in (qn, kn, vb))
    dov, lsev, dlv = _dil_view(dob, dil), _dil_view(lse, dil), _dil_view(delta_rows, dil)
    view_shape = (L, dil * HD)
    one = lambda m: _dil_spec(dil, HD, 1, 0, m)
    grp = lambda m: _dil_spec(dil, HD, ncb, cb, m)
    dq, dk, dv = pl.pallas_call(
        body, name=name, grid=(dil, nb + 1),
        in_specs=[grp(cur), grp(cur), grp(prev), grp(cur), grp(prev), one(cur), one(cur), one(cur)],
        out_specs=[one(cur), one(late), one(late)],
        out_shape=[SDS(view_shape, BF16)] * 3,
        scratch_shapes=[pltpu.VMEM((B, HD), F32), pltpu.VMEM((B, HD), F32)],
        compiler_params=_cp(("arbitrary", "arbitrary"), 40 * 2 ** 20),
    )(qv, kv, kv, vv, vv, dov, lsev, dlv)
    return _dil_unview(dq, S), _dil_unview(dk, S), _dil_unview(dv, S)


def _cast_into_gathered(name, w, l, sa, chip_arr):
    L, r, c = w.shape
    tr = _row_tile(r, c * 4)
    nrb = r // tr
    src = pl.BlockSpec((None, tr, c), lambda i, chip: (l, i, 0))
    if sa == 0:
        dst, shape = pl.BlockSpec((tr, c), lambda i, chip: (chip[0] * nrb + i, 0)), (N_CHIPS * r, c)
    else:
        dst, shape = pl.BlockSpec((tr, c), lambda i, chip: (i, chip[0])), (r, N_CHIPS * c)

    def body(chip_ref, i_ref, o_ref):
        o_ref[...] = i_ref[...].astype(BF16)

    return pl.pallas_call(
        body, name=name,
        grid_spec=pltpu.PrefetchScalarGridSpec(num_scalar_prefetch=1, grid=(nrb,), in_specs=[src], out_specs=dst),
        out_shape=SDS(shape, BF16), compiler_params=_cp(("parallel",), 32 * 2 ** 20),
    )(chip_arr, w)


def _add_half(name, dw, land, sa, c_arr):
    hr, hc = land.shape
    tr = _row_tile(hr, hc * 2)
    nrb = hr // tr
    if sa == 1:
        mine = pl.BlockSpec((tr, hc), lambda i, c: (c[0] * nrb + i, 0))
    else:
        mine = pl.BlockSpec((tr, hc), lambda i, c: (i, c[0]))
    other = pl.BlockSpec((tr, hc), lambda i, c: (i, 0))

    def body(c_ref, a_ref, b_ref, o_ref):
        o_ref[...] = (a_ref[...].astype(F32) + b_ref[...].astype(F32)).astype(BF16)

    return pl.pallas_call(
        body, name=name,
        grid_spec=pltpu.PrefetchScalarGridSpec(num_scalar_prefetch=1, grid=(nrb,), in_specs=[mine, other], out_specs=other),
        out_shape=SDS((hr, hc), BF16), compiler_params=_cp(("parallel",), 32 * 2 ** 20),
    )(c_arr, dw, land)


def _sum_slots(name, slots, out_dtype=F32):
    n, R, C = slots.shape
    tr = _row_tile(R, C * n * jnp.dtype(slots.dtype).itemsize, 2 ** 21)

    def body(s_ref, o_ref):
        acc = s_ref[0].astype(F32)
        for k in range(1, n):
            acc = acc + s_ref[k].astype(F32)
        o_ref[...] = acc.astype(out_dtype)

    return pl.pallas_call(
        body, name=name, grid=(R // tr,),
        in_specs=[pl.BlockSpec((n, tr, C), lambda i: (0, i, 0))], out_specs=pl.BlockSpec((tr, C), lambda i: (i, 0)),
        out_shape=SDS((R, C), out_dtype), compiler_params=_cp(("parallel",), 32 * 2 ** 20),
    )(slots)


def _sum_into(name, part, slots, buf, l, sa, where):
    n, pr, pc = slots.shape
    tr = _row_tile(pr, pc * (n + 1) * jnp.dtype(slots.dtype).itemsize, 2 ** 21)
    nrb = pr // tr
    if sa == 1:
        dst = pl.BlockSpec((None, tr, pc), lambda i, c, j: (l, c[0] * nrb + i, 0))
        own = pl.BlockSpec((tr, pc), lambda i, c, j: (i, j[0]))
    else:
        dst = pl.BlockSpec((None, tr, pc), lambda i, c, j: (l, i, c[0]))
        own = pl.BlockSpec((tr, pc), lambda i, c, j: (j[0] * nrb + i, 0))

    def body(c_ref, j_ref, p_ref, s_ref, b_ref, o_ref):
        acc = p_ref[...].astype(F32)
        for k in range(n):
            acc = acc + s_ref[k].astype(F32)
        o_ref[...] = acc

    return pl.pallas_call(
        body, name=name,
        grid_spec=pltpu.PrefetchScalarGridSpec(
            num_scalar_prefetch=2, grid=(nrb,),
            in_specs=[own, pl.BlockSpec((n, tr, pc), lambda i, c, j: (0, i, 0)), ANY], out_specs=dst),
        out_shape=SDS(buf.shape, buf.dtype), input_output_aliases={4: 0},
        compiler_params=_cp(("parallel",), 32 * 2 ** 20),
    )(where[0], where[1], part, slots, buf)


def _adamw_body(g_ref, w_ref, m_ref, v_ref, go_ref, d_ref, mo_ref, vo_ref):
    bc1 = 1.0 - ADAM_B1 ** ADAM_STEP
    bc2 = 1.0 - ADAM_B2 ** ADAM_STEP
    gv = g_ref[...]
    mn = ADAM_B1 * m_ref[...] + (1.0 - ADAM_B1) * gv
    vn = ADAM_B2 * v_ref[...] + (1.0 - ADAM_B2) * (gv * gv)
    go_ref[...] = gv
    mo_ref[...] = mn
    vo_ref[...] = vn
    d_ref[...] = -ADAM_LR * ((mn / bc1) / (jnp.sqrt(vn / bc2) + ADAM_EPS) + ADAM_WD * w_ref[...])


def _adamw_layer(name, g, w, m, v, l, outs):
    L, r, c = g.shape
    tr = _row_tile(r, c * 4, 2 ** 19)
    spec = pl.BlockSpec((None, tr, c), lambda i: (l, i, 0))

    def body(g_ref, w_ref, m_ref, v_ref, a0, a1, a2, a3, go_ref, d_ref, mo_ref, vo_ref):
        _adamw_body(g_ref, w_ref, m_ref, v_ref, go_ref, d_ref, mo_ref, vo_ref)

    return list(pl.pallas_call(
        body, name=name, grid=(r // tr,), in_specs=[spec] * 4 + [ANY] * 4, out_specs=[spec] * 4,
        out_shape=[SDS((L, r, c), F32)] * 4, input_output_aliases={4 + k: k for k in range(4)},
        compiler_params=_cp(("parallel",), 32 * 2 ** 20),
    )(g, w, m, v, *outs))


def _adamw(name, g, w, m, v):
    R, C = g.shape
    tr = _row_tile(R, C * 4, 2 ** 19)

    def body(g_ref, w_ref, m_ref, v_ref, go_ref, d_ref, mo_ref, vo_ref):
        _adamw_body(g_ref, w_ref, m_ref, v_ref, go_ref, d_ref, mo_ref, vo_ref)

    spec = pl.BlockSpec((tr, C), lambda i: (i, 0))
    return pl.pallas_call(
        body, name=name, grid=(R // tr,), in_specs=[spec] * 4, out_specs=[spec] * 4,
        out_shape=[SDS((R, C), F32)] * 4, compiler_params=_cp(("parallel",), 32 * 2 ** 20),
    )(g, w, m, v)


ANY = pl.BlockSpec(memory_space=pl.ANY)


def _coords():
    return lax.axis_index("x"), lax.axis_index("y"), lax.axis_index("c")


def _other_chips(x, y):
    return [((1 - x, y), 2 * (1 - x) + y), ((x, 1 - y), 2 * x + (1 - y)), ((1 - x, 1 - y), 2 * (1 - x) + (1 - y))]


def _win(ref, axis, start, size):
    if not isinstance(start, int):
        start = pl.multiple_of(start, LANE if axis == 1 else BF16_ROWS)
    if axis == 0:
        return ref.at[pl.ds(start, size), :]
    return ref.at[:, pl.ds(start, size)]


def _rcopy(src, dst, ssem, rsem, dev):
    return pltpu.make_async_remote_copy(src_ref=src, dst_ref=dst, send_sem=ssem, recv_sem=rsem,
                                        device_id=dev, device_id_type=MESH)


HBM = pl.BlockSpec(memory_space=pltpu.HBM)
SEM = pl.BlockSpec(memory_space=pltpu.SEMAPHORE)
TOKEN = pl.BlockSpec(memory_space=pltpu.VMEM)
EFFECT = pltpu.SideEffectType.DATAFLOW_SIDE_EFFECTING


def _in_hbm(a):
    return pltpu.with_memory_space_constraint(a, pltpu.HBM)


def _ag_geometry(mats, sas, o):
    sa = sas[o]
    return sa, 1 - sa, mats[o].shape[sa] // N_CHIPS, mats[o].shape[1 - sa] // 2


def _ag_ici_copy(mats, sas, refs, o, k, sems, x, y, c):
    sa, ha, wl, hl = _ag_geometry(mats, sas, o)
    chip, jk = _other_chips(x, y)[k]
    mine = _win(_win(refs[o], sa, (2 * x + y) * wl, wl), ha, c * hl, hl)
    landed = _win(_win(refs[o], sa, jk * wl, wl), ha, c * hl, hl)
    return mine, landed, (*chip, c)


def _ag_start(name, mats, sas, after):
    n = len(mats)

    def body(*refs):
        ins = refs[:n]
        s_sem, r_sem = refs[n + 1], refs[n + 2]
        token = refs[2 * n + 3]
        x, y, c = _coords()
        for o in range(n):
            for k in range(3):
                mine, _, dev = _ag_ici_copy(mats, sas, ins, o, k, None, x, y, c)
                _rcopy(mine, mine, s_sem.at[3 * o + k], r_sem.at[3 * o + k], dev).start()
        token[...] = jnp.zeros_like(token)

    out = pl.pallas_call(
        body, name=name,
        out_shape=(pltpu.SemaphoreType.DMA((3 * n,)), pltpu.SemaphoreType.DMA((3 * n,)),
                   *[pltpu.HBM(m.shape, m.dtype) for m in mats], SDS((8, LANE), F32)),
        in_specs=[HBM] * n + [ANY], out_specs=(SEM, SEM, *[HBM] * n, TOKEN),
        input_output_aliases={k: 2 + k for k in range(n)},
        compiler_params=pltpu.CompilerParams(has_side_effects=EFFECT),
    )(*[_in_hbm(m) for m in mats], after)
    return out[0], out[1], list(out[2:2 + n]), out[2 + n]


def _ag_wait(name, mats, sas, s_sem, r_sem, after):
    n = len(mats)

    def body(*refs):
        ins = refs[:n]
        s_ref, r_ref = refs[n], refs[n + 1]
        x, y, c = _coords()
        for o in range(n):
            for k in range(3):
                mine, landed, dev = _ag_ici_copy(mats, sas, ins, o, k, None, x, y, c)
                cp = _rcopy(mine, landed, s_ref.at[3 * o + k], r_ref.at[3 * o + k], dev)
                cp.wait_send()
                cp.wait_recv()

    after = list(after) if isinstance(after, (list, tuple)) else [after]
    return list(pl.pallas_call(
        body, name=name, out_shape=[pltpu.HBM(m.shape, m.dtype) for m in mats],
        in_specs=[HBM] * n + [SEM, SEM] + [ANY] * len(after), out_specs=[HBM] * n,
        input_output_aliases={k: k for k in range(n)},
        compiler_params=pltpu.CompilerParams(has_side_effects=EFFECT),
    )(*mats, s_sem, r_sem, *after))


def _ag_forward(name, mats, sas):
    n = len(mats)

    def body(*refs):
        outs = refs[n:2 * n]
        s_fwd, r_fwd = refs[2 * n:]
        x, y, c = _coords()
        sibling = (x, y, 1 - c)
        sends = []
        for o in range(n):
            sa, ha, wl, hl = _ag_geometry(mats, sas, o)
            for k, (chip, jk) in enumerate(_other_chips(x, y)):
                landed = _win(_win(outs[o], sa, jk * wl, wl), ha, c * hl, hl)
                fwd = _rcopy(landed, landed, s_fwd.at[3 * o + k], r_fwd.at[3 * o + k], sibling)
                fwd.start()
                sends.append(fwd)
        for o in range(n):
            sa, ha, wl, hl = _ag_geometry(mats, sas, o)
            for k, (chip, jk) in enumerate(_other_chips(x, y)):
                got = _win(_win(outs[o], sa, jk * wl, wl), ha, (1 - c) * hl, hl)
                _rcopy(got, got, s_fwd.at[3 * o + k], r_fwd.at[3 * o + k], sibling).wait_recv()
        for cp in sends:
            cp.wait_send()

    return list(pl.pallas_call(
        body, name=name, in_specs=[ANY] * n, out_specs=[ANY] * n,
        out_shape=[SDS(m.shape, m.dtype) for m in mats], input_output_aliases={k: k for k in range(n)},
        scratch_shapes=[pltpu.SemaphoreType.DMA((3 * n,)), pltpu.SemaphoreType.DMA((3 * n,))],
        compiler_params=pltpu.CompilerParams(has_side_effects=True),
    )(*mats))


def _all_gather_vec(name, v):
    Lv, cv = v.shape

    def body(v_ref, o_ref, loc_sem, s_sem, r_sem):
        x, y, c = _coords()
        jme = 2 * x + y
        chips = _other_chips(x, y)
        mine = _win(o_ref, 1, jme * cv, cv)
        loc = pltpu.make_async_copy(v_ref, mine, loc_sem)
        loc.start()
        sends = []
        for k, (chip, _) in enumerate(chips):
            cp = _rcopy(v_ref, mine, s_sem.at[k], r_sem.at[k], (*chip, c))
            cp.start()
            sends.append(cp)
        for k, (chip, jk) in enumerate(chips):
            got = _win(o_ref, 1, jk * cv, cv)
            _rcopy(got, got, s_sem.at[k], r_sem.at[k], (*chip, c)).wait_recv()
        for cp in sends:
            cp.wait_send()
        loc.wait()

    return pl.pallas_call(
        body, name=name, in_specs=[ANY], out_specs=ANY, out_shape=SDS((Lv, 4 * cv), v.dtype),
        scratch_shapes=[pltpu.SemaphoreType.DMA, pltpu.SemaphoreType.DMA((3,)), pltpu.SemaphoreType.DMA((3,))],
        compiler_params=pltpu.CompilerParams(has_side_effects=True),
    )(v)


def _half_shape(shape, sa):
    R, C = shape
    return (R // 2, C) if sa == 1 else (R, C // 2)


def _rs_sibling_copy(dws, sas, d_refs, land_refs, o, x, y, c):
    ha = 1 - sas[o]
    hl = dws[o].shape[ha] // 2
    return _win(d_refs[o], ha, (1 - c) * hl, hl), land_refs[o], (x, y, 1 - c)


def _rs_sibling_start(name, dws, sas, after):
    n = len(dws)

    def body(*refs):
        ins = refs[:n]
        s_sem, r_sem = refs[n + 1], refs[n + 2]
        lands = refs[2 * n + 3:3 * n + 3]
        token = refs[3 * n + 3]
        x, y, c = _coords()
        for o in range(n):
            src, dst, dev = _rs_sibling_copy(dws, sas, ins, lands, o, x, y, c)
            _rcopy(src, dst, s_sem.at[o], r_sem.at[o], dev).start()
        token[...] = jnp.zeros_like(token)

    out = pl.pallas_call(
        body, name=name,
        out_shape=(pltpu.SemaphoreType.DMA((n,)), pltpu.SemaphoreType.DMA((n,)),
                   *[pltpu.HBM(d.shape, d.dtype) for d in dws],
                   *[pltpu.HBM(_half_shape(d.shape, sa), d.dtype) for d, sa in zip(dws, sas)],
                   SDS((8, LANE), F32)),
        in_specs=[HBM] * n + [ANY], out_specs=(SEM, SEM, *[HBM] * (2 * n), TOKEN),
        input_output_aliases={k: 2 + k for k in range(n)},
        compiler_params=pltpu.CompilerParams(has_side_effects=EFFECT),
    )(*[_in_hbm(d) for d in dws], after)
    return out[0], out[1], list(out[2:2 + n]), list(out[2 + n:2 + 2 * n]), out[2 + 2 * n]


def _rs_sibling_wait(name, dws, lands, sas, s_sem, r_sem, after):
    n = len(dws)

    def body(*refs):
        d_refs, land_refs = refs[:n], refs[n:2 * n]
        s_ref, r_ref = refs[2 * n], refs[2 * n + 1]
        x, y, c = _coords()
        for o in range(n):
            src, dst, dev = _rs_sibling_copy(dws, sas, d_refs, land_refs, o, x, y, c)
            cp = _rcopy(src, dst, s_ref.at[o], r_ref.at[o], dev)
            cp.wait_send()
            cp.wait_recv()

    after = list(after) if isinstance(after, (list, tuple)) else [after]
    out = pl.pallas_call(
        body, name=name,
        out_shape=[pltpu.HBM(a.shape, a.dtype) for a in (*dws, *lands)],
        in_specs=[HBM] * (2 * n) + [SEM, SEM] + [ANY] * len(after), out_specs=[HBM] * (2 * n),
        input_output_aliases={k: k for k in range(2 * n)},
        compiler_params=pltpu.CompilerParams(has_side_effects=EFFECT),
    )(*dws, *lands, s_sem, r_sem, *after)
    return list(out[:n]), list(out[n:])


def _rs_piece_shape(p, sa):
    hr, hc = p.shape
    return (hr // N_CHIPS, hc) if sa == 0 else (hr, hc // N_CHIPS)


def _rs_ici_copy(parts, sas, p_refs, slot_refs, o, k, x, y, c):
    sa = sas[o]
    pl_ = parts[o].shape[sa] // N_CHIPS
    chip, jk = _other_chips(x, y)[k]
    return _win(p_refs[o], sa, jk * pl_, pl_), slot_refs[o].at[k], (*chip, c)


def _rs_start(name, parts, sas, after):
    n = len(parts)

    def body(*refs):
        ins = refs[:n]
        s_sem, r_sem = refs[n + 1], refs[n + 2]
        slots = refs[2 * n + 3:3 * n + 3]
        token = refs[3 * n + 3]
        x, y, c = _coords()
        for o in range(n):
            for k in range(3):
                src, dst, dev = _rs_ici_copy(parts, sas, ins, slots, o, k, x, y, c)
                _rcopy(src, dst, s_sem.at[3 * o + k], r_sem.at[3 * o + k], dev).start()
        token[...] = jnp.zeros_like(token)

    out = pl.pallas_call(
        body, name=name,
        out_shape=(pltpu.SemaphoreType.DMA((3 * n,)), pltpu.SemaphoreType.DMA((3 * n,)),
                   *[pltpu.HBM(p.shape, p.dtype) for p in parts],
                   *[pltpu.HBM((3,) + _rs_piece_shape(p, sa), p.dtype) for p, sa in zip(parts, sas)],
                   SDS((8, LANE), F32)),
        in_specs=[HBM] * n + [ANY], out_specs=(SEM, SEM, *[HBM] * (2 * n), TOKEN),
        input_output_aliases={k: 2 + k for k in range(n)},
        compiler_params=pltpu.CompilerParams(has_side_effects=EFFECT),
    )(*[_in_hbm(p) for p in parts], after)
    return out[0], out[1], list(out[2:2 + n]), list(out[2 + n:2 + 2 * n]), out[2 + 2 * n]


def _rs_wait(name, parts, slots, sas, s_sem, r_sem, after):
    n = len(parts)

    def body(*refs):
        p_refs, slot_refs = refs[:n], refs[n:2 * n]
        s_ref, r_ref = refs[2 * n], refs[2 * n + 1]
        x, y, c = _coords()
        for o in range(n):
            for k in range(3):
                src, dst, dev = _rs_ici_copy(parts, sas, p_refs, slot_refs, o, k, x, y, c)
                cp = _rcopy(src, dst, s_ref.at[3 * o + k], r_ref.at[3 * o + k], dev)
                cp.wait_send()
                cp.wait_recv()

    after = list(after) if isinstance(after, (list, tuple)) else [after]
    out = pl.pallas_call(
        body, name=name,
        out_shape=[pltpu.HBM(a.shape, a.dtype) for a in (*parts, *slots)],
        in_specs=[HBM] * (2 * n) + [SEM, SEM] + [ANY] * len(after), out_specs=[HBM] * (2 * n),
        input_output_aliases={k: k for k in range(2 * n)},
        compiler_params=pltpu.CompilerParams(has_side_effects=EFFECT),
    )(*parts, *slots, s_sem, r_sem, *after)
    return list(out[:n]), list(out[n:])


def _rs_share_copy(sas, layers, buf_idx, bufs, refs, o, x, y, c):
    ha = 1 - sas[o]
    hl = bufs[buf_idx[o]].shape[1 + ha] // 2
    layer = refs[buf_idx[o]].at[layers[o]]
    return _win(layer, ha, c * hl, hl), _win(layer, ha, (1 - c) * hl, hl), (x, y, 1 - c)


def _rs_share_start(name, sas, layers, buf_idx, bufs, after):
    n, nbuf = len(sas), len(bufs)

    def body(*refs):
        ins = refs[:nbuf]
        s_sem, r_sem = refs[nbuf + 1], refs[nbuf + 2]
        token = refs[2 * nbuf + 3]
        x, y, c = _coords()
        for o in range(n):
            mine, _, dev = _rs_share_copy(sas, layers, buf_idx, bufs, ins, o, x, y, c)
            _rcopy(mine, mine, s_sem.at[o], r_sem.at[o], dev).start()
        token[...] = jnp.zeros_like(token)

    out = pl.pallas_call(
        body, name=name,
        out_shape=(pltpu.SemaphoreType.DMA((n,)), pltpu.SemaphoreType.DMA((n,)),
                   *[pltpu.HBM(b.shape, b.dtype) for b in bufs], SDS((8, LANE), F32)),
        in_specs=[HBM] * nbuf + [ANY], out_specs=(SEM, SEM, *[HBM] * nbuf, TOKEN),
        input_output_aliases={k: 2 + k for k in range(nbuf)},
        compiler_params=pltpu.CompilerParams(has_side_effects=EFFECT),
    )(*[_in_hbm(b) for b in bufs], after)
    return out[0], out[1], list(out[2:2 + nbuf]), out[2 + nbuf]


def _rs_share_wait(name, sas, layers, buf_idx, bufs, s_sem, r_sem, after):
    n, nbuf = len(sas), len(bufs)

    def body(*refs):
        ins = refs[:nbuf]
        s_ref, r_ref = refs[nbuf], refs[nbuf + 1]
        x, y, c = _coords()
        for o in range(n):
            mine, got, dev = _rs_share_copy(sas, layers, buf_idx, bufs, ins, o, x, y, c)
            cp = _rcopy(mine, got, s_ref.at[o], r_ref.at[o], dev)
            cp.wait_send()
            cp.wait_recv()

    after = list(after) if isinstance(after, (list, tuple)) else [after]
    return list(pl.pallas_call(
        body, name=name, out_shape=[pltpu.HBM(b.shape, b.dtype) for b in bufs],
        in_specs=[HBM] * nbuf + [SEM, SEM] + [ANY] * len(after), out_specs=[HBM] * nbuf,
        input_output_aliases={k: k for k in range(nbuf)},
        compiler_params=pltpu.CompilerParams(has_side_effects=EFFECT),
    )(*bufs, s_sem, r_sem, *after))


def _place_own(name, packed, me_arr):
    R, C = packed.shape
    tr = _row_tile(R, C * 4)

    def body(me_ref, p_ref, o_ref):
        o_ref[...] = p_ref[...]

    return pl.pallas_call(
        body, name=name,
        grid_spec=pltpu.PrefetchScalarGridSpec(
            num_scalar_prefetch=1, grid=(R // tr,),
            in_specs=[pl.BlockSpec((tr, C), lambda i, me: (i, 0))],
            out_specs=pl.BlockSpec((None, tr, C), lambda i, me: (me[0], i, 0))),
        out_shape=SDS((N_DEV, R, C), packed.dtype), compiler_params=_cp(("parallel",), 32 * 2 ** 20),
    )(me_arr, packed)


def _exchange_copy(s_refs, k, x, y, c):
    px = 1 - x if k & 4 else x
    py = 1 - y if k & 2 else y
    pc = 1 - c if k & 1 else c
    return s_refs.at[4 * x + 2 * y + c], s_refs.at[4 * px + 2 * py + pc], (px, py, pc)


def _exchange_start(name, slots, after):
    def body(s_ref, after_ref, s_sem, r_sem, out_ref, token):
        x, y, c = _coords()
        for k in range(1, N_DEV):
            mine, _, dev = _exchange_copy(s_ref, k, x, y, c)
            _rcopy(mine, mine, s_sem.at[k - 1], r_sem.at[k - 1], dev).start()
        token[...] = jnp.zeros_like(token)

    out = pl.pallas_call(
        body, name=name,
        out_shape=(pltpu.SemaphoreType.DMA((N_DEV - 1,)), pltpu.SemaphoreType.DMA((N_DEV - 1,)),
                   pltpu.HBM(slots.shape, slots.dtype), SDS((8, LANE), F32)),
        in_specs=[HBM, ANY], out_specs=(SEM, SEM, HBM, TOKEN), input_output_aliases={0: 2},
        compiler_params=pltpu.CompilerParams(has_side_effects=EFFECT),
    )(_in_hbm(slots), after)
    return out


def _exchange_wait(name, slots, s_sem, r_sem, after):
    def body(s_ref, s_sem_ref, r_sem_ref, *rest):
        x, y, c = _coords()
        for k in range(1, N_DEV):
            mine, theirs, dev = _exchange_copy(s_ref, k, x, y, c)
            cp = _rcopy(mine, theirs, s_sem_ref.at[k - 1], r_sem_ref.at[k - 1], dev)
            cp.wait_send()
            cp.wait_recv()

    after = list(after) if isinstance(after, (list, tuple)) else [after]
    return pl.pallas_call(
        body, name=name, out_shape=pltpu.HBM(slots.shape, slots.dtype),
        in_specs=[HBM, SEM, SEM] + [ANY] * len(after), out_specs=HBM, input_output_aliases={0: 0},
        compiler_params=pltpu.CompilerParams(has_side_effects=EFFECT),
    )(slots, s_sem, r_sem, *after)


def _pack(arrays):
    rows = []
    for a in arrays:
        flat = a.reshape(-1).astype(F32)
        pad = (-flat.size) % PACK_TILE
        rows.append(jnp.pad(flat, (0, pad)).reshape(-1, LANE))
    return jnp.concatenate(rows, axis=0)


def _unpack(packed, shapes):
    out, row = [], 0
    for s in shapes:
        size = int(np.prod(s)) if len(s) else 1
        nrows = -(-size // PACK_TILE) * (PACK_TILE // LANE)
        out.append(packed[row:row + nrows].reshape(-1)[:size].reshape(s))
        row += nrows
    return out


BIG_WEIGHTS = {
    "ffn1_w_gate": 1, "ffn1_w_up": 1, "ffn1_w_down": 0, "ffn2_w_gate": 1, "ffn2_w_up": 1, "ffn2_w_down": 0,
    "gmlp_w_in": 1, "gmlp_w_out": 0, "w_kv": 1, "attn_w_q": 1, "attn_w_o": 0,
}
SMALL_WEIGHTS = ("ffn1_norm", "mix_norm", "ffn2_norm", "gmlp_w_s", "gmlp_b_s", "kv_norm", "k_norm", "attn_q_norm")
WEIGHT_ORDER = ("ffn1_norm", "ffn1_w_gate", "ffn1_w_up", "ffn1_w_down", "mix_norm", "ffn2_norm", "ffn2_w_gate",
                "ffn2_w_up", "ffn2_w_down", "gmlp_w_in", "gmlp_v_norm", "gmlp_w_s", "gmlp_b_s", "gmlp_w_out",
                "kv_norm", "w_kv", "k_norm", "attn_w_q", "attn_q_norm", "attn_w_o")


def _ep_all(accs, ex):
    return list(accs)


def _as3d(w):
    return w if w.ndim == 3 else w.reshape((1,) + w.shape)


def kernel(x, ffn1_norm, ffn1_w_gate, ffn1_w_up, ffn1_w_down, mix_norm, ffn2_norm, ffn2_w_gate, ffn2_w_up, ffn2_w_down, gmlp_w_in, gmlp_v_norm, gmlp_w_s, gmlp_b_s, gmlp_w_out, kv_norm, w_kv, k_norm, attn_w_q, attn_q_norm, attn_w_o, loss_target, m_ffn1_norm, m_ffn1_w_gate, m_ffn1_w_up, m_ffn1_w_down, m_mix_norm, m_ffn2_norm, m_ffn2_w_gate, m_ffn2_w_up, m_ffn2_w_down, m_gmlp_w_in, m_gmlp_v_norm, m_gmlp_w_s, m_gmlp_b_s, m_gmlp_w_out, m_kv_norm, m_w_kv, m_k_norm, m_attn_w_q, m_attn_q_norm, m_attn_w_o, v_ffn1_norm, v_ffn1_w_gate, v_ffn1_w_up, v_ffn1_w_down, v_mix_norm, v_ffn2_norm, v_ffn2_w_gate, v_ffn2_w_up, v_ffn2_w_down, v_gmlp_w_in, v_gmlp_v_norm, v_gmlp_w_s, v_gmlp_b_s, v_gmlp_w_out, v_kv_norm, v_w_kv, v_k_norm, v_attn_w_q, v_attn_q_norm, v_attn_w_o):
    P = dict(locals())
    assert x.shape[0] == 1, "one sample per device"
    S, D = x.shape[1], x.shape[2]
    NL = ffn1_norm.shape[0]
    NG = len(DILATIONS)
    HD = attn_w_o.shape[1] * N_CHIPS
    H = HD // HEAD_DIM
    DG = gmlp_w_out.shape[1] * N_CHIPS
    G = DG // GMLP_GROUP_WIDTH
    assert all((S // d) % ATTN_BLOCK == 0 for d in DILATIONS) and S % GMLP_CHUNK == 0
    xs = x.reshape(S, D)
    tgt = loss_target.reshape(S, D)
    c_arr = lax.axis_index("c").astype(jnp.int32).reshape(1)
    chip = 2 * lax.axis_index("x") + lax.axis_index("y")
    chip_arr = chip.astype(jnp.int32).reshape(1)
    kv_layer = N_A_LAYERS - 1


    def layer_weights(l):
        names = [("ffn1_w_gate", l), ("ffn1_w_up", l), ("ffn1_w_down", l), ("ffn2_w_gate", l), ("ffn2_w_up", l), ("ffn2_w_down", l)]
        if l < N_A_LAYERS:
            names += [("gmlp_w_in", l), ("gmlp_w_out", l)]
        else:
            names += [("attn_w_q", l - N_A_LAYERS), ("attn_w_o", l - N_A_LAYERS)]
        if l == kv_layer:
            names += [("w_kv", 0)]
        return names

    W = {}
    ag_open = {}
    n_first = 3
    ag_units = {"0a": layer_weights(0)[:n_first], "0b": layer_weights(0)[n_first:]}
    ag_units.update({l: layer_weights(l) for l in range(1, NL)})

    def cast_unit(u):
        return [_cast_into_gathered("cast_shard", _as3d(P[n]), li, BIG_WEIGHTS[n], chip_arr) for n, li in ag_units[u]]

    def ag_begin(u, after, mats):
        sas = [BIG_WEIGHTS[n] for n, _ in ag_units[u]]
        s_sem, r_sem, mats, token = _ag_start(f"ag_start_l{u}", mats, sas, after)
        ag_open[u] = (sas, s_sem, r_sem, mats)
        return token

    def ag_finish(u, after):
        sas, s_sem, r_sem, mats = ag_open.pop(u)
        mats = _ag_wait(f"ag_wait_l{u}", mats, sas, s_sem, r_sem, after)
        W.update(dict(zip(ag_units[u], _ag_forward(f"ag_forward_l{u}", mats, sas))))

    vnorm_full = _all_gather_vec("ag_vnorm", gmlp_v_norm)
    ag_token = ag_begin("0a", vnorm_full, cast_unit("0a"))
    ag_token = ag_begin("0b", ag_token, cast_unit("0b"))
    if NL > 1:
        ag_token = ag_begin(1, ag_token, cast_unit(1))
    cast_ahead = {l: cast_unit(l) for l in range(2, NL)}
    ag_finish("0a", [ag_token] + [m for l in cast_ahead for m in cast_ahead[l]])

    kgain = jnp.tile(k_norm[:, None, :], (1, H, 1)).reshape(1, NG * HD)
    qgain = [jnp.tile(attn_q_norm[j][:, None, :], (1, H, 1)).reshape(1, NG * HD) for j in range(NL - N_A_LAYERS)]
    q_scale = HEAD_DIM ** -0.5
    one = [(0, 0, 0)]

    def ffn_fwd(xc, gamma, wg, wu, wd, dep=None):
        n = _rms_fwd("ffn_norm", xc, gamma, dep)
        g, u, act = _mm("ffn_up", [n], [wg, wu], [(0, 0, 0), (0, 1, 1)], _ep_swiglu, [BF16] * 3)
        (x2,) = _mm("ffn_down", [act], [wd], one, _ep_residual(0.5), [F32], extras=[xc])
        return x2, (xc, n, g, u, act)

    saved = {}
    xc = xs
    for l in range(NL):
        if l > 0:
            ag_finish(l, xc)
        if l + 2 < NL:
            ag_token = ag_begin(l + 2, ag_token, cast_ahead.pop(l + 2))
        xc, saved["f1", l] = ffn_fwd(xc, ffn1_norm[l], W["ffn1_w_gate", l], W["ffn1_w_up", l], W["ffn1_w_down", l], ag_token)
        if l == 0:
            ag_finish("0b", xc)
        h = _rms_fwd("mix_norm", xc, mix_norm[l])
        if l < N_A_LAYERS:
            zpre, z = _mm("gmlp_in", [h], [W["gmlp_w_in", l]], one, _ep_gelu, [BF16, F32])
            bias_full = jnp.repeat(gmlp_b_s[l].T, GMLP_GROUP_WIDTH, axis=1)
            gated = _gmlp_gate_fwd("gmlp_gate", z, vnorm_full[l:l + 1], gmlp_w_s[l], bias_full)
            (x2,) = _mm("gmlp_out", [gated], [W["gmlp_w_out", l]], one, _ep_residual(1.0), [F32], extras=[xc])
            saved["mix", l] = (xc, h, zpre, z, gated, bias_full)
        else:
            j = l - N_A_LAYERS
            (q_raw,) = _mm("attn_q", [h], [W["attn_w_q", j]], one, _ep_plain, [F32])
            qn = _head_norm_fwd("q_norm", q_raw, 0, 1, qgain[j], q_scale, False)
            os_, lses = [], []
            for gi, dil in enumerate(DILATIONS):
                o, lse = _attn_fwd(f"attn_fwd_d{dil}", qn, kn, vb, gi, dil)
                os_.append(o)
                lses.append(lse)
            ob = _attn_combine("attn_mix", os_, lses)
            (x2,) = _mm("attn_o", [ob], [W["attn_w_o", j]], one, _ep_residual(1.0), [F32], extras=[xc])
            saved["mix", l] = (xc, h, q_raw, qn, os_, lses, ob)
        xc = x2
        xc, saved["f2", l] = ffn_fwd(xc, ffn2_norm[l], W["ffn2_w_gate", l], W["ffn2_w_up", l], W["ffn2_w_down", l])
        if l == kv_layer:
            kvn = _rms_fwd("kv_norm", xc, kv_norm)
            (kv_raw,) = _mm("kv_proj", [kvn], [W["w_kv", 0]], one, _ep_plain, [F32])
            kn, vb = _head_norm_fwd("k_norm", kv_raw, 0, 2, kgain, 1.0, True)
            saved["kv"] = (xc, kvn, kv_raw)

    dx, dxb, loss_rows = _loss_grad("loss", xc, tgt, 0.5)
    dW = {}
    dsmall = {n: [None] * P[n].shape[0] for n in ("ffn1_norm", "mix_norm", "ffn2_norm")}
    dsmall.update(gmlp_w_s=[None] * N_A_LAYERS, gmlp_b_s=[None] * N_A_LAYERS, gmlp_v_norm=[None] * N_A_LAYERS,
                  attn_q_norm=[None] * (NL - N_A_LAYERS))
    dks = [[] for _ in DILATIONS]
    dvs = [[] for _ in DILATIONS]

    names_big = list(BIG_WEIGHTS)
    gbuf = [lax.empty(_as3d(P[n]).shape, F32) for n in names_big]
    where = (c_arr, chip_arr)
    sib_open = {}
    rs_open = {}

    def sib_begin(l, after):
        names = layer_weights(l)
        sas = [BIG_WEIGHTS[n] for n, _ in names]
        s_sem, r_sem, dws, lands, token = _rs_sibling_start(f"rs_sibling_start_l{l}", [dW[k] for k in names], sas, after)
        sib_open[l] = (names, sas, s_sem, r_sem, dws, lands)
        return token

    def rs_begin(l, after):
        names, sas, s_sem, r_sem, dws, lands = sib_open.pop(l)
        dws, lands = _rs_sibling_wait(f"rs_sibling_wait_l{l}", dws, lands, sas, s_sem, r_sem, after)
        parts = [_add_half("rs_add_half", d, ln, sa, c_arr) for d, ln, sa in zip(dws, lands, sas)]
        s_sem, r_sem, parts, slots, token = _rs_start(f"rs_start_l{l}", parts, sas, dws[0])
        rs_open[l] = (names, sas, s_sem, r_sem, parts, slots)
        return token

    share_open = {}
    rs_landed = {}

    def rs_collect(l, after):
        names, sas, s_sem, r_sem, parts, slots = rs_open.pop(l)
        parts, slots = _rs_wait(f"rs_wait_l{l}", parts, slots, sas, s_sem, r_sem, after)
        rs_landed[l] = (names, sas, parts, slots)

    def rs_reduce(l):
        names, sas, parts, slots = rs_landed.pop(l)
        for p, s, sa, (n, li) in zip(parts, slots, sas, names):
            bi = names_big.index(n)
            gbuf[bi] = _sum_into("rs_sum_chips", p, s, gbuf[bi], li, sa, where)
        layers, bidx = [li for _, li in names], [names_big.index(n) for n, _ in names]
        s_sem, r_sem, bufs, _ = _rs_share_start(f"rs_share_start_l{l}", sas, layers, bidx, gbuf, parts[0])
        gbuf[:] = bufs
        share_open[l] = (sas, layers, bidx, s_sem, r_sem)

    def share_done(l, after):
        if l in share_open:
            sas, layers, bidx, s_sem, r_sem = share_open.pop(l)
            gbuf[:] = _rs_share_wait(f"rs_share_wait_l{l}", sas, layers, bidx, gbuf, s_sem, r_sem, after)

    def ffn_bwd(dx, dxb, sv, gamma, wg, wu, wd, key, l, next_scale, dep=None):
        xin, n, g, u, act = sv
        dg, du = _mm("ffn_dact", [dxb], [wd], one, _ep_swiglu_bwd, [BF16, BF16], tb=True, extras=[g, u], dep=dep)
        (dW[key + "_w_down", l],) = _mm("ffn_dwd", [act], [dxb], one, _ep_plain, [BF16], ta=True)
        dW[key + "_w_gate", l], dW[key + "_w_up", l] = _mm("ffn_dwgu", [n], [dg, du], [(0, 0, 0), (0, 1, 1)], _ep_all, [BF16, BF16], ta=True)
        (dn,) = _mm("ffn_dn", [dg, du], [wg, wu], [(0, 0, 0), (1, 1, 0)], _ep_plain, [F32], tb=True)
        dx, dxb, dsmall[key + "_norm"][l] = _rms_bwd("ffn_norm_bwd", xin, gamma, dn, dx, next_scale)
        return dx, dxb

    dep = None
    for l in reversed(range(NL)):
        if l == kv_layer:
            x_kv, kvn, kv_raw = saved["kv"]
            dkv_raw, dkgain = _head_norm_bwd("k_norm_bwd", kv_raw, 0, 2, kgain, 1.0, dks, dvs)
            (dW["w_kv", 0],) = _mm("kv_dw", [kvn], [dkv_raw], one, _ep_plain, [BF16], ta=True, dep=dep)
            (dkvn,) = _mm("kv_dn", [dkv_raw], [W["w_kv", 0]], one, _ep_plain, [F32], tb=True)
            dx, dxb, dkvnorm = _rms_bwd("kv_norm_bwd", x_kv, kv_norm, dkvn, dx, 0.5)
        dx, dxb = ffn_bwd(dx, dxb, saved["f2", l], ffn2_norm[l], W["ffn2_w_gate", l], W["ffn2_w_up", l], W["ffn2_w_down", l], "ffn2", l, 1.0, dep)
        if l + 1 < NL:
            dep = rs_begin(l + 1, dx)
        if l < N_A_LAYERS:
            xin, h, zpre, z, gated, bias_full = saved["mix", l]
            (dW["gmlp_w_out", l],) = _mm("gmlp_dwout", [gated], [dxb], one, _ep_plain, [BF16], ta=True, dep=dep)
            (dgated,) = _mm("gmlp_dgated", [dxb], [W["gmlp_w_out", l]], one, _ep_plain, [F32], tb=True)
            dzpre, dws, dbacc, dvn = _gmlp_gate_bwd("gmlp_gate_bwd", z, zpre, dgated, vnorm_full[l:l + 1], gmlp_w_s[l],
                                                    jnp.swapaxes(gmlp_w_s[l], 1, 2), bias_full)
            (dW["gmlp_w_in", l],) = _mm("gmlp_dwin", [h], [dzpre], one, _ep_plain, [BF16], ta=True)
            (dh,) = _mm("gmlp_dh", [dzpre], [W["gmlp_w_in", l]], one, _ep_plain, [F32], tb=True)
            dsmall["gmlp_w_s"][l] = dws
            dsmall["gmlp_b_s"][l] = dbacc.reshape(GMLP_CHUNK, G, GMLP_GROUP_WIDTH).sum(-1).T
            dsmall["gmlp_v_norm"][l] = dvn.reshape(DG)
        else:
            j = l - N_A_LAYERS
            xin, h, q_raw, qn, os_, lses, ob = saved["mix", l]
            (dW["attn_w_o", j],) = _mm("attn_dwo", [ob], [dxb], one, _ep_plain, [BF16], ta=True, dep=dep)
            (d_ob,) = _mm("attn_dob", [dxb], [W["attn_w_o", j]], one, _ep_plain, [F32], tb=True)
            lse_t, dl_rows, dob = _attn_combine("attn_mix_bwd", os_, lses, d_o=d_ob)
            dqs = []
            for gi, dil in enumerate(DILATIONS):
                dq, dk, dv = _attn_bwd(f"attn_bwd_d{dil}", qn, kn, vb, dob, lse_t, dl_rows, gi, dil)
                dqs.append([dq])
                dks[gi].append(dk)
                dvs[gi].append(dv)
            dq_raw, dqgain = _head_norm_bwd("q_norm_bwd", q_raw, 0, 1, qgain[j], q_scale, dqs, None)
            (dW["attn_w_q", j],) = _mm("attn_dwq", [h], [dq_raw], one, _ep_plain, [BF16], ta=True)
            (dh,) = _mm("attn_dh", [dq_raw], [W["attn_w_q", j]], one, _ep_plain, [F32], tb=True)
            dsmall["attn_q_norm"][j] = dqgain.reshape(NG, H, HEAD_DIM).sum(1)
        dx, dxb, dsmall["mix_norm"][l] = _rms_bwd("mix_norm_bwd", xin, mix_norm[l], dh, dx, 0.5)
        dx, dxb = ffn_bwd(dx, dxb, saved["f1", l], ffn1_norm[l], W["ffn1_w_gate", l], W["ffn1_w_up", l], W["ffn1_w_down", l], "ffn1", l, 0.5)
        dep = sib_begin(l, dx)
        if l + 2 < NL:
            share_done(l + 3, dep)
            rs_collect(l + 2, dep)
            rs_reduce(l + 2)
    grad_x = dx.reshape(x.shape)

    loss_part = (0.5 / D) * jnp.sum(loss_rows)
    small_grads = [jnp.stack([g.reshape(P[n].shape[1:]) for g in dsmall[n]]) for n in ("ffn1_norm", "mix_norm", "ffn2_norm", "gmlp_w_s", "gmlp_b_s")]
    small_grads += [dkvnorm.reshape(kv_norm.shape), dkgain.reshape(NG, H, HEAD_DIM).sum(1), jnp.stack(dsmall["attn_q_norm"])]
    vn_grad_full = jnp.stack(dsmall["gmlp_v_norm"])
    me_arr = (2 * chip + lax.axis_index("c")).astype(jnp.int32).reshape(1)
    small_slots = _place_own("place_small", _pack(small_grads + [vn_grad_full, loss_part.reshape(1)]), me_arr)

    adam_out = {n: [lax.empty(_as3d(P[n]).shape, F32) for _ in range(4)] for n in names_big}

    def adam_layer(l):
        for n, li in layer_weights(l):
            adam_out[n] = _adamw_layer("adamw", gbuf[names_big.index(n)], _as3d(P[n]), _as3d(P["m_" + n]),
                                       _as3d(P["v_" + n]), li, adam_out[n])

    def updated():
        return [adam_out[n][0] for n in names_big]

    dep = rs_begin(0, dep)
    ex_s, ex_r, small_slots, dep = _exchange_start("exchange_small_start", small_slots, dep)
    for l in range(2, NL):
        share_done(l, dep)
    for l in reversed(range(2, NL)):
        adam_layer(l)
    if NL > 1:
        rs_collect(1, [dep] + updated())
        rs_reduce(1)
    rs_collect(0, [dep] + updated())
    if NL > 1:
        share_done(1, dep)
    rs_reduce(0)
    share_done(0, dep)
    for l in reversed(range(min(2, NL))):
        adam_layer(l)
    big_out = {n: [o.reshape(P[n].shape) for o in adam_out[n]] for n in names_big}

    total = _sum_slots("sum_devices", _exchange_wait("exchange_small_wait", small_slots, ex_s, ex_r, updated()))
    shapes = [P[n].shape for n in SMALL_WEIGHTS] + [vn_grad_full.shape, (1,)]
    red = _unpack(total, shapes)
    loss = red[-1].reshape(())
    cv = gmlp_v_norm.shape[1]
    vn_grad = lax.dynamic_slice_in_dim(red[-2], chip * cv, cv, axis=1)
    names_small = list(SMALL_WEIGHTS) + ["gmlp_v_norm"]
    g_small = red[:len(SMALL_WEIGHTS)] + [vn_grad]
    outs = _adamw("adamw_small", _pack(g_small), _pack([P[n] for n in names_small]),
                  _pack([P["m_" + n] for n in names_small]), _pack([P["v_" + n] for n in names_small]))
    small_shapes = [P[n].shape for n in names_small]
    small_out = {n: [] for n in names_small}
    for o in outs:
        for n, a in zip(names_small, _unpack(o, small_shapes)):
            small_out[n].append(a)

    res = {**big_out, **small_out}
    return (loss, grad_x, *[res[n][0] for n in WEIGHT_ORDER], *[res[n][1] for n in WEIGHT_ORDER],
            *[res[n][2] for n in WEIGHT_ORDER], *[res[n][3] for n in WEIGHT_ORDER])
```

```python
import numpy as np
import jax
import jax.numpy as jnp
from jax import lax
from jax.experimental import pallas as pl
from jax.experimental.pallas import tpu as pltpu

F32 = jnp.float32
BF16 = jnp.bfloat16
SDS = jax.ShapeDtypeStruct

EPS = 1e-6
HEAD_DIM = 128
GMLP_CHUNK = 128
GMLP_GROUP_WIDTH = 128
DILATIONS = (1, 4, 16)
ATTN_BLOCK = 128
N_A_LAYERS = 2
ADAM_LR, ADAM_B1, ADAM_B2, ADAM_EPS, ADAM_WD, ADAM_STEP = 0.001, 0.9, 0.999, 1e-08, 0.01, 10

N_CHIPS = 4
N_DEV = 8
MESH = pl.DeviceIdType.MESH
V7X_VMEM_BYTES = 64 * 2 ** 20
VMEM_CEILING = V7X_VMEM_BYTES - 6 * 2 ** 20
VMEM_BLOCK_BUDGET = 38 * 2 ** 20
LANE = 128
BF16_ROWS = 16
PACK_TILE = 8 * LANE


def _cp(sem=None, vmem=None):
    kw = {}
    if sem is not None:
        kw["dimension_semantics"] = sem
    if vmem is not None:
        kw["vmem_limit_bytes"] = int(min(max(vmem, 16 * 2 ** 20), VMEM_CEILING))
    return pltpu.CompilerParams(**kw)


def _pick(dim, cands):
    for c in cands:
        if c <= dim and dim % c == 0:
            return c
    return dim


def _row_tile(rows, row_bytes, target=2 ** 20):
    t = 1024
    while t > 8 and (t * row_bytes > target or rows % t):
        t //= 2
    return t if rows % t == 0 else rows


def _sigmoid(x):
    return 1.0 / (1.0 + jnp.exp(-x))


_GELU_C = 0.7978845608028654
_GELU_A = 0.044715


def _gelu(x):
    return 0.5 * x * (1.0 + jnp.tanh(_GELU_C * (x + _GELU_A * (x * x * x))))


def _gelu_grad(x):
    t = jnp.tanh(_GELU_C * (x + _GELU_A * (x * x * x)))
    return 0.5 * (1.0 + t) + 0.5 * x * (1.0 - t * t) * (_GELU_C * (1.0 + 3.0 * _GELU_A * x * x))


def _mm_tiles(M, N, K, n_a, n_b, n_acc, io_bytes):
    tks = [K] + [d for d in (4096, 3072, 2816, 2048, 1024, 512, 256, 128) if d < K and K % d == 0]
    tms = [t for t in (1024, 512, 256, 128) if M % t == 0] or [M]
    tns = [t for t in (512, 256, 128) if N % t == 0] or [N]
    best = None
    for tk in tks:
        for tm in tms:
            for tn in tns:
                est = 2 * 2 * (n_a * tm * tk + n_b * tk * tn) + 2 * tm * tn * io_bytes
                est += n_acc * tm * tn * 4 * (2 if tk < K else 1)
                if est <= VMEM_BLOCK_BUDGET:
                    return tm, tn, tk, est
                if best is None or est < best[3]:
                    best = (tm, tn, tk, est)
    return best


def _mm(name, a_list, b_list, terms, epilogue, out_dtypes, *, ta=False, tb=False, extras=(), dep=None):
    n_acc = 1 + max(t[2] for t in terms)
    a0, b0 = a_list[0], b_list[0]
    (K, M) = a0.shape if ta else a0.shape[::-1]
    N = b0.shape[0] if tb else b0.shape[1]
    io_bytes = sum(jnp.dtype(e.dtype).itemsize for e in extras) + sum(jnp.dtype(d).itemsize for d in out_dtypes)
    tm, tn, tk, est = _mm_tiles(M, N, K, len(a_list), len(b_list), n_acc, io_bytes)
    nk = K // tk
    na, nb, ne, no = len(a_list), len(b_list), len(extras), len(out_dtypes)
    deps = [] if dep is None else [dep]
    nd = len(deps)
    dn = (((0 if ta else 1,), (1 if tb else 0,)), ((), ()))

    def body(*refs):
        a_refs = refs[:na]
        b_refs = refs[na:na + nb]
        e_refs = refs[na + nb:na + nb + ne]
        o_refs = refs[na + nb + ne + nd:na + nb + ne + nd + no]
        acc_refs = refs[na + nb + ne + nd + no:]
        parts = [None] * n_acc
        for ai, bi, qi in terms:
            d = lax.dot_general(a_refs[ai][...], b_refs[bi][...], dn, preferred_element_type=F32)
            parts[qi] = d if parts[qi] is None else parts[qi] + d

        def finish(accs):
            outs = epilogue(accs, [e[...] for e in e_refs])
            for o_ref, o in zip(o_refs, outs):
                o_ref[...] = o.astype(o_ref.dtype)

        if nk == 1:
            finish(parts)
        else:
            k = pl.program_id(2)

            @pl.when(k == 0)
            def _():
                for q in range(n_acc):
                    acc_refs[q][...] = parts[q]

            @pl.when(k > 0)
            def _():
                for q in range(n_acc):
                    acc_refs[q][...] += parts[q]

            @pl.when(k == nk - 1)
            def _():
                finish([acc_refs[q][...] for q in range(n_acc)])

    a_spec = pl.BlockSpec((tk, tm), lambda i, j, k: (k, i)) if ta else pl.BlockSpec((tm, tk), lambda i, j, k: (i, k))
    b_spec = pl.BlockSpec((tn, tk), lambda i, j, k: (j, k)) if tb else pl.BlockSpec((tk, tn), lambda i, j, k: (k, j))
    e_spec = pl.BlockSpec((tm, tn), lambda i, j, k: (i, j))
    outs = pl.pallas_call(
        body, name=name, grid=(M // tm, N // tn, nk),
        in_specs=[a_spec] * na + [b_spec] * nb + [e_spec] * ne + [pl.BlockSpec((8, LANE), lambda i, j, k: (0, 0))] * nd,
        out_specs=[e_spec] * no,
        out_shape=[SDS((M, N), d) for d in out_dtypes],
        scratch_shapes=[pltpu.VMEM((tm, tn), F32) for _ in range(n_acc)] if nk > 1 else [],
        compiler_params=_cp(("parallel", "parallel", "arbitrary"), est + 12 * 2 ** 20),
    )(*a_list, *b_list, *extras, *deps)
    return outs


def _ep_plain(accs, ex):
    return [accs[0]]


def _ep_swiglu(accs, ex):
    g, u = accs
    s = _sigmoid(g)
    sg = g * s
    return [u * (s + sg * (1.0 - s)), sg, sg * u]


def _ep_swiglu_bwd(accs, ex):
    da = accs[0]
    return [da * ex[0].astype(F32), da * ex[1].astype(F32)]


def _ep_gelu(accs, ex):
    return [_gelu_grad(accs[0]), _gelu(accs[0])]


def _ep_residual(scale):
    def ep(accs, ex):
        return [ex[0] + scale * accs[0]]
    return ep


def _rms_fwd(name, x, gamma, dep=None):
    S, D = x.shape
    tr = _row_tile(S, D * 4)
    deps = [] if dep is None else [dep]

    def body(x_ref, g_ref, *rest):
        o_ref = rest[-1]
        xv = x_ref[...]
        r = lax.rsqrt(jnp.mean(xv * xv, axis=-1, keepdims=True) + EPS)
        o_ref[...] = (xv * r * g_ref[...]).astype(BF16)

    return pl.pallas_call(
        body, name=name, grid=(S // tr,),
        in_specs=[pl.BlockSpec((tr, D), lambda i: (i, 0)), pl.BlockSpec((1, D), lambda i: (0, 0))]
        + [pl.BlockSpec((8, LANE), lambda i: (0, 0))] * len(deps),
        out_specs=pl.BlockSpec((tr, D), lambda i: (i, 0)),
        out_shape=SDS((S, D), BF16),
        compiler_params=_cp(("parallel",), 32 * 2 ** 20),
    )(x, gamma.reshape(1, D), *deps)


def _rms_bwd(name, x, gamma, dn, dx_in, out_scale):
    S, D = x.shape
    tr = _row_tile(S, D * 4, 2 ** 19)

    def body(x_ref, g_ref, dn_ref, dxi_ref, dxo_ref, dxb_ref, dg_ref):
        i = pl.program_id(0)
        xv = x_ref[...]
        r = lax.rsqrt(jnp.mean(xv * xv, axis=-1, keepdims=True) + EPS)
        xh = xv * r
        dnv = dn_ref[...]
        dxh = dnv * g_ref[...]
        dx = dxi_ref[...] + r * (dxh - xh * jnp.mean(dxh * xh, axis=-1, keepdims=True))
        dxo_ref[...] = dx
        dxb_ref[...] = (out_scale * dx).astype(BF16)
        part = jnp.sum(dnv * xh, axis=0, keepdims=True)

        @pl.when(i == 0)
        def _():
            dg_ref[...] = part

        @pl.when(i > 0)
        def _():
            dg_ref[...] += part

    row = pl.BlockSpec((tr, D), lambda i: (i, 0))
    vec = pl.BlockSpec((1, D), lambda i: (0, 0))
    return pl.pallas_call(
        body, name=name, grid=(S // tr,),
        in_specs=[row, vec, row, row], out_specs=[row, row, vec],
        out_shape=[SDS((S, D), F32), SDS((S, D), BF16), SDS((1, D), F32)],
        compiler_params=_cp(("arbitrary",), 40 * 2 ** 20),
    )(x, gamma.reshape(1, D), dn, dx_in)


def _loss_grad(name, y, t, out_scale):
    S, D = y.shape
    tr = _row_tile(S, D * 4, 2 ** 19)
    inv_d = 1.0 / D

    def body(y_ref, t_ref, dy_ref, dyb_ref, ls_ref):
        i = pl.program_id(0)
        e = y_ref[...] - t_ref[...]
        dy = e * inv_d
        dy_ref[...] = dy
        dyb_ref[...] = (out_scale * dy).astype(BF16)
        part = jnp.sum(e * e, axis=0, keepdims=True)

        @pl.when(i == 0)
        def _():
            ls_ref[...] = part

        @pl.when(i > 0)
        def _():
            ls_ref[...] += part

    row = pl.BlockSpec((tr, D), lambda i: (i, 0))
    vec = pl.BlockSpec((1, D), lambda i: (0, 0))
    return pl.pallas_call(
        body, name=name, grid=(S // tr,),
        in_specs=[row, row], out_specs=[row, row, vec],
        out_shape=[SDS((S, D), F32), SDS((S, D), BF16), SDS((1, D), F32)],
        compiler_params=_cp(("arbitrary",), 32 * 2 ** 20),
    )(y, t)


def _head_mean(v):
    return jnp.mean(v, axis=-1, keepdims=True)


def _head_norm_fwd(name, raw, part, n_parts, gain_t, scale, with_pass):
    S = raw.shape[0]
    W = raw.shape[1] // n_parts
    nh = W // HEAD_DIM
    tr = _row_tile(S, W * (4 + 2) * (2 if with_pass else 1), 4 * 2 ** 20)

    def body(*refs):
        if with_pass:
            x_ref, p_ref, g_ref, o_ref, po_ref = refs
            po_ref[...] = p_ref[...].astype(BF16)
        else:
            x_ref, g_ref, o_ref = refs
        for h in range(nh):
            sl = slice(h * HEAD_DIM, (h + 1) * HEAD_DIM)
            xv = x_ref[:, sl]
            r = lax.rsqrt(_head_mean(xv * xv) + EPS)
            o_ref[:, sl] = (xv * r * g_ref[:, sl] * scale).astype(BF16)

    xspec = pl.BlockSpec((tr, W), lambda i: (i, part))
    ospec = pl.BlockSpec((tr, W), lambda i: (i, 0))
    gspec = pl.BlockSpec((1, W), lambda i: (0, 0))
    if with_pass:
        in_specs = [xspec, pl.BlockSpec((tr, W), lambda i: (i, 1)), gspec]
        args = (raw, raw, gain_t)
        out_specs, out_shape = [ospec, ospec], [SDS((S, W), BF16), SDS((S, W), BF16)]
    else:
        in_specs, args = [xspec, gspec], (raw, gain_t)
        out_specs, out_shape = ospec, SDS((S, W), BF16)
    return pl.pallas_call(
        body, name=name, grid=(S // tr,), in_specs=in_specs, out_specs=out_specs, out_shape=out_shape,
        compiler_params=_cp(("parallel",), 40 * 2 ** 20),
    )(*args)


def _head_norm_bwd(name, raw, part, n_parts, gain_t, scale, dy_groups, pass_groups):
    S = raw.shape[0]
    W = raw.shape[1] // n_parts
    ng = len(dy_groups)
    HD = W // ng
    nhg = HD // HEAD_DIM
    n_dy = [len(g) for g in dy_groups]
    n_ps = [len(g) for g in pass_groups] if pass_groups is not None else []
    flat = [a for g in dy_groups for a in g] + ([a for g in pass_groups for a in g] if pass_groups is not None else [])
    out_w = 2 * W if pass_groups is not None else W
    tr = _row_tile(S, W * 4 + sum(HD * jnp.dtype(a.dtype).itemsize for a in flat) + out_w * 2, 8 * 2 ** 20)

    def body(*refs):
        x_ref, g_ref = refs[0], refs[1]
        d_refs = refs[2:2 + len(flat)]
        o_ref, dg_ref = refs[2 + len(flat)], refs[3 + len(flat)]
        i = pl.program_id(0)

        @pl.when(i == 0)
        def _():
            dg_ref[...] = jnp.zeros_like(dg_ref)

        pos = 0
        for gi in range(ng):
            dys = d_refs[pos:pos + n_dy[gi]]
            pos += n_dy[gi]
            for h in range(nhg):
                sl = slice(gi * HD + h * HEAD_DIM, gi * HD + (h + 1) * HEAD_DIM)
                hs = slice(h * HEAD_DIM, (h + 1) * HEAD_DIM)
                dy = dys[0][:, hs].astype(F32)
                for extra in dys[1:]:
                    dy = dy + extra[:, hs].astype(F32)
                xv = x_ref[:, sl]
                r = lax.rsqrt(_head_mean(xv * xv) + EPS)
                xh = xv * r
                dxh = dy * (g_ref[:, sl] * scale)
                o_ref[:, sl] = (r * (dxh - xh * _head_mean(dxh * xh))).astype(BF16)
                dg_ref[:, sl] += jnp.sum(dy * xh, axis=0, keepdims=True) * scale
        for gi in range(len(n_ps)):
            ps = d_refs[pos:pos + n_ps[gi]]
            pos += n_ps[gi]
            acc = ps[0][...].astype(F32)
            for extra in ps[1:]:
                acc = acc + extra[...].astype(F32)
            o_ref[:, W + gi * HD:W + (gi + 1) * HD] = acc.astype(BF16)

    dspec = pl.BlockSpec((tr, HD), lambda i: (i, 0))
    return pl.pallas_call(
        body, name=name, grid=(S // tr,),
        in_specs=[pl.BlockSpec((tr, W), lambda i: (i, part)), pl.BlockSpec((1, W), lambda i: (0, 0))] + [dspec] * len(flat),
        out_specs=[pl.BlockSpec((tr, out_w), lambda i: (i, 0)), pl.BlockSpec((1, W), lambda i: (0, 0))],
        out_shape=[SDS((S, out_w), BF16), SDS((1, W), F32)],
        compiler_params=_cp(("arbitrary",), 48 * 2 ** 20),
    )(raw, gain_t, *flat)


def _tril_mask():
    r = lax.broadcasted_iota(jnp.int32, (GMLP_CHUNK, GMLP_CHUNK), 0)
    c = lax.broadcasted_iota(jnp.int32, (GMLP_CHUNK, GMLP_CHUNK), 1)
    return r >= c


def _gmlp_gate_fwd(name, z, v_norm, w_s, bias_full):
    S, DG2 = z.shape
    DG = DG2 // 2
    G = DG // GMLP_GROUP_WIDTH
    C = GMLP_CHUNK

    def body(u_ref, v_ref, vn_ref, ws_ref, b_ref, o_ref):
        mask = _tril_mask()
        v = v_ref[...]
        r = lax.rsqrt(jnp.mean(v * v, axis=-1, keepdims=True) + EPS)
        vn = (v * r * vn_ref[...]).astype(BF16)
        for g in range(G):
            sl = slice(g * GMLP_GROUP_WIDTH, (g + 1) * GMLP_GROUP_WIDTH)
            wm = jnp.where(mask, ws_ref[g], 0.0).astype(BF16)
            sv = jnp.dot(wm, vn[:, sl], preferred_element_type=F32) + b_ref[:, sl]
            o_ref[:, sl] = (u_ref[:, sl] * sv).astype(BF16)

    return pl.pallas_call(
        body, name=name, grid=(S // C,),
        in_specs=[pl.BlockSpec((C, DG), lambda i: (i, 0)), pl.BlockSpec((C, DG), lambda i: (i, 1)),
                  pl.BlockSpec((1, DG), lambda i: (0, 0)), pl.BlockSpec((G, C, C), lambda i: (0, 0, 0)),
                  pl.BlockSpec((C, DG), lambda i: (0, 0))],
        out_specs=pl.BlockSpec((C, DG), lambda i: (i, 0)),
        out_shape=SDS((S, DG), BF16),
        compiler_params=_cp(("parallel",), 32 * 2 ** 20),
    )(z, z, v_norm, w_s, bias_full)


def _gmlp_gate_bwd(name, z, zpre, dgated, v_norm, w_s, w_s_t, bias_full):
    S, DG2 = z.shape
    DG = DG2 // 2
    G = DG // GMLP_GROUP_WIDTH
    C = GMLP_CHUNK

    def body(z_ref, zp_ref, dg_ref, vn_ref, ws_ref, wst_ref, b_ref, dz_ref, dws_ref, db_ref, dvn_ref):
        i = pl.program_id(0)
        mask = _tril_mask()
        mask_t = jnp.logical_not(mask) | (lax.broadcasted_iota(jnp.int32, (C, C), 0) == lax.broadcasted_iota(jnp.int32, (C, C), 1))
        u = z_ref[:, :DG]
        v = z_ref[:, DG:]
        r = lax.rsqrt(jnp.mean(v * v, axis=-1, keepdims=True) + EPS)
        vh = v * r
        gain = vn_ref[...]
        vn = (vh * gain).astype(BF16)
        dgt = dg_ref[...]

        @pl.when(i == 0)
        def _():
            dws_ref[...] = jnp.zeros_like(dws_ref)
            db_ref[...] = jnp.zeros_like(db_ref)
            dvn_ref[...] = jnp.zeros_like(dvn_ref)

        dvn_parts = []
        for g in range(G):
            sl = slice(g * GMLP_GROUP_WIDTH, (g + 1) * GMLP_GROUP_WIDTH)
            wm = jnp.where(mask, ws_ref[g], 0.0).astype(BF16)
            wmt = jnp.where(mask_t, wst_ref[g], 0.0).astype(BF16)
            sv = jnp.dot(wm, vn[:, sl], preferred_element_type=F32) + b_ref[:, sl]
            dgs = dgt[:, sl]
            du = dgs * sv
            dsv = dgs * u[:, sl]
            db_ref[:, sl] += dsv
            dsv_b = dsv.astype(BF16)
            dws = lax.dot_general(dsv_b, vn[:, sl], (((1,), (1,)), ((), ())), preferred_element_type=F32)
            dws_ref[g] += jnp.where(mask, dws, 0.0)
            dvn_parts.append(jnp.dot(wmt, dsv_b, preferred_element_type=F32))
            dz_ref[:, sl] = (du * zp_ref[:, sl].astype(F32)).astype(BF16)
        dvn_full = jnp.concatenate(dvn_parts, axis=1)
        dvn_ref[...] += jnp.sum(dvn_full * vh, axis=0, keepdims=True)
        dxh = dvn_full * gain
        dv = r * (dxh - vh * jnp.mean(dxh * vh, axis=-1, keepdims=True))
        dz_ref[:, DG:] = (dv * zp_ref[:, DG:].astype(F32)).astype(BF16)

    full = pl.BlockSpec((C, DG2), lambda i: (i, 0))
    wspec = pl.BlockSpec((G, C, C), lambda i: (0, 0, 0))
    return pl.pallas_call(
        body, name=name, grid=(S // C,),
        in_specs=[full, full, pl.BlockSpec((C, DG), lambda i: (i, 0)), pl.BlockSpec((1, DG), lambda i: (0, 0)),
                  wspec, wspec, pl.BlockSpec((C, DG), lambda i: (0, 0))],
        out_specs=[full, wspec, pl.BlockSpec((C, DG), lambda i: (0, 0)), pl.BlockSpec((1, DG), lambda i: (0, 0))],
        out_shape=[SDS((S, DG2), BF16), SDS((G, C, C), F32), SDS((C, DG), F32), SDS((1, DG), F32)],
        compiler_params=_cp(("arbitrary",), 40 * 2 ** 20),
    )(z, zpre, dgated, v_norm, w_s, w_s_t, bias_full)


def _alibi_slopes(n_heads):
    return [float(v) for v in np.exp2(np.float32(-8.0) * np.arange(1, n_heads + 1, dtype=np.float32) / np.float32(n_heads))]


def _dil_view(arr, dil):
    S, C = arr.shape
    return arr if dil == 1 else arr.reshape(S // dil, dil * C)


def _dil_spec(dil, HD, ncb, cb, bmap):
    return pl.BlockSpec((ATTN_BLOCK, HD), lambda r, b: (bmap(b), r * ncb + cb))


def _group_view(arr, gi, dil, HD):
    if dil == 1:
        return arr, arr.shape[1] // HD, gi
    return _dil_view(arr[:, gi * HD:(gi + 1) * HD], dil), 1, 0


def _dil_unview(arr, S):
    return arr.reshape(S, arr.size // S)


def _attn_mask(b):
    qi = lax.broadcasted_iota(jnp.int32, (ATTN_BLOCK, 2 * ATTN_BLOCK), 0)
    kj = lax.broadcasted_iota(jnp.int32, (ATTN_BLOCK, 2 * ATTN_BLOCK), 1)
    delta = qi + ATTN_BLOCK - kj
    valid = (delta >= 0) & (delta <= ATTN_BLOCK) & ((kj >= ATTN_BLOCK) | (b > 0))
    return valid, delta.astype(F32)


def _pack_heads(cols, rows):
    lane = lax.broadcasted_iota(jnp.int32, (rows, LANE), 1)
    tile = jnp.zeros((rows, LANE), F32)
    for h, col in enumerate(cols):
        tile = jnp.where(lane == h, col, tile)
    return tile


def _attn_fwd(name, qn, kn, vb, gi, dil):
    S, C = qn.shape
    NG = len(DILATIONS)
    HD = C // NG
    H = HD // HEAD_DIM
    L = S // dil
    nb = L // ATTN_BLOCK
    slopes = _alibi_slopes(H)

    def body(q_ref, kc_ref, kp_ref, vc_ref, vp_ref, o_ref, lse_ref):
        b = pl.program_id(1)
        valid, delta = _attn_mask(b)
        dist = delta * float(dil)
        lse_cols = []
        for h in range(H):
            sl = slice(h * HEAD_DIM, (h + 1) * HEAD_DIM)
            k = jnp.concatenate([kp_ref[:, sl], kc_ref[:, sl]], axis=0)
            v = jnp.concatenate([vp_ref[:, sl], vc_ref[:, sl]], axis=0)
            s = lax.dot_general(q_ref[:, sl], k, (((1,), (1,)), ((), ())), preferred_element_type=F32)
            s = jnp.where(valid, s - slopes[h] * dist, -jnp.inf)
            m = jnp.max(s, axis=-1, keepdims=True)
            p = jnp.exp(s - m)
            l = jnp.sum(p, axis=-1, keepdims=True)
            o = jnp.dot(p.astype(BF16), v, preferred_element_type=F32)
            o_ref[:, sl] = o / l
            lse_cols.append(m + jnp.log(l))
        lse_ref[...] = _pack_heads(lse_cols, ATTN_BLOCK)

    cur = lambda b: b
    prev = lambda b: jnp.maximum(b - 1, 0)
    (qv, ncb, cb), (kv, _, _), (vv, _, _) = (_group_view(a, gi, dil, HD) for a in (qn, kn, vb))
    view_shape = (L, dil * HD)
    o, lse = pl.pallas_call(
        body, name=name, grid=(dil, nb),
        in_specs=[_dil_spec(dil, HD, ncb, cb, cur), _dil_spec(dil, HD, ncb, cb, cur), _dil_spec(dil, HD, ncb, cb, prev),
                  _dil_spec(dil, HD, ncb, cb, cur), _dil_spec(dil, HD, ncb, cb, prev)],
        out_specs=[_dil_spec(dil, HD, 1, 0, cur), _dil_spec(dil, LANE, 1, 0, cur)],
        out_shape=[SDS(view_shape, F32), SDS((L, dil * LANE), F32)],
        compiler_params=_cp(("parallel", "parallel"), 32 * 2 ** 20),
    )(qv, kv, kv, vv, vv)
    return _dil_unview(o, S), _dil_unview(lse, S)


def _attn_combine(name, o_list, lse_list, d_o=None):
    S, HD = o_list[0].shape
    H = HD // HEAD_DIM
    ng = len(o_list)
    tr = _row_tile(S, HD * 4, 2 ** 19)

    def body(*refs):
        o_refs = refs[:ng]
        l_refs = refs[ng:2 * ng]
        w_scr = refs[-1]
        ls = [r[...] for r in l_refs]
        m = ls[0]
        for t in ls[1:]:
            m = jnp.maximum(m, t)
        es = [jnp.exp(t - m) for t in ls]
        z = es[0]
        for t in es[1:]:
            z = z + t
        for g in range(ng):
            w_scr[g] = es[g] / z
        if d_o is not None:
            do_ref, lse_ref, dl_ref, dob_ref = refs[2 * ng:2 * ng + 4]
            lse_ref[...] = m + jnp.log(z)
            dob_ref[...] = do_ref[...].astype(BF16)
        dl_cols = []
        for h in range(H):
            sl = slice(h * HEAD_DIM, (h + 1) * HEAD_DIM)
            o = w_scr[0, :, h:h + 1] * o_refs[0][:, sl]
            for g in range(1, ng):
                o = o + w_scr[g, :, h:h + 1] * o_refs[g][:, sl]
            if d_o is None:
                refs[2 * ng][:, sl] = o.astype(BF16)
            else:
                dl_cols.append(jnp.sum(do_ref[:, sl] * o, axis=-1, keepdims=True))
        if d_o is not None:
            dl_ref[...] = _pack_heads(dl_cols, tr)

    row = pl.BlockSpec((tr, HD), lambda i: (i, 0))
    col = pl.BlockSpec((tr, LANE), lambda i: (i, 0))
    scratch = [pltpu.VMEM((ng, tr, LANE), F32)]
    if d_o is None:
        return pl.pallas_call(
            body, name=name, grid=(S // tr,), in_specs=[row] * ng + [col] * ng, out_specs=row,
            out_shape=SDS((S, HD), BF16), scratch_shapes=scratch, compiler_params=_cp(("parallel",), 40 * 2 ** 20),
        )(*o_list, *lse_list)
    return pl.pallas_call(
        body, name=name, grid=(S // tr,), in_specs=[row] * ng + [col] * ng + [row], out_specs=[col, col, row],
        out_shape=[SDS((S, LANE), F32), SDS((S, LANE), F32), SDS((S, HD), BF16)], scratch_shapes=scratch,
        compiler_params=_cp(("parallel",), 48 * 2 ** 20),
    )(*o_list, *lse_list, d_o)


def _attn_bwd(name, qn, kn, vb, dob, lse, delta_rows, gi, dil):
    S, C = qn.shape
    NG = len(DILATIONS)
    HD = C // NG
    H = HD // HEAD_DIM
    L = S // dil
    nb = L // ATTN_BLOCK
    slopes = _alibi_slopes(H)
    B = ATTN_BLOCK

    def body(q_ref, kc_ref, kp_ref, vc_ref, vp_ref, do_ref, lse_ref, dl_ref, dq_ref, dk_ref, dv_ref, ck_ref, cv_ref):
        b = pl.program_id(1)

        @pl.when(b == 0)
        def _():
            ck_ref[...] = jnp.zeros_like(ck_ref)
            cv_ref[...] = jnp.zeros_like(cv_ref)

        @pl.when(b < nb)
        def _():
            valid, delta = _attn_mask(b)
            dist = delta * float(dil)
            for h in range(H):
                sl = slice(h * HEAD_DIM, (h + 1) * HEAD_DIM)
                q = q_ref[:, sl]
                k = jnp.concatenate([kp_ref[:, sl], kc_ref[:, sl]], axis=0)
                v = jnp.concatenate([vp_ref[:, sl], vc_ref[:, sl]], axis=0)
                do = do_ref[:, sl]
                s = lax.dot_general(q, k, (((1,), (1,)), ((), ())), preferred_element_type=F32)
                s = jnp.where(valid, s - slopes[h] * dist, -jnp.inf)
                p = jnp.exp(s - lse_ref[:, h:h + 1])
                dp = lax.dot_general(do, v, (((1,), (1,)), ((), ())), preferred_element_type=F32)
                ds = (p * (dp - dl_ref[:, h:h + 1])).astype(BF16)
                dq_ref[:, sl] = jnp.dot(ds, k, preferred_element_type=F32).astype(BF16)
                dk2 = lax.dot_general(ds, q, (((0,), (0,)), ((), ())), preferred_element_type=F32)
                dv2 = lax.dot_general(p.astype(BF16), do, (((0,), (0,)), ((), ())), preferred_element_type=F32)
                dk_ref[:, sl] = (ck_ref[:, sl] + dk2[:B]).astype(BF16)
                dv_ref[:, sl] = (cv_ref[:, sl] + dv2[:B]).astype(BF16)
                ck_ref[:, sl] = dk2[B:]
                cv_ref[:, sl] = dv2[B:]

        @pl.when(b == nb)
        def _():
            dk_ref[...] = ck_ref[...].astype(BF16)
            dv_ref[...] = cv_ref[...].astype(BF16)

    cur = lambda b: jnp.minimum(b, nb - 1)
    prev = lambda b: jnp.maximum(jnp.minimum(b, nb - 1) - 1, 0)
    late = lambda b: jnp.maximum(b - 1, 0)
    (qv, ncb, cb), (kv, _, _), (vv, _, _) = (_group_view(a, gi, dil, HD) for a in (qn, kn, vb))
    dov, lsev, dlv = _dil_view(dob, dil), _dil_view(lse, dil), _dil_view(delta_rows, dil)
    view_shape = (L, dil * HD)
    one = lambda m: _dil_spec(dil, HD, 1, 0, m)
    grp = lambda m: _dil_spec(dil, HD, ncb, cb, m)
    heads = _dil_spec(dil, LANE, 1, 0, cur)
    dq, dk, dv = pl.pallas_call(
        body, name=name, grid=(dil, nb + 1),
        in_specs=[grp(cur), grp(cur), grp(prev), grp(cur), grp(prev), one(cur), heads, heads],
        out_specs=[one(cur), one(late), one(late)],
        out_shape=[SDS(view_shape, BF16)] * 3,
        scratch_shapes=[pltpu.VMEM((B, HD), F32), pltpu.VMEM((B, HD), F32)],
        compiler_params=_cp(("arbitrary", "arbitrary"), 40 * 2 ** 20),
    )(qv, kv, kv, vv, vv, dov, lsev, dlv)
    return _dil_unview(dq, S), _dil_unview(dk, S), _dil_unview(dv, S)


def _cast_into_gathered(name, w, l, sa, chip_arr):
    L, r, c = w.shape
    tr = _row_tile(r, c * 4)
    nrb = r // tr
    src = pl.BlockSpec((None, tr, c), lambda i, chip: (l, i, 0))
    if sa == 0:
        dst, shape = pl.BlockSpec((tr, c), lambda i, chip: (chip[0] * nrb + i, 0)), (N_CHIPS * r, c)
    else:
        dst, shape = pl.BlockSpec((tr, c), lambda i, chip: (i, chip[0])), (r, N_CHIPS * c)

    def body(chip_ref, i_ref, o_ref):
        o_ref[...] = i_ref[...].astype(BF16)

    return pl.pallas_call(
        body, name=name,
        grid_spec=pltpu.PrefetchScalarGridSpec(num_scalar_prefetch=1, grid=(nrb,), in_specs=[src], out_specs=dst),
        out_shape=SDS(shape, BF16), compiler_params=_cp(("parallel",), 32 * 2 ** 20),
    )(chip_arr, w)


def _add_half(name, dw, land, sa, c_arr):
    hr, hc = land.shape
    tr = _row_tile(hr, hc * 2)
    nrb = hr // tr
    if sa == 1:
        mine = pl.BlockSpec((tr, hc), lambda i, c: (c[0] * nrb + i, 0))
    else:
        mine = pl.BlockSpec((tr, hc), lambda i, c: (i, c[0]))
    other = pl.BlockSpec((tr, hc), lambda i, c: (i, 0))

    def body(c_ref, a_ref, b_ref, o_ref):
        o_ref[...] = (a_ref[...].astype(F32) + b_ref[...].astype(F32)).astype(BF16)

    return pl.pallas_call(
        body, name=name,
        grid_spec=pltpu.PrefetchScalarGridSpec(num_scalar_prefetch=1, grid=(nrb,), in_specs=[mine, other], out_specs=other),
        out_shape=SDS((hr, hc), BF16), compiler_params=_cp(("parallel",), 32 * 2 ** 20),
    )(c_arr, dw, land)


def _sum_slots(name, slots, out_dtype=F32):
    n, R, C = slots.shape
    tr = _row_tile(R, C * n * jnp.dtype(slots.dtype).itemsize, 2 ** 21)

    def body(s_ref, o_ref):
        acc = s_ref[0].astype(F32)
        for k in range(1, n):
            acc = acc + s_ref[k].astype(F32)
        o_ref[...] = acc.astype(out_dtype)

    return pl.pallas_call(
        body, name=name, grid=(R // tr,),
        in_specs=[pl.BlockSpec((n, tr, C), lambda i: (0, i, 0))], out_specs=pl.BlockSpec((tr, C), lambda i: (i, 0)),
        out_shape=SDS((R, C), out_dtype), compiler_params=_cp(("parallel",), 32 * 2 ** 20),
    )(slots)


def _sum_into(name, part, slots, buf, l, sa, where):
    n, pr, pc = slots.shape
    tr = _row_tile(pr, pc * (n + 1) * jnp.dtype(slots.dtype).itemsize, 2 ** 21)
    nrb = pr // tr
    if sa == 1:
        dst = pl.BlockSpec((None, tr, pc), lambda i, c, j: (l, c[0] * nrb + i, 0))
        own = pl.BlockSpec((tr, pc), lambda i, c, j: (i, j[0]))
    else:
        dst = pl.BlockSpec((None, tr, pc), lambda i, c, j: (l, i, c[0]))
        own = pl.BlockSpec((tr, pc), lambda i, c, j: (j[0] * nrb + i, 0))

    def body(c_ref, j_ref, p_ref, s_ref, b_ref, o_ref):
        acc = p_ref[...].astype(F32)
        for k in range(n):
            acc = acc + s_ref[k].astype(F32)
        o_ref[...] = acc

    return pl.pallas_call(
        body, name=name,
        grid_spec=pltpu.PrefetchScalarGridSpec(
            num_scalar_prefetch=2, grid=(nrb,),
            in_specs=[own, pl.BlockSpec((n, tr, pc), lambda i, c, j: (0, i, 0)), ANY], out_specs=dst),
        out_shape=SDS(buf.shape, buf.dtype), input_output_aliases={4: 0},
        compiler_params=_cp(("parallel",), 32 * 2 ** 20),
    )(where[0], where[1], part, slots, buf)


def _adamw_body(g_ref, w_ref, m_ref, v_ref, go_ref, d_ref, mo_ref, vo_ref):
    bc1 = 1.0 - ADAM_B1 ** ADAM_STEP
    bc2 = 1.0 - ADAM_B2 ** ADAM_STEP
    gv = g_ref[...]
    mn = ADAM_B1 * m_ref[...] + (1.0 - ADAM_B1) * gv
    vn = ADAM_B2 * v_ref[...] + (1.0 - ADAM_B2) * (gv * gv)
    go_ref[...] = gv
    mo_ref[...] = mn
    vo_ref[...] = vn
    d_ref[...] = -ADAM_LR * ((mn / bc1) / (jnp.sqrt(vn / bc2) + ADAM_EPS) + ADAM_WD * w_ref[...])


def _adamw_layer(name, g, w, m, v, l, outs):
    L, r, c = g.shape
    tr = _row_tile(r, c * 4, 2 ** 19)
    spec = pl.BlockSpec((None, tr, c), lambda i: (l, i, 0))

    def body(g_ref, w_ref, m_ref, v_ref, a0, a1, a2, a3, go_ref, d_ref, mo_ref, vo_ref):
        _adamw_body(g_ref, w_ref, m_ref, v_ref, go_ref, d_ref, mo_ref, vo_ref)

    return list(pl.pallas_call(
        body, name=name, grid=(r // tr,), in_specs=[spec] * 4 + [ANY] * 4, out_specs=[spec] * 4,
        out_shape=[SDS((L, r, c), F32)] * 4, input_output_aliases={4 + k: k for k in range(4)},
        compiler_params=_cp(("parallel",), 32 * 2 ** 20),
    )(g, w, m, v, *outs))


def _adamw(name, g, w, m, v):
    R, C = g.shape
    tr = _row_tile(R, C * 4, 2 ** 19)

    def body(g_ref, w_ref, m_ref, v_ref, go_ref, d_ref, mo_ref, vo_ref):
        _adamw_body(g_ref, w_ref, m_ref, v_ref, go_ref, d_ref, mo_ref, vo_ref)

    spec = pl.BlockSpec((tr, C), lambda i: (i, 0))
    return pl.pallas_call(
        body, name=name, grid=(R // tr,), in_specs=[spec] * 4, out_specs=[spec] * 4,
        out_shape=[SDS((R, C), F32)] * 4, compiler_params=_cp(("parallel",), 32 * 2 ** 20),
    )(g, w, m, v)


ANY = pl.BlockSpec(memory_space=pl.ANY)


def _coords():
    return lax.axis_index("x"), lax.axis_index("y"), lax.axis_index("c")


def _other_chips(x, y):
    return [((1 - x, y), 2 * (1 - x) + y), ((x, 1 - y), 2 * x + (1 - y)), ((1 - x, 1 - y), 2 * (1 - x) + (1 - y))]


def _win(ref, axis, start, size):
    if not isinstance(start, int):
        start = pl.multiple_of(start, LANE if axis == 1 else BF16_ROWS)
    if axis == 0:
        return ref.at[pl.ds(start, size), :]
    return ref.at[:, pl.ds(start, size)]


def _rcopy(src, dst, ssem, rsem, dev):
    return pltpu.make_async_remote_copy(src_ref=src, dst_ref=dst, send_sem=ssem, recv_sem=rsem,
                                        device_id=dev, device_id_type=MESH)


HBM = pl.BlockSpec(memory_space=pltpu.HBM)
SEM = pl.BlockSpec(memory_space=pltpu.SEMAPHORE)
TOKEN = pl.BlockSpec(memory_space=pltpu.VMEM)
EFFECT = pltpu.SideEffectType.DATAFLOW_SIDE_EFFECTING


def _in_hbm(a):
    return pltpu.with_memory_space_constraint(a, pltpu.HBM)


def _ag_geometry(mats, sas, o):
    sa = sas[o]
    return sa, 1 - sa, mats[o].shape[sa] // N_CHIPS, mats[o].shape[1 - sa] // 2


def _ag_ici_copy(mats, sas, refs, o, k, sems, x, y, c):
    sa, ha, wl, hl = _ag_geometry(mats, sas, o)
    chip, jk = _other_chips(x, y)[k]
    mine = _win(_win(refs[o], sa, (2 * x + y) * wl, wl), ha, c * hl, hl)
    landed = _win(_win(refs[o], sa, jk * wl, wl), ha, c * hl, hl)
    return mine, landed, (*chip, c)


def _ag_start(name, mats, sas, after):
    n = len(mats)

    def body(*refs):
        ins = refs[:n]
        s_sem, r_sem = refs[n + 1], refs[n + 2]
        token = refs[2 * n + 3]
        x, y, c = _coords()
        for o in range(n):
            for k in range(3):
                mine, _, dev = _ag_ici_copy(mats, sas, ins, o, k, None, x, y, c)
                _rcopy(mine, mine, s_sem.at[3 * o + k], r_sem.at[3 * o + k], dev).start()
        token[...] = jnp.zeros_like(token)

    out = pl.pallas_call(
        body, name=name,
        out_shape=(pltpu.SemaphoreType.DMA((3 * n,)), pltpu.SemaphoreType.DMA((3 * n,)),
                   *[pltpu.HBM(m.shape, m.dtype) for m in mats], SDS((8, LANE), F32)),
        in_specs=[HBM] * n + [ANY], out_specs=(SEM, SEM, *[HBM] * n, TOKEN),
        input_output_aliases={k: 2 + k for k in range(n)},
        compiler_params=pltpu.CompilerParams(has_side_effects=EFFECT),
    )(*[_in_hbm(m) for m in mats], after)
    return out[0], out[1], list(out[2:2 + n]), out[2 + n]


def _ag_wait(name, mats, sas, s_sem, r_sem, after):
    n = len(mats)

    def body(*refs):
        ins = refs[:n]
        s_ref, r_ref = refs[n], refs[n + 1]
        x, y, c = _coords()
        for o in range(n):
            for k in range(3):
                mine, landed, dev = _ag_ici_copy(mats, sas, ins, o, k, None, x, y, c)
                cp = _rcopy(mine, landed, s_ref.at[3 * o + k], r_ref.at[3 * o + k], dev)
                cp.wait_send()
                cp.wait_recv()

    after = list(after) if isinstance(after, (list, tuple)) else [after]
    return list(pl.pallas_call(
        body, name=name, out_shape=[pltpu.HBM(m.shape, m.dtype) for m in mats],
        in_specs=[HBM] * n + [SEM, SEM] + [ANY] * len(after), out_specs=[HBM] * n,
        input_output_aliases={k: k for k in range(n)},
        compiler_params=pltpu.CompilerParams(has_side_effects=EFFECT),
    )(*mats, s_sem, r_sem, *after))


def _ag_forward(name, mats, sas):
    n = len(mats)

    def body(*refs):
        outs = refs[n:2 * n]
        s_fwd, r_fwd = refs[2 * n:]
        x, y, c = _coords()
        sibling = (x, y, 1 - c)
        sends = []
        for o in range(n):
            sa, ha, wl, hl = _ag_geometry(mats, sas, o)
            for k, (chip, jk) in enumerate(_other_chips(x, y)):
                landed = _win(_win(outs[o], sa, jk * wl, wl), ha, c * hl, hl)
                fwd = _rcopy(landed, landed, s_fwd.at[3 * o + k], r_fwd.at[3 * o + k], sibling)
                fwd.start()
                sends.append(fwd)
        for o in range(n):
            sa, ha, wl, hl = _ag_geometry(mats, sas, o)
            for k, (chip, jk) in enumerate(_other_chips(x, y)):
                got = _win(_win(outs[o], sa, jk * wl, wl), ha, (1 - c) * hl, hl)
                _rcopy(got, got, s_fwd.at[3 * o + k], r_fwd.at[3 * o + k], sibling).wait_recv()
        for cp in sends:
            cp.wait_send()

    return list(pl.pallas_call(
        body, name=name, in_specs=[ANY] * n, out_specs=[ANY] * n,
        out_shape=[SDS(m.shape, m.dtype) for m in mats], input_output_aliases={k: k for k in range(n)},
        scratch_shapes=[pltpu.SemaphoreType.DMA((3 * n,)), pltpu.SemaphoreType.DMA((3 * n,))],
        compiler_params=pltpu.CompilerParams(has_side_effects=True),
    )(*mats))


def _all_gather_vec(name, v):
    Lv, cv = v.shape

    def body(v_ref, o_ref, loc_sem, s_sem, r_sem):
        x, y, c = _coords()
        jme = 2 * x + y
        chips = _other_chips(x, y)
        mine = _win(o_ref, 1, jme * cv, cv)
        loc = pltpu.make_async_copy(v_ref, mine, loc_sem)
        loc.start()
        sends = []
        for k, (chip, _) in enumerate(chips):
            cp = _rcopy(v_ref, mine, s_sem.at[k], r_sem.at[k], (*chip, c))
            cp.start()
            sends.append(cp)
        for k, (chip, jk) in enumerate(chips):
            got = _win(o_ref, 1, jk * cv, cv)
            _rcopy(got, got, s_sem.at[k], r_sem.at[k], (*chip, c)).wait_recv()
        for cp in sends:
            cp.wait_send()
        loc.wait()

    return pl.pallas_call(
        body, name=name, in_specs=[ANY], out_specs=ANY, out_shape=SDS((Lv, 4 * cv), v.dtype),
        scratch_shapes=[pltpu.SemaphoreType.DMA, pltpu.SemaphoreType.DMA((3,)), pltpu.SemaphoreType.DMA((3,))],
        compiler_params=pltpu.CompilerParams(has_side_effects=True),
    )(v)


def _half_shape(shape, sa):
    R, C = shape
    return (R // 2, C) if sa == 1 else (R, C // 2)


def _rs_sibling_copy(dws, sas, d_refs, land_refs, o, x, y, c):
    ha = 1 - sas[o]
    hl = dws[o].shape[ha] // 2
    return _win(d_refs[o], ha, (1 - c) * hl, hl), land_refs[o], (x, y, 1 - c)


def _rs_sibling_start(name, dws, sas, after):
    n = len(dws)

    def body(*refs):
        ins = refs[:n]
        s_sem, r_sem = refs[n + 1], refs[n + 2]
        lands = refs[2 * n + 3:3 * n + 3]
        token = refs[3 * n + 3]
        x, y, c = _coords()
        for o in range(n):
            src, dst, dev = _rs_sibling_copy(dws, sas, ins, lands, o, x, y, c)
            _rcopy(src, dst, s_sem.at[o], r_sem.at[o], dev).start()
        token[...] = jnp.zeros_like(token)

    out = pl.pallas_call(
        body, name=name,
        out_shape=(pltpu.SemaphoreType.DMA((n,)), pltpu.SemaphoreType.DMA((n,)),
                   *[pltpu.HBM(d.shape, d.dtype) for d in dws],
                   *[pltpu.HBM(_half_shape(d.shape, sa), d.dtype) for d, sa in zip(dws, sas)],
                   SDS((8, LANE), F32)),
        in_specs=[HBM] * n + [ANY], out_specs=(SEM, SEM, *[HBM] * (2 * n), TOKEN),
        input_output_aliases={k: 2 + k for k in range(n)},
        compiler_params=pltpu.CompilerParams(has_side_effects=EFFECT),
    )(*[_in_hbm(d) for d in dws], after)
    return out[0], out[1], list(out[2:2 + n]), list(out[2 + n:2 + 2 * n]), out[2 + 2 * n]


def _rs_sibling_wait(name, dws, lands, sas, s_sem, r_sem, after):
    n = len(dws)

    def body(*refs):
        d_refs, land_refs = refs[:n], refs[n:2 * n]
        s_ref, r_ref = refs[2 * n], refs[2 * n + 1]
        x, y, c = _coords()
        for o in range(n):
            src, dst, dev = _rs_sibling_copy(dws, sas, d_refs, land_refs, o, x, y, c)
            cp = _rcopy(src, dst, s_ref.at[o], r_ref.at[o], dev)
            cp.wait_send()
            cp.wait_recv()

    after = list(after) if isinstance(after, (list, tuple)) else [after]
    out = pl.pallas_call(
        body, name=name,
        out_shape=[pltpu.HBM(a.shape, a.dtype) for a in (*dws, *lands)],
        in_specs=[HBM] * (2 * n) + [SEM, SEM] + [ANY] * len(after), out_specs=[HBM] * (2 * n),
        input_output_aliases={k: k for k in range(2 * n)},
        compiler_params=pltpu.CompilerParams(has_side_effects=EFFECT),
    )(*dws, *lands, s_sem, r_sem, *after)
    return list(out[:n]), list(out[n:])


def _rs_piece_shape(p, sa):
    hr, hc = p.shape
    return (hr // N_CHIPS, hc) if sa == 0 else (hr, hc // N_CHIPS)


def _rs_ici_copy(parts, sas, p_refs, slot_refs, o, k, x, y, c):
    sa = sas[o]
    pl_ = parts[o].shape[sa] // N_CHIPS
    chip, jk = _other_chips(x, y)[k]
    return _win(p_refs[o], sa, jk * pl_, pl_), slot_refs[o].at[k], (*chip, c)


def _rs_start(name, parts, sas, after):
    n = len(parts)

    def body(*refs):
        ins = refs[:n]
        s_sem, r_sem = refs[n + 1], refs[n + 2]
        slots = refs[2 * n + 3:3 * n + 3]
        token = refs[3 * n + 3]
        x, y, c = _coords()
        for o in range(n):
            for k in range(3):
                src, dst, dev = _rs_ici_copy(parts, sas, ins, slots, o, k, x, y, c)
                _rcopy(src, dst, s_sem.at[3 * o + k], r_sem.at[3 * o + k], dev).start()
        token[...] = jnp.zeros_like(token)

    out = pl.pallas_call(
        body, name=name,
        out_shape=(pltpu.SemaphoreType.DMA((3 * n,)), pltpu.SemaphoreType.DMA((3 * n,)),
                   *[pltpu.HBM(p.shape, p.dtype) for p in parts],
                   *[pltpu.HBM((3,) + _rs_piece_shape(p, sa), p.dtype) for p, sa in zip(parts, sas)],
                   SDS((8, LANE), F32)),
        in_specs=[HBM] * n + [ANY], out_specs=(SEM, SEM, *[HBM] * (2 * n), TOKEN),
        input_output_aliases={k: 2 + k for k in range(n)},
        compiler_params=pltpu.CompilerParams(has_side_effects=EFFECT),
    )(*[_in_hbm(p) for p in parts], after)
    return out[0], out[1], list(out[2:2 + n]), list(out[2 + n:2 + 2 * n]), out[2 + 2 * n]


def _rs_wait(name, parts, slots, sas, s_sem, r_sem, after):
    n = len(parts)

    def body(*refs):
        p_refs, slot_refs = refs[:n], refs[n:2 * n]
        s_ref, r_ref = refs[2 * n], refs[2 * n + 1]
        x, y, c = _coords()
        for o in range(n):
            for k in range(3):
                src, dst, dev = _rs_ici_copy(parts, sas, p_refs, slot_refs, o, k, x, y, c)
                cp = _rcopy(src, dst, s_ref.at[3 * o + k], r_ref.at[3 * o + k], dev)
                cp.wait_send()
                cp.wait_recv()

    after = list(after) if isinstance(after, (list, tuple)) else [after]
    out = pl.pallas_call(
        body, name=name,
        out_shape=[pltpu.HBM(a.shape, a.dtype) for a in (*parts, *slots)],
        in_specs=[HBM] * (2 * n) + [SEM, SEM] + [ANY] * len(after), out_specs=[HBM] * (2 * n),
        input_output_aliases={k: k for k in range(2 * n)},
        compiler_params=pltpu.CompilerParams(has_side_effects=EFFECT),
    )(*parts, *slots, s_sem, r_sem, *after)
    return list(out[:n]), list(out[n:])


def _rs_share_copy(sas, layers, buf_idx, bufs, refs, o, x, y, c):
    ha = 1 - sas[o]
    hl = bufs[buf_idx[o]].shape[1 + ha] // 2
    layer = refs[buf_idx[o]].at[layers[o]]
    return _win(layer, ha, c * hl, hl), _win(layer, ha, (1 - c) * hl, hl), (x, y, 1 - c)


def _rs_share_start(name, sas, layers, buf_idx, bufs, after):
    n, nbuf = len(sas), len(bufs)

    def body(*refs):
        ins = refs[:nbuf]
        s_sem, r_sem = refs[nbuf + 1], refs[nbuf + 2]
        token = refs[2 * nbuf + 3]
        x, y, c = _coords()
        for o in range(n):
            mine, _, dev = _rs_share_copy(sas, layers, buf_idx, bufs, ins, o, x, y, c)
            _rcopy(mine, mine, s_sem.at[o], r_sem.at[o], dev).start()
        token[...] = jnp.zeros_like(token)

    out = pl.pallas_call(
        body, name=name,
        out_shape=(pltpu.SemaphoreType.DMA((n,)), pltpu.SemaphoreType.DMA((n,)),
                   *[pltpu.HBM(b.shape, b.dtype) for b in bufs], SDS((8, LANE), F32)),
        in_specs=[HBM] * nbuf + [ANY], out_specs=(SEM, SEM, *[HBM] * nbuf, TOKEN),
        input_output_aliases={k: 2 + k for k in range(nbuf)},
        compiler_params=pltpu.CompilerParams(has_side_effects=EFFECT),
    )(*[_in_hbm(b) for b in bufs], after)
    return out[0], out[1], list(out[2:2 + nbuf]), out[2 + nbuf]


def _rs_share_wait(name, sas, layers, buf_idx, bufs, s_sem, r_sem, after):
    n, nbuf = len(sas), len(bufs)

    def body(*refs):
        ins = refs[:nbuf]
        s_ref, r_ref = refs[nbuf], refs[nbuf + 1]
        x, y, c = _coords()
        for o in range(n):
            mine, got, dev = _rs_share_copy(sas, layers, buf_idx, bufs, ins, o, x, y, c)
            cp = _rcopy(mine, got, s_ref.at[o], r_ref.at[o], dev)
            cp.wait_send()
            cp.wait_recv()

    after = list(after) if isinstance(after, (list, tuple)) else [after]
    return list(pl.pallas_call(
        body, name=name, out_shape=[pltpu.HBM(b.shape, b.dtype) for b in bufs],
        in_specs=[HBM] * nbuf + [SEM, SEM] + [ANY] * len(after), out_specs=[HBM] * nbuf,
        input_output_aliases={k: k for k in range(nbuf)},
        compiler_params=pltpu.CompilerParams(has_side_effects=EFFECT),
    )(*bufs, s_sem, r_sem, *after))


def _place_own(name, packed, me_arr):
    R, C = packed.shape
    tr = _row_tile(R, C * 4)

    def body(me_ref, p_ref, o_ref):
        o_ref[...] = p_ref[...]

    return pl.pallas_call(
        body, name=name,
        grid_spec=pltpu.PrefetchScalarGridSpec(
            num_scalar_prefetch=1, grid=(R // tr,),
            in_specs=[pl.BlockSpec((tr, C), lambda i, me: (i, 0))],
            out_specs=pl.BlockSpec((None, tr, C), lambda i, me: (me[0], i, 0))),
        out_shape=SDS((N_DEV, R, C), packed.dtype), compiler_params=_cp(("parallel",), 32 * 2 ** 20),
    )(me_arr, packed)


def _exchange_copy(s_refs, k, x, y, c):
    px = 1 - x if k & 4 else x
    py = 1 - y if k & 2 else y
    pc = 1 - c if k & 1 else c
    return s_refs.at[4 * x + 2 * y + c], s_refs.at[4 * px + 2 * py + pc], (px, py, pc)


def _exchange_start(name, slots, after):
    def body(s_ref, after_ref, s_sem, r_sem, out_ref, token):
        x, y, c = _coords()
        for k in range(1, N_DEV):
            mine, _, dev = _exchange_copy(s_ref, k, x, y, c)
            _rcopy(mine, mine, s_sem.at[k - 1], r_sem.at[k - 1], dev).start()
        token[...] = jnp.zeros_like(token)

    out = pl.pallas_call(
        body, name=name,
        out_shape=(pltpu.SemaphoreType.DMA((N_DEV - 1,)), pltpu.SemaphoreType.DMA((N_DEV - 1,)),
                   pltpu.HBM(slots.shape, slots.dtype), SDS((8, LANE), F32)),
        in_specs=[HBM, ANY], out_specs=(SEM, SEM, HBM, TOKEN), input_output_aliases={0: 2},
        compiler_params=pltpu.CompilerParams(has_side_effects=EFFECT),
    )(_in_hbm(slots), after)
    return out


def _exchange_wait(name, slots, s_sem, r_sem, after):
    def body(s_ref, s_sem_ref, r_sem_ref, *rest):
        x, y, c = _coords()
        for k in range(1, N_DEV):
            mine, theirs, dev = _exchange_copy(s_ref, k, x, y, c)
            cp = _rcopy(mine, theirs, s_sem_ref.at[k - 1], r_sem_ref.at[k - 1], dev)
            cp.wait_send()
            cp.wait_recv()

    after = list(after) if isinstance(after, (list, tuple)) else [after]
    return pl.pallas_call(
        body, name=name, out_shape=pltpu.HBM(slots.shape, slots.dtype),
        in_specs=[HBM, SEM, SEM] + [ANY] * len(after), out_specs=HBM, input_output_aliases={0: 0},
        compiler_params=pltpu.CompilerParams(has_side_effects=EFFECT),
    )(slots, s_sem, r_sem, *after)


def _pack(arrays):
    rows = []
    for a in arrays:
        flat = a.reshape(-1).astype(F32)
        pad = (-flat.size) % PACK_TILE
        rows.append(jnp.pad(flat, (0, pad)).reshape(-1, LANE))
    return jnp.concatenate(rows, axis=0)


def _unpack(packed, shapes):
    out, row = [], 0
    for s in shapes:
        size = int(np.prod(s)) if len(s) else 1
        nrows = -(-size // PACK_TILE) * (PACK_TILE // LANE)
        out.append(packed[row:row + nrows].reshape(-1)[:size].reshape(s))
        row += nrows
    return out


BIG_WEIGHTS = {
    "ffn1_w_gate": 1, "ffn1_w_up": 1, "ffn1_w_down": 0, "ffn2_w_gate": 1, "ffn2_w_up": 1, "ffn2_w_down": 0,
    "gmlp_w_in": 1, "gmlp_w_out": 0, "w_kv": 1, "attn_w_q": 1, "attn_w_o": 0,
}
SMALL_WEIGHTS = ("ffn1_norm", "mix_norm", "ffn2_norm", "gmlp_w_s", "gmlp_b_s", "kv_norm", "k_norm", "attn_q_norm")
WEIGHT_ORDER = ("ffn1_norm", "ffn1_w_gate", "ffn1_w_up", "ffn1_w_down", "mix_norm", "ffn2_norm", "ffn2_w_gate",
                "ffn2_w_up", "ffn2_w_down", "gmlp_w_in", "gmlp_v_norm", "gmlp_w_s", "gmlp_b_s", "gmlp_w_out",
                "kv_norm", "w_kv", "k_norm", "attn_w_q", "attn_q_norm", "attn_w_o")


def _ep_all(accs, ex):
    return list(accs)


def _as3d(w):
    return w if w.ndim == 3 else w.reshape((1,) + w.shape)


def kernel(x, ffn1_norm, ffn1_w_gate, ffn1_w_up, ffn1_w_down, mix_norm, ffn2_norm, ffn2_w_gate, ffn2_w_up, ffn2_w_down, gmlp_w_in, gmlp_v_norm, gmlp_w_s, gmlp_b_s, gmlp_w_out, kv_norm, w_kv, k_norm, attn_w_q, attn_q_norm, attn_w_o, loss_target, m_ffn1_norm, m_ffn1_w_gate, m_ffn1_w_up, m_ffn1_w_down, m_mix_norm, m_ffn2_norm, m_ffn2_w_gate, m_ffn2_w_up, m_ffn2_w_down, m_gmlp_w_in, m_gmlp_v_norm, m_gmlp_w_s, m_gmlp_b_s, m_gmlp_w_out, m_kv_norm, m_w_kv, m_k_norm, m_attn_w_q, m_attn_q_norm, m_attn_w_o, v_ffn1_norm, v_ffn1_w_gate, v_ffn1_w_up, v_ffn1_w_down, v_mix_norm, v_ffn2_norm, v_ffn2_w_gate, v_ffn2_w_up, v_ffn2_w_down, v_gmlp_w_in, v_gmlp_v_norm, v_gmlp_w_s, v_gmlp_b_s, v_gmlp_w_out, v_kv_norm, v_w_kv, v_k_norm, v_attn_w_q, v_attn_q_norm, v_attn_w_o):
    P = dict(locals())
    assert x.shape[0] == 1, "one sample per device"
    S, D = x.shape[1], x.shape[2]
    NL = ffn1_norm.shape[0]
    NG = len(DILATIONS)
    HD = attn_w_o.shape[1] * N_CHIPS
    H = HD // HEAD_DIM
    DG = gmlp_w_out.shape[1] * N_CHIPS
    G = DG // GMLP_GROUP_WIDTH
    assert all((S // d) % ATTN_BLOCK == 0 for d in DILATIONS) and S % GMLP_CHUNK == 0
    xs = x.reshape(S, D)
    tgt = loss_target.reshape(S, D)
    c_arr = lax.axis_index("c").astype(jnp.int32).reshape(1)
    chip = 2 * lax.axis_index("x") + lax.axis_index("y")
    chip_arr = chip.astype(jnp.int32).reshape(1)
    kv_layer = N_A_LAYERS - 1


    def layer_weights(l):
        names = [("ffn1_w_gate", l), ("ffn1_w_up", l), ("ffn1_w_down", l), ("ffn2_w_gate", l), ("ffn2_w_up", l), ("ffn2_w_down", l)]
        if l < N_A_LAYERS:
            names += [("gmlp_w_in", l), ("gmlp_w_out", l)]
        else:
            names += [("attn_w_q", l - N_A_LAYERS), ("attn_w_o", l - N_A_LAYERS)]
        if l == kv_layer:
            names += [("w_kv", 0)]
        return names

    W = {}
    ag_open = {}
    n_first = 3
    ag_units = {}
    for l in range(NL):
        ag_units[f"{l}a"], ag_units[f"{l}b"] = layer_weights(l)[:n_first], layer_weights(l)[n_first:]

    def cast_unit(u):
        return [_cast_into_gathered("cast_shard", _as3d(P[n]), li, BIG_WEIGHTS[n], chip_arr) for n, li in ag_units[u]]

    def ag_begin(u, after, mats):
        sas = [BIG_WEIGHTS[n] for n, _ in ag_units[u]]
        s_sem, r_sem, mats, token = _ag_start(f"ag_start_l{u}", mats, sas, after)
        ag_open[u] = (sas, s_sem, r_sem, mats)
        return token

    def ag_finish(u, after):
        sas, s_sem, r_sem, mats = ag_open.pop(u)
        mats = _ag_wait(f"ag_wait_l{u}", mats, sas, s_sem, r_sem, after)
        W.update(dict(zip(ag_units[u], _ag_forward(f"ag_forward_l{u}", mats, sas))))

    vnorm_full = _all_gather_vec("ag_vnorm", gmlp_v_norm)
    ag_token = vnorm_full
    for u in [f"{l}{h}" for l in range(min(2, NL)) for h in "ab"]:
        ag_token = ag_begin(u, ag_token, cast_unit(u))
    cast_ahead = {f"{l}{h}": cast_unit(f"{l}{h}") for l in range(2, NL) for h in "ab"}
    ag_finish("0a", [ag_token] + [m for u in cast_ahead for m in cast_ahead[u]])

    kgain = jnp.tile(k_norm[:, None, :], (1, H, 1)).reshape(1, NG * HD)
    qgain = [jnp.tile(attn_q_norm[j][:, None, :], (1, H, 1)).reshape(1, NG * HD) for j in range(NL - N_A_LAYERS)]
    q_scale = HEAD_DIM ** -0.5
    one = [(0, 0, 0)]

    def ffn_fwd(xc, gamma, wg, wu, wd, dep=None):
        n = _rms_fwd("ffn_norm", xc, gamma, dep)
        g, u, act = _mm("ffn_up", [n], [wg, wu], [(0, 0, 0), (0, 1, 1)], _ep_swiglu, [BF16] * 3)
        (x2,) = _mm("ffn_down", [act], [wd], one, _ep_residual(0.5), [F32], extras=[xc])
        return x2, (xc, n, g, u, act)

    saved = {}
    xc = xs
    for l in range(NL):
        if l > 0:
            ag_finish(f"{l}a", xc)
        if l + 2 < NL:
            for h in "ab":
                ag_token = ag_begin(f"{l + 2}{h}", ag_token, cast_ahead.pop(f"{l + 2}{h}"))
        xc, saved["f1", l] = ffn_fwd(xc, ffn1_norm[l], W["ffn1_w_gate", l], W["ffn1_w_up", l], W["ffn1_w_down", l], ag_token)
        ag_finish(f"{l}b", xc)
        h = _rms_fwd("mix_norm", xc, mix_norm[l])
        if l < N_A_LAYERS:
            zpre, z = _mm("gmlp_in", [h], [W["gmlp_w_in", l]], one, _ep_gelu, [BF16, F32])
            bias_full = jnp.repeat(gmlp_b_s[l].T, GMLP_GROUP_WIDTH, axis=1)
            gated = _gmlp_gate_fwd("gmlp_gate", z, vnorm_full[l:l + 1], gmlp_w_s[l], bias_full)
            (x2,) = _mm("gmlp_out", [gated], [W["gmlp_w_out", l]], one, _ep_residual(1.0), [F32], extras=[xc])
            saved["mix", l] = (xc, h, zpre, z, gated, bias_full)
        else:
            j = l - N_A_LAYERS
            (q_raw,) = _mm("attn_q", [h], [W["attn_w_q", j]], one, _ep_plain, [F32])
            qn = _head_norm_fwd("q_norm", q_raw, 0, 1, qgain[j], q_scale, False)
            os_, lses = [], []
            for gi, dil in enumerate(DILATIONS):
                o, lse = _attn_fwd(f"attn_fwd_d{dil}", qn, kn, vb, gi, dil)
                os_.append(o)
                lses.append(lse)
            ob = _attn_combine("attn_mix", os_, lses)
            (x2,) = _mm("attn_o", [ob], [W["attn_w_o", j]], one, _ep_residual(1.0), [F32], extras=[xc])
            saved["mix", l] = (xc, h, q_raw, qn, os_, lses, ob)
        xc = x2
        xc, saved["f2", l] = ffn_fwd(xc, ffn2_norm[l], W["ffn2_w_gate", l], W["ffn2_w_up", l], W["ffn2_w_down", l])
        if l == kv_layer:
            kvn = _rms_fwd("kv_norm", xc, kv_norm)
            (kv_raw,) = _mm("kv_proj", [kvn], [W["w_kv", 0]], one, _ep_plain, [F32])
            kn, vb = _head_norm_fwd("k_norm", kv_raw, 0, 2, kgain, 1.0, True)
            saved["kv"] = (xc, kvn, kv_raw)

    dx, dxb, loss_rows = _loss_grad("loss", xc, tgt, 0.5)
    dW = {}
    dsmall = {n: [None] * P[n].shape[0] for n in ("ffn1_norm", "mix_norm", "ffn2_norm")}
    dsmall.update(gmlp_w_s=[None] * N_A_LAYERS, gmlp_b_s=[None] * N_A_LAYERS, gmlp_v_norm=[None] * N_A_LAYERS,
                  attn_q_norm=[None] * (NL - N_A_LAYERS))
    dks = [[] for _ in DILATIONS]
    dvs = [[] for _ in DILATIONS]

    names_big = list(BIG_WEIGHTS)
    gbuf = [lax.empty(_as3d(P[n]).shape, F32) for n in names_big]
    where = (c_arr, chip_arr)
    sib_open = {}
    rs_open = {}

    def sib_begin(l, after):
        names = layer_weights(l)
        sas = [BIG_WEIGHTS[n] for n, _ in names]
        s_sem, r_sem, dws, lands, token = _rs_sibling_start(f"rs_sibling_start_l{l}", [dW[k] for k in names], sas, after)
        sib_open[l] = (names, sas, s_sem, r_sem, dws, lands)
        return token

    def rs_begin(l, after):
        names, sas, s_sem, r_sem, dws, lands = sib_open.pop(l)
        dws, lands = _rs_sibling_wait(f"rs_sibling_wait_l{l}", dws, lands, sas, s_sem, r_sem, after)
        parts = [_add_half("rs_add_half", d, ln, sa, c_arr) for d, ln, sa in zip(dws, lands, sas)]
        s_sem, r_sem, parts, slots, token = _rs_start(f"rs_start_l{l}", parts, sas, dws[0])
        rs_open[l] = (names, sas, s_sem, r_sem, parts, slots)
        return token

    share_open = {}
    rs_landed = {}

    def rs_collect(l, after):
        names, sas, s_sem, r_sem, parts, slots = rs_open.pop(l)
        parts, slots = _rs_wait(f"rs_wait_l{l}", parts, slots, sas, s_sem, r_sem, after)
        rs_landed[l] = (names, sas, parts, slots)

    def rs_reduce(l):
        names, sas, parts, slots = rs_landed.pop(l)
        for p, s, sa, (n, li) in zip(parts, slots, sas, names):
            bi = names_big.index(n)
            gbuf[bi] = _sum_into("rs_sum_chips", p, s, gbuf[bi], li, sa, where)
        layers, bidx = [li for _, li in names], [names_big.index(n) for n, _ in names]
        s_sem, r_sem, bufs, _ = _rs_share_start(f"rs_share_start_l{l}", sas, layers, bidx, gbuf, parts[0])
        gbuf[:] = bufs
        share_open[l] = (sas, layers, bidx, s_sem, r_sem)

    def share_done(l, after):
        if l in share_open:
            sas, layers, bidx, s_sem, r_sem = share_open.pop(l)
            gbuf[:] = _rs_share_wait(f"rs_share_wait_l{l}", sas, layers, bidx, gbuf, s_sem, r_sem, after)

    def ffn_bwd(dx, dxb, sv, gamma, wg, wu, wd, key, l, next_scale, dep=None):
        xin, n, g, u, act = sv
        dg, du = _mm("ffn_dact", [dxb], [wd], one, _ep_swiglu_bwd, [BF16, BF16], tb=True, extras=[g, u], dep=dep)
        (dW[key + "_w_down", l],) = _mm("ffn_dwd", [act], [dxb], one, _ep_plain, [BF16], ta=True)
        dW[key + "_w_gate", l], dW[key + "_w_up", l] = _mm("ffn_dwgu", [n], [dg, du], [(0, 0, 0), (0, 1, 1)], _ep_all, [BF16, BF16], ta=True)
        (dn,) = _mm("ffn_dn", [dg, du], [wg, wu], [(0, 0, 0), (1, 1, 0)], _ep_plain, [F32], tb=True)
        dx, dxb, dsmall[key + "_norm"][l] = _rms_bwd("ffn_norm_bwd", xin, gamma, dn, dx, next_scale)
        return dx, dxb

    dep = None
    for l in reversed(range(NL)):
        if l == kv_layer:
            x_kv, kvn, kv_raw = saved["kv"]
            dkv_raw, dkgain = _head_norm_bwd("k_norm_bwd", kv_raw, 0, 2, kgain, 1.0, dks, dvs)
            (dW["w_kv", 0],) = _mm("kv_dw", [kvn], [dkv_raw], one, _ep_plain, [BF16], ta=True, dep=dep)
            (dkvn,) = _mm("kv_dn", [dkv_raw], [W["w_kv", 0]], one, _ep_plain, [F32], tb=True)
            dx, dxb, dkvnorm = _rms_bwd("kv_norm_bwd", x_kv, kv_norm, dkvn, dx, 0.5)
        dx, dxb = ffn_bwd(dx, dxb, saved["f2", l], ffn2_norm[l], W["ffn2_w_gate", l], W["ffn2_w_up", l], W["ffn2_w_down", l], "ffn2", l, 1.0, dep)
        if l + 1 < NL:
            dep = rs_begin(l + 1, dx)
        if l < N_A_LAYERS:
            xin, h, zpre, z, gated, bias_full = saved["mix", l]
            (dW["gmlp_w_out", l],) = _mm("gmlp_dwout", [gated], [dxb], one, _ep_plain, [BF16], ta=True, dep=dep)
            (dgated,) = _mm("gmlp_dgated", [dxb], [W["gmlp_w_out", l]], one, _ep_plain, [F32], tb=True)
            dzpre, dws, dbacc, dvn = _gmlp_gate_bwd("gmlp_gate_bwd", z, zpre, dgated, vnorm_full[l:l + 1], gmlp_w_s[l],
                                                    jnp.swapaxes(gmlp_w_s[l], 1, 2), bias_full)
            (dW["gmlp_w_in", l],) = _mm("gmlp_dwin", [h], [dzpre], one, _ep_plain, [BF16], ta=True)
            (dh,) = _mm("gmlp_dh", [dzpre], [W["gmlp_w_in", l]], one, _ep_plain, [F32], tb=True)
            dsmall["gmlp_w_s"][l] = dws
            dsmall["gmlp_b_s"][l] = dbacc.reshape(GMLP_CHUNK, G, GMLP_GROUP_WIDTH).sum(-1).T
            dsmall["gmlp_v_norm"][l] = dvn.reshape(DG)
        else:
            j = l - N_A_LAYERS
            xin, h, q_raw, qn, os_, lses, ob = saved["mix", l]
            (dW["attn_w_o", j],) = _mm("attn_dwo", [ob], [dxb], one, _ep_plain, [BF16], ta=True, dep=dep)
            (d_ob,) = _mm("attn_dob", [dxb], [W["attn_w_o", j]], one, _ep_plain, [F32], tb=True)
            lse_t, dl_rows, dob = _attn_combine("attn_mix_bwd", os_, lses, d_o=d_ob)
            dqs = []
            for gi, dil in enumerate(DILATIONS):
                dq, dk, dv = _attn_bwd(f"attn_bwd_d{dil}", qn, kn, vb, dob, lse_t, dl_rows, gi, dil)
                dqs.append([dq])
                dks[gi].append(dk)
                dvs[gi].append(dv)
            dq_raw, dqgain = _head_norm_bwd("q_norm_bwd", q_raw, 0, 1, qgain[j], q_scale, dqs, None)
            (dW["attn_w_q", j],) = _mm("attn_dwq", [h], [dq_raw], one, _ep_plain, [BF16], ta=True)
            (dh,) = _mm("attn_dh", [dq_raw], [W["attn_w_q", j]], one, _ep_plain, [F32], tb=True)
            dsmall["attn_q_norm"][j] = dqgain.reshape(NG, H, HEAD_DIM).sum(1)
        dx, dxb, dsmall["mix_norm"][l] = _rms_bwd("mix_norm_bwd", xin, mix_norm[l], dh, dx, 0.5)
        dx, dxb = ffn_bwd(dx, dxb, saved["f1", l], ffn1_norm[l], W["ffn1_w_gate", l], W["ffn1_w_up", l], W["ffn1_w_down", l], "ffn1", l, 0.5)
        dep = sib_begin(l, dx)
        if l + 2 < NL:
            share_done(l + 3, dep)
            rs_collect(l + 2, dep)
            rs_reduce(l + 2)
    grad_x = dx.reshape(x.shape)

    loss_part = (0.5 / D) * jnp.sum(loss_rows)
    small_grads = [jnp.stack([g.reshape(P[n].shape[1:]) for g in dsmall[n]]) for n in ("ffn1_norm", "mix_norm", "ffn2_norm", "gmlp_w_s", "gmlp_b_s")]
    small_grads += [dkvnorm.reshape(kv_norm.shape), dkgain.reshape(NG, H, HEAD_DIM).sum(1), jnp.stack(dsmall["attn_q_norm"])]
    vn_grad_full = jnp.stack(dsmall["gmlp_v_norm"])
    me_arr = (2 * chip + lax.axis_index("c")).astype(jnp.int32).reshape(1)
    small_slots = _place_own("place_small", _pack(small_grads + [vn_grad_full, loss_part.reshape(1)]), me_arr)

    adam_out = {n: [lax.empty(_as3d(P[n]).shape, F32) for _ in range(4)] for n in names_big}

    def adam_layer(l):
        for n, li in layer_weights(l):
            adam_out[n] = _adamw_layer("adamw", gbuf[names_big.index(n)], _as3d(P[n]), _as3d(P["m_" + n]),
                                       _as3d(P["v_" + n]), li, adam_out[n])

    def updated():
        return [adam_out[n][0] for n in names_big]

    dep = rs_begin(0, dep)
    ex_s, ex_r, small_slots, dep = _exchange_start("exchange_small_start", small_slots, dep)
    for l in range(2, NL):
        share_done(l, dep)
    for l in reversed(range(2, NL)):
        adam_layer(l)
    if NL > 1:
        rs_collect(1, [dep] + updated())
        rs_reduce(1)
    rs_collect(0, [dep] + updated())
    if NL > 1:
        share_done(1, dep)
    rs_reduce(0)
    share_done(0, dep)
    for l in reversed(range(min(2, NL))):
        adam_layer(l)
    big_out = {n: [o.reshape(P[n].shape) for o in adam_out[n]] for n in names_big}

    total = _sum_slots("sum_devices", _exchange_wait("exchange_small_wait", small_slots, ex_s, ex_r, updated()))
    shapes = [P[n].shape for n in SMALL_WEIGHTS] + [vn_grad_full.shape, (1,)]
    red = _unpack(total, shapes)
    loss = red[-1].reshape(())
    cv = gmlp_v_norm.shape[1]
    vn_grad = lax.dynamic_slice_in_dim(red[-2], chip * cv, cv, axis=1)
    names_small = list(SMALL_WEIGHTS) + ["gmlp_v_norm"]
    g_small = red[:len(SMALL_WEIGHTS)] + [vn_grad]
    outs = _adamw("adamw_small", _pack(g_small), _pack([P[n] for n in names_small]),
                  _pack([P["m_" + n] for n in names_small]), _pack([P["v_" + n] for n in names_small]))
    small_shapes = [P[n].shape for n in names_small]
    small_out = {n: [] for n in names_small}
    for o in outs:
        for n, a in zip(names_small, _unpack(o, small_shapes)):
            small_out[n].append(a)

    res = {**big_out, **small_out}
    return (loss, grad_x, *[res[n][0] for n in WEIGHT_ORDER], *[res[n][1] for n in WEIGHT_ORDER],
            *[res[n][2] for n in WEIGHT_ORDER], *[res[n][3] for n in WEIGHT_ORDER])
```

```python
import numpy as np
import jax
import jax.numpy as jnp
from jax import lax
from jax.experimental import pallas as pl
from jax.experimental.pallas import tpu as pltpu

F32 = jnp.float32
BF16 = jnp.bfloat16
SDS = jax.ShapeDtypeStruct

EPS = 1e-6
HEAD_DIM = 128
GMLP_CHUNK = 128
GMLP_GROUP_WIDTH = 128
DILATIONS = (1, 4, 16)
ATTN_BLOCK = 128
N_A_LAYERS = 2
ADAM_LR, ADAM_B1, ADAM_B2, ADAM_EPS, ADAM_WD, ADAM_STEP = 0.001, 0.9, 0.999, 1e-08, 0.01, 10

N_CHIPS = 4
N_DEV = 8
MESH = pl.DeviceIdType.MESH
V7X_VMEM_BYTES = 64 * 2 ** 20
VMEM_CEILING = V7X_VMEM_BYTES - 6 * 2 ** 20
VMEM_BLOCK_BUDGET = 38 * 2 ** 20
LANE = 128
BF16_ROWS = 16
PACK_TILE = 8 * LANE


def _cp(sem=None, vmem=None):
    kw = {}
    if sem is not None:
        kw["dimension_semantics"] = sem
    if vmem is not None:
        kw["vmem_limit_bytes"] = int(min(max(vmem, 16 * 2 ** 20), VMEM_CEILING))
    return pltpu.CompilerParams(**kw)


def _pick(dim, cands):
    for c in cands:
        if c <= dim and dim % c == 0:
            return c
    return dim


def _row_tile(rows, row_bytes, target=2 ** 20):
    t = 1024
    while t > 8 and (t * row_bytes > target or rows % t):
        t //= 2
    return t if rows % t == 0 else rows


def _sigmoid(x):
    return 1.0 / (1.0 + jnp.exp(-x))


_GELU_C = 0.7978845608028654
_GELU_A = 0.044715


def _gelu(x):
    return 0.5 * x * (1.0 + jnp.tanh(_GELU_C * (x + _GELU_A * (x * x * x))))


def _gelu_grad(x):
    t = jnp.tanh(_GELU_C * (x + _GELU_A * (x * x * x)))
    return 0.5 * (1.0 + t) + 0.5 * x * (1.0 - t * t) * (_GELU_C * (1.0 + 3.0 * _GELU_A * x * x))


def _mm_tiles(M, N, K, n_a, n_b, n_acc, io_bytes, a_copies=2):
    tks = [K] + [d for d in (4096, 3072, 2816, 2048, 1024, 512, 256, 128) if d < K and K % d == 0]
    tms = [t for t in (1024, 512, 256, 128) if M % t == 0] or [M]
    tns = [t for t in (512, 256, 128) if N % t == 0] or [N]
    best = None
    for tk in tks:
        for tm in tms:
            for tn in tns:
                est = 2 * (a_copies * n_a * tm * tk + 2 * n_b * tk * tn) + 2 * tm * tn * io_bytes
                est += n_acc * tm * tn * 4 * (2 if tk < K else 1)
                if est <= VMEM_BLOCK_BUDGET:
                    return tm, tn, tk, est
                if best is None or est < best[3]:
                    best = (tm, tn, tk, est)
    return best


def _mm(name, a_list, b_list, terms, epilogue, out_dtypes, *, ta=False, tb=False, extras=(), dep=None):
    n_acc = 1 + max(t[2] for t in terms)
    a0, b0 = a_list[0], b_list[0]
    (K, M) = a0.shape if ta else a0.shape[::-1]
    N = b0.shape[0] if tb else b0.shape[1]
    io_bytes = sum(jnp.dtype(e.dtype).itemsize for e in extras) + sum(jnp.dtype(d).itemsize for d in out_dtypes)
    tm, tn, tk, est = _mm_tiles(M, N, K, len(a_list), len(b_list), n_acc, io_bytes, 3 if ta else 2)
    nk = K // tk
    na, nb, ne, no = len(a_list), len(b_list), len(extras), len(out_dtypes)
    deps = [] if dep is None else [dep]
    nd = len(deps)
    keep_t = ta and nk == 1
    n_acc_scr = n_acc if nk > 1 else 0
    dn = (((0 if (ta and not keep_t) else 1,), (1 if tb else 0,)), ((), ()))

    def body(*refs):
        a_refs = refs[:na]
        b_refs = refs[na:na + nb]
        e_refs = refs[na + nb:na + nb + ne]
        o_refs = refs[na + nb + ne + nd:na + nb + ne + nd + no]
        acc_refs = refs[na + nb + ne + nd + no:na + nb + ne + nd + no + n_acc_scr]
        lhs_refs = a_refs
        if keep_t:
            lhs_refs = refs[na + nb + ne + nd + no + n_acc_scr:]

            @pl.when(pl.program_id(1) == 0)
            def _():
                for ai in range(na):
                    lhs_refs[ai][...] = a_refs[ai][...].T

        parts = [None] * n_acc
        for ai, bi, qi in terms:
            d = lax.dot_general(lhs_refs[ai][...], b_refs[bi][...], dn, preferred_element_type=F32)
            parts[qi] = d if parts[qi] is None else parts[qi] + d

        def finish(accs):
            outs = epilogue(accs, [e[...] for e in e_refs])
            for o_ref, o in zip(o_refs, outs):
                o_ref[...] = o.astype(o_ref.dtype)

        if nk == 1:
            finish(parts)
        else:
            k = pl.program_id(2)

            @pl.when(k == 0)
            def _():
                for q in range(n_acc):
                    acc_refs[q][...] = parts[q]

            @pl.when(k > 0)
            def _():
                for q in range(n_acc):
                    acc_refs[q][...] += parts[q]

            @pl.when(k == nk - 1)
            def _():
                finish([acc_refs[q][...] for q in range(n_acc)])

    a_spec = pl.BlockSpec((tk, tm), lambda i, j, k: (k, i)) if ta else pl.BlockSpec((tm, tk), lambda i, j, k: (i, k))
    b_spec = pl.BlockSpec((tn, tk), lambda i, j, k: (j, k)) if tb else pl.BlockSpec((tk, tn), lambda i, j, k: (k, j))
    e_spec = pl.BlockSpec((tm, tn), lambda i, j, k: (i, j))
    outs = pl.pallas_call(
        body, name=name, grid=(M // tm, N // tn, nk),
        in_specs=[a_spec] * na + [b_spec] * nb + [e_spec] * ne + [pl.BlockSpec((8, LANE), lambda i, j, k: (0, 0))] * nd,
        out_specs=[e_spec] * no,
        out_shape=[SDS((M, N), d) for d in out_dtypes],
        scratch_shapes=[pltpu.VMEM((tm, tn), F32) for _ in range(n_acc_scr)]
        + ([pltpu.VMEM((tm, tk), a.dtype) for a in a_list] if keep_t else []),
        compiler_params=_cp(("parallel", "arbitrary" if keep_t else "parallel", "arbitrary"), est + 12 * 2 ** 20),
    )(*[_in_hbm(a) for a in a_list], *[_in_hbm(b) for b in b_list], *extras, *deps)
    return outs


def _ep_plain(accs, ex):
    return [accs[0]]


def _ep_swiglu(accs, ex):
    g, u = accs
    s = _sigmoid(g)
    sg = g * s
    return [u * (s + sg * (1.0 - s)), sg, sg * u]


def _ep_swiglu_bwd(accs, ex):
    da = accs[0]
    return [da * ex[0].astype(F32), da * ex[1].astype(F32)]


def _ep_gelu(accs, ex):
    return [_gelu_grad(accs[0]), _gelu(accs[0])]


def _ep_residual(scale):
    def ep(accs, ex):
        return [ex[0] + scale * accs[0]]
    return ep


def _rms_fwd(name, x, gamma, dep=None):
    S, D = x.shape
    tr = _row_tile(S, D * 4)
    deps = [] if dep is None else [dep]

    def body(x_ref, g_ref, *rest):
        o_ref = rest[-1]
        xv = x_ref[...]
        r = lax.rsqrt(jnp.mean(xv * xv, axis=-1, keepdims=True) + EPS)
        o_ref[...] = (xv * r * g_ref[...]).astype(BF16)

    return pl.pallas_call(
        body, name=name, grid=(S // tr,),
        in_specs=[pl.BlockSpec((tr, D), lambda i: (i, 0)), pl.BlockSpec((1, D), lambda i: (0, 0))]
        + [pl.BlockSpec((8, LANE), lambda i: (0, 0))] * len(deps),
        out_specs=pl.BlockSpec((tr, D), lambda i: (i, 0)),
        out_shape=SDS((S, D), BF16),
        compiler_params=_cp(("parallel",), 32 * 2 ** 20),
    )(x, gamma.reshape(1, D), *deps)


def _rms_bwd(name, x, gamma, dn, dx_in, out_scale):
    S, D = x.shape
    tr = _row_tile(S, D * 4, 2 ** 19)

    def body(x_ref, g_ref, dn_ref, dxi_ref, dxo_ref, dxb_ref, dg_ref):
        i = pl.program_id(0)
        xv = x_ref[...]
        r = lax.rsqrt(jnp.mean(xv * xv, axis=-1, keepdims=True) + EPS)
        xh = xv * r
        dnv = dn_ref[...]
        dxh = dnv * g_ref[...]
        dx = dxi_ref[...] + r * (dxh - xh * jnp.mean(dxh * xh, axis=-1, keepdims=True))
        dxo_ref[...] = dx
        dxb_ref[...] = (out_scale * dx).astype(BF16)
        part = jnp.sum(dnv * xh, axis=0, keepdims=True)

        @pl.when(i == 0)
        def _():
            dg_ref[...] = part

        @pl.when(i > 0)
        def _():
            dg_ref[...] += part

    row = pl.BlockSpec((tr, D), lambda i: (i, 0))
    vec = pl.BlockSpec((1, D), lambda i: (0, 0))
    return pl.pallas_call(
        body, name=name, grid=(S // tr,),
        in_specs=[row, vec, row, row], out_specs=[row, row, vec],
        out_shape=[SDS((S, D), F32), SDS((S, D), BF16), SDS((1, D), F32)],
        compiler_params=_cp(("arbitrary",), 40 * 2 ** 20),
    )(x, gamma.reshape(1, D), dn, dx_in)


def _loss_grad(name, y, t, out_scale):
    S, D = y.shape
    tr = _row_tile(S, D * 4, 2 ** 19)
    inv_d = 1.0 / D

    def body(y_ref, t_ref, dy_ref, dyb_ref, ls_ref):
        i = pl.program_id(0)
        e = y_ref[...] - t_ref[...]
        dy = e * inv_d
        dy_ref[...] = dy
        dyb_ref[...] = (out_scale * dy).astype(BF16)
        part = jnp.sum(e * e, axis=0, keepdims=True)

        @pl.when(i == 0)
        def _():
            ls_ref[...] = part

        @pl.when(i > 0)
        def _():
            ls_ref[...] += part

    row = pl.BlockSpec((tr, D), lambda i: (i, 0))
    vec = pl.BlockSpec((1, D), lambda i: (0, 0))
    return pl.pallas_call(
        body, name=name, grid=(S // tr,),
        in_specs=[row, row], out_specs=[row, row, vec],
        out_shape=[SDS((S, D), F32), SDS((S, D), BF16), SDS((1, D), F32)],
        compiler_params=_cp(("arbitrary",), 32 * 2 ** 20),
    )(y, t)


def _head_mean(v):
    return jnp.mean(v, axis=-1, keepdims=True)


def _head_norm_fwd(name, raw, part, n_parts, gain_t, scale, with_pass):
    S = raw.shape[0]
    W = raw.shape[1] // n_parts
    nh = W // HEAD_DIM
    tr = _row_tile(S, W * (4 + 2) * (2 if with_pass else 1), 4 * 2 ** 20)

    def body(*refs):
        if with_pass:
            x_ref, p_ref, g_ref, o_ref, po_ref = refs
            po_ref[...] = p_ref[...].astype(BF16)
        else:
            x_ref, g_ref, o_ref = refs
        for h in range(nh):
            sl = slice(h * HEAD_DIM, (h + 1) * HEAD_DIM)
            xv = x_ref[:, sl]
            r = lax.rsqrt(_head_mean(xv * xv) + EPS)
            o_ref[:, sl] = (xv * r * g_ref[:, sl] * scale).astype(BF16)

    xspec = pl.BlockSpec((tr, W), lambda i: (i, part))
    ospec = pl.BlockSpec((tr, W), lambda i: (i, 0))
    gspec = pl.BlockSpec((1, W), lambda i: (0, 0))
    if with_pass:
        in_specs = [xspec, pl.BlockSpec((tr, W), lambda i: (i, 1)), gspec]
        args = (raw, raw, gain_t)
        out_specs, out_shape = [ospec, ospec], [SDS((S, W), BF16), SDS((S, W), BF16)]
    else:
        in_specs, args = [xspec, gspec], (raw, gain_t)
        out_specs, out_shape = ospec, SDS((S, W), BF16)
    return pl.pallas_call(
        body, name=name, grid=(S // tr,), in_specs=in_specs, out_specs=out_specs, out_shape=out_shape,
        compiler_params=_cp(("parallel",), 40 * 2 ** 20),
    )(*args)


def _head_norm_bwd(name, raw, part, n_parts, gain_t, scale, dy_groups, pass_groups):
    S = raw.shape[0]
    W = raw.shape[1] // n_parts
    ng = len(dy_groups)
    HD = W // ng
    nhg = HD // HEAD_DIM
    n_dy = [len(g) for g in dy_groups]
    n_ps = [len(g) for g in pass_groups] if pass_groups is not None else []
    flat = [a for g in dy_groups for a in g] + ([a for g in pass_groups for a in g] if pass_groups is not None else [])
    out_w = 2 * W if pass_groups is not None else W
    tr = _row_tile(S, W * 4 + sum(HD * jnp.dtype(a.dtype).itemsize for a in flat) + out_w * 2, 8 * 2 ** 20)

    def body(*refs):
        x_ref, g_ref = refs[0], refs[1]
        d_refs = refs[2:2 + len(flat)]
        o_ref, dg_ref = refs[2 + len(flat)], refs[3 + len(flat)]
        i = pl.program_id(0)

        @pl.when(i == 0)
        def _():
            dg_ref[...] = jnp.zeros_like(dg_ref)

        pos = 0
        for gi in range(ng):
            dys = d_refs[pos:pos + n_dy[gi]]
            pos += n_dy[gi]
            for h in range(nhg):
                sl = slice(gi * HD + h * HEAD_DIM, gi * HD + (h + 1) * HEAD_DIM)
                hs = slice(h * HEAD_DIM, (h + 1) * HEAD_DIM)
                dy = dys[0][:, hs].astype(F32)
                for extra in dys[1:]:
                    dy = dy + extra[:, hs].astype(F32)
                xv = x_ref[:, sl]
                r = lax.rsqrt(_head_mean(xv * xv) + EPS)
                xh = xv * r
                dxh = dy * (g_ref[:, sl] * scale)
                o_ref[:, sl] = (r * (dxh - xh * _head_mean(dxh * xh))).astype(BF16)
                dg_ref[:, sl] += jnp.sum(dy * xh, axis=0, keepdims=True) * scale
        for gi in range(len(n_ps)):
            ps = d_refs[pos:pos + n_ps[gi]]
            pos += n_ps[gi]
            acc = ps[0][...].astype(F32)
            for extra in ps[1:]:
                acc = acc + extra[...].astype(F32)
            o_ref[:, W + gi * HD:W + (gi + 1) * HD] = acc.astype(BF16)

    dspec = pl.BlockSpec((tr, HD), lambda i: (i, 0))
    return pl.pallas_call(
        body, name=name, grid=(S // tr,),
        in_specs=[pl.BlockSpec((tr, W), lambda i: (i, part)), pl.BlockSpec((1, W), lambda i: (0, 0))] + [dspec] * len(flat),
        out_specs=[pl.BlockSpec((tr, out_w), lambda i: (i, 0)), pl.BlockSpec((1, W), lambda i: (0, 0))],
        out_shape=[SDS((S, out_w), BF16), SDS((1, W), F32)],
        compiler_params=_cp(("arbitrary",), 48 * 2 ** 20),
    )(raw, gain_t, *flat)


def _tril_mask():
    r = lax.broadcasted_iota(jnp.int32, (GMLP_CHUNK, GMLP_CHUNK), 0)
    c = lax.broadcasted_iota(jnp.int32, (GMLP_CHUNK, GMLP_CHUNK), 1)
    return r >= c


def _gmlp_gate_fwd(name, z, v_norm, w_s, bias_full):
    S, DG2 = z.shape
    DG = DG2 // 2
    G = DG // GMLP_GROUP_WIDTH
    C = GMLP_CHUNK

    def body(u_ref, v_ref, vn_ref, ws_ref, b_ref, o_ref):
        mask = _tril_mask()
        v = v_ref[...]
        r = lax.rsqrt(jnp.mean(v * v, axis=-1, keepdims=True) + EPS)
        vn = (v * r * vn_ref[...]).astype(BF16)
        for g in range(G):
            sl = slice(g * GMLP_GROUP_WIDTH, (g + 1) * GMLP_GROUP_WIDTH)
            wm = jnp.where(mask, ws_ref[g], 0.0).astype(BF16)
            sv = jnp.dot(wm, vn[:, sl], preferred_element_type=F32) + b_ref[:, sl]
            o_ref[:, sl] = (u_ref[:, sl] * sv).astype(BF16)

    return pl.pallas_call(
        body, name=name, grid=(S // C,),
        in_specs=[pl.BlockSpec((C, DG), lambda i: (i, 0)), pl.BlockSpec((C, DG), lambda i: (i, 1)),
                  pl.BlockSpec((1, DG), lambda i: (0, 0)), pl.BlockSpec((G, C, C), lambda i: (0, 0, 0)),
                  pl.BlockSpec((C, DG), lambda i: (0, 0))],
        out_specs=pl.BlockSpec((C, DG), lambda i: (i, 0)),
        out_shape=SDS((S, DG), BF16),
        compiler_params=_cp(("parallel",), 32 * 2 ** 20),
    )(z, z, v_norm, w_s, bias_full)


def _gmlp_gate_bwd(name, z, zpre, dgated, v_norm, w_s, w_s_t, bias_full):
    S, DG2 = z.shape
    DG = DG2 // 2
    G = DG // GMLP_GROUP_WIDTH
    C = GMLP_CHUNK

    def body(z_ref, zp_ref, dg_ref, vn_ref, ws_ref, wst_ref, b_ref, dz_ref, dws_ref, db_ref, dvn_ref):
        i = pl.program_id(0)
        mask = _tril_mask()
        mask_t = jnp.logical_not(mask) | (lax.broadcasted_iota(jnp.int32, (C, C), 0) == lax.broadcasted_iota(jnp.int32, (C, C), 1))
        u = z_ref[:, :DG]
        v = z_ref[:, DG:]
        r = lax.rsqrt(jnp.mean(v * v, axis=-1, keepdims=True) + EPS)
        vh = v * r
        gain = vn_ref[...]
        vn = (vh * gain).astype(BF16)
        dgt = dg_ref[...]

        @pl.when(i == 0)
        def _():
            dws_ref[...] = jnp.zeros_like(dws_ref)
            db_ref[...] = jnp.zeros_like(db_ref)
            dvn_ref[...] = jnp.zeros_like(dvn_ref)

        dvn_parts = []
        for g in range(G):
            sl = slice(g * GMLP_GROUP_WIDTH, (g + 1) * GMLP_GROUP_WIDTH)
            wm = jnp.where(mask, ws_ref[g], 0.0).astype(BF16)
            wmt = jnp.where(mask_t, wst_ref[g], 0.0).astype(BF16)
            sv = jnp.dot(wm, vn[:, sl], preferred_element_type=F32) + b_ref[:, sl]
            dgs = dgt[:, sl]
            du = dgs * sv
            dsv = dgs * u[:, sl]
            db_ref[:, sl] += dsv
            dsv_b = dsv.astype(BF16)
            dws = lax.dot_general(dsv_b, vn[:, sl], (((1,), (1,)), ((), ())), preferred_element_type=F32)
            dws_ref[g] += jnp.where(mask, dws, 0.0)
            dvn_parts.append(jnp.dot(wmt, dsv_b, preferred_element_type=F32))
            dz_ref[:, sl] = (du * zp_ref[:, sl].astype(F32)).astype(BF16)
        dvn_full = jnp.concatenate(dvn_parts, axis=1)
        dvn_ref[...] += jnp.sum(dvn_full * vh, axis=0, keepdims=True)
        dxh = dvn_full * gain
        dv = r * (dxh - vh * jnp.mean(dxh * vh, axis=-1, keepdims=True))
        dz_ref[:, DG:] = (dv * zp_ref[:, DG:].astype(F32)).astype(BF16)

    full = pl.BlockSpec((C, DG2), lambda i: (i, 0))
    wspec = pl.BlockSpec((G, C, C), lambda i: (0, 0, 0))
    return pl.pallas_call(
        body, name=name, grid=(S // C,),
        in_specs=[full, full, pl.BlockSpec((C, DG), lambda i: (i, 0)), pl.BlockSpec((1, DG), lambda i: (0, 0)),
                  wspec, wspec, pl.BlockSpec((C, DG), lambda i: (0, 0))],
        out_specs=[full, wspec, pl.BlockSpec((C, DG), lambda i: (0, 0)), pl.BlockSpec((1, DG), lambda i: (0, 0))],
        out_shape=[SDS((S, DG2), BF16), SDS((G, C, C), F32), SDS((C, DG), F32), SDS((1, DG), F32)],
        compiler_params=_cp(("arbitrary",), 40 * 2 ** 20),
    )(z, zpre, dgated, v_norm, w_s, w_s_t, bias_full)


def _alibi_slopes(n_heads):
    return [float(v) for v in np.exp2(np.float32(-8.0) * np.arange(1, n_heads + 1, dtype=np.float32) / np.float32(n_heads))]


def _dil_view(arr, dil):
    S, C = arr.shape
    return arr if dil == 1 else arr.reshape(S // dil, dil * C)


def _dil_spec(dil, HD, ncb, cb, bmap):
    return pl.BlockSpec((ATTN_BLOCK, HD), lambda r, b: (bmap(b), r * ncb + cb))


def _group_view(arr, gi, dil, HD):
    if dil == 1:
        return arr, arr.shape[1] // HD, gi
    return _dil_view(arr[:, gi * HD:(gi + 1) * HD], dil), 1, 0


def _dil_unview(arr, S):
    return arr.reshape(S, arr.size // S)


def _attn_mask(b):
    qi = lax.broadcasted_iota(jnp.int32, (ATTN_BLOCK, 2 * ATTN_BLOCK), 0)
    kj = lax.broadcasted_iota(jnp.int32, (ATTN_BLOCK, 2 * ATTN_BLOCK), 1)
    delta = qi + ATTN_BLOCK - kj
    valid = (delta >= 0) & (delta <= ATTN_BLOCK) & ((kj >= ATTN_BLOCK) | (b > 0))
    return valid, delta.astype(F32)


def _pack_heads(cols, rows):
    lane = lax.broadcasted_iota(jnp.int32, (rows, LANE), 1)
    tile = jnp.zeros((rows, LANE), F32)
    for h, col in enumerate(cols):
        tile = jnp.where(lane == h, col, tile)
    return tile


def _attn_fwd(name, qn, kn, vb, gi, dil):
    S, C = qn.shape
    NG = len(DILATIONS)
    HD = C // NG
    H = HD // HEAD_DIM
    L = S // dil
    nb = L // ATTN_BLOCK
    slopes = _alibi_slopes(H)

    def body(q_ref, kc_ref, kp_ref, vc_ref, vp_ref, o_ref, lse_ref):
        b = pl.program_id(1)
        valid, delta = _attn_mask(b)
        dist = delta * float(dil)
        lse_cols = []
        for h in range(H):
            sl = slice(h * HEAD_DIM, (h + 1) * HEAD_DIM)
            k = jnp.concatenate([kp_ref[:, sl], kc_ref[:, sl]], axis=0)
            v = jnp.concatenate([vp_ref[:, sl], vc_ref[:, sl]], axis=0)
            s = lax.dot_general(q_ref[:, sl], k, (((1,), (1,)), ((), ())), preferred_element_type=F32)
            s = jnp.where(valid, s - slopes[h] * dist, -jnp.inf)
            m = jnp.max(s, axis=-1, keepdims=True)
            p = jnp.exp(s - m)
            l = jnp.sum(p, axis=-1, keepdims=True)
            o = jnp.dot(p.astype(BF16), v, preferred_element_type=F32)
            o_ref[:, sl] = o / l
            lse_cols.append(m + jnp.log(l))
        lse_ref[...] = _pack_heads(lse_cols, ATTN_BLOCK)

    cur = lambda b: b
    prev = lambda b: jnp.maximum(b - 1, 0)
    (qv, ncb, cb), (kv, _, _), (vv, _, _) = (_group_view(a, gi, dil, HD) for a in (qn, kn, vb))
    view_shape = (L, dil * HD)
    o, lse = pl.pallas_call(
        body, name=name, grid=(dil, nb),
        in_specs=[_dil_spec(dil, HD, ncb, cb, cur), _dil_spec(dil, HD, ncb, cb, cur), _dil_spec(dil, HD, ncb, cb, prev),
                  _dil_spec(dil, HD, ncb, cb, cur), _dil_spec(dil, HD, ncb, cb, prev)],
        out_specs=[_dil_spec(dil, HD, 1, 0, cur), _dil_spec(dil, LANE, 1, 0, cur)],
        out_shape=[SDS(view_shape, F32), SDS((L, dil * LANE), F32)],
        compiler_params=_cp(("parallel", "parallel"), 32 * 2 ** 20),
    )(qv, kv, kv, vv, vv)
    return _dil_unview(o, S), _dil_unview(lse, S)


def _attn_combine(name, o_list, lse_list, d_o=None):
    S, HD = o_list[0].shape
    H = HD // HEAD_DIM
    ng = len(o_list)
    tr = _row_tile(S, HD * 4, 2 ** 19)

    def body(*refs):
        o_refs = refs[:ng]
        l_refs = refs[ng:2 * ng]
        w_scr = refs[-1]
        ls = [r[...] for r in l_refs]
        m = ls[0]
        for t in ls[1:]:
            m = jnp.maximum(m, t)
        es = [jnp.exp(t - m) for t in ls]
        z = es[0]
        for t in es[1:]:
            z = z + t
        for g in range(ng):
            w_scr[g] = es[g] / z
        if d_o is not None:
            do_ref, lse_ref, dl_ref, dob_ref = refs[2 * ng:2 * ng + 4]
            lse_ref[...] = m + jnp.log(z)
            dob_ref[...] = do_ref[...].astype(BF16)
        dl_cols = []
        for h in range(H):
            sl = slice(h * HEAD_DIM, (h + 1) * HEAD_DIM)
            o = w_scr[0, :, h:h + 1] * o_refs[0][:, sl]
            for g in range(1, ng):
                o = o + w_scr[g, :, h:h + 1] * o_refs[g][:, sl]
            if d_o is None:
                refs[2 * ng][:, sl] = o.astype(BF16)
            else:
                dl_cols.append(jnp.sum(do_ref[:, sl] * o, axis=-1, keepdims=True))
        if d_o is not None:
            dl_ref[...] = _pack_heads(dl_cols, tr)

    row = pl.BlockSpec((tr, HD), lambda i: (i, 0))
    col = pl.BlockSpec((tr, LANE), lambda i: (i, 0))
    scratch = [pltpu.VMEM((ng, tr, LANE), F32)]
    if d_o is None:
        return pl.pallas_call(
            body, name=name, grid=(S // tr,), in_specs=[row] * ng + [col] * ng, out_specs=row,
            out_shape=SDS((S, HD), BF16), scratch_shapes=scratch, compiler_params=_cp(("parallel",), 40 * 2 ** 20),
        )(*o_list, *lse_list)
    return pl.pallas_call(
        body, name=name, grid=(S // tr,), in_specs=[row] * ng + [col] * ng + [row], out_specs=[col, col, row],
        out_shape=[SDS((S, LANE), F32), SDS((S, LANE), F32), SDS((S, HD), BF16)], scratch_shapes=scratch,
        compiler_params=_cp(("parallel",), 48 * 2 ** 20),
    )(*o_list, *lse_list, d_o)


def _attn_bwd(name, qn, kn, vb, dob, lse, delta_rows, gi, dil):
    S, C = qn.shape
    NG = len(DILATIONS)
    HD = C // NG
    H = HD // HEAD_DIM
    L = S // dil
    nb = L // ATTN_BLOCK
    slopes = _alibi_slopes(H)
    B = ATTN_BLOCK

    def body(q_ref, kc_ref, kp_ref, vc_ref, vp_ref, do_ref, lse_ref, dl_ref, dq_ref, dk_ref, dv_ref, ck_ref, cv_ref):
        b = pl.program_id(1)

        @pl.when(b == 0)
        def _():
            ck_ref[...] = jnp.zeros_like(ck_ref)
            cv_ref[...] = jnp.zeros_like(cv_ref)

        @pl.when(b < nb)
        def _():
            valid, delta = _attn_mask(b)
            dist = delta * float(dil)
            for h in range(H):
                sl = slice(h * HEAD_DIM, (h + 1) * HEAD_DIM)
                q = q_ref[:, sl]
                k = jnp.concatenate([kp_ref[:, sl], kc_ref[:, sl]], axis=0)
                v = jnp.concatenate([vp_ref[:, sl], vc_ref[:, sl]], axis=0)
                do = do_ref[:, sl]
                s = lax.dot_general(q, k, (((1,), (1,)), ((), ())), preferred_element_type=F32)
                s = jnp.where(valid, s - slopes[h] * dist, -jnp.inf)
                p = jnp.exp(s - lse_ref[:, h:h + 1])
                dp = lax.dot_general(do, v, (((1,), (1,)), ((), ())), preferred_element_type=F32)
                ds = (p * (dp - dl_ref[:, h:h + 1])).astype(BF16)
                dq_ref[:, sl] = jnp.dot(ds, k, preferred_element_type=F32).astype(BF16)
                dk2 = lax.dot_general(ds, q, (((0,), (0,)), ((), ())), preferred_element_type=F32)
                dv2 = lax.dot_general(p.astype(BF16), do, (((0,), (0,)), ((), ())), preferred_element_type=F32)
                dk_ref[:, sl] = (ck_ref[:, sl] + dk2[:B]).astype(BF16)
                dv_ref[:, sl] = (cv_ref[:, sl] + dv2[:B]).astype(BF16)
                ck_ref[:, sl] = dk2[B:]
                cv_ref[:, sl] = dv2[B:]

        @pl.when(b == nb)
        def _():
            dk_ref[...] = ck_ref[...].astype(BF16)
            dv_ref[...] = cv_ref[...].astype(BF16)

    cur = lambda b: jnp.minimum(b, nb - 1)
    prev = lambda b: jnp.maximum(jnp.minimum(b, nb - 1) - 1, 0)
    late = lambda b: jnp.maximum(b - 1, 0)
    (qv, ncb, cb), (kv, _, _), (vv, _, _) = (_group_view(a, gi, dil, HD) for a in (qn, kn, vb))
    dov, lsev, dlv = _dil_view(dob, dil), _dil_view(lse, dil), _dil_view(delta_rows, dil)
    view_shape = (L, dil * HD)
    one = lambda m: _dil_spec(dil, HD, 1, 0, m)
    grp = lambda m: _dil_spec(dil, HD, ncb, cb, m)
    heads = _dil_spec(dil, LANE, 1, 0, cur)
    dq, dk, dv = pl.pallas_call(
        body, name=name, grid=(dil, nb + 1),
        in_specs=[grp(cur), grp(cur), grp(prev), grp(cur), grp(prev), one(cur), heads, heads],
        out_specs=[one(cur), one(late), one(late)],
        out_shape=[SDS(view_shape, BF16)] * 3,
        scratch_shapes=[pltpu.VMEM((B, HD), F32), pltpu.VMEM((B, HD), F32)],
        compiler_params=_cp(("arbitrary", "arbitrary"), 40 * 2 ** 20),
    )(qv, kv, kv, vv, vv, dov, lsev, dlv)
    return _dil_unview(dq, S), _dil_unview(dk, S), _dil_unview(dv, S)


def _cast_into_gathered(name, w, l, sa, chip_arr):
    L, r, c = w.shape
    tr = _row_tile(r, c * 4)
    nrb = r // tr
    src = pl.BlockSpec((None, tr, c), lambda i, chip: (l, i, 0))
    if sa == 0:
        dst, shape = pl.BlockSpec((tr, c), lambda i, chip: (chip[0] * nrb + i, 0)), (N_CHIPS * r, c)
    else:
        dst, shape = pl.BlockSpec((tr, c), lambda i, chip: (i, chip[0])), (r, N_CHIPS * c)

    def body(chip_ref, i_ref, o_ref):
        o_ref[...] = i_ref[...].astype(BF16)

    return pl.pallas_call(
        body, name=name,
        grid_spec=pltpu.PrefetchScalarGridSpec(num_scalar_prefetch=1, grid=(nrb,), in_specs=[src], out_specs=dst),
        out_shape=SDS(shape, BF16), compiler_params=_cp(("parallel",), 32 * 2 ** 20),
    )(chip_arr, w)


def _add_half(name, dw, land, sa, c_arr):
    hr, hc = land.shape
    tr = _row_tile(hr, hc * 2)
    nrb = hr // tr
    if sa == 1:
        mine = pl.BlockSpec((tr, hc), lambda i, c: (c[0] * nrb + i, 0))
    else:
        mine = pl.BlockSpec((tr, hc), lambda i, c: (i, c[0]))
    other = pl.BlockSpec((tr, hc), lambda i, c: (i, 0))

    def body(c_ref, a_ref, b_ref, o_ref):
        o_ref[...] = (a_ref[...].astype(F32) + b_ref[...].astype(F32)).astype(BF16)

    return pl.pallas_call(
        body, name=name,
        grid_spec=pltpu.PrefetchScalarGridSpec(num_scalar_prefetch=1, grid=(nrb,), in_specs=[mine, other], out_specs=other),
        out_shape=SDS((hr, hc), BF16), compiler_params=_cp(("parallel",), 32 * 2 ** 20),
    )(c_arr, dw, land)


def _sum_slots(name, slots, out_dtype=F32):
    n, R, C = slots.shape
    tr = _row_tile(R, C * n * jnp.dtype(slots.dtype).itemsize, 2 ** 21)

    def body(s_ref, o_ref):
        acc = s_ref[0].astype(F32)
        for k in range(1, n):
            acc = acc + s_ref[k].astype(F32)
        o_ref[...] = acc.astype(out_dtype)

    return pl.pallas_call(
        body, name=name, grid=(R // tr,),
        in_specs=[pl.BlockSpec((n, tr, C), lambda i: (0, i, 0))], out_specs=pl.BlockSpec((tr, C), lambda i: (i, 0)),
        out_shape=SDS((R, C), out_dtype), compiler_params=_cp(("parallel",), 32 * 2 ** 20),
    )(slots)


def _sum_into(name, part, slots, buf, l, sa, where):
    n, pr, pc = slots.shape
    tr = _row_tile(pr, pc * (n + 1) * jnp.dtype(slots.dtype).itemsize, 2 ** 21)
    nrb = pr // tr
    if sa == 1:
        dst = pl.BlockSpec((None, tr, pc), lambda i, c, j: (l, c[0] * nrb + i, 0))
        own = pl.BlockSpec((tr, pc), lambda i, c, j: (i, j[0]))
    else:
        dst = pl.BlockSpec((None, tr, pc), lambda i, c, j: (l, i, c[0]))
        own = pl.BlockSpec((tr, pc), lambda i, c, j: (j[0] * nrb + i, 0))

    def body(c_ref, j_ref, p_ref, s_ref, b_ref, o_ref):
        acc = p_ref[...].astype(F32)
        for k in range(n):
            acc = acc + s_ref[k].astype(F32)
        o_ref[...] = acc

    return pl.pallas_call(
        body, name=name,
        grid_spec=pltpu.PrefetchScalarGridSpec(
            num_scalar_prefetch=2, grid=(nrb,),
            in_specs=[own, pl.BlockSpec((n, tr, pc), lambda i, c, j: (0, i, 0)), ANY], out_specs=dst),
        out_shape=SDS(buf.shape, buf.dtype), input_output_aliases={4: 0},
        compiler_params=_cp(("parallel",), 32 * 2 ** 20),
    )(where[0], where[1], part, slots, buf)


def _adamw_body(g_ref, w_ref, m_ref, v_ref, go_ref, d_ref, mo_ref, vo_ref):
    bc1 = 1.0 - ADAM_B1 ** ADAM_STEP
    bc2 = 1.0 - ADAM_B2 ** ADAM_STEP
    gv = g_ref[...]
    mn = ADAM_B1 * m_ref[...] + (1.0 - ADAM_B1) * gv
    vn = ADAM_B2 * v_ref[...] + (1.0 - ADAM_B2) * (gv * gv)
    go_ref[...] = gv
    mo_ref[...] = mn
    vo_ref[...] = vn
    d_ref[...] = -ADAM_LR * ((mn / bc1) / (jnp.sqrt(vn / bc2) + ADAM_EPS) + ADAM_WD * w_ref[...])


def _adamw_layer(name, g, w, m, v, l, outs):
    L, r, c = g.shape
    tr = _row_tile(r, c * 4, 2 ** 20)
    spec = pl.BlockSpec((None, tr, c), lambda i: (l, i, 0))

    def body(g_ref, w_ref, m_ref, v_ref, a0, a1, a2, a3, go_ref, d_ref, mo_ref, vo_ref):
        _adamw_body(g_ref, w_ref, m_ref, v_ref, go_ref, d_ref, mo_ref, vo_ref)

    return list(pl.pallas_call(
        body, name=name, grid=(r // tr,), in_specs=[spec] * 4 + [ANY] * 4, out_specs=[spec] * 4,
        out_shape=[SDS((L, r, c), F32)] * 4, input_output_aliases={4 + k: k for k in range(4)},
        compiler_params=_cp(("parallel",), 32 * 2 ** 20),
    )(g, w, m, v, *outs))


def _adamw(name, g, w, m, v):
    R, C = g.shape
    tr = _row_tile(R, C * 4, 2 ** 19)

    def body(g_ref, w_ref, m_ref, v_ref, go_ref, d_ref, mo_ref, vo_ref):
        _adamw_body(g_ref, w_ref, m_ref, v_ref, go_ref, d_ref, mo_ref, vo_ref)

    spec = pl.BlockSpec((tr, C), lambda i: (i, 0))
    return pl.pallas_call(
        body, name=name, grid=(R // tr,), in_specs=[spec] * 4, out_specs=[spec] * 4,
        out_shape=[SDS((R, C), F32)] * 4, compiler_params=_cp(("parallel",), 32 * 2 ** 20),
    )(g, w, m, v)


ANY = pl.BlockSpec(memory_space=pl.ANY)


def _coords():
    return lax.axis_index("x"), lax.axis_index("y"), lax.axis_index("c")


def _other_chips(x, y):
    return [((1 - x, y), 2 * (1 - x) + y), ((x, 1 - y), 2 * x + (1 - y)), ((1 - x, 1 - y), 2 * (1 - x) + (1 - y))]


def _win(ref, axis, start, size):
    if not isinstance(start, int):
        start = pl.multiple_of(start, LANE if axis == 1 else BF16_ROWS)
    if axis == 0:
        return ref.at[pl.ds(start, size), :]
    return ref.at[:, pl.ds(start, size)]


def _rcopy(src, dst, ssem, rsem, dev):
    return pltpu.make_async_remote_copy(src_ref=src, dst_ref=dst, send_sem=ssem, recv_sem=rsem,
                                        device_id=dev, device_id_type=MESH)


HBM = pl.BlockSpec(memory_space=pltpu.HBM)
SEM = pl.BlockSpec(memory_space=pltpu.SEMAPHORE)
TOKEN = pl.BlockSpec(memory_space=pltpu.VMEM)
EFFECT = pltpu.SideEffectType.DATAFLOW_SIDE_EFFECTING


def _in_hbm(a):
    return pltpu.with_memory_space_constraint(a, pltpu.HBM)


def _ag_geometry(mats, sas, o):
    sa = sas[o]
    return sa, 1 - sa, mats[o].shape[sa] // N_CHIPS, mats[o].shape[1 - sa] // 2


def _ag_ici_copy(mats, sas, refs, o, k, sems, x, y, c):
    sa, ha, wl, hl = _ag_geometry(mats, sas, o)
    chip, jk = _other_chips(x, y)[k]
    mine = _win(_win(refs[o], sa, (2 * x + y) * wl, wl), ha, c * hl, hl)
    landed = _win(_win(refs[o], sa, jk * wl, wl), ha, c * hl, hl)
    return mine, landed, (*chip, c)


def _ag_start(name, mats, sas, after):
    n = len(mats)

    def body(*refs):
        ins = refs[:n]
        s_sem, r_sem = refs[n + 1], refs[n + 2]
        token = refs[2 * n + 3]
        x, y, c = _coords()
        for o in range(n):
            for k in range(3):
                mine, _, dev = _ag_ici_copy(mats, sas, ins, o, k, None, x, y, c)
                _rcopy(mine, mine, s_sem.at[3 * o + k], r_sem.at[3 * o + k], dev).start()
        token[...] = jnp.zeros_like(token)

    out = pl.pallas_call(
        body, name=name,
        out_shape=(pltpu.SemaphoreType.DMA((3 * n,)), pltpu.SemaphoreType.DMA((3 * n,)),
                   *[pltpu.HBM(m.shape, m.dtype) for m in mats], SDS((8, LANE), F32)),
        in_specs=[HBM] * n + [ANY], out_specs=(SEM, SEM, *[HBM] * n, TOKEN),
        input_output_aliases={k: 2 + k for k in range(n)},
        compiler_params=pltpu.CompilerParams(has_side_effects=EFFECT),
    )(*[_in_hbm(m) for m in mats], after)
    return out[0], out[1], list(out[2:2 + n]), out[2 + n]


def _ag_wait(name, mats, sas, s_sem, r_sem, after):
    n = len(mats)

    def body(*refs):
        ins = refs[:n]
        s_ref, r_ref = refs[n], refs[n + 1]
        x, y, c = _coords()
        for o in range(n):
            for k in range(3):
                mine, landed, dev = _ag_ici_copy(mats, sas, ins, o, k, None, x, y, c)
                cp = _rcopy(mine, landed, s_ref.at[3 * o + k], r_ref.at[3 * o + k], dev)
                cp.wait_send()
                cp.wait_recv()

    after = list(after) if isinstance(after, (list, tuple)) else [after]
    return list(pl.pallas_call(
        body, name=name, out_shape=[pltpu.HBM(m.shape, m.dtype) for m in mats],
        in_specs=[HBM] * n + [SEM, SEM] + [ANY] * len(after), out_specs=[HBM] * n,
        input_output_aliases={k: k for k in range(n)},
        compiler_params=pltpu.CompilerParams(has_side_effects=EFFECT),
    )(*mats, s_sem, r_sem, *after))


def _ag_forward(name, mats, sas):
    n = len(mats)

    def body(*refs):
        outs = refs[n:2 * n]
        s_fwd, r_fwd = refs[2 * n:]
        x, y, c = _coords()
        sibling = (x, y, 1 - c)
        sends = []
        for o in range(n):
            sa, ha, wl, hl = _ag_geometry(mats, sas, o)
            for k, (chip, jk) in enumerate(_other_chips(x, y)):
                landed = _win(_win(outs[o], sa, jk * wl, wl), ha, c * hl, hl)
                fwd = _rcopy(landed, landed, s_fwd.at[3 * o + k], r_fwd.at[3 * o + k], sibling)
                fwd.start()
                sends.append(fwd)
        for o in range(n):
            sa, ha, wl, hl = _ag_geometry(mats, sas, o)
            for k, (chip, jk) in enumerate(_other_chips(x, y)):
                got = _win(_win(outs[o], sa, jk * wl, wl), ha, (1 - c) * hl, hl)
                _rcopy(got, got, s_fwd.at[3 * o + k], r_fwd.at[3 * o + k], sibling).wait_recv()
        for cp in sends:
            cp.wait_send()

    return list(pl.pallas_call(
        body, name=name, in_specs=[ANY] * n, out_specs=[ANY] * n,
        out_shape=[SDS(m.shape, m.dtype) for m in mats], input_output_aliases={k: k for k in range(n)},
        scratch_shapes=[pltpu.SemaphoreType.DMA((3 * n,)), pltpu.SemaphoreType.DMA((3 * n,))],
        compiler_params=pltpu.CompilerParams(has_side_effects=True),
    )(*mats))


def _all_gather_vec(name, v):
    Lv, cv = v.shape

    def body(v_ref, o_ref, loc_sem, s_sem, r_sem):
        x, y, c = _coords()
        jme = 2 * x + y
        chips = _other_chips(x, y)
        mine = _win(o_ref, 1, jme * cv, cv)
        loc = pltpu.make_async_copy(v_ref, mine, loc_sem)
        loc.start()
        sends = []
        for k, (chip, _) in enumerate(chips):
            cp = _rcopy(v_ref, mine, s_sem.at[k], r_sem.at[k], (*chip, c))
            cp.start()
            sends.append(cp)
        for k, (chip, jk) in enumerate(chips):
            got = _win(o_ref, 1, jk * cv, cv)
            _rcopy(got, got, s_sem.at[k], r_sem.at[k], (*chip, c)).wait_recv()
        for cp in sends:
            cp.wait_send()
        loc.wait()

    return pl.pallas_call(
        body, name=name, in_specs=[ANY], out_specs=ANY, out_shape=SDS((Lv, 4 * cv), v.dtype),
        scratch_shapes=[pltpu.SemaphoreType.DMA, pltpu.SemaphoreType.DMA((3,)), pltpu.SemaphoreType.DMA((3,))],
        compiler_params=pltpu.CompilerParams(has_side_effects=True),
    )(v)


def _half_shape(shape, sa):
    R, C = shape
    return (R // 2, C) if sa == 1 else (R, C // 2)


def _rs_sibling_copy(dws, sas, d_refs, land_refs, o, x, y, c):
    ha = 1 - sas[o]
    hl = dws[o].shape[ha] // 2
    return _win(d_refs[o], ha, (1 - c) * hl, hl), land_refs[o], (x, y, 1 - c)


def _rs_sibling_start(name, dws, sas, after):
    n = len(dws)

    def body(*refs):
        ins = refs[:n]
        s_sem, r_sem = refs[n + 1], refs[n + 2]
        lands = refs[2 * n + 3:3 * n + 3]
        token = refs[3 * n + 3]
        x, y, c = _coords()
        for o in range(n):
            src, dst, dev = _rs_sibling_copy(dws, sas, ins, lands, o, x, y, c)
            _rcopy(src, dst, s_sem.at[o], r_sem.at[o], dev).start()
        token[...] = jnp.zeros_like(token)

    out = pl.pallas_call(
        body, name=name,
        out_shape=(pltpu.SemaphoreType.DMA((n,)), pltpu.SemaphoreType.DMA((n,)),
                   *[pltpu.HBM(d.shape, d.dtype) for d in dws],
                   *[pltpu.HBM(_half_shape(d.shape, sa), d.dtype) for d, sa in zip(dws, sas)],
                   SDS((8, LANE), F32)),
        in_specs=[HBM] * n + [ANY], out_specs=(SEM, SEM, *[HBM] * (2 * n), TOKEN),
        input_output_aliases={k: 2 + k for k in range(n)},
        compiler_params=pltpu.CompilerParams(has_side_effects=EFFECT),
    )(*[_in_hbm(d) for d in dws], after)
    return out[0], out[1], list(out[2:2 + n]), list(out[2 + n:2 + 2 * n]), out[2 + 2 * n]


def _rs_sibling_wait(name, dws, lands, sas, s_sem, r_sem, after):
    n = len(dws)

    def body(*refs):
        d_refs, land_refs = refs[:n], refs[n:2 * n]
        s_ref, r_ref = refs[2 * n], refs[2 * n + 1]
        x, y, c = _coords()
        for o in range(n):
            src, dst, dev = _rs_sibling_copy(dws, sas, d_refs, land_refs, o, x, y, c)
            cp = _rcopy(src, dst, s_ref.at[o], r_ref.at[o], dev)
            cp.wait_send()
            cp.wait_recv()

    after = list(after) if isinstance(after, (list, tuple)) else [after]
    out = pl.pallas_call(
        body, name=name,
        out_shape=[pltpu.HBM(a.shape, a.dtype) for a in (*dws, *lands)],
        in_specs=[HBM] * (2 * n) + [SEM, SEM] + [ANY] * len(after), out_specs=[HBM] * (2 * n),
        input_output_aliases={k: k for k in range(2 * n)},
        compiler_params=pltpu.CompilerParams(has_side_effects=EFFECT),
    )(*dws, *lands, s_sem, r_sem, *after)
    return list(out[:n]), list(out[n:])


def _rs_piece_shape(p, sa):
    hr, hc = p.shape
    return (hr // N_CHIPS, hc) if sa == 0 else (hr, hc // N_CHIPS)


def _rs_ici_copy(parts, sas, p_refs, slot_refs, o, k, x, y, c):
    sa = sas[o]
    pl_ = parts[o].shape[sa] // N_CHIPS
    chip, jk = _other_chips(x, y)[k]
    return _win(p_refs[o], sa, jk * pl_, pl_), slot_refs[o].at[k], (*chip, c)


def _rs_start(name, parts, sas, after):
    n = len(parts)

    def body(*refs):
        ins = refs[:n]
        s_sem, r_sem = refs[n + 1], refs[n + 2]
        slots = refs[2 * n + 3:3 * n + 3]
        token = refs[3 * n + 3]
        x, y, c = _coords()
        for o in range(n):
            for k in range(3):
                src, dst, dev = _rs_ici_copy(parts, sas, ins, slots, o, k, x, y, c)
                _rcopy(src, dst, s_sem.at[3 * o + k], r_sem.at[3 * o + k], dev).start()
        token[...] = jnp.zeros_like(token)

    out = pl.pallas_call(
        body, name=name,
        out_shape=(pltpu.SemaphoreType.DMA((3 * n,)), pltpu.SemaphoreType.DMA((3 * n,)),
                   *[pltpu.HBM(p.shape, p.dtype) for p in parts],
                   *[pltpu.HBM((3,) + _rs_piece_shape(p, sa), p.dtype) for p, sa in zip(parts, sas)],
                   SDS((8, LANE), F32)),
        in_specs=[HBM] * n + [ANY], out_specs=(SEM, SEM, *[HBM] * (2 * n), TOKEN),
        input_output_aliases={k: 2 + k for k in range(n)},
        compiler_params=pltpu.CompilerParams(has_side_effects=EFFECT),
    )(*[_in_hbm(p) for p in parts], after)
    return out[0], out[1], list(out[2:2 + n]), list(out[2 + n:2 + 2 * n]), out[2 + 2 * n]


def _rs_wait(name, parts, slots, sas, s_sem, r_sem, after):
    n = len(parts)

    def body(*refs):
        p_refs, slot_refs = refs[:n], refs[n:2 * n]
        s_ref, r_ref = refs[2 * n], refs[2 * n + 1]
        x, y, c = _coords()
        for o in range(n):
            for k in range(3):
                src, dst, dev = _rs_ici_copy(parts, sas, p_refs, slot_refs, o, k, x, y, c)
                cp = _rcopy(src, dst, s_ref.at[3 * o + k], r_ref.at[3 * o + k], dev)
                cp.wait_send()
                cp.wait_recv()

    after = list(after) if isinstance(after, (list, tuple)) else [after]
    out = pl.pallas_call(
        body, name=name,
        out_shape=[pltpu.HBM(a.shape, a.dtype) for a in (*parts, *slots)],
        in_specs=[HBM] * (2 * n) + [SEM, SEM] + [ANY] * len(after), out_specs=[HBM] * (2 * n),
        input_output_aliases={k: k for k in range(2 * n)},
        compiler_params=pltpu.CompilerParams(has_side_effects=EFFECT),
    )(*parts, *slots, s_sem, r_sem, *after)
    return list(out[:n]), list(out[n:])


def _rs_share_copy(sas, layers, buf_idx, bufs, refs, o, x, y, c):
    ha = 1 - sas[o]
    hl = bufs[buf_idx[o]].shape[1 + ha] // 2
    layer = refs[buf_idx[o]].at[layers[o]]
    return _win(layer, ha, c * hl, hl), _win(layer, ha, (1 - c) * hl, hl), (x, y, 1 - c)


def _rs_share_start(name, sas, layers, buf_idx, bufs, after):
    n, nbuf = len(sas), len(bufs)

    def body(*refs):
        ins = refs[:nbuf]
        s_sem, r_sem = refs[nbuf + 1], refs[nbuf + 2]
        token = refs[2 * nbuf + 3]
        x, y, c = _coords()
        for o in range(n):
            mine, _, dev = _rs_share_copy(sas, layers, buf_idx, bufs, ins, o, x, y, c)
            _rcopy(mine, mine, s_sem.at[o], r_sem.at[o], dev).start()
        token[...] = jnp.zeros_like(token)

    out = pl.pallas_call(
        body, name=name,
        out_shape=(pltpu.SemaphoreType.DMA((n,)), pltpu.SemaphoreType.DMA((n,)),
                   *[pltpu.HBM(b.shape, b.dtype) for b in bufs], SDS((8, LANE), F32)),
        in_specs=[HBM] * nbuf + [ANY], out_specs=(SEM, SEM, *[HBM] * nbuf, TOKEN),
        input_output_aliases={k: 2 + k for k in range(nbuf)},
        compiler_params=pltpu.CompilerParams(has_side_effects=EFFECT),
    )(*[_in_hbm(b) for b in bufs], after)
    return out[0], out[1], list(out[2:2 + nbuf]), out[2 + nbuf]


def _rs_share_wait(name, sas, layers, buf_idx, bufs, s_sem, r_sem, after):
    n, nbuf = len(sas), len(bufs)

    def body(*refs):
        ins = refs[:nbuf]
        s_ref, r_ref = refs[nbuf], refs[nbuf + 1]
        x, y, c = _coords()
        for o in range(n):
            mine, got, dev = _rs_share_copy(sas, layers, buf_idx, bufs, ins, o, x, y, c)
            cp = _rcopy(mine, got, s_ref.at[o], r_ref.at[o], dev)
            cp.wait_send()
            cp.wait_recv()

    after = list(after) if isinstance(after, (list, tuple)) else [after]
    return list(pl.pallas_call(
        body, name=name, out_shape=[pltpu.HBM(b.shape, b.dtype) for b in bufs],
        in_specs=[HBM] * nbuf + [SEM, SEM] + [ANY] * len(after), out_specs=[HBM] * nbuf,
        input_output_aliases={k: k for k in range(nbuf)},
        compiler_params=pltpu.CompilerParams(has_side_effects=EFFECT),
    )(*bufs, s_sem, r_sem, *after))


def _place_own(name, packed, me_arr):
    R, C = packed.shape
    tr = _row_tile(R, C * 4)

    def body(me_ref, p_ref, o_ref):
        o_ref[...] = p_ref[...]

    return pl.pallas_call(
        body, name=name,
        grid_spec=pltpu.PrefetchScalarGridSpec(
            num_scalar_prefetch=1, grid=(R // tr,),
            in_specs=[pl.BlockSpec((tr, C), lambda i, me: (i, 0))],
            out_specs=pl.BlockSpec((None, tr, C), lambda i, me: (me[0], i, 0))),
        out_shape=SDS((N_DEV, R, C), packed.dtype), compiler_params=_cp(("parallel",), 32 * 2 ** 20),
    )(me_arr, packed)


def _exchange_copy(s_refs, k, x, y, c):
    px = 1 - x if k & 4 else x
    py = 1 - y if k & 2 else y
    pc = 1 - c if k & 1 else c
    return s_refs.at[4 * x + 2 * y + c], s_refs.at[4 * px + 2 * py + pc], (px, py, pc)


def _exchange_start(name, slots, after):
    def body(s_ref, after_ref, s_sem, r_sem, out_ref, token):
        x, y, c = _coords()
        for k in range(1, N_DEV):
            mine, _, dev = _exchange_copy(s_ref, k, x, y, c)
            _rcopy(mine, mine, s_sem.at[k - 1], r_sem.at[k - 1], dev).start()
        token[...] = jnp.zeros_like(token)

    out = pl.pallas_call(
        body, name=name,
        out_shape=(pltpu.SemaphoreType.DMA((N_DEV - 1,)), pltpu.SemaphoreType.DMA((N_DEV - 1,)),
                   pltpu.HBM(slots.shape, slots.dtype), SDS((8, LANE), F32)),
        in_specs=[HBM, ANY], out_specs=(SEM, SEM, HBM, TOKEN), input_output_aliases={0: 2},
        compiler_params=pltpu.CompilerParams(has_side_effects=EFFECT),
    )(_in_hbm(slots), after)
    return out


def _exchange_wait(name, slots, s_sem, r_sem, after):
    def body(s_ref, s_sem_ref, r_sem_ref, *rest):
        x, y, c = _coords()
        for k in range(1, N_DEV):
            mine, theirs, dev = _exchange_copy(s_ref, k, x, y, c)
            cp = _rcopy(mine, theirs, s_sem_ref.at[k - 1], r_sem_ref.at[k - 1], dev)
            cp.wait_send()
            cp.wait_recv()

    after = list(after) if isinstance(after, (list, tuple)) else [after]
    return pl.pallas_call(
        body, name=name, out_shape=pltpu.HBM(slots.shape, slots.dtype),
        in_specs=[HBM, SEM, SEM] + [ANY] * len(after), out_specs=HBM, input_output_aliases={0: 0},
        compiler_params=pltpu.CompilerParams(has_side_effects=EFFECT),
    )(slots, s_sem, r_sem, *after)


def _pack(arrays):
    rows = []
    for a in arrays:
        flat = a.reshape(-1).astype(F32)
        pad = (-flat.size) % PACK_TILE
        rows.append(jnp.pad(flat, (0, pad)).reshape(-1, LANE))
    return jnp.concatenate(rows, axis=0)


def _unpack(packed, shapes):
    out, row = [], 0
    for s in shapes:
        size = int(np.prod(s)) if len(s) else 1
        nrows = -(-size // PACK_TILE) * (PACK_TILE // LANE)
        out.append(packed[row:row + nrows].reshape(-1)[:size].reshape(s))
        row += nrows
    return out


BIG_WEIGHTS = {
    "ffn1_w_gate": 1, "ffn1_w_up": 1, "ffn1_w_down": 0, "ffn2_w_gate": 1, "ffn2_w_up": 1, "ffn2_w_down": 0,
    "gmlp_w_in": 1, "gmlp_w_out": 0, "w_kv": 1, "attn_w_q": 1, "attn_w_o": 0,
}
SMALL_WEIGHTS = ("ffn1_norm", "mix_norm", "ffn2_norm", "gmlp_w_s", "gmlp_b_s", "kv_norm", "k_norm", "attn_q_norm")
WEIGHT_ORDER = ("ffn1_norm", "ffn1_w_gate", "ffn1_w_up", "ffn1_w_down", "mix_norm", "ffn2_norm", "ffn2_w_gate",
                "ffn2_w_up", "ffn2_w_down", "gmlp_w_in", "gmlp_v_norm", "gmlp_w_s", "gmlp_b_s", "gmlp_w_out",
                "kv_norm", "w_kv", "k_norm", "attn_w_q", "attn_q_norm", "attn_w_o")


def _ep_all(accs, ex):
    return list(accs)


def _as3d(w):
    return w if w.ndim == 3 else w.reshape((1,) + w.shape)


def kernel(x, ffn1_norm, ffn1_w_gate, ffn1_w_up, ffn1_w_down, mix_norm, ffn2_norm, ffn2_w_gate, ffn2_w_up, ffn2_w_down, gmlp_w_in, gmlp_v_norm, gmlp_w_s, gmlp_b_s, gmlp_w_out, kv_norm, w_kv, k_norm, attn_w_q, attn_q_norm, attn_w_o, loss_target, m_ffn1_norm, m_ffn1_w_gate, m_ffn1_w_up, m_ffn1_w_down, m_mix_norm, m_ffn2_norm, m_ffn2_w_gate, m_ffn2_w_up, m_ffn2_w_down, m_gmlp_w_in, m_gmlp_v_norm, m_gmlp_w_s, m_gmlp_b_s, m_gmlp_w_out, m_kv_norm, m_w_kv, m_k_norm, m_attn_w_q, m_attn_q_norm, m_attn_w_o, v_ffn1_norm, v_ffn1_w_gate, v_ffn1_w_up, v_ffn1_w_down, v_mix_norm, v_ffn2_norm, v_ffn2_w_gate, v_ffn2_w_up, v_ffn2_w_down, v_gmlp_w_in, v_gmlp_v_norm, v_gmlp_w_s, v_gmlp_b_s, v_gmlp_w_out, v_kv_norm, v_w_kv, v_k_norm, v_attn_w_q, v_attn_q_norm, v_attn_w_o):
    P = dict(locals())
    assert x.shape[0] == 1, "one sample per device"
    S, D = x.shape[1], x.shape[2]
    NL = ffn1_norm.shape[0]
    NG = len(DILATIONS)
    HD = attn_w_o.shape[1] * N_CHIPS
    H = HD // HEAD_DIM
    DG = gmlp_w_out.shape[1] * N_CHIPS
    G = DG // GMLP_GROUP_WIDTH
    assert all((S // d) % ATTN_BLOCK == 0 for d in DILATIONS) and S % GMLP_CHUNK == 0
    xs = x.reshape(S, D)
    tgt = loss_target.reshape(S, D)
    c_arr = lax.axis_index("c").astype(jnp.int32).reshape(1)
    chip = 2 * lax.axis_index("x") + lax.axis_index("y")
    chip_arr = chip.astype(jnp.int32).reshape(1)
    kv_layer = N_A_LAYERS - 1


    def layer_weights(l):
        names = [("ffn1_w_gate", l), ("ffn1_w_up", l), ("ffn1_w_down", l), ("ffn2_w_gate", l), ("ffn2_w_up", l), ("ffn2_w_down", l)]
        if l < N_A_LAYERS:
            names += [("gmlp_w_in", l), ("gmlp_w_out", l)]
        else:
            names += [("attn_w_q", l - N_A_LAYERS), ("attn_w_o", l - N_A_LAYERS)]
        if l == kv_layer:
            names += [("w_kv", 0)]
        return names

    W = {}
    ag_open = {}
    n_first = 3
    ag_units = {}
    for l in range(NL):
        ag_units[f"{l}a"], ag_units[f"{l}b"] = layer_weights(l)[:n_first], layer_weights(l)[n_first:]

    def cast_unit(u):
        return [_cast_into_gathered("cast_shard", _as3d(P[n]), li, BIG_WEIGHTS[n], chip_arr) for n, li in ag_units[u]]

    def ag_begin(u, after, mats):
        sas = [BIG_WEIGHTS[n] for n, _ in ag_units[u]]
        s_sem, r_sem, mats, token = _ag_start(f"ag_start_l{u}", mats, sas, after)
        ag_open[u] = (sas, s_sem, r_sem, mats)
        return token

    def ag_finish(u, after):
        sas, s_sem, r_sem, mats = ag_open.pop(u)
        mats = _ag_wait(f"ag_wait_l{u}", mats, sas, s_sem, r_sem, after)
        W.update(dict(zip(ag_units[u], _ag_forward(f"ag_forward_l{u}", mats, sas))))

    vnorm_full = _all_gather_vec("ag_vnorm", gmlp_v_norm)
    ag_token = vnorm_full
    for u in [f"{l}{h}" for l in range(min(2, NL)) for h in "ab"]:
        ag_token = ag_begin(u, ag_token, cast_unit(u))
    cast_ahead = {f"{l}{h}": cast_unit(f"{l}{h}") for l in range(2, NL) for h in "ab"}
    ag_finish("0a", [ag_token] + [m for u in cast_ahead for m in cast_ahead[u]])

    kgain = jnp.tile(k_norm[:, None, :], (1, H, 1)).reshape(1, NG * HD)
    qgain = [jnp.tile(attn_q_norm[j][:, None, :], (1, H, 1)).reshape(1, NG * HD) for j in range(NL - N_A_LAYERS)]
    q_scale = HEAD_DIM ** -0.5
    one = [(0, 0, 0)]

    def ffn_fwd(xc, gamma, wg, wu, wd, dep=None):
        n = _rms_fwd("ffn_norm", xc, gamma, dep)
        g, u, act = _mm("ffn_up", [n], [wg, wu], [(0, 0, 0), (0, 1, 1)], _ep_swiglu, [BF16] * 3)
        (x2,) = _mm("ffn_down", [act], [wd], one, _ep_residual(0.5), [F32], extras=[xc])
        return x2, (xc, n, g, u, act)

    saved = {}
    xc = xs
    for l in range(NL):
        if l > 0:
            ag_finish(f"{l}a", xc)
        if l + 2 < NL:
            for h in "ab":
                ag_token = ag_begin(f"{l + 2}{h}", ag_token, cast_ahead.pop(f"{l + 2}{h}"))
        xc, saved["f1", l] = ffn_fwd(xc, ffn1_norm[l], W["ffn1_w_gate", l], W["ffn1_w_up", l], W["ffn1_w_down", l], ag_token)
        ag_finish(f"{l}b", xc)
        h = _rms_fwd("mix_norm", xc, mix_norm[l])
        if l < N_A_LAYERS:
            zpre, z = _mm("gmlp_in", [h], [W["gmlp_w_in", l]], one, _ep_gelu, [BF16, F32])
            bias_full = jnp.repeat(gmlp_b_s[l].T, GMLP_GROUP_WIDTH, axis=1)
            gated = _gmlp_gate_fwd("gmlp_gate", z, vnorm_full[l:l + 1], gmlp_w_s[l], bias_full)
            (x2,) = _mm("gmlp_out", [gated], [W["gmlp_w_out", l]], one, _ep_residual(1.0), [F32], extras=[xc])
            saved["mix", l] = (xc, h, zpre, z, gated, bias_full)
        else:
            j = l - N_A_LAYERS
            (q_raw,) = _mm("attn_q", [h], [W["attn_w_q", j]], one, _ep_plain, [F32])
            qn = _head_norm_fwd("q_norm", q_raw, 0, 1, qgain[j], q_scale, False)
            os_, lses = [], []
            for gi, dil in enumerate(DILATIONS):
                o, lse = _attn_fwd(f"attn_fwd_d{dil}", qn, kn, vb, gi, dil)
                os_.append(o)
                lses.append(lse)
            ob = _attn_combine("attn_mix", os_, lses)
            (x2,) = _mm("attn_o", [ob], [W["attn_w_o", j]], one, _ep_residual(1.0), [F32], extras=[xc])
            saved["mix", l] = (xc, h, q_raw, qn, os_, lses, ob)
        xc = x2
        xc, saved["f2", l] = ffn_fwd(xc, ffn2_norm[l], W["ffn2_w_gate", l], W["ffn2_w_up", l], W["ffn2_w_down", l])
        if l == kv_layer:
            kvn = _rms_fwd("kv_norm", xc, kv_norm)
            (kv_raw,) = _mm("kv_proj", [kvn], [W["w_kv", 0]], one, _ep_plain, [F32])
            kn, vb = _head_norm_fwd("k_norm", kv_raw, 0, 2, kgain, 1.0, True)
            saved["kv"] = (xc, kvn, kv_raw)

    dx, dxb, loss_rows = _loss_grad("loss", xc, tgt, 0.5)
    dW = {}
    dsmall = {n: [None] * P[n].shape[0] for n in ("ffn1_norm", "mix_norm", "ffn2_norm")}
    dsmall.update(gmlp_w_s=[None] * N_A_LAYERS, gmlp_b_s=[None] * N_A_LAYERS, gmlp_v_norm=[None] * N_A_LAYERS,
                  attn_q_norm=[None] * (NL - N_A_LAYERS))
    dks = [[] for _ in DILATIONS]
    dvs = [[] for _ in DILATIONS]

    names_big = list(BIG_WEIGHTS)
    gbuf = [lax.empty(_as3d(P[n]).shape, F32) for n in names_big]
    where = (c_arr, chip_arr)
    sib_open = {}
    rs_open = {}

    def sib_begin(l, after):
        names = layer_weights(l)
        sas = [BIG_WEIGHTS[n] for n, _ in names]
        s_sem, r_sem, dws, lands, token = _rs_sibling_start(f"rs_sibling_start_l{l}", [dW[k] for k in names], sas, after)
        sib_open[l] = (names, sas, s_sem, r_sem, dws, lands)
        return token

    def rs_begin(l, after):
        names, sas, s_sem, r_sem, dws, lands = sib_open.pop(l)
        dws, lands = _rs_sibling_wait(f"rs_sibling_wait_l{l}", dws, lands, sas, s_sem, r_sem, after)
        parts = [_add_half("rs_add_half", d, ln, sa, c_arr) for d, ln, sa in zip(dws, lands, sas)]
        s_sem, r_sem, parts, slots, token = _rs_start(f"rs_start_l{l}", parts, sas, dws[0])
        rs_open[l] = (names, sas, s_sem, r_sem, parts, slots)
        return token

    share_open = {}
    rs_landed = {}

    def rs_collect(l, after):
        names, sas, s_sem, r_sem, parts, slots = rs_open.pop(l)
        parts, slots = _rs_wait(f"rs_wait_l{l}", parts, slots, sas, s_sem, r_sem, after)
        rs_landed[l] = (names, sas, parts, slots)

    def rs_reduce(l):
        names, sas, parts, slots = rs_landed.pop(l)
        for p, s, sa, (n, li) in zip(parts, slots, sas, names):
            bi = names_big.index(n)
            gbuf[bi] = _sum_into("rs_sum_chips", p, s, gbuf[bi], li, sa, where)
        layers, bidx = [li for _, li in names], [names_big.index(n) for n, _ in names]
        s_sem, r_sem, bufs, _ = _rs_share_start(f"rs_share_start_l{l}", sas, layers, bidx, gbuf, parts[0])
        gbuf[:] = bufs
        share_open[l] = (sas, layers, bidx, s_sem, r_sem)

    def share_done(l, after):
        if l in share_open:
            sas, layers, bidx, s_sem, r_sem = share_open.pop(l)
            gbuf[:] = _rs_share_wait(f"rs_share_wait_l{l}", sas, layers, bidx, gbuf, s_sem, r_sem, after)

    def ffn_bwd(dx, dxb, sv, gamma, wg, wu, wd, key, l, next_scale, dep=None):
        xin, n, g, u, act = sv
        dg, du = _mm("ffn_dact", [dxb], [wd], one, _ep_swiglu_bwd, [BF16, BF16], tb=True, extras=[g, u], dep=dep)
        (dW[key + "_w_down", l],) = _mm("ffn_dwd", [act], [dxb], one, _ep_plain, [BF16], ta=True)
        dW[key + "_w_gate", l], dW[key + "_w_up", l] = _mm("ffn_dwgu", [n], [dg, du], [(0, 0, 0), (0, 1, 1)], _ep_all, [BF16, BF16], ta=True)
        (dn,) = _mm("ffn_dn", [dg, du], [wg, wu], [(0, 0, 0), (1, 1, 0)], _ep_plain, [F32], tb=True)
        dx, dxb, dsmall[key + "_norm"][l] = _rms_bwd("ffn_norm_bwd", xin, gamma, dn, dx, next_scale)
        return dx, dxb

    dep = None
    for l in reversed(range(NL)):
        if l == kv_layer:
            x_kv, kvn, kv_raw = saved["kv"]
            dkv_raw, dkgain = _head_norm_bwd("k_norm_bwd", kv_raw, 0, 2, kgain, 1.0, dks, dvs)
            (dW["w_kv", 0],) = _mm("kv_dw", [kvn], [dkv_raw], one, _ep_plain, [BF16], ta=True, dep=dep)
            (dkvn,) = _mm("kv_dn", [dkv_raw], [W["w_kv", 0]], one, _ep_plain, [F32], tb=True)
            dx, dxb, dkvnorm = _rms_bwd("kv_norm_bwd", x_kv, kv_norm, dkvn, dx, 0.5)
        dx, dxb = ffn_bwd(dx, dxb, saved["f2", l], ffn2_norm[l], W["ffn2_w_gate", l], W["ffn2_w_up", l], W["ffn2_w_down", l], "ffn2", l, 1.0, dep)
        if l + 1 < NL:
            dep = rs_begin(l + 1, dx)
        if l < N_A_LAYERS:
            xin, h, zpre, z, gated, bias_full = saved["mix", l]
            (dW["gmlp_w_out", l],) = _mm("gmlp_dwout", [gated], [dxb], one, _ep_plain, [BF16], ta=True, dep=dep)
            (dgated,) = _mm("gmlp_dgated", [dxb], [W["gmlp_w_out", l]], one, _ep_plain, [F32], tb=True)
            dzpre, dws, dbacc, dvn = _gmlp_gate_bwd("gmlp_gate_bwd", z, zpre, dgated, vnorm_full[l:l + 1], gmlp_w_s[l],
                                                    jnp.swapaxes(gmlp_w_s[l], 1, 2), bias_full)
            (dW["gmlp_w_in", l],) = _mm("gmlp_dwin", [h], [dzpre], one, _ep_plain, [BF16], ta=True)
            (dh,) = _mm("gmlp_dh", [dzpre], [W["gmlp_w_in", l]], one, _ep_plain, [F32], tb=True)
            dsmall["gmlp_w_s"][l] = dws
            dsmall["gmlp_b_s"][l] = dbacc.reshape(GMLP_CHUNK, G, GMLP_GROUP_WIDTH).sum(-1).T
            dsmall["gmlp_v_norm"][l] = dvn.reshape(DG)
        else:
            j = l - N_A_LAYERS
            xin, h, q_raw, qn, os_, lses, ob = saved["mix", l]
            (dW["attn_w_o", j],) = _mm("attn_dwo", [ob], [dxb], one, _ep_plain, [BF16], ta=True, dep=dep)
            (d_ob,) = _mm("attn_dob", [dxb], [W["attn_w_o", j]], one, _ep_plain, [F32], tb=True)
            lse_t, dl_rows, dob = _attn_combine("attn_mix_bwd", os_, lses, d_o=d_ob)
            dqs = []
            for gi, dil in enumerate(DILATIONS):
                dq, dk, dv = _attn_bwd(f"attn_bwd_d{dil}", qn, kn, vb, dob, lse_t, dl_rows, gi, dil)
                dqs.append([dq])
                dks[gi].append(dk)
                dvs[gi].append(dv)
            dq_raw, dqgain = _head_norm_bwd("q_norm_bwd", q_raw, 0, 1, qgain[j], q_scale, dqs, None)
            (dW["attn_w_q", j],) = _mm("attn_dwq", [h], [dq_raw], one, _ep_plain, [BF16], ta=True)
            (dh,) = _mm("attn_dh", [dq_raw], [W["attn_w_q", j]], one, _ep_plain, [F32], tb=True)
            dsmall["attn_q_norm"][j] = dqgain.reshape(NG, H, HEAD_DIM).sum(1)
        dx, dxb, dsmall["mix_norm"][l] = _rms_bwd("mix_norm_bwd", xin, mix_norm[l], dh, dx, 0.5)
        dx, dxb = ffn_bwd(dx, dxb, saved["f1", l], ffn1_norm[l], W["ffn1_w_gate", l], W["ffn1_w_up", l], W["ffn1_w_down", l], "ffn1", l, 0.5)
        dep = sib_begin(l, dx)
        if l + 2 < NL:
            share_done(l + 3, dep)
            rs_collect(l + 2, dep)
            rs_reduce(l + 2)
    grad_x = dx.reshape(x.shape)

    loss_part = (0.5 / D) * jnp.sum(loss_rows)
    small_grads = [jnp.stack([g.reshape(P[n].shape[1:]) for g in dsmall[n]]) for n in ("ffn1_norm", "mix_norm", "ffn2_norm", "gmlp_w_s", "gmlp_b_s")]
    small_grads += [dkvnorm.reshape(kv_norm.shape), dkgain.reshape(NG, H, HEAD_DIM).sum(1), jnp.stack(dsmall["attn_q_norm"])]
    vn_grad_full = jnp.stack(dsmall["gmlp_v_norm"])
    me_arr = (2 * chip + lax.axis_index("c")).astype(jnp.int32).reshape(1)
    small_slots = _place_own("place_small", _pack(small_grads + [vn_grad_full, loss_part.reshape(1)]), me_arr)

    adam_out = {n: [lax.empty(_as3d(P[n]).shape, F32) for _ in range(4)] for n in names_big}

    def adam_layer(l):
        for n, li in layer_weights(l):
            adam_out[n] = _adamw_layer("adamw", gbuf[names_big.index(n)], _as3d(P[n]), _as3d(P["m_" + n]),
                                       _as3d(P["v_" + n]), li, adam_out[n])

    def updated():
        return [adam_out[n][0] for n in names_big]

    dep = rs_begin(0, dep)
    ex_s, ex_r, small_slots, dep = _exchange_start("exchange_small_start", small_slots, dep)
    for l in range(2, NL):
        share_done(l, dep)
    for l in reversed(range(2, NL)):
        adam_layer(l)
    if NL > 1:
        rs_collect(1, [dep] + updated())
        rs_reduce(1)
    rs_collect(0, [dep] + updated())
    if NL > 1:
        share_done(1, dep)
    rs_reduce(0)
    share_done(0, dep)
    for l in reversed(range(min(2, NL))):
        adam_layer(l)
    big_out = {n: [o.reshape(P[n].shape) for o in adam_out[n]] for n in names_big}

    total = _sum_slots("sum_devices", _exchange_wait("exchange_small_wait", small_slots, ex_s, ex_r, updated()))
    shapes = [P[n].shape for n in SMALL_WEIGHTS] + [vn_grad_full.shape, (1,)]
    red = _unpack(total, shapes)
    loss = red[-1].reshape(())
    cv = gmlp_v_norm.shape[1]
    vn_grad = lax.dynamic_slice_in_dim(red[-2], chip * cv, cv, axis=1)
    names_small = list(SMALL_WEIGHTS) + ["gmlp_v_norm"]
    g_small = red[:len(SMALL_WEIGHTS)] + [vn_grad]
    outs = _adamw("adamw_small", _pack(g_small), _pack([P[n] for n in names_small]),
                  _pack([P["m_" + n] for n in names_small]), _pack([P["v_" + n] for n in names_small]))
    small_shapes = [P[n].shape for n in names_small]
    small_out = {n: [] for n in names_small}
    for o in outs:
        for n, a in zip(names_small, _unpack(o, small_shapes)):
            small_out[n].append(a)

    res = {**big_out, **small_out}
    return (loss, grad_x, *[res[n][0] for n in WEIGHT_ORDER], *[res[n][1] for n in WEIGHT_ORDER],
            *[res[n][2] for n in WEIGHT_ORDER], *[res[n][3] for n in WEIGHT_ORDER])
```

```python
import numpy as np
import jax
import jax.numpy as jnp
from jax import lax
from jax.experimental import pallas as pl
from jax.experimental.pallas import tpu as pltpu

F32 = jnp.float32
BF16 = jnp.bfloat16
SDS = jax.ShapeDtypeStruct

EPS = 1e-6
HEAD_DIM = 128
GMLP_CHUNK = 128
GMLP_GROUP_WIDTH = 128
DILATIONS = (1, 4, 16)
ATTN_BLOCK = 128
N_A_LAYERS = 2
ADAM_LR, ADAM_B1, ADAM_B2, ADAM_EPS, ADAM_WD, ADAM_STEP = 0.001, 0.9, 0.999, 1e-08, 0.01, 10

N_CHIPS = 4
N_DEV = 8
MESH = pl.DeviceIdType.MESH
V7X_VMEM_BYTES = 64 * 2 ** 20
VMEM_CEILING = V7X_VMEM_BYTES - 6 * 2 ** 20
VMEM_BLOCK_BUDGET = 46 * 2 ** 20
LANE = 128
BF16_ROWS = 16
PACK_TILE = 8 * LANE


def _cp(sem=None, vmem=None):
    kw = {}
    if sem is not None:
        kw["dimension_semantics"] = sem
    if vmem is not None:
        kw["vmem_limit_bytes"] = int(min(max(vmem, 16 * 2 ** 20), VMEM_CEILING))
    return pltpu.CompilerParams(**kw)


def _pick(dim, cands):
    for c in cands:
        if c <= dim and dim % c == 0:
            return c
    return dim


def _row_tile(rows, row_bytes, target=2 ** 20):
    t = 1024
    while t > 8 and (t * row_bytes > target or rows % t):
        t //= 2
    return t if rows % t == 0 else rows


def _sigmoid(x):
    return 1.0 / (1.0 + jnp.exp(-x))


_GELU_C = 0.7978845608028654
_GELU_A = 0.044715


def _gelu(x):
    return 0.5 * x * (1.0 + jnp.tanh(_GELU_C * (x + _GELU_A * (x * x * x))))


def _gelu_grad(x):
    t = jnp.tanh(_GELU_C * (x + _GELU_A * (x * x * x)))
    return 0.5 * (1.0 + t) + 0.5 * x * (1.0 - t * t) * (_GELU_C * (1.0 + 3.0 * _GELU_A * x * x))


def _mm_tiles(M, N, K, n_a, n_b, n_acc, io_bytes):
    tks = [K] + [d for d in (4096, 3072, 2816, 2048, 1024, 512, 256, 128) if d < K and K % d == 0]
    tms = [t for t in (1024, 512, 256, 128) if M % t == 0] or [M]
    tns = [t for t in (512, 256, 128) if N % t == 0] or [N]
    best = None
    for tk in tks:
        for tm in tms:
            for tn in tns:
                est = 2 * 2 * (n_a * tm * tk + n_b * tk * tn) + 2 * tm * tn * io_bytes
                est += n_acc * tm * tn * 4 * (2 if tk < K else 1)
                if est <= VMEM_BLOCK_BUDGET:
                    return tm, tn, tk, est
                if best is None or est < best[3]:
                    best = (tm, tn, tk, est)
    return best


def _mm(name, a_list, b_list, terms, epilogue, out_dtypes, *, ta=False, tb=False, extras=(), dep=None):
    n_acc = 1 + max(t[2] for t in terms)
    a0, b0 = a_list[0], b_list[0]
    (K, M) = a0.shape if ta else a0.shape[::-1]
    N = b0.shape[0] if tb else b0.shape[1]
    io_bytes = sum(jnp.dtype(e.dtype).itemsize for e in extras) + sum(jnp.dtype(d).itemsize for d in out_dtypes)
    tm, tn, tk, est = _mm_tiles(M, N, K, len(a_list), len(b_list), n_acc, io_bytes)
    nk = K // tk
    na, nb, ne, no = len(a_list), len(b_list), len(extras), len(out_dtypes)
    deps = [] if dep is None else [dep]
    nd = len(deps)
    n_acc_scr = n_acc if nk > 1 else 0
    dn = (((0 if ta else 1,), (1 if tb else 0,)), ((), ()))

    def body(*refs):
        a_refs = refs[:na]
        b_refs = refs[na:na + nb]
        e_refs = refs[na + nb:na + nb + ne]
        o_refs = refs[na + nb + ne + nd:na + nb + ne + nd + no]
        acc_refs = refs[na + nb + ne + nd + no:na + nb + ne + nd + no + n_acc_scr]
        parts = [None] * n_acc
        for ai, bi, qi in terms:
            d = lax.dot_general(a_refs[ai][...], b_refs[bi][...], dn, preferred_element_type=F32)
            parts[qi] = d if parts[qi] is None else parts[qi] + d

        def finish(accs):
            outs = epilogue(accs, [e[...] for e in e_refs])
            for o_ref, o in zip(o_refs, outs):
                o_ref[...] = o.astype(o_ref.dtype)

        if nk == 1:
            finish(parts)
        else:
            k = pl.program_id(2)

            @pl.when(k == 0)
            def _():
                for q in range(n_acc):
                    acc_refs[q][...] = parts[q]

            @pl.when(k > 0)
            def _():
                for q in range(n_acc):
                    acc_refs[q][...] += parts[q]

            @pl.when(k == nk - 1)
            def _():
                finish([acc_refs[q][...] for q in range(n_acc)])

    a_spec = pl.BlockSpec((tk, tm), lambda i, j, k: (k, i)) if ta else pl.BlockSpec((tm, tk), lambda i, j, k: (i, k))
    b_spec = pl.BlockSpec((tn, tk), lambda i, j, k: (j, k)) if tb else pl.BlockSpec((tk, tn), lambda i, j, k: (k, j))
    e_spec = pl.BlockSpec((tm, tn), lambda i, j, k: (i, j))
    outs = pl.pallas_call(
        body, name=name, grid=(M // tm, N // tn, nk),
        in_specs=[a_spec] * na + [b_spec] * nb + [e_spec] * ne + [pl.BlockSpec((8, LANE), lambda i, j, k: (0, 0))] * nd,
        out_specs=[e_spec] * no,
        out_shape=[SDS((M, N), d) for d in out_dtypes],
        scratch_shapes=[pltpu.VMEM((tm, tn), F32) for _ in range(n_acc_scr)],
        compiler_params=_cp(("parallel", "parallel", "arbitrary"), est + 12 * 2 ** 20),
    )(*a_list, *b_list, *extras, *deps)
    return outs


def _ep_plain(accs, ex):
    return [accs[0]]


def _ep_swiglu(accs, ex):
    g, u = accs
    s = _sigmoid(g)
    sg = g * s
    return [u * (s + sg * (1.0 - s)), sg, sg * u]


def _ep_swiglu_bwd(accs, ex):
    da = accs[0]
    return [da * ex[0].astype(F32), da * ex[1].astype(F32)]


def _ep_gelu(accs, ex):
    return [_gelu_grad(accs[0]), _gelu(accs[0])]


def _ep_residual(scale):
    def ep(accs, ex):
        return [ex[0] + scale * accs[0]]
    return ep


def _rms_fwd(name, x, gamma, dep=None):
    S, D = x.shape
    tr = _row_tile(S, D * 4)
    deps = [] if dep is None else [dep]

    def body(x_ref, g_ref, *rest):
        o_ref = rest[-1]
        xv = x_ref[...]
        r = lax.rsqrt(jnp.mean(xv * xv, axis=-1, keepdims=True) + EPS)
        o_ref[...] = (xv * r * g_ref[...]).astype(BF16)

    return pl.pallas_call(
        body, name=name, grid=(S // tr,),
        in_specs=[pl.BlockSpec((tr, D), lambda i: (i, 0)), pl.BlockSpec((1, D), lambda i: (0, 0))]
        + [pl.BlockSpec((8, LANE), lambda i: (0, 0))] * len(deps),
        out_specs=pl.BlockSpec((tr, D), lambda i: (i, 0)),
        out_shape=SDS((S, D), BF16),
        compiler_params=_cp(("parallel",), 32 * 2 ** 20),
    )(x, gamma.reshape(1, D), *deps)


def _rms_bwd(name, x, gamma, dn, dx_in, out_scale):
    S, D = x.shape
    tr = _row_tile(S, D * 4, 2 ** 20)

    def body(x_ref, g_ref, dn_ref, dxi_ref, dxo_ref, dxb_ref, dg_ref):
        i = pl.program_id(0)
        xv = x_ref[...]
        r = lax.rsqrt(jnp.mean(xv * xv, axis=-1, keepdims=True) + EPS)
        xh = xv * r
        dnv = dn_ref[...]
        dxh = dnv * g_ref[...]
        dx = dxi_ref[...] + r * (dxh - xh * jnp.mean(dxh * xh, axis=-1, keepdims=True))
        dxo_ref[...] = dx
        dxb_ref[...] = (out_scale * dx).astype(BF16)
        part = jnp.sum(dnv * xh, axis=0, keepdims=True)

        @pl.when(i == 0)
        def _():
            dg_ref[...] = part

        @pl.when(i > 0)
        def _():
            dg_ref[...] += part

    row = pl.BlockSpec((tr, D), lambda i: (i, 0))
    vec = pl.BlockSpec((1, D), lambda i: (0, 0))
    return pl.pallas_call(
        body, name=name, grid=(S // tr,),
        in_specs=[row, vec, row, row], out_specs=[row, row, vec],
        out_shape=[SDS((S, D), F32), SDS((S, D), BF16), SDS((1, D), F32)],
        compiler_params=_cp(("arbitrary",), 40 * 2 ** 20),
    )(x, gamma.reshape(1, D), dn, dx_in)


def _loss_grad(name, y, t, out_scale):
    S, D = y.shape
    tr = _row_tile(S, D * 4, 2 ** 20)
    inv_d = 1.0 / D

    def body(y_ref, t_ref, dy_ref, dyb_ref, ls_ref):
        i = pl.program_id(0)
        e = y_ref[...] - t_ref[...]
        dy = e * inv_d
        dy_ref[...] = dy
        dyb_ref[...] = (out_scale * dy).astype(BF16)
        part = jnp.sum(e * e, axis=0, keepdims=True)

        @pl.when(i == 0)
        def _():
            ls_ref[...] = part

        @pl.when(i > 0)
        def _():
            ls_ref[...] += part

    row = pl.BlockSpec((tr, D), lambda i: (i, 0))
    vec = pl.BlockSpec((1, D), lambda i: (0, 0))
    return pl.pallas_call(
        body, name=name, grid=(S // tr,),
        in_specs=[row, row], out_specs=[row, row, vec],
        out_shape=[SDS((S, D), F32), SDS((S, D), BF16), SDS((1, D), F32)],
        compiler_params=_cp(("arbitrary",), 32 * 2 ** 20),
    )(y, t)


def _head_mean(v):
    return jnp.mean(v, axis=-1, keepdims=True)


def _head_norm_fwd(name, raw, part, n_parts, gain_t, scale, with_pass):
    S = raw.shape[0]
    W = raw.shape[1] // n_parts
    nh = W // HEAD_DIM
    tr = _row_tile(S, W * (4 + 2) * (2 if with_pass else 1), 4 * 2 ** 20)

    def body(*refs):
        if with_pass:
            x_ref, p_ref, g_ref, o_ref, po_ref = refs
            po_ref[...] = p_ref[...].astype(BF16)
        else:
            x_ref, g_ref, o_ref = refs
        for h in range(nh):
            sl = slice(h * HEAD_DIM, (h + 1) * HEAD_DIM)
            xv = x_ref[:, sl]
            r = lax.rsqrt(_head_mean(xv * xv) + EPS)
            o_ref[:, sl] = (xv * r * g_ref[:, sl] * scale).astype(BF16)

    xspec = pl.BlockSpec((tr, W), lambda i: (i, part))
    ospec = pl.BlockSpec((tr, W), lambda i: (i, 0))
    gspec = pl.BlockSpec((1, W), lambda i: (0, 0))
    if with_pass:
        in_specs = [xspec, pl.BlockSpec((tr, W), lambda i: (i, 1)), gspec]
        args = (raw, raw, gain_t)
        out_specs, out_shape = [ospec, ospec], [SDS((S, W), BF16), SDS((S, W), BF16)]
    else:
        in_specs, args = [xspec, gspec], (raw, gain_t)
        out_specs, out_shape = ospec, SDS((S, W), BF16)
    return pl.pallas_call(
        body, name=name, grid=(S // tr,), in_specs=in_specs, out_specs=out_specs, out_shape=out_shape,
        compiler_params=_cp(("parallel",), 40 * 2 ** 20),
    )(*args)


def _head_norm_bwd(name, raw, part, n_parts, gain_t, scale, dy_groups, pass_groups):
    S = raw.shape[0]
    W = raw.shape[1] // n_parts
    ng = len(dy_groups)
    HD = W // ng
    nhg = HD // HEAD_DIM
    n_dy = [len(g) for g in dy_groups]
    n_ps = [len(g) for g in pass_groups] if pass_groups is not None else []
    flat = [a for g in dy_groups for a in g] + ([a for g in pass_groups for a in g] if pass_groups is not None else [])
    out_w = 2 * W if pass_groups is not None else W
    tr = _row_tile(S, W * 4 + sum(HD * jnp.dtype(a.dtype).itemsize for a in flat) + out_w * 2, 8 * 2 ** 20)

    def body(*refs):
        x_ref, g_ref = refs[0], refs[1]
        d_refs = refs[2:2 + len(flat)]
        o_ref, dg_ref = refs[2 + len(flat)], refs[3 + len(flat)]
        i = pl.program_id(0)

        @pl.when(i == 0)
        def _():
            dg_ref[...] = jnp.zeros_like(dg_ref)

        pos = 0
        for gi in range(ng):
            dys = d_refs[pos:pos + n_dy[gi]]
            pos += n_dy[gi]
            for h in range(nhg):
                sl = slice(gi * HD + h * HEAD_DIM, gi * HD + (h + 1) * HEAD_DIM)
                hs = slice(h * HEAD_DIM, (h + 1) * HEAD_DIM)
                dy = dys[0][:, hs].astype(F32)
                for extra in dys[1:]:
                    dy = dy + extra[:, hs].astype(F32)
                xv = x_ref[:, sl]
                r = lax.rsqrt(_head_mean(xv * xv) + EPS)
                xh = xv * r
                dxh = dy * (g_ref[:, sl] * scale)
                o_ref[:, sl] = (r * (dxh - xh * _head_mean(dxh * xh))).astype(BF16)
                dg_ref[:, sl] += jnp.sum(dy * xh, axis=0, keepdims=True) * scale
        for gi in range(len(n_ps)):
            ps = d_refs[pos:pos + n_ps[gi]]
            pos += n_ps[gi]
            acc = ps[0][...].astype(F32)
            for extra in ps[1:]:
                acc = acc + extra[...].astype(F32)
            o_ref[:, W + gi * HD:W + (gi + 1) * HD] = acc.astype(BF16)

    dspec = pl.BlockSpec((tr, HD), lambda i: (i, 0))
    return pl.pallas_call(
        body, name=name, grid=(S // tr,),
        in_specs=[pl.BlockSpec((tr, W), lambda i: (i, part)), pl.BlockSpec((1, W), lambda i: (0, 0))] + [dspec] * len(flat),
        out_specs=[pl.BlockSpec((tr, out_w), lambda i: (i, 0)), pl.BlockSpec((1, W), lambda i: (0, 0))],
        out_shape=[SDS((S, out_w), BF16), SDS((1, W), F32)],
        compiler_params=_cp(("arbitrary",), 48 * 2 ** 20),
    )(raw, gain_t, *flat)


def _tril_mask():
    r = lax.broadcasted_iota(jnp.int32, (GMLP_CHUNK, GMLP_CHUNK), 0)
    c = lax.broadcasted_iota(jnp.int32, (GMLP_CHUNK, GMLP_CHUNK), 1)
    return r >= c


def _gmlp_gate_fwd(name, z, v_norm, w_s, bias_full):
    S, DG2 = z.shape
    DG = DG2 // 2
    G = DG // GMLP_GROUP_WIDTH
    C = GMLP_CHUNK

    def body(u_ref, v_ref, vn_ref, ws_ref, b_ref, o_ref):
        mask = _tril_mask()
        v = v_ref[...]
        r = lax.rsqrt(jnp.mean(v * v, axis=-1, keepdims=True) + EPS)
        vn = (v * r * vn_ref[...]).astype(BF16)
        for g in range(G):
            sl = slice(g * GMLP_GROUP_WIDTH, (g + 1) * GMLP_GROUP_WIDTH)
            wm = jnp.where(mask, ws_ref[g], 0.0).astype(BF16)
            sv = jnp.dot(wm, vn[:, sl], preferred_element_type=F32) + b_ref[:, sl]
            o_ref[:, sl] = (u_ref[:, sl] * sv).astype(BF16)

    return pl.pallas_call(
        body, name=name, grid=(S // C,),
        in_specs=[pl.BlockSpec((C, DG), lambda i: (i, 0)), pl.BlockSpec((C, DG), lambda i: (i, 1)),
                  pl.BlockSpec((1, DG), lambda i: (0, 0)), pl.BlockSpec((G, C, C), lambda i: (0, 0, 0)),
                  pl.BlockSpec((C, DG), lambda i: (0, 0))],
        out_specs=pl.BlockSpec((C, DG), lambda i: (i, 0)),
        out_shape=SDS((S, DG), BF16),
        compiler_params=_cp(("parallel",), 32 * 2 ** 20),
    )(z, z, v_norm, w_s, bias_full)


def _gmlp_gate_bwd(name, z, zpre, dgated, v_norm, w_s, w_s_t, bias_full):
    S, DG2 = z.shape
    DG = DG2 // 2
    G = DG // GMLP_GROUP_WIDTH
    C = GMLP_CHUNK

    def body(z_ref, zp_ref, dg_ref, vn_ref, ws_ref, wst_ref, b_ref, dz_ref, dws_ref, db_ref, dvn_ref):
        i = pl.program_id(0)
        mask = _tril_mask()
        mask_t = jnp.logical_not(mask) | (lax.broadcasted_iota(jnp.int32, (C, C), 0) == lax.broadcasted_iota(jnp.int32, (C, C), 1))
        u = z_ref[:, :DG]
        v = z_ref[:, DG:]
        r = lax.rsqrt(jnp.mean(v * v, axis=-1, keepdims=True) + EPS)
        vh = v * r
        gain = vn_ref[...]
        vn = (vh * gain).astype(BF16)
        dgt = dg_ref[...]

        @pl.when(i == 0)
        def _():
            dws_ref[...] = jnp.zeros_like(dws_ref)
            db_ref[...] = jnp.zeros_like(db_ref)
            dvn_ref[...] = jnp.zeros_like(dvn_ref)

        dvn_parts = []
        for g in range(G):
            sl = slice(g * GMLP_GROUP_WIDTH, (g + 1) * GMLP_GROUP_WIDTH)
            wm = jnp.where(mask, ws_ref[g], 0.0).astype(BF16)
            wmt = jnp.where(mask_t, wst_ref[g], 0.0).astype(BF16)
            sv = jnp.dot(wm, vn[:, sl], preferred_element_type=F32) + b_ref[:, sl]
            dgs = dgt[:, sl]
            du = dgs * sv
            dsv = dgs * u[:, sl]
            db_ref[:, sl] += dsv
            dsv_b = dsv.astype(BF16)
            dws = lax.dot_general(dsv_b, vn[:, sl], (((1,), (1,)), ((), ())), preferred_element_type=F32)
            dws_ref[g] += jnp.where(mask, dws, 0.0)
            dvn_parts.append(jnp.dot(wmt, dsv_b, preferred_element_type=F32))
            dz_ref[:, sl] = (du * zp_ref[:, sl].astype(F32)).astype(BF16)
        dvn_full = jnp.concatenate(dvn_parts, axis=1)
        dvn_ref[...] += jnp.sum(dvn_full * vh, axis=0, keepdims=True)
        dxh = dvn_full * gain
        dv = r * (dxh - vh * jnp.mean(dxh * vh, axis=-1, keepdims=True))
        dz_ref[:, DG:] = (dv * zp_ref[:, DG:].astype(F32)).astype(BF16)

    full = pl.BlockSpec((C, DG2), lambda i: (i, 0))
    wspec = pl.BlockSpec((G, C, C), lambda i: (0, 0, 0))
    return pl.pallas_call(
        body, name=name, grid=(S // C,),
        in_specs=[full, full, pl.BlockSpec((C, DG), lambda i: (i, 0)), pl.BlockSpec((1, DG), lambda i: (0, 0)),
                  wspec, wspec, pl.BlockSpec((C, DG), lambda i: (0, 0))],
        out_specs=[full, wspec, pl.BlockSpec((C, DG), lambda i: (0, 0)), pl.BlockSpec((1, DG), lambda i: (0, 0))],
        out_shape=[SDS((S, DG2), BF16), SDS((G, C, C), F32), SDS((C, DG), F32), SDS((1, DG), F32)],
        compiler_params=_cp(("arbitrary",), 40 * 2 ** 20),
    )(z, zpre, dgated, v_norm, w_s, w_s_t, bias_full)


def _alibi_slopes(n_heads):
    return [float(v) for v in np.exp2(np.float32(-8.0) * np.arange(1, n_heads + 1, dtype=np.float32) / np.float32(n_heads))]


def _dil_view(arr, dil):
    S, C = arr.shape
    return arr if dil == 1 else arr.reshape(S // dil, dil * C)


def _dil_spec(dil, HD, ncb, cb, bmap):
    return pl.BlockSpec((ATTN_BLOCK, HD), lambda r, b: (bmap(b), r * ncb + cb))


def _group_view(arr, gi, dil, HD):
    if dil == 1:
        return arr, arr.shape[1] // HD, gi
    return _dil_view(arr[:, gi * HD:(gi + 1) * HD], dil), 1, 0


def _dil_unview(arr, S):
    return arr.reshape(S, arr.size // S)


def _attn_mask(b):
    qi = lax.broadcasted_iota(jnp.int32, (ATTN_BLOCK, 2 * ATTN_BLOCK), 0)
    kj = lax.broadcasted_iota(jnp.int32, (ATTN_BLOCK, 2 * ATTN_BLOCK), 1)
    delta = qi + ATTN_BLOCK - kj
    valid = (delta >= 0) & (delta <= ATTN_BLOCK) & ((kj >= ATTN_BLOCK) | (b > 0))
    return valid, delta.astype(F32)


def _pack_heads(cols, rows):
    lane = lax.broadcasted_iota(jnp.int32, (rows, LANE), 1)
    tile = jnp.zeros((rows, LANE), F32)
    for h, col in enumerate(cols):
        tile = jnp.where(lane == h, col, tile)
    return tile


def _attn_fwd(name, qn, kn, vb, gi, dil):
    S, C = qn.shape
    NG = len(DILATIONS)
    HD = C // NG
    H = HD // HEAD_DIM
    L = S // dil
    nb = L // ATTN_BLOCK
    slopes = _alibi_slopes(H)

    def body(q_ref, kc_ref, kp_ref, vc_ref, vp_ref, o_ref, lse_ref):
        b = pl.program_id(1)
        valid, delta = _attn_mask(b)
        dist = delta * float(dil)
        lse_cols = []
        for h in range(H):
            sl = slice(h * HEAD_DIM, (h + 1) * HEAD_DIM)
            k = jnp.concatenate([kp_ref[:, sl], kc_ref[:, sl]], axis=0)
            v = jnp.concatenate([vp_ref[:, sl], vc_ref[:, sl]], axis=0)
            s = lax.dot_general(q_ref[:, sl], k, (((1,), (1,)), ((), ())), preferred_element_type=F32)
            s = jnp.where(valid, s - slopes[h] * dist, -jnp.inf)
            m = jnp.max(s, axis=-1, keepdims=True)
            p = jnp.exp(s - m)
            l = jnp.sum(p, axis=-1, keepdims=True)
            o = jnp.dot(p.astype(BF16), v, preferred_element_type=F32)
            o_ref[:, sl] = o / l
            lse_cols.append(m + jnp.log(l))
        lse_ref[...] = _pack_heads(lse_cols, ATTN_BLOCK)

    cur = lambda b: b
    prev = lambda b: jnp.maximum(b - 1, 0)
    (qv, ncb, cb), (kv, _, _), (vv, _, _) = (_group_view(a, gi, dil, HD) for a in (qn, kn, vb))
    view_shape = (L, dil * HD)
    o, lse = pl.pallas_call(
        body, name=name, grid=(dil, nb),
        in_specs=[_dil_spec(dil, HD, ncb, cb, cur), _dil_spec(dil, HD, ncb, cb, cur), _dil_spec(dil, HD, ncb, cb, prev),
                  _dil_spec(dil, HD, ncb, cb, cur), _dil_spec(dil, HD, ncb, cb, prev)],
        out_specs=[_dil_spec(dil, HD, 1, 0, cur), _dil_spec(dil, LANE, 1, 0, cur)],
        out_shape=[SDS(view_shape, F32), SDS((L, dil * LANE), F32)],
        compiler_params=_cp(("parallel", "parallel"), 32 * 2 ** 20),
    )(qv, kv, kv, vv, vv)
    return _dil_unview(o, S), _dil_unview(lse, S)


def _attn_combine(name, o_list, lse_list, d_o=None):
    S, HD = o_list[0].shape
    H = HD // HEAD_DIM
    ng = len(o_list)
    tr = _row_tile(S, HD * 4, 2 ** 19)

    def body(*refs):
        o_refs = refs[:ng]
        l_refs = refs[ng:2 * ng]
        w_scr = refs[-1]
        ls = [r[...] for r in l_refs]
        m = ls[0]
        for t in ls[1:]:
            m = jnp.maximum(m, t)
        es = [jnp.exp(t - m) for t in ls]
        z = es[0]
        for t in es[1:]:
            z = z + t
        for g in range(ng):
            w_scr[g] = es[g] / z
        if d_o is not None:
            do_ref, lse_ref, dl_ref, dob_ref = refs[2 * ng:2 * ng + 4]
            lse_ref[...] = m + jnp.log(z)
            dob_ref[...] = do_ref[...].astype(BF16)
        dl_cols = []
        for h in range(H):
            sl = slice(h * HEAD_DIM, (h + 1) * HEAD_DIM)
            o = w_scr[0, :, h:h + 1] * o_refs[0][:, sl]
            for g in range(1, ng):
                o = o + w_scr[g, :, h:h + 1] * o_refs[g][:, sl]
            if d_o is None:
                refs[2 * ng][:, sl] = o.astype(BF16)
            else:
                dl_cols.append(jnp.sum(do_ref[:, sl] * o, axis=-1, keepdims=True))
        if d_o is not None:
            dl_ref[...] = _pack_heads(dl_cols, tr)

    row = pl.BlockSpec((tr, HD), lambda i: (i, 0))
    col = pl.BlockSpec((tr, LANE), lambda i: (i, 0))
    scratch = [pltpu.VMEM((ng, tr, LANE), F32)]
    if d_o is None:
        return pl.pallas_call(
            body, name=name, grid=(S // tr,), in_specs=[row] * ng + [col] * ng, out_specs=row,
            out_shape=SDS((S, HD), BF16), scratch_shapes=scratch, compiler_params=_cp(("parallel",), 40 * 2 ** 20),
        )(*o_list, *lse_list)
    return pl.pallas_call(
        body, name=name, grid=(S // tr,), in_specs=[row] * ng + [col] * ng + [row], out_specs=[col, col, row],
        out_shape=[SDS((S, LANE), F32), SDS((S, LANE), F32), SDS((S, HD), BF16)], scratch_shapes=scratch,
        compiler_params=_cp(("parallel",), 48 * 2 ** 20),
    )(*o_list, *lse_list, d_o)


def _attn_bwd(name, qn, kn, vb, dob, lse, delta_rows, gi, dil):
    S, C = qn.shape
    NG = len(DILATIONS)
    HD = C // NG
    H = HD // HEAD_DIM
    L = S // dil
    nb = L // ATTN_BLOCK
    slopes = _alibi_slopes(H)
    B = ATTN_BLOCK

    def body(q_ref, kc_ref, kp_ref, vc_ref, vp_ref, do_ref, lse_ref, dl_ref, dq_ref, dk_ref, dv_ref, ck_ref, cv_ref):
        b = pl.program_id(1)

        @pl.when(b == 0)
        def _():
            ck_ref[...] = jnp.zeros_like(ck_ref)
            cv_ref[...] = jnp.zeros_like(cv_ref)

        @pl.when(b < nb)
        def _():
            valid, delta = _attn_mask(b)
            dist = delta * float(dil)
            for h in range(H):
                sl = slice(h * HEAD_DIM, (h + 1) * HEAD_DIM)
                q = q_ref[:, sl]
                k = jnp.concatenate([kp_ref[:, sl], kc_ref[:, sl]], axis=0)
                v = jnp.concatenate([vp_ref[:, sl], vc_ref[:, sl]], axis=0)
                do = do_ref[:, sl]
                s = lax.dot_general(q, k, (((1,), (1,)), ((), ())), preferred_element_type=F32)
                s = jnp.where(valid, s - slopes[h] * dist, -jnp.inf)
                p = jnp.exp(s - lse_ref[:, h:h + 1])
                dp = lax.dot_general(do, v, (((1,), (1,)), ((), ())), preferred_element_type=F32)
                ds = (p * (dp - dl_ref[:, h:h + 1])).astype(BF16)
                dq_ref[:, sl] = jnp.dot(ds, k, preferred_element_type=F32).astype(BF16)
                dk2 = lax.dot_general(ds, q, (((0,), (0,)), ((), ())), preferred_element_type=F32)
                dv2 = lax.dot_general(p.astype(BF16), do, (((0,), (0,)), ((), ())), preferred_element_type=F32)
                dk_ref[:, sl] = (ck_ref[:, sl] + dk2[:B]).astype(BF16)
                dv_ref[:, sl] = (cv_ref[:, sl] + dv2[:B]).astype(BF16)
                ck_ref[:, sl] = dk2[B:]
                cv_ref[:, sl] = dv2[B:]

        @pl.when(b == nb)
        def _():
            dk_ref[...] = ck_ref[...].astype(BF16)
            dv_ref[...] = cv_ref[...].astype(BF16)

    cur = lambda b: jnp.minimum(b, nb - 1)
    prev = lambda b: jnp.maximum(jnp.minimum(b, nb - 1) - 1, 0)
    late = lambda b: jnp.maximum(b - 1, 0)
    (qv, ncb, cb), (kv, _, _), (vv, _, _) = (_group_view(a, gi, dil, HD) for a in (qn, kn, vb))
    dov, lsev, dlv = _dil_view(dob, dil), _dil_view(lse, dil), _dil_view(delta_rows, dil)
    view_shape = (L, dil * HD)
    one = lambda m: _dil_spec(dil, HD, 1, 0, m)
    grp = lambda m: _dil_spec(dil, HD, ncb, cb, m)
    heads = _dil_spec(dil, LANE, 1, 0, cur)
    dq, dk, dv = pl.pallas_call(
        body, name=name, grid=(dil, nb + 1),
        in_specs=[grp(cur), grp(cur), grp(prev), grp(cur), grp(prev), one(cur), heads, heads],
        out_specs=[one(cur), one(late), one(late)],
        out_shape=[SDS(view_shape, BF16)] * 3,
        scratch_shapes=[pltpu.VMEM((B, HD), F32), pltpu.VMEM((B, HD), F32)],
        compiler_params=_cp(("arbitrary", "arbitrary"), 40 * 2 ** 20),
    )(qv, kv, kv, vv, vv, dov, lsev, dlv)
    return _dil_unview(dq, S), _dil_unview(dk, S), _dil_unview(dv, S)


def _cast_into_gathered(name, w, l, sa, chip_arr):
    L, r, c = w.shape
    tr = _row_tile(r, c * 4)
    nrb = r // tr
    src = pl.BlockSpec((None, tr, c), lambda i, chip: (l, i, 0))
    if sa == 0:
        dst, shape = pl.BlockSpec((tr, c), lambda i, chip: (chip[0] * nrb + i, 0)), (N_CHIPS * r, c)
    else:
        dst, shape = pl.BlockSpec((tr, c), lambda i, chip: (i, chip[0])), (r, N_CHIPS * c)

    def body(chip_ref, i_ref, o_ref):
        o_ref[...] = i_ref[...].astype(BF16)

    return pl.pallas_call(
        body, name=name,
        grid_spec=pltpu.PrefetchScalarGridSpec(num_scalar_prefetch=1, grid=(nrb,), in_specs=[src], out_specs=dst),
        out_shape=SDS(shape, BF16), compiler_params=_cp(("parallel",), 32 * 2 ** 20),
    )(chip_arr, w)


def _add_half(name, dw, land, sa, c_arr):
    hr, hc = land.shape
    tr = _row_tile(hr, hc * 2)
    nrb = hr // tr
    if sa == 1:
        mine = pl.BlockSpec((tr, hc), lambda i, c: (c[0] * nrb + i, 0))
    else:
        mine = pl.BlockSpec((tr, hc), lambda i, c: (i, c[0]))
    other = pl.BlockSpec((tr, hc), lambda i, c: (i, 0))

    def body(c_ref, a_ref, b_ref, o_ref):
        o_ref[...] = (a_ref[...].astype(F32) + b_ref[...].astype(F32)).astype(BF16)

    return pl.pallas_call(
        body, name=name,
        grid_spec=pltpu.PrefetchScalarGridSpec(num_scalar_prefetch=1, grid=(nrb,), in_specs=[mine, other], out_specs=other),
        out_shape=SDS((hr, hc), BF16), compiler_params=_cp(("parallel",), 32 * 2 ** 20),
    )(c_arr, dw, land)


def _sum_slots(name, slots, out_dtype=F32):
    n, R, C = slots.shape
    tr = _row_tile(R, C * n * jnp.dtype(slots.dtype).itemsize, 2 ** 21)

    def body(s_ref, o_ref):
        acc = s_ref[0].astype(F32)
        for k in range(1, n):
            acc = acc + s_ref[k].astype(F32)
        o_ref[...] = acc.astype(out_dtype)

    return pl.pallas_call(
        body, name=name, grid=(R // tr,),
        in_specs=[pl.BlockSpec((n, tr, C), lambda i: (0, i, 0))], out_specs=pl.BlockSpec((tr, C), lambda i: (i, 0)),
        out_shape=SDS((R, C), out_dtype), compiler_params=_cp(("parallel",), 32 * 2 ** 20),
    )(slots)


def _sum_into(name, part, slots, buf, l, sa, where):
    n, pr, pc = slots.shape
    tr = _row_tile(pr, pc * (n + 1) * jnp.dtype(slots.dtype).itemsize, 2 ** 21)
    nrb = pr // tr
    if sa == 1:
        dst = pl.BlockSpec((None, tr, pc), lambda i, c, j: (l, c[0] * nrb + i, 0))
        own = pl.BlockSpec((tr, pc), lambda i, c, j: (i, j[0]))
    else:
        dst = pl.BlockSpec((None, tr, pc), lambda i, c, j: (l, i, c[0]))
        own = pl.BlockSpec((tr, pc), lambda i, c, j: (j[0] * nrb + i, 0))

    def body(c_ref, j_ref, p_ref, s_ref, b_ref, o_ref):
        acc = p_ref[...].astype(F32)
        for k in range(n):
            acc = acc + s_ref[k].astype(F32)
        o_ref[...] = acc

    return pl.pallas_call(
        body, name=name,
        grid_spec=pltpu.PrefetchScalarGridSpec(
            num_scalar_prefetch=2, grid=(nrb,),
            in_specs=[own, pl.BlockSpec((n, tr, pc), lambda i, c, j: (0, i, 0)), ANY], out_specs=dst),
        out_shape=SDS(buf.shape, buf.dtype), input_output_aliases={4: 0},
        compiler_params=_cp(("parallel",), 32 * 2 ** 20),
    )(where[0], where[1], part, slots, buf)


def _adamw_body(g_ref, w_ref, m_ref, v_ref, go_ref, d_ref, mo_ref, vo_ref):
    bc1 = 1.0 - ADAM_B1 ** ADAM_STEP
    bc2 = 1.0 - ADAM_B2 ** ADAM_STEP
    gv = g_ref[...]
    mn = ADAM_B1 * m_ref[...] + (1.0 - ADAM_B1) * gv
    vn = ADAM_B2 * v_ref[...] + (1.0 - ADAM_B2) * (gv * gv)
    go_ref[...] = gv
    mo_ref[...] = mn
    vo_ref[...] = vn
    d_ref[...] = -ADAM_LR * ((mn / bc1) / (jnp.sqrt(vn / bc2) + ADAM_EPS) + ADAM_WD * w_ref[...])


def _adamw_layer(name, g, w, m, v, l, outs):
    L, r, c = g.shape
    tr = _row_tile(r, c * 4, 2 ** 20)
    spec = pl.BlockSpec((None, tr, c), lambda i: (l, i, 0))

    def body(g_ref, w_ref, m_ref, v_ref, a0, a1, a2, a3, go_ref, d_ref, mo_ref, vo_ref):
        _adamw_body(g_ref, w_ref, m_ref, v_ref, go_ref, d_ref, mo_ref, vo_ref)

    return list(pl.pallas_call(
        body, name=name, grid=(r // tr,), in_specs=[spec] * 4 + [ANY] * 4, out_specs=[spec] * 4,
        out_shape=[SDS((L, r, c), F32)] * 4, input_output_aliases={4 + k: k for k in range(4)},
        compiler_params=_cp(("parallel",), 32 * 2 ** 20),
    )(g, w, m, v, *outs))


def _adamw(name, g, w, m, v):
    R, C = g.shape
    tr = _row_tile(R, C * 4, 2 ** 19)

    def body(g_ref, w_ref, m_ref, v_ref, go_ref, d_ref, mo_ref, vo_ref):
        _adamw_body(g_ref, w_ref, m_ref, v_ref, go_ref, d_ref, mo_ref, vo_ref)

    spec = pl.BlockSpec((tr, C), lambda i: (i, 0))
    return pl.pallas_call(
        body, name=name, grid=(R // tr,), in_specs=[spec] * 4, out_specs=[spec] * 4,
        out_shape=[SDS((R, C), F32)] * 4, compiler_params=_cp(("parallel",), 32 * 2 ** 20),
    )(g, w, m, v)


ANY = pl.BlockSpec(memory_space=pl.ANY)


def _coords():
    return lax.axis_index("x"), lax.axis_index("y"), lax.axis_index("c")


def _other_chips(x, y):
    return [((1 - x, y), 2 * (1 - x) + y), ((x, 1 - y), 2 * x + (1 - y)), ((1 - x, 1 - y), 2 * (1 - x) + (1 - y))]


def _win(ref, axis, start, size):
    if not isinstance(start, int):
        start = pl.multiple_of(start, LANE if axis == 1 else BF16_ROWS)
    if axis == 0:
        return ref.at[pl.ds(start, size), :]
    return ref.at[:, pl.ds(start, size)]


def _rcopy(src, dst, ssem, rsem, dev):
    return pltpu.make_async_remote_copy(src_ref=src, dst_ref=dst, send_sem=ssem, recv_sem=rsem,
                                        device_id=dev, device_id_type=MESH)


HBM = pl.BlockSpec(memory_space=pltpu.HBM)
SEM = pl.BlockSpec(memory_space=pltpu.SEMAPHORE)
TOKEN = pl.BlockSpec(memory_space=pltpu.VMEM)
EFFECT = pltpu.SideEffectType.DATAFLOW_SIDE_EFFECTING


def _in_hbm(a):
    return pltpu.with_memory_space_constraint(a, pltpu.HBM)


def _ag_geometry(mats, sas, o):
    sa = sas[o]
    return sa, 1 - sa, mats[o].shape[sa] // N_CHIPS, mats[o].shape[1 - sa] // 2


def _ag_ici_copy(mats, sas, refs, o, k, sems, x, y, c):
    sa, ha, wl, hl = _ag_geometry(mats, sas, o)
    chip, jk = _other_chips(x, y)[k]
    mine = _win(_win(refs[o], sa, (2 * x + y) * wl, wl), ha, c * hl, hl)
    landed = _win(_win(refs[o], sa, jk * wl, wl), ha, c * hl, hl)
    return mine, landed, (*chip, c)


def _ag_start(name, mats, sas, after):
    n = len(mats)

    def body(*refs):
        ins = refs[:n]
        s_sem, r_sem = refs[n + 1], refs[n + 2]
        token = refs[2 * n + 3]
        x, y, c = _coords()
        for o in range(n):
            for k in range(3):
                mine, _, dev = _ag_ici_copy(mats, sas, ins, o, k, None, x, y, c)
                _rcopy(mine, mine, s_sem.at[3 * o + k], r_sem.at[3 * o + k], dev).start()
        token[...] = jnp.zeros_like(token)

    out = pl.pallas_call(
        body, name=name,
        out_shape=(pltpu.SemaphoreType.DMA((3 * n,)), pltpu.SemaphoreType.DMA((3 * n,)),
                   *[pltpu.HBM(m.shape, m.dtype) for m in mats], SDS((8, LANE), F32)),
        in_specs=[HBM] * n + [ANY], out_specs=(SEM, SEM, *[HBM] * n, TOKEN),
        input_output_aliases={k: 2 + k for k in range(n)},
        compiler_params=pltpu.CompilerParams(has_side_effects=EFFECT),
    )(*[_in_hbm(m) for m in mats], after)
    return out[0], out[1], list(out[2:2 + n]), out[2 + n]


def _ag_wait(name, mats, sas, s_sem, r_sem, after):
    n = len(mats)

    def body(*refs):
        ins = refs[:n]
        s_ref, r_ref = refs[n], refs[n + 1]
        x, y, c = _coords()
        for o in range(n):
            for k in range(3):
                mine, landed, dev = _ag_ici_copy(mats, sas, ins, o, k, None, x, y, c)
                cp = _rcopy(mine, landed, s_ref.at[3 * o + k], r_ref.at[3 * o + k], dev)
                cp.wait_send()
                cp.wait_recv()

    after = list(after) if isinstance(after, (list, tuple)) else [after]
    return list(pl.pallas_call(
        body, name=name, out_shape=[pltpu.HBM(m.shape, m.dtype) for m in mats],
        in_specs=[HBM] * n + [SEM, SEM] + [ANY] * len(after), out_specs=[HBM] * n,
        input_output_aliases={k: k for k in range(n)},
        compiler_params=pltpu.CompilerParams(has_side_effects=EFFECT),
    )(*mats, s_sem, r_sem, *after))


def _ag_forward(name, mats, sas):
    n = len(mats)

    def body(*refs):
        outs = refs[n:2 * n]
        s_fwd, r_fwd = refs[2 * n:]
        x, y, c = _coords()
        sibling = (x, y, 1 - c)
        sends = []
        for o in range(n):
            sa, ha, wl, hl = _ag_geometry(mats, sas, o)
            for k, (chip, jk) in enumerate(_other_chips(x, y)):
                landed = _win(_win(outs[o], sa, jk * wl, wl), ha, c * hl, hl)
                fwd = _rcopy(landed, landed, s_fwd.at[3 * o + k], r_fwd.at[3 * o + k], sibling)
                fwd.start()
                sends.append(fwd)
        for o in range(n):
            sa, ha, wl, hl = _ag_geometry(mats, sas, o)
            for k, (chip, jk) in enumerate(_other_chips(x, y)):
                got = _win(_win(outs[o], sa, jk * wl, wl), ha, (1 - c) * hl, hl)
                _rcopy(got, got, s_fwd.at[3 * o + k], r_fwd.at[3 * o + k], sibling).wait_recv()
        for cp in sends:
            cp.wait_send()

    return list(pl.pallas_call(
        body, name=name, in_specs=[ANY] * n, out_specs=[ANY] * n,
        out_shape=[SDS(m.shape, m.dtype) for m in mats], input_output_aliases={k: k for k in range(n)},
        scratch_shapes=[pltpu.SemaphoreType.DMA((3 * n,)), pltpu.SemaphoreType.DMA((3 * n,))],
        compiler_params=pltpu.CompilerParams(has_side_effects=True),
    )(*mats))


def _all_gather_vec(name, v):
    Lv, cv = v.shape

    def body(v_ref, o_ref, loc_sem, s_sem, r_sem):
        x, y, c = _coords()
        jme = 2 * x + y
        chips = _other_chips(x, y)
        mine = _win(o_ref, 1, jme * cv, cv)
        loc = pltpu.make_async_copy(v_ref, mine, loc_sem)
        loc.start()
        sends = []
        for k, (chip, _) in enumerate(chips):
            cp = _rcopy(v_ref, mine, s_sem.at[k], r_sem.at[k], (*chip, c))
            cp.start()
            sends.append(cp)
        for k, (chip, jk) in enumerate(chips):
            got = _win(o_ref, 1, jk * cv, cv)
            _rcopy(got, got, s_sem.at[k], r_sem.at[k], (*chip, c)).wait_recv()
        for cp in sends:
            cp.wait_send()
        loc.wait()

    return pl.pallas_call(
        body, name=name, in_specs=[ANY], out_specs=ANY, out_shape=SDS((Lv, 4 * cv), v.dtype),
        scratch_shapes=[pltpu.SemaphoreType.DMA, pltpu.SemaphoreType.DMA((3,)), pltpu.SemaphoreType.DMA((3,))],
        compiler_params=pltpu.CompilerParams(has_side_effects=True),
    )(v)


def _half_shape(shape, sa):
    R, C = shape
    return (R // 2, C) if sa == 1 else (R, C // 2)


def _rs_sibling_copy(dws, sas, d_refs, land_refs, o, x, y, c):
    ha = 1 - sas[o]
    hl = dws[o].shape[ha] // 2
    return _win(d_refs[o], ha, (1 - c) * hl, hl), land_refs[o], (x, y, 1 - c)


def _rs_sibling_start(name, dws, sas, after):
    n = len(dws)

    def body(*refs):
        ins = refs[:n]
        s_sem, r_sem = refs[n + 1], refs[n + 2]
        lands = refs[2 * n + 3:3 * n + 3]
        token = refs[3 * n + 3]
        x, y, c = _coords()
        for o in range(n):
            src, dst, dev = _rs_sibling_copy(dws, sas, ins, lands, o, x, y, c)
            _rcopy(src, dst, s_sem.at[o], r_sem.at[o], dev).start()
        token[...] = jnp.zeros_like(token)

    out = pl.pallas_call(
        body, name=name,
        out_shape=(pltpu.SemaphoreType.DMA((n,)), pltpu.SemaphoreType.DMA((n,)),
                   *[pltpu.HBM(d.shape, d.dtype) for d in dws],
                   *[pltpu.HBM(_half_shape(d.shape, sa), d.dtype) for d, sa in zip(dws, sas)],
                   SDS((8, LANE), F32)),
        in_specs=[HBM] * n + [ANY], out_specs=(SEM, SEM, *[HBM] * (2 * n), TOKEN),
        input_output_aliases={k: 2 + k for k in range(n)},
        compiler_params=pltpu.CompilerParams(has_side_effects=EFFECT),
    )(*[_in_hbm(d) for d in dws], after)
    return out[0], out[1], list(out[2:2 + n]), list(out[2 + n:2 + 2 * n]), out[2 + 2 * n]


def _rs_sibling_wait(name, dws, lands, sas, s_sem, r_sem, after):
    n = len(dws)

    def body(*refs):
        d_refs, land_refs = refs[:n], refs[n:2 * n]
        s_ref, r_ref = refs[2 * n], refs[2 * n + 1]
        x, y, c = _coords()
        for o in range(n):
            src, dst, dev = _rs_sibling_copy(dws, sas, d_refs, land_refs, o, x, y, c)
            cp = _rcopy(src, dst, s_ref.at[o], r_ref.at[o], dev)
            cp.wait_send()
            cp.wait_recv()

    after = list(after) if isinstance(after, (list, tuple)) else [after]
    out = pl.pallas_call(
        body, name=name,
        out_shape=[pltpu.HBM(a.shape, a.dtype) for a in (*dws, *lands)],
        in_specs=[HBM] * (2 * n) + [SEM, SEM] + [ANY] * len(after), out_specs=[HBM] * (2 * n),
        input_output_aliases={k: k for k in range(2 * n)},
        compiler_params=pltpu.CompilerParams(has_side_effects=EFFECT),
    )(*dws, *lands, s_sem, r_sem, *after)
    return list(out[:n]), list(out[n:])


def _rs_piece_shape(p, sa):
    hr, hc = p.shape
    return (hr // N_CHIPS, hc) if sa == 0 else (hr, hc // N_CHIPS)


def _rs_ici_copy(parts, sas, p_refs, slot_refs, o, k, x, y, c):
    sa = sas[o]
    pl_ = parts[o].shape[sa] // N_CHIPS
    chip, jk = _other_chips(x, y)[k]
    return _win(p_refs[o], sa, jk * pl_, pl_), slot_refs[o].at[k], (*chip, c)


def _rs_start(name, parts, sas, after):
    n = len(parts)

    def body(*refs):
        ins = refs[:n]
        s_sem, r_sem = refs[n + 1], refs[n + 2]
        slots = refs[2 * n + 3:3 * n + 3]
        token = refs[3 * n + 3]
        x, y, c = _coords()
        for o in range(n):
            for k in range(3):
                src, dst, dev = _rs_ici_copy(parts, sas, ins, slots, o, k, x, y, c)
                _rcopy(src, dst, s_sem.at[3 * o + k], r_sem.at[3 * o + k], dev).start()
        token[...] = jnp.zeros_like(token)

    out = pl.pallas_call(
        body, name=name,
        out_shape=(pltpu.SemaphoreType.DMA((3 * n,)), pltpu.SemaphoreType.DMA((3 * n,)),
                   *[pltpu.HBM(p.shape, p.dtype) for p in parts],
                   *[pltpu.HBM((3,) + _rs_piece_shape(p, sa), p.dtype) for p, sa in zip(parts, sas)],
                   SDS((8, LANE), F32)),
        in_specs=[HBM] * n + [ANY], out_specs=(SEM, SEM, *[HBM] * (2 * n), TOKEN),
        input_output_aliases={k: 2 + k for k in range(n)},
        compiler_params=pltpu.CompilerParams(has_side_effects=EFFECT),
    )(*[_in_hbm(p) for p in parts], after)
    return out[0], out[1], list(out[2:2 + n]), list(out[2 + n:2 + 2 * n]), out[2 + 2 * n]


def _rs_wait(name, parts, slots, sas, s_sem, r_sem, after):
    n = len(parts)

    def body(*refs):
        p_refs, slot_refs = refs[:n], refs[n:2 * n]
        s_ref, r_ref = refs[2 * n], refs[2 * n + 1]
        x, y, c = _coords()
        for o in range(n):
            for k in range(3):
                src, dst, dev = _rs_ici_copy(parts, sas, p_refs, slot_refs, o, k, x, y, c)
                cp = _rcopy(src, dst, s_ref.at[3 * o + k], r_ref.at[3 * o + k], dev)
                cp.wait_send()
                cp.wait_recv()

    after = list(after) if isinstance(after, (list, tuple)) else [after]
    out = pl.pallas_call(
        body, name=name,
        out_shape=[pltpu.HBM(a.shape, a.dtype) for a in (*parts, *slots)],
        in_specs=[HBM] * (2 * n) + [SEM, SEM] + [ANY] * len(after), out_specs=[HBM] * (2 * n),
        input_output_aliases={k: k for k in range(2 * n)},
        compiler_params=pltpu.CompilerParams(has_side_effects=EFFECT),
    )(*parts, *slots, s_sem, r_sem, *after)
    return list(out[:n]), list(out[n:])


def _rs_share_copy(sas, layers, buf_idx, bufs, refs, o, x, y, c):
    ha = 1 - sas[o]
    hl = bufs[buf_idx[o]].shape[1 + ha] // 2
    layer = refs[buf_idx[o]].at[layers[o]]
    return _win(layer, ha, c * hl, hl), _win(layer, ha, (1 - c) * hl, hl), (x, y, 1 - c)


def _rs_share_start(name, sas, layers, buf_idx, bufs, after):
    n, nbuf = len(sas), len(bufs)

    def body(*refs):
        ins = refs[:nbuf]
        s_sem, r_sem = refs[nbuf + 1], refs[nbuf + 2]
        token = refs[2 * nbuf + 3]
        x, y, c = _coords()
        for o in range(n):
            mine, _, dev = _rs_share_copy(sas, layers, buf_idx, bufs, ins, o, x, y, c)
            _rcopy(mine, mine, s_sem.at[o], r_sem.at[o], dev).start()
        token[...] = jnp.zeros_like(token)

    out = pl.pallas_call(
        body, name=name,
        out_shape=(pltpu.SemaphoreType.DMA((n,)), pltpu.SemaphoreType.DMA((n,)),
                   *[pltpu.HBM(b.shape, b.dtype) for b in bufs], SDS((8, LANE), F32)),
        in_specs=[HBM] * nbuf + [ANY], out_specs=(SEM, SEM, *[HBM] * nbuf, TOKEN),
        input_output_aliases={k: 2 + k for k in range(nbuf)},
        compiler_params=pltpu.CompilerParams(has_side_effects=EFFECT),
    )(*[_in_hbm(b) for b in bufs], after)
    return out[0], out[1], list(out[2:2 + nbuf]), out[2 + nbuf]


def _rs_share_wait(name, sas, layers, buf_idx, bufs, s_sem, r_sem, after):
    n, nbuf = len(sas), len(bufs)

    def body(*refs):
        ins = refs[:nbuf]
        s_ref, r_ref = refs[nbuf], refs[nbuf + 1]
        x, y, c = _coords()
        for o in range(n):
            mine, got, dev = _rs_share_copy(sas, layers, buf_idx, bufs, ins, o, x, y, c)
            cp = _rcopy(mine, got, s_ref.at[o], r_ref.at[o], dev)
            cp.wait_send()
            cp.wait_recv()

    after = list(after) if isinstance(after, (list, tuple)) else [after]
    return list(pl.pallas_call(
        body, name=name, out_shape=[pltpu.HBM(b.shape, b.dtype) for b in bufs],
        in_specs=[HBM] * nbuf + [SEM, SEM] + [ANY] * len(after), out_specs=[HBM] * nbuf,
        input_output_aliases={k: k for k in range(nbuf)},
        compiler_params=pltpu.CompilerParams(has_side_effects=EFFECT),
    )(*bufs, s_sem, r_sem, *after))


def _place_own(name, packed, me_arr):
    R, C = packed.shape
    tr = _row_tile(R, C * 4)

    def body(me_ref, p_ref, o_ref):
        o_ref[...] = p_ref[...]

    return pl.pallas_call(
        body, name=name,
        grid_spec=pltpu.PrefetchScalarGridSpec(
            num_scalar_prefetch=1, grid=(R // tr,),
            in_specs=[pl.BlockSpec((tr, C), lambda i, me: (i, 0))],
            out_specs=pl.BlockSpec((None, tr, C), lambda i, me: (me[0], i, 0))),
        out_shape=SDS((N_DEV, R, C), packed.dtype), compiler_params=_cp(("parallel",), 32 * 2 ** 20),
    )(me_arr, packed)


def _exchange_copy(s_refs, k, x, y, c):
    px = 1 - x if k & 4 else x
    py = 1 - y if k & 2 else y
    pc = 1 - c if k & 1 else c
    return s_refs.at[4 * x + 2 * y + c], s_refs.at[4 * px + 2 * py + pc], (px, py, pc)


def _exchange_start(name, slots, after):
    def body(s_ref, after_ref, s_sem, r_sem, out_ref, token):
        x, y, c = _coords()
        for k in range(1, N_DEV):
            mine, _, dev = _exchange_copy(s_ref, k, x, y, c)
            _rcopy(mine, mine, s_sem.at[k - 1], r_sem.at[k - 1], dev).start()
        token[...] = jnp.zeros_like(token)

    out = pl.pallas_call(
        body, name=name,
        out_shape=(pltpu.SemaphoreType.DMA((N_DEV - 1,)), pltpu.SemaphoreType.DMA((N_DEV - 1,)),
                   pltpu.HBM(slots.shape, slots.dtype), SDS((8, LANE), F32)),
        in_specs=[HBM, ANY], out_specs=(SEM, SEM, HBM, TOKEN), input_output_aliases={0: 2},
        compiler_params=pltpu.CompilerParams(has_side_effects=EFFECT),
    )(_in_hbm(slots), after)
    return out


def _exchange_wait(name, slots, s_sem, r_sem, after):
    def body(s_ref, s_sem_ref, r_sem_ref, *rest):
        x, y, c = _coords()
        for k in range(1, N_DEV):
            mine, theirs, dev = _exchange_copy(s_ref, k, x, y, c)
            cp = _rcopy(mine, theirs, s_sem_ref.at[k - 1], r_sem_ref.at[k - 1], dev)
            cp.wait_send()
            cp.wait_recv()

    after = list(after) if isinstance(after, (list, tuple)) else [after]
    return pl.pallas_call(
        body, name=name, out_shape=pltpu.HBM(slots.shape, slots.dtype),
        in_specs=[HBM, SEM, SEM] + [ANY] * len(after), out_specs=HBM, input_output_aliases={0: 0},
        compiler_params=pltpu.CompilerParams(has_side_effects=EFFECT),
    )(slots, s_sem, r_sem, *after)


def _pack(arrays):
    rows = []
    for a in arrays:
        flat = a.reshape(-1).astype(F32)
        pad = (-flat.size) % PACK_TILE
        rows.append(jnp.pad(flat, (0, pad)).reshape(-1, LANE))
    return jnp.concatenate(rows, axis=0)


def _unpack(packed, shapes):
    out, row = [], 0
    for s in shapes:
        size = int(np.prod(s)) if len(s) else 1
        nrows = -(-size // PACK_TILE) * (PACK_TILE // LANE)
        out.append(packed[row:row + nrows].reshape(-1)[:size].reshape(s))
        row += nrows
    return out


BIG_WEIGHTS = {
    "ffn1_w_gate": 1, "ffn1_w_up": 1, "ffn1_w_down": 0, "ffn2_w_gate": 1, "ffn2_w_up": 1, "ffn2_w_down": 0,
    "gmlp_w_in": 1, "gmlp_w_out": 0, "w_kv": 1, "attn_w_q": 1, "attn_w_o": 0,
}
SMALL_WEIGHTS = ("ffn1_norm", "mix_norm", "ffn2_norm", "gmlp_w_s", "gmlp_b_s", "kv_norm", "k_norm", "attn_q_norm")
WEIGHT_ORDER = ("ffn1_norm", "ffn1_w_gate", "ffn1_w_up", "ffn1_w_down", "mix_norm", "ffn2_norm", "ffn2_w_gate",
                "ffn2_w_up", "ffn2_w_down", "gmlp_w_in", "gmlp_v_norm", "gmlp_w_s", "gmlp_b_s", "gmlp_w_out",
                "kv_norm", "w_kv", "k_norm", "attn_w_q", "attn_q_norm", "attn_w_o")


def _ep_all(accs, ex):
    return list(accs)


def _as3d(w):
    return w if w.ndim == 3 else w.reshape((1,) + w.shape)


def kernel(x, ffn1_norm, ffn1_w_gate, ffn1_w_up, ffn1_w_down, mix_norm, ffn2_norm, ffn2_w_gate, ffn2_w_up, ffn2_w_down, gmlp_w_in, gmlp_v_norm, gmlp_w_s, gmlp_b_s, gmlp_w_out, kv_norm, w_kv, k_norm, attn_w_q, attn_q_norm, attn_w_o, loss_target, m_ffn1_norm, m_ffn1_w_gate, m_ffn1_w_up, m_ffn1_w_down, m_mix_norm, m_ffn2_norm, m_ffn2_w_gate, m_ffn2_w_up, m_ffn2_w_down, m_gmlp_w_in, m_gmlp_v_norm, m_gmlp_w_s, m_gmlp_b_s, m_gmlp_w_out, m_kv_norm, m_w_kv, m_k_norm, m_attn_w_q, m_attn_q_norm, m_attn_w_o, v_ffn1_norm, v_ffn1_w_gate, v_ffn1_w_up, v_ffn1_w_down, v_mix_norm, v_ffn2_norm, v_ffn2_w_gate, v_ffn2_w_up, v_ffn2_w_down, v_gmlp_w_in, v_gmlp_v_norm, v_gmlp_w_s, v_gmlp_b_s, v_gmlp_w_out, v_kv_norm, v_w_kv, v_k_norm, v_attn_w_q, v_attn_q_norm, v_attn_w_o):
    P = dict(locals())
    assert x.shape[0] == 1, "one sample per device"
    S, D = x.shape[1], x.shape[2]
    NL = ffn1_norm.shape[0]
    NG = len(DILATIONS)
    HD = attn_w_o.shape[1] * N_CHIPS
    H = HD // HEAD_DIM
    DG = gmlp_w_out.shape[1] * N_CHIPS
    G = DG // GMLP_GROUP_WIDTH
    assert all((S // d) % ATTN_BLOCK == 0 for d in DILATIONS) and S % GMLP_CHUNK == 0
    xs = x.reshape(S, D)
    tgt = loss_target.reshape(S, D)
    c_arr = lax.axis_index("c").astype(jnp.int32).reshape(1)
    chip = 2 * lax.axis_index("x") + lax.axis_index("y")
    chip_arr = chip.astype(jnp.int32).reshape(1)
    kv_layer = N_A_LAYERS - 1


    def layer_weights(l):
        names = [("ffn1_w_gate", l), ("ffn1_w_up", l), ("ffn1_w_down", l), ("ffn2_w_gate", l), ("ffn2_w_up", l), ("ffn2_w_down", l)]
        if l < N_A_LAYERS:
            names += [("gmlp_w_in", l), ("gmlp_w_out", l)]
        else:
            names += [("attn_w_q", l - N_A_LAYERS), ("attn_w_o", l - N_A_LAYERS)]
        if l == kv_layer:
            names += [("w_kv", 0)]
        return names

    W = {}
    ag_open = {}
    n_first = 3
    ag_units = {}
    for l in range(NL):
        ag_units[f"{l}a"], ag_units[f"{l}b"] = layer_weights(l)[:n_first], layer_weights(l)[n_first:]

    def cast_unit(u):
        return [_cast_into_gathered("cast_shard", _as3d(P[n]), li, BIG_WEIGHTS[n], chip_arr) for n, li in ag_units[u]]

    def ag_begin(u, after, mats):
        sas = [BIG_WEIGHTS[n] for n, _ in ag_units[u]]
        s_sem, r_sem, mats, token = _ag_start(f"ag_start_l{u}", mats, sas, after)
        ag_open[u] = (sas, s_sem, r_sem, mats)
        return token

    def ag_finish(u, after):
        sas, s_sem, r_sem, mats = ag_open.pop(u)
        mats = _ag_wait(f"ag_wait_l{u}", mats, sas, s_sem, r_sem, after)
        W.update(dict(zip(ag_units[u], _ag_forward(f"ag_forward_l{u}", mats, sas))))

    vnorm_full = _all_gather_vec("ag_vnorm", gmlp_v_norm)
    ag_token = vnorm_full
    for u in [f"{l}{h}" for l in range(min(2, NL)) for h in "ab"]:
        ag_token = ag_begin(u, ag_token, cast_unit(u))
    cast_ahead = {f"{l}{h}": cast_unit(f"{l}{h}") for l in range(2, NL) for h in "ab"}
    ag_finish("0a", [ag_token] + [m for u in cast_ahead for m in cast_ahead[u]])

    kgain = jnp.tile(k_norm[:, None, :], (1, H, 1)).reshape(1, NG * HD)
    qgain = [jnp.tile(attn_q_norm[j][:, None, :], (1, H, 1)).reshape(1, NG * HD) for j in range(NL - N_A_LAYERS)]
    q_scale = HEAD_DIM ** -0.5
    one = [(0, 0, 0)]

    def ffn_fwd(xc, gamma, wg, wu, wd, dep=None):
        n = _rms_fwd("ffn_norm", xc, gamma, dep)
        g, u, act = _mm("ffn_up", [n], [wg, wu], [(0, 0, 0), (0, 1, 1)], _ep_swiglu, [BF16] * 3)
        (x2,) = _mm("ffn_down", [act], [wd], one, _ep_residual(0.5), [F32], extras=[xc])
        return x2, (xc, n, g, u, act)

    saved = {}
    xc = xs
    for l in range(NL):
        if l > 0:
            ag_finish(f"{l}a", xc)
        if l + 2 < NL:
            for h in "ab":
                ag_token = ag_begin(f"{l + 2}{h}", ag_token, cast_ahead.pop(f"{l + 2}{h}"))
        xc, saved["f1", l] = ffn_fwd(xc, ffn1_norm[l], W["ffn1_w_gate", l], W["ffn1_w_up", l], W["ffn1_w_down", l], ag_token)
        ag_finish(f"{l}b", xc)
        h = _rms_fwd("mix_norm", xc, mix_norm[l])
        if l < N_A_LAYERS:
            zpre, z = _mm("gmlp_in", [h], [W["gmlp_w_in", l]], one, _ep_gelu, [BF16, F32])
            bias_full = jnp.repeat(gmlp_b_s[l].T, GMLP_GROUP_WIDTH, axis=1)
            gated = _gmlp_gate_fwd("gmlp_gate", z, vnorm_full[l:l + 1], gmlp_w_s[l], bias_full)
            (x2,) = _mm("gmlp_out", [gated], [W["gmlp_w_out", l]], one, _ep_residual(1.0), [F32], extras=[xc])
            saved["mix", l] = (xc, h, zpre, z, gated, bias_full)
        else:
            j = l - N_A_LAYERS
            (q_raw,) = _mm("attn_q", [h], [W["attn_w_q", j]], one, _ep_plain, [F32])
            qn = _head_norm_fwd("q_norm", q_raw, 0, 1, qgain[j], q_scale, False)
            os_, lses = [], []
            for gi, dil in enumerate(DILATIONS):
                o, lse = _attn_fwd(f"attn_fwd_d{dil}", qn, kn, vb, gi, dil)
                os_.append(o)
                lses.append(lse)
            ob = _attn_combine("attn_mix", os_, lses)
            (x2,) = _mm("attn_o", [ob], [W["attn_w_o", j]], one, _ep_residual(1.0), [F32], extras=[xc])
            saved["mix", l] = (xc, h, q_raw, qn, os_, lses, ob)
        xc = x2
        xc, saved["f2", l] = ffn_fwd(xc, ffn2_norm[l], W["ffn2_w_gate", l], W["ffn2_w_up", l], W["ffn2_w_down", l])
        if l == kv_layer:
            kvn = _rms_fwd("kv_norm", xc, kv_norm)
            (kv_raw,) = _mm("kv_proj", [kvn], [W["w_kv", 0]], one, _ep_plain, [F32])
            kn, vb = _head_norm_fwd("k_norm", kv_raw, 0, 2, kgain, 1.0, True)
            saved["kv"] = (xc, kvn, kv_raw)

    dx, dxb, loss_rows = _loss_grad("loss", xc, tgt, 0.5)
    dW = {}
    dsmall = {n: [None] * P[n].shape[0] for n in ("ffn1_norm", "mix_norm", "ffn2_norm")}
    dsmall.update(gmlp_w_s=[None] * N_A_LAYERS, gmlp_b_s=[None] * N_A_LAYERS, gmlp_v_norm=[None] * N_A_LAYERS,
                  attn_q_norm=[None] * (NL - N_A_LAYERS))
    dks = [[] for _ in DILATIONS]
    dvs = [[] for _ in DILATIONS]

    names_big = list(BIG_WEIGHTS)
    gbuf = [lax.empty(_as3d(P[n]).shape, F32) for n in names_big]
    where = (c_arr, chip_arr)
    sib_open = {}
    rs_open = {}

    def sib_begin(l, after):
        names = layer_weights(l)
        sas = [BIG_WEIGHTS[n] for n, _ in names]
        s_sem, r_sem, dws, lands, token = _rs_sibling_start(f"rs_sibling_start_l{l}", [dW[k] for k in names], sas, after)
        sib_open[l] = (names, sas, s_sem, r_sem, dws, lands)
        return token

    def rs_begin(l, after):
        names, sas, s_sem, r_sem, dws, lands = sib_open.pop(l)
        dws, lands = _rs_sibling_wait(f"rs_sibling_wait_l{l}", dws, lands, sas, s_sem, r_sem, after)
        parts = [_add_half("rs_add_half", d, ln, sa, c_arr) for d, ln, sa in zip(dws, lands, sas)]
        s_sem, r_sem, parts, slots, token = _rs_start(f"rs_start_l{l}", parts, sas, dws[0])
        rs_open[l] = (names, sas, s_sem, r_sem, parts, slots)
        return token

    share_open = {}
    rs_landed = {}

    def rs_collect(l, after):
        names, sas, s_sem, r_sem, parts, slots = rs_open.pop(l)
        parts, slots = _rs_wait(f"rs_wait_l{l}", parts, slots, sas, s_sem, r_sem, after)
        rs_landed[l] = (names, sas, parts, slots)

    def rs_reduce(l):
        names, sas, parts, slots = rs_landed.pop(l)
        for p, s, sa, (n, li) in zip(parts, slots, sas, names):
            bi = names_big.index(n)
            gbuf[bi] = _sum_into("rs_sum_chips", p, s, gbuf[bi], li, sa, where)
        layers, bidx = [li for _, li in names], [names_big.index(n) for n, _ in names]
        s_sem, r_sem, bufs, _ = _rs_share_start(f"rs_share_start_l{l}", sas, layers, bidx, gbuf, parts[0])
        gbuf[:] = bufs
        share_open[l] = (sas, layers, bidx, s_sem, r_sem)

    def share_done(l, after):
        if l in share_open:
            sas, layers, bidx, s_sem, r_sem = share_open.pop(l)
            gbuf[:] = _rs_share_wait(f"rs_share_wait_l{l}", sas, layers, bidx, gbuf, s_sem, r_sem, after)

    def ffn_bwd(dx, dxb, sv, gamma, wg, wu, wd, key, l, next_scale, dep=None):
        xin, n, g, u, act = sv
        dg, du = _mm("ffn_dact", [dxb], [wd], one, _ep_swiglu_bwd, [BF16, BF16], tb=True, extras=[g, u], dep=dep)
        (dW[key + "_w_down", l],) = _mm("ffn_dwd", [act], [dxb], one, _ep_plain, [BF16], ta=True)
        dW[key + "_w_gate", l], dW[key + "_w_up", l] = _mm("ffn_dwgu", [n], [dg, du], [(0, 0, 0), (0, 1, 1)], _ep_all, [BF16, BF16], ta=True)
        (dn,) = _mm("ffn_dn", [dg, du], [wg, wu], [(0, 0, 0), (1, 1, 0)], _ep_plain, [F32], tb=True)
        dx, dxb, dsmall[key + "_norm"][l] = _rms_bwd("ffn_norm_bwd", xin, gamma, dn, dx, next_scale)
        return dx, dxb

    dep = None
    for l in reversed(range(NL)):
        if l == kv_layer:
            x_kv, kvn, kv_raw = saved["kv"]
            dkv_raw, dkgain = _head_norm_bwd("k_norm_bwd", kv_raw, 0, 2, kgain, 1.0, dks, dvs)
            (dW["w_kv", 0],) = _mm("kv_dw", [kvn], [dkv_raw], one, _ep_plain, [BF16], ta=True, dep=dep)
            (dkvn,) = _mm("kv_dn", [dkv_raw], [W["w_kv", 0]], one, _ep_plain, [F32], tb=True)
            dx, dxb, dkvnorm = _rms_bwd("kv_norm_bwd", x_kv, kv_norm, dkvn, dx, 0.5)
        dx, dxb = ffn_bwd(dx, dxb, saved["f2", l], ffn2_norm[l], W["ffn2_w_gate", l], W["ffn2_w_up", l], W["ffn2_w_down", l], "ffn2", l, 1.0, dep)
        if l + 1 < NL:
            dep = rs_begin(l + 1, dx)
        if l < N_A_LAYERS:
            xin, h, zpre, z, gated, bias_full = saved["mix", l]
            (dW["gmlp_w_out", l],) = _mm("gmlp_dwout", [gated], [dxb], one, _ep_plain, [BF16], ta=True, dep=dep)
            (dgated,) = _mm("gmlp_dgated", [dxb], [W["gmlp_w_out", l]], one, _ep_plain, [F32], tb=True)
            dzpre, dws, dbacc, dvn = _gmlp_gate_bwd("gmlp_gate_bwd", z, zpre, dgated, vnorm_full[l:l + 1], gmlp_w_s[l],
                                                    jnp.swapaxes(gmlp_w_s[l], 1, 2), bias_full)
            (dW["gmlp_w_in", l],) = _mm("gmlp_dwin", [h], [dzpre], one, _ep_plain, [BF16], ta=True)
            (dh,) = _mm("gmlp_dh", [dzpre], [W["gmlp_w_in", l]], one, _ep_plain, [F32], tb=True)
            dsmall["gmlp_w_s"][l] = dws
            dsmall["gmlp_b_s"][l] = dbacc.reshape(GMLP_CHUNK, G, GMLP_GROUP_WIDTH).sum(-1).T
            dsmall["gmlp_v_norm"][l] = dvn.reshape(DG)
        else:
            j = l - N_A_LAYERS
            xin, h, q_raw, qn, os_, lses, ob = saved["mix", l]
            (dW["attn_w_o", j],) = _mm("attn_dwo", [ob], [dxb], one, _ep_plain, [BF16], ta=True, dep=dep)
            (d_ob,) = _mm("attn_dob", [dxb], [W["attn_w_o", j]], one, _ep_plain, [F32], tb=True)
            lse_t, dl_rows, dob = _attn_combine("attn_mix_bwd", os_, lses, d_o=d_ob)
            dqs = []
            for gi, dil in enumerate(DILATIONS):
                dq, dk, dv = _attn_bwd(f"attn_bwd_d{dil}", qn, kn, vb, dob, lse_t, dl_rows, gi, dil)
                dqs.append([dq])
                dks[gi].append(dk)
                dvs[gi].append(dv)
            dq_raw, dqgain = _head_norm_bwd("q_norm_bwd", q_raw, 0, 1, qgain[j], q_scale, dqs, None)
            (dW["attn_w_q", j],) = _mm("attn_dwq", [h], [dq_raw], one, _ep_plain, [BF16], ta=True)
            (dh,) = _mm("attn_dh", [dq_raw], [W["attn_w_q", j]], one, _ep_plain, [F32], tb=True)
            dsmall["attn_q_norm"][j] = dqgain.reshape(NG, H, HEAD_DIM).sum(1)
        dx, dxb, dsmall["mix_norm"][l] = _rms_bwd("mix_norm_bwd", xin, mix_norm[l], dh, dx, 0.5)
        dx, dxb = ffn_bwd(dx, dxb, saved["f1", l], ffn1_norm[l], W["ffn1_w_gate", l], W["ffn1_w_up", l], W["ffn1_w_down", l], "ffn1", l, 0.5)
        dep = sib_begin(l, dx)
        if l + 2 < NL:
            share_done(l + 3, dep)
            rs_collect(l + 2, dep)
            rs_reduce(l + 2)
    grad_x = dx.reshape(x.shape)

    loss_part = (0.5 / D) * jnp.sum(loss_rows)
    small_grads = [jnp.stack([g.reshape(P[n].shape[1:]) for g in dsmall[n]]) for n in ("ffn1_norm", "mix_norm", "ffn2_norm", "gmlp_w_s", "gmlp_b_s")]
    small_grads += [dkvnorm.reshape(kv_norm.shape), dkgain.reshape(NG, H, HEAD_DIM).sum(1), jnp.stack(dsmall["attn_q_norm"])]
    vn_grad_full = jnp.stack(dsmall["gmlp_v_norm"])
    me_arr = (2 * chip + lax.axis_index("c")).astype(jnp.int32).reshape(1)
    small_slots = _place_own("place_small", _pack(small_grads + [vn_grad_full, loss_part.reshape(1)]), me_arr)

    adam_out = {n: [lax.empty(_as3d(P[n]).shape, F32) for _ in range(4)] for n in names_big}

    def adam_layer(l):
        for n, li in layer_weights(l):
            adam_out[n] = _adamw_layer("adamw", gbuf[names_big.index(n)], _as3d(P[n]), _as3d(P["m_" + n]),
                                       _as3d(P["v_" + n]), li, adam_out[n])

    def updated():
        return [adam_out[n][0] for n in names_big]

    dep = rs_begin(0, dep)
    ex_s, ex_r, small_slots, dep = _exchange_start("exchange_small_start", small_slots, dep)
    for l in range(2, NL):
        share_done(l, dep)
    for l in reversed(range(2, NL)):
        adam_layer(l)
    if NL > 1:
        rs_collect(1, [dep] + updated())
        rs_reduce(1)
    rs_collect(0, [dep] + updated())
    if NL > 1:
        share_done(1, dep)
    rs_reduce(0)
    share_done(0, dep)
    for l in reversed(range(min(2, NL))):
        adam_layer(l)
    big_out = {n: [o.reshape(P[n].shape) for o in adam_out[n]] for n in names_big}

    total = _sum_slots("sum_devices", _exchange_wait("exchange_small_wait", small_slots, ex_s, ex_r, updated()))
    shapes = [P[n].shape for n in SMALL_WEIGHTS] + [vn_grad_full.shape, (1,)]
    red = _unpack(total, shapes)
    loss = red[-1].reshape(())
    cv = gmlp_v_norm.shape[1]
    vn_grad = lax.dynamic_slice_in_dim(red[-2], chip * cv, cv, axis=1)
    names_small = list(SMALL_WEIGHTS) + ["gmlp_v_norm"]
    g_small = red[:len(SMALL_WEIGHTS)] + [vn_grad]
    outs = _adamw("adamw_small", _pack(g_small), _pack([P[n] for n in names_small]),
                  _pack([P["m_" + n] for n in names_small]), _pack([P["v_" + n] for n in names_small]))
    small_shapes = [P[n].shape for n in names_small]
    small_out = {n: [] for n in names_small}
    for o in outs:
        for n, a in zip(names_small, _unpack(o, small_shapes)):
            small_out[n].append(a)

    res = {**big_out, **small_out}
    return (loss, grad_x, *[res[n][0] for n in WEIGHT_ORDER], *[res[n][1] for n in WEIGHT_ORDER],
            *[res[n][2] for n in WEIGHT_ORDER], *[res[n][3] for n in WEIGHT_ORDER])
```

```python
import numpy as np
import jax
import jax.numpy as jnp
from jax import lax
from jax.experimental import pallas as pl
from jax.experimental.pallas import tpu as pltpu

F32 = jnp.float32
BF16 = jnp.bfloat16
SDS = jax.ShapeDtypeStruct

EPS = 1e-6
HEAD_DIM = 128
GMLP_CHUNK = 128
GMLP_GROUP_WIDTH = 128
DILATIONS = (1, 4, 16)
ATTN_BLOCK = 128
N_A_LAYERS = 2
ADAM_LR, ADAM_B1, ADAM_B2, ADAM_EPS, ADAM_WD, ADAM_STEP = 0.001, 0.9, 0.999, 1e-08, 0.01, 10

N_CHIPS = 4
N_DEV = 8
MESH = pl.DeviceIdType.MESH
V7X_VMEM_BYTES = 64 * 2 ** 20
VMEM_CEILING = V7X_VMEM_BYTES - 6 * 2 ** 20
VMEM_BLOCK_BUDGET = 46 * 2 ** 20
LANE = 128
BF16_ROWS = 16
PACK_TILE = 8 * LANE


def _cp(sem=None, vmem=None):
    kw = {}
    if sem is not None:
        kw["dimension_semantics"] = sem
    if vmem is not None:
        kw["vmem_limit_bytes"] = int(min(max(vmem, 16 * 2 ** 20), VMEM_CEILING))
    return pltpu.CompilerParams(**kw)


def _pick(dim, cands):
    for c in cands:
        if c <= dim and dim % c == 0:
            return c
    return dim


def _row_tile(rows, row_bytes, target=2 ** 20):
    t = 1024
    while t > 8 and (t * row_bytes > target or rows % t):
        t //= 2
    return t if rows % t == 0 else rows


def _sigmoid(x):
    return 1.0 / (1.0 + jnp.exp(-x))


_GELU_C = 0.7978845608028654
_GELU_A = 0.044715


def _gelu(x):
    return 0.5 * x * (1.0 + jnp.tanh(_GELU_C * (x + _GELU_A * (x * x * x))))


def _gelu_grad(x):
    t = jnp.tanh(_GELU_C * (x + _GELU_A * (x * x * x)))
    return 0.5 * (1.0 + t) + 0.5 * x * (1.0 - t * t) * (_GELU_C * (1.0 + 3.0 * _GELU_A * x * x))


def _mm_tiles(M, N, K, n_a, n_b, n_acc, io_bytes):
    tks = [K] + [d for d in (4096, 3072, 2816, 2048, 1024, 512, 256, 128) if d < K and K % d == 0]
    tms = [t for t in (1024, 512, 256, 128) if M % t == 0] or [M]
    tns = [t for t in (512, 256, 128) if N % t == 0] or [N]
    best = None
    for tk in tks:
        for tm in tms:
            for tn in tns:
                est = 2 * 2 * (n_a * tm * tk + n_b * tk * tn) + 2 * tm * tn * io_bytes
                est += n_acc * tm * tn * 4 * (2 if tk < K else 1)
                if est <= VMEM_BLOCK_BUDGET:
                    return tm, tn, tk, est
                if best is None or est < best[3]:
                    best = (tm, tn, tk, est)
    return best


def _mm(name, a_list, b_list, terms, epilogue, out_dtypes, *, ta=False, tb=False, extras=(), dep=None):
    n_acc = 1 + max(t[2] for t in terms)
    a0, b0 = a_list[0], b_list[0]
    (K, M) = a0.shape if ta else a0.shape[::-1]
    N = b0.shape[0] if tb else b0.shape[1]
    io_bytes = sum(jnp.dtype(e.dtype).itemsize for e in extras) + sum(jnp.dtype(d).itemsize for d in out_dtypes)
    tm, tn, tk, est = _mm_tiles(M, N, K, len(a_list), len(b_list), n_acc, io_bytes)
    nk = K // tk
    na, nb, ne, no = len(a_list), len(b_list), len(extras), len(out_dtypes)
    deps = [] if dep is None else [dep]
    nd = len(deps)
    n_acc_scr = n_acc if nk > 1 else 0
    dn = (((0 if ta else 1,), (1 if tb else 0,)), ((), ()))

    def body(*refs):
        a_refs = refs[:na]
        b_refs = refs[na:na + nb]
        e_refs = refs[na + nb:na + nb + ne]
        o_refs = refs[na + nb + ne + nd:na + nb + ne + nd + no]
        acc_refs = refs[na + nb + ne + nd + no:na + nb + ne + nd + no + n_acc_scr]
        parts = [None] * n_acc
        for ai, bi, qi in terms:
            d = lax.dot_general(a_refs[ai][...], b_refs[bi][...], dn, preferred_element_type=F32)
            parts[qi] = d if parts[qi] is None else parts[qi] + d

        def finish(accs):
            outs = epilogue(accs, [e[...] for e in e_refs])
            for o_ref, o in zip(o_refs, outs):
                o_ref[...] = o.astype(o_ref.dtype)

        if nk == 1:
            finish(parts)
        else:
            k = pl.program_id(2)

            @pl.when(k == 0)
            def _():
                for q in range(n_acc):
                    acc_refs[q][...] = parts[q]

            @pl.when(k > 0)
            def _():
                for q in range(n_acc):
                    acc_refs[q][...] += parts[q]

            @pl.when(k == nk - 1)
            def _():
                finish([acc_refs[q][...] for q in range(n_acc)])

    a_spec = pl.BlockSpec((tk, tm), lambda i, j, k: (k, i)) if ta else pl.BlockSpec((tm, tk), lambda i, j, k: (i, k))
    b_spec = pl.BlockSpec((tn, tk), lambda i, j, k: (j, k)) if tb else pl.BlockSpec((tk, tn), lambda i, j, k: (k, j))
    e_spec = pl.BlockSpec((tm, tn), lambda i, j, k: (i, j))
    outs = pl.pallas_call(
        body, name=name, grid=(M // tm, N // tn, nk),
        in_specs=[a_spec] * na + [b_spec] * nb + [e_spec] * ne + [pl.BlockSpec((8, LANE), lambda i, j, k: (0, 0))] * nd,
        out_specs=[e_spec] * no,
        out_shape=[SDS((M, N), d) for d in out_dtypes],
        scratch_shapes=[pltpu.VMEM((tm, tn), F32) for _ in range(n_acc_scr)],
        compiler_params=_cp(("parallel", "parallel", "arbitrary"), est + 12 * 2 ** 20),
    )(*a_list, *b_list, *extras, *deps)
    return outs


def _ep_plain(accs, ex):
    return [accs[0]]


def _ep_swiglu(accs, ex):
    g, u = accs
    s = _sigmoid(g)
    sg = g * s
    return [u * (s + sg * (1.0 - s)), sg, sg * u]


def _ep_swiglu_bwd(accs, ex):
    da = accs[0]
    return [da * ex[0].astype(F32), da * ex[1].astype(F32)]


def _ep_gelu(accs, ex):
    return [_gelu_grad(accs[0]), _gelu(accs[0])]


def _ep_residual(scale):
    def ep(accs, ex):
        return [ex[0] + scale * accs[0]]
    return ep


def _rms_fwd(name, x, gamma, dep=None):
    S, D = x.shape
    tr = _row_tile(S, D * 4)
    deps = [] if dep is None else [dep]

    def body(x_ref, g_ref, *rest):
        o_ref = rest[-1]
        xv = x_ref[...]
        r = lax.rsqrt(jnp.mean(xv * xv, axis=-1, keepdims=True) + EPS)
        o_ref[...] = (xv * r * g_ref[...]).astype(BF16)

    return pl.pallas_call(
        body, name=name, grid=(S // tr,),
        in_specs=[pl.BlockSpec((tr, D), lambda i: (i, 0)), pl.BlockSpec((1, D), lambda i: (0, 0))]
        + [pl.BlockSpec((8, LANE), lambda i: (0, 0))] * len(deps),
        out_specs=pl.BlockSpec((tr, D), lambda i: (i, 0)),
        out_shape=SDS((S, D), BF16),
        compiler_params=_cp(("parallel",), 32 * 2 ** 20),
    )(x, gamma.reshape(1, D), *deps)


def _rms_bwd(name, x, gamma, dn, dx_in, out_scale):
    S, D = x.shape
    tr = _row_tile(S, D * 4, 2 ** 20)

    def body(x_ref, g_ref, dn_ref, dxi_ref, dxo_ref, dxb_ref, dg_ref):
        i = pl.program_id(0)
        xv = x_ref[...]
        r = lax.rsqrt(jnp.mean(xv * xv, axis=-1, keepdims=True) + EPS)
        xh = xv * r
        dnv = dn_ref[...]
        dxh = dnv * g_ref[...]
        dx = dxi_ref[...] + r * (dxh - xh * jnp.mean(dxh * xh, axis=-1, keepdims=True))
        dxo_ref[...] = dx
        dxb_ref[...] = (out_scale * dx).astype(BF16)
        part = jnp.sum(dnv * xh, axis=0, keepdims=True)

        @pl.when(i == 0)
        def _():
            dg_ref[...] = part

        @pl.when(i > 0)
        def _():
            dg_ref[...] += part

    row = pl.BlockSpec((tr, D), lambda i: (i, 0))
    vec = pl.BlockSpec((1, D), lambda i: (0, 0))
    return pl.pallas_call(
        body, name=name, grid=(S // tr,),
        in_specs=[row, vec, row, row], out_specs=[row, row, vec],
        out_shape=[SDS((S, D), F32), SDS((S, D), BF16), SDS((1, D), F32)],
        compiler_params=_cp(("arbitrary",), 40 * 2 ** 20),
    )(x, gamma.reshape(1, D), dn, dx_in)


def _loss_grad(name, y, t, out_scale):
    S, D = y.shape
    tr = _row_tile(S, D * 4, 2 ** 20)
    inv_d = 1.0 / D

    def body(y_ref, t_ref, dy_ref, dyb_ref, ls_ref):
        i = pl.program_id(0)
        e = y_ref[...] - t_ref[...]
        dy = e * inv_d
        dy_ref[...] = dy
        dyb_ref[...] = (out_scale * dy).astype(BF16)
        part = jnp.sum(e * e, axis=0, keepdims=True)

        @pl.when(i == 0)
        def _():
            ls_ref[...] = part

        @pl.when(i > 0)
        def _():
            ls_ref[...] += part

    row = pl.BlockSpec((tr, D), lambda i: (i, 0))
    vec = pl.BlockSpec((1, D), lambda i: (0, 0))
    return pl.pallas_call(
        body, name=name, grid=(S // tr,),
        in_specs=[row, row], out_specs=[row, row, vec],
        out_shape=[SDS((S, D), F32), SDS((S, D), BF16), SDS((1, D), F32)],
        compiler_params=_cp(("arbitrary",), 32 * 2 ** 20),
    )(y, t)


def _head_mean(v):
    return jnp.mean(v, axis=-1, keepdims=True)


def _head_norm_fwd(name, raw, part, n_parts, gain_t, scale, with_pass):
    S = raw.shape[0]
    W = raw.shape[1] // n_parts
    nh = W // HEAD_DIM
    tr = _row_tile(S, W * (4 + 2) * (2 if with_pass else 1), 4 * 2 ** 20)

    def body(*refs):
        if with_pass:
            x_ref, p_ref, g_ref, o_ref, po_ref = refs
            po_ref[...] = p_ref[...].astype(BF16)
        else:
            x_ref, g_ref, o_ref = refs
        for h in range(nh):
            sl = slice(h * HEAD_DIM, (h + 1) * HEAD_DIM)
            xv = x_ref[:, sl]
            r = lax.rsqrt(_head_mean(xv * xv) + EPS)
            o_ref[:, sl] = (xv * r * g_ref[:, sl] * scale).astype(BF16)

    xspec = pl.BlockSpec((tr, W), lambda i: (i, part))
    ospec = pl.BlockSpec((tr, W), lambda i: (i, 0))
    gspec = pl.BlockSpec((1, W), lambda i: (0, 0))
    if with_pass:
        in_specs = [xspec, pl.BlockSpec((tr, W), lambda i: (i, 1)), gspec]
        args = (raw, raw, gain_t)
        out_specs, out_shape = [ospec, ospec], [SDS((S, W), BF16), SDS((S, W), BF16)]
    else:
        in_specs, args = [xspec, gspec], (raw, gain_t)
        out_specs, out_shape = ospec, SDS((S, W), BF16)
    return pl.pallas_call(
        body, name=name, grid=(S // tr,), in_specs=in_specs, out_specs=out_specs, out_shape=out_shape,
        compiler_params=_cp(("parallel",), 40 * 2 ** 20),
    )(*args)


def _head_norm_bwd(name, raw, part, n_parts, gain_t, scale, dy_groups, pass_groups):
    S = raw.shape[0]
    W = raw.shape[1] // n_parts
    ng = len(dy_groups)
    HD = W // ng
    nhg = HD // HEAD_DIM
    n_dy = [len(g) for g in dy_groups]
    n_ps = [len(g) for g in pass_groups] if pass_groups is not None else []
    flat = [a for g in dy_groups for a in g] + ([a for g in pass_groups for a in g] if pass_groups is not None else [])
    out_w = 2 * W if pass_groups is not None else W
    tr = _row_tile(S, W * 4 + sum(HD * jnp.dtype(a.dtype).itemsize for a in flat) + out_w * 2, 8 * 2 ** 20)

    def body(*refs):
        x_ref, g_ref = refs[0], refs[1]
        d_refs = refs[2:2 + len(flat)]
        o_ref, dg_ref = refs[2 + len(flat)], refs[3 + len(flat)]
        i = pl.program_id(0)

        @pl.when(i == 0)
        def _():
            dg_ref[...] = jnp.zeros_like(dg_ref)

        pos = 0
        for gi in range(ng):
            dys = d_refs[pos:pos + n_dy[gi]]
            pos += n_dy[gi]
            for h in range(nhg):
                sl = slice(gi * HD + h * HEAD_DIM, gi * HD + (h + 1) * HEAD_DIM)
                hs = slice(h * HEAD_DIM, (h + 1) * HEAD_DIM)
                dy = dys[0][:, hs].astype(F32)
                for extra in dys[1:]:
                    dy = dy + extra[:, hs].astype(F32)
                xv = x_ref[:, sl]
                r = lax.rsqrt(_head_mean(xv * xv) + EPS)
                xh = xv * r
                dxh = dy * (g_ref[:, sl] * scale)
                o_ref[:, sl] = (r * (dxh - xh * _head_mean(dxh * xh))).astype(BF16)
                dg_ref[:, sl] += jnp.sum(dy * xh, axis=0, keepdims=True) * scale
        for gi in range(len(n_ps)):
            ps = d_refs[pos:pos + n_ps[gi]]
            pos += n_ps[gi]
            acc = ps[0][...].astype(F32)
            for extra in ps[1:]:
                acc = acc + extra[...].astype(F32)
            o_ref[:, W + gi * HD:W + (gi + 1) * HD] = acc.astype(BF16)

    dspec = pl.BlockSpec((tr, HD), lambda i: (i, 0))
    return pl.pallas_call(
        body, name=name, grid=(S // tr,),
        in_specs=[pl.BlockSpec((tr, W), lambda i: (i, part)), pl.BlockSpec((1, W), lambda i: (0, 0))] + [dspec] * len(flat),
        out_specs=[pl.BlockSpec((tr, out_w), lambda i: (i, 0)), pl.BlockSpec((1, W), lambda i: (0, 0))],
        out_shape=[SDS((S, out_w), BF16), SDS((1, W), F32)],
        compiler_params=_cp(("arbitrary",), 48 * 2 ** 20),
    )(raw, gain_t, *flat)


def _tril_mask():
    r = lax.broadcasted_iota(jnp.int32, (GMLP_CHUNK, GMLP_CHUNK), 0)
    c = lax.broadcasted_iota(jnp.int32, (GMLP_CHUNK, GMLP_CHUNK), 1)
    return r >= c


def _gmlp_gate_fwd(name, z, v_norm, w_s, bias_full):
    S, DG2 = z.shape
    DG = DG2 // 2
    G = DG // GMLP_GROUP_WIDTH
    C = GMLP_CHUNK

    def body(u_ref, v_ref, vn_ref, ws_ref, b_ref, o_ref):
        mask = _tril_mask()
        v = v_ref[...]
        r = lax.rsqrt(jnp.mean(v * v, axis=-1, keepdims=True) + EPS)
        vn = (v * r * vn_ref[...]).astype(BF16)
        for g in range(G):
            sl = slice(g * GMLP_GROUP_WIDTH, (g + 1) * GMLP_GROUP_WIDTH)
            wm = jnp.where(mask, ws_ref[g], 0.0).astype(BF16)
            sv = jnp.dot(wm, vn[:, sl], preferred_element_type=F32) + b_ref[:, sl]
            o_ref[:, sl] = (u_ref[:, sl] * sv).astype(BF16)

    return pl.pallas_call(
        body, name=name, grid=(S // C,),
        in_specs=[pl.BlockSpec((C, DG), lambda i: (i, 0)), pl.BlockSpec((C, DG), lambda i: (i, 1)),
                  pl.BlockSpec((1, DG), lambda i: (0, 0)), pl.BlockSpec((G, C, C), lambda i: (0, 0, 0)),
                  pl.BlockSpec((C, DG), lambda i: (0, 0))],
        out_specs=pl.BlockSpec((C, DG), lambda i: (i, 0)),
        out_shape=SDS((S, DG), BF16),
        compiler_params=_cp(("parallel",), 32 * 2 ** 20),
    )(z, z, v_norm, w_s, bias_full)


def _gmlp_gate_bwd(name, z, zpre, dgated, v_norm, w_s, w_s_t, bias_full):
    S, DG2 = z.shape
    DG = DG2 // 2
    G = DG // GMLP_GROUP_WIDTH
    C = GMLP_CHUNK

    def body(z_ref, zp_ref, dg_ref, vn_ref, ws_ref, wst_ref, b_ref, dz_ref, dws_ref, db_ref, dvn_ref):
        i = pl.program_id(0)
        mask = _tril_mask()
        mask_t = jnp.logical_not(mask) | (lax.broadcasted_iota(jnp.int32, (C, C), 0) == lax.broadcasted_iota(jnp.int32, (C, C), 1))
        u = z_ref[:, :DG]
        v = z_ref[:, DG:]
        r = lax.rsqrt(jnp.mean(v * v, axis=-1, keepdims=True) + EPS)
        vh = v * r
        gain = vn_ref[...]
        vn = (vh * gain).astype(BF16)
        dgt = dg_ref[...]

        @pl.when(i == 0)
        def _():
            dws_ref[...] = jnp.zeros_like(dws_ref)
            db_ref[...] = jnp.zeros_like(db_ref)
            dvn_ref[...] = jnp.zeros_like(dvn_ref)

        dvn_parts = []
        for g in range(G):
            sl = slice(g * GMLP_GROUP_WIDTH, (g + 1) * GMLP_GROUP_WIDTH)
            wm = jnp.where(mask, ws_ref[g], 0.0).astype(BF16)
            wmt = jnp.where(mask_t, wst_ref[g], 0.0).astype(BF16)
            sv = jnp.dot(wm, vn[:, sl], preferred_element_type=F32) + b_ref[:, sl]
            dgs = dgt[:, sl]
            du = dgs * sv
            dsv = dgs * u[:, sl]
            db_ref[:, sl] += dsv
            dsv_b = dsv.astype(BF16)
            dws = lax.dot_general(dsv_b, vn[:, sl], (((1,), (1,)), ((), ())), preferred_element_type=F32)
            dws_ref[g] += jnp.where(mask, dws, 0.0)
            dvn_parts.append(jnp.dot(wmt, dsv_b, preferred_element_type=F32))
            dz_ref[:, sl] = (du * zp_ref[:, sl].astype(F32)).astype(BF16)
        dvn_full = jnp.concatenate(dvn_parts, axis=1)
        dvn_ref[...] += jnp.sum(dvn_full * vh, axis=0, keepdims=True)
        dxh = dvn_full * gain
        dv = r * (dxh - vh * jnp.mean(dxh * vh, axis=-1, keepdims=True))
        dz_ref[:, DG:] = (dv * zp_ref[:, DG:].astype(F32)).astype(BF16)

    full = pl.BlockSpec((C, DG2), lambda i: (i, 0))
    wspec = pl.BlockSpec((G, C, C), lambda i: (0, 0, 0))
    return pl.pallas_call(
        body, name=name, grid=(S // C,),
        in_specs=[full, full, pl.BlockSpec((C, DG), lambda i: (i, 0)), pl.BlockSpec((1, DG), lambda i: (0, 0)),
                  wspec, wspec, pl.BlockSpec((C, DG), lambda i: (0, 0))],
        out_specs=[full, wspec, pl.BlockSpec((C, DG), lambda i: (0, 0)), pl.BlockSpec((1, DG), lambda i: (0, 0))],
        out_shape=[SDS((S, DG2), BF16), SDS((G, C, C), F32), SDS((C, DG), F32), SDS((1, DG), F32)],
        compiler_params=_cp(("arbitrary",), 40 * 2 ** 20),
    )(z, zpre, dgated, v_norm, w_s, w_s_t, bias_full)


def _alibi_slopes(n_heads):
    return [float(v) for v in np.exp2(np.float32(-8.0) * np.arange(1, n_heads + 1, dtype=np.float32) / np.float32(n_heads))]


def _dil_view(arr, dil):
    S, C = arr.shape
    return arr if dil == 1 else arr.reshape(S // dil, dil * C)


def _dil_spec(dil, HD, ncb, cb, bmap):
    return pl.BlockSpec((ATTN_BLOCK, HD), lambda r, b: (bmap(b), r * ncb + cb))


def _group_view(arr, gi, dil, HD):
    if dil == 1:
        return arr, arr.shape[1] // HD, gi
    return _dil_view(arr[:, gi * HD:(gi + 1) * HD], dil), 1, 0


def _dil_unview(arr, S):
    return arr.reshape(S, arr.size // S)


def _attn_mask(b):
    qi = lax.broadcasted_iota(jnp.int32, (ATTN_BLOCK, 2 * ATTN_BLOCK), 0)
    kj = lax.broadcasted_iota(jnp.int32, (ATTN_BLOCK, 2 * ATTN_BLOCK), 1)
    delta = qi + ATTN_BLOCK - kj
    valid = (delta >= 0) & (delta <= ATTN_BLOCK) & ((kj >= ATTN_BLOCK) | (b > 0))
    return valid, delta.astype(F32)


def _pack_heads(cols, rows):
    lane = lax.broadcasted_iota(jnp.int32, (rows, LANE), 1)
    tile = jnp.zeros((rows, LANE), F32)
    for h, col in enumerate(cols):
        tile = jnp.where(lane == h, col, tile)
    return tile


def _attn_fwd(name, qn, kn, vb, gi, dil):
    S, C = qn.shape
    NG = len(DILATIONS)
    HD = C // NG
    H = HD // HEAD_DIM
    L = S // dil
    nb = L // ATTN_BLOCK
    slopes = _alibi_slopes(H)

    def body(q_ref, kc_ref, kp_ref, vc_ref, vp_ref, o_ref, lse_ref):
        b = pl.program_id(1)
        valid, delta = _attn_mask(b)
        dist = delta * float(dil)
        lse_cols = []
        for h in range(H):
            sl = slice(h * HEAD_DIM, (h + 1) * HEAD_DIM)
            k = jnp.concatenate([kp_ref[:, sl], kc_ref[:, sl]], axis=0)
            v = jnp.concatenate([vp_ref[:, sl], vc_ref[:, sl]], axis=0)
            s = lax.dot_general(q_ref[:, sl], k, (((1,), (1,)), ((), ())), preferred_element_type=F32)
            s = jnp.where(valid, s - slopes[h] * dist, -jnp.inf)
            m = jnp.max(s, axis=-1, keepdims=True)
            p = jnp.exp(s - m)
            l = jnp.sum(p, axis=-1, keepdims=True)
            o = jnp.dot(p.astype(BF16), v, preferred_element_type=F32)
            o_ref[:, sl] = o / l
            lse_cols.append(m + jnp.log(l))
        lse_ref[...] = _pack_heads(lse_cols, ATTN_BLOCK)

    cur = lambda b: b
    prev = lambda b: jnp.maximum(b - 1, 0)
    (qv, ncb, cb), (kv, _, _), (vv, _, _) = (_group_view(a, gi, dil, HD) for a in (qn, kn, vb))
    view_shape = (L, dil * HD)
    o, lse = pl.pallas_call(
        body, name=name, grid=(dil, nb),
        in_specs=[_dil_spec(dil, HD, ncb, cb, cur), _dil_spec(dil, HD, ncb, cb, cur), _dil_spec(dil, HD, ncb, cb, prev),
                  _dil_spec(dil, HD, ncb, cb, cur), _dil_spec(dil, HD, ncb, cb, prev)],
        out_specs=[_dil_spec(dil, HD, 1, 0, cur), _dil_spec(dil, LANE, 1, 0, cur)],
        out_shape=[SDS(view_shape, F32), SDS((L, dil * LANE), F32)],
        compiler_params=_cp(("parallel", "parallel"), 32 * 2 ** 20),
    )(qv, kv, kv, vv, vv)
    return _dil_unview(o, S), _dil_unview(lse, S)


def _attn_combine(name, o_list, lse_list, d_o=None):
    S, HD = o_list[0].shape
    H = HD // HEAD_DIM
    ng = len(o_list)
    tr = _row_tile(S, HD * 4, 2 ** 20)

    def body(*refs):
        o_refs = refs[:ng]
        l_refs = refs[ng:2 * ng]
        w_scr = refs[-1]
        ls = [r[...] for r in l_refs]
        m = ls[0]
        for t in ls[1:]:
            m = jnp.maximum(m, t)
        es = [jnp.exp(t - m) for t in ls]
        z = es[0]
        for t in es[1:]:
            z = z + t
        for g in range(ng):
            w_scr[g] = es[g] / z
        if d_o is not None:
            do_ref, lse_ref, dl_ref, dob_ref = refs[2 * ng:2 * ng + 4]
            lse_ref[...] = m + jnp.log(z)
            dob_ref[...] = do_ref[...].astype(BF16)
        dl_cols = []
        for h in range(H):
            sl = slice(h * HEAD_DIM, (h + 1) * HEAD_DIM)
            o = w_scr[0, :, h:h + 1] * o_refs[0][:, sl]
            for g in range(1, ng):
                o = o + w_scr[g, :, h:h + 1] * o_refs[g][:, sl]
            if d_o is None:
                refs[2 * ng][:, sl] = o.astype(BF16)
            else:
                dl_cols.append(jnp.sum(do_ref[:, sl] * o, axis=-1, keepdims=True))
        if d_o is not None:
            dl_ref[...] = _pack_heads(dl_cols, tr)

    row = pl.BlockSpec((tr, HD), lambda i: (i, 0))
    col = pl.BlockSpec((tr, LANE), lambda i: (i, 0))
    scratch = [pltpu.VMEM((ng, tr, LANE), F32)]
    if d_o is None:
        return pl.pallas_call(
            body, name=name, grid=(S // tr,), in_specs=[row] * ng + [col] * ng, out_specs=row,
            out_shape=SDS((S, HD), BF16), scratch_shapes=scratch, compiler_params=_cp(("parallel",), 40 * 2 ** 20),
        )(*o_list, *lse_list)
    return pl.pallas_call(
        body, name=name, grid=(S // tr,), in_specs=[row] * ng + [col] * ng + [row], out_specs=[col, col, row],
        out_shape=[SDS((S, LANE), F32), SDS((S, LANE), F32), SDS((S, HD), BF16)], scratch_shapes=scratch,
        compiler_params=_cp(("parallel",), 48 * 2 ** 20),
    )(*o_list, *lse_list, d_o)


def _attn_bwd(name, qn, kn, vb, dob, lse, delta_rows, gi, dil):
    S, C = qn.shape
    NG = len(DILATIONS)
    HD = C // NG
    H = HD // HEAD_DIM
    L = S // dil
    nb = L // ATTN_BLOCK
    slopes = _alibi_slopes(H)
    B = ATTN_BLOCK

    def body(q_ref, kc_ref, kp_ref, vc_ref, vp_ref, do_ref, lse_ref, dl_ref, dq_ref, dk_ref, dv_ref, ck_ref, cv_ref):
        b = pl.program_id(1)

        @pl.when(b == 0)
        def _():
            ck_ref[...] = jnp.zeros_like(ck_ref)
            cv_ref[...] = jnp.zeros_like(cv_ref)

        @pl.when(b < nb)
        def _():
            valid, delta = _attn_mask(b)
            dist = delta * float(dil)
            for h in range(H):
                sl = slice(h * HEAD_DIM, (h + 1) * HEAD_DIM)
                q = q_ref[:, sl]
                k = jnp.concatenate([kp_ref[:, sl], kc_ref[:, sl]], axis=0)
                v = jnp.concatenate([vp_ref[:, sl], vc_ref[:, sl]], axis=0)
                do = do_ref[:, sl]
                s = lax.dot_general(q, k, (((1,), (1,)), ((), ())), preferred_element_type=F32)
                s = jnp.where(valid, s - slopes[h] * dist, -jnp.inf)
                p = jnp.exp(s - lse_ref[:, h:h + 1])
                dp = lax.dot_general(do, v, (((1,), (1,)), ((), ())), preferred_element_type=F32)
                ds = (p * (dp - dl_ref[:, h:h + 1])).astype(BF16)
                dq_ref[:, sl] = jnp.dot(ds, k, preferred_element_type=F32).astype(BF16)
                dk2 = lax.dot_general(ds, q, (((0,), (0,)), ((), ())), preferred_element_type=F32)
                dv2 = lax.dot_general(p.astype(BF16), do, (((0,), (0,)), ((), ())), preferred_element_type=F32)
                dk_ref[:, sl] = (ck_ref[:, sl] + dk2[:B]).astype(BF16)
                dv_ref[:, sl] = (cv_ref[:, sl] + dv2[:B]).astype(BF16)
                ck_ref[:, sl] = dk2[B:]
                cv_ref[:, sl] = dv2[B:]

        @pl.when(b == nb)
        def _():
            dk_ref[...] = ck_ref[...].astype(BF16)
            dv_ref[...] = cv_ref[...].astype(BF16)

    cur = lambda b: jnp.minimum(b, nb - 1)
    prev = lambda b: jnp.maximum(jnp.minimum(b, nb - 1) - 1, 0)
    late = lambda b: jnp.maximum(b - 1, 0)
    (qv, ncb, cb), (kv, _, _), (vv, _, _) = (_group_view(a, gi, dil, HD) for a in (qn, kn, vb))
    dov, lsev, dlv = _dil_view(dob, dil), _dil_view(lse, dil), _dil_view(delta_rows, dil)
    view_shape = (L, dil * HD)
    one = lambda m: _dil_spec(dil, HD, 1, 0, m)
    grp = lambda m: _dil_spec(dil, HD, ncb, cb, m)
    heads = _dil_spec(dil, LANE, 1, 0, cur)
    dq, dk, dv = pl.pallas_call(
        body, name=name, grid=(dil, nb + 1),
        in_specs=[grp(cur), grp(cur), grp(prev), grp(cur), grp(prev), one(cur), heads, heads],
        out_specs=[one(cur), one(late), one(late)],
        out_shape=[SDS(view_shape, BF16)] * 3,
        scratch_shapes=[pltpu.VMEM((B, HD), F32), pltpu.VMEM((B, HD), F32)],
        compiler_params=_cp(("arbitrary", "arbitrary"), 40 * 2 ** 20),
    )(qv, kv, kv, vv, vv, dov, lsev, dlv)
    return _dil_unview(dq, S), _dil_unview(dk, S), _dil_unview(dv, S)


def _cast_into_gathered(name, w, l, sa, chip_arr):
    L, r, c = w.shape
    tr = _row_tile(r, c * 4)
    nrb = r // tr
    src = pl.BlockSpec((None, tr, c), lambda i, chip: (l, i, 0))
    if sa == 0:
        dst, shape = pl.BlockSpec((tr, c), lambda i, chip: (chip[0] * nrb + i, 0)), (N_CHIPS * r, c)
    else:
        dst, shape = pl.BlockSpec((tr, c), lambda i, chip: (i, chip[0])), (r, N_CHIPS * c)

    def body(chip_ref, i_ref, o_ref):
        o_ref[...] = i_ref[...].astype(BF16)

    return pl.pallas_call(
        body, name=name,
        grid_spec=pltpu.PrefetchScalarGridSpec(num_scalar_prefetch=1, grid=(nrb,), in_specs=[src], out_specs=dst),
        out_shape=SDS(shape, BF16), compiler_params=_cp(("parallel",), 32 * 2 ** 20),
    )(chip_arr, w)


def _add_half(name, dw, land, sa, c_arr):
    hr, hc = land.shape
    tr = _row_tile(hr, hc * 2)
    nrb = hr // tr
    if sa == 1:
        mine = pl.BlockSpec((tr, hc), lambda i, c: (c[0] * nrb + i, 0))
    else:
        mine = pl.BlockSpec((tr, hc), lambda i, c: (i, c[0]))
    other = pl.BlockSpec((tr, hc), lambda i, c: (i, 0))

    def body(c_ref, a_ref, b_ref, o_ref):
        o_ref[...] = (a_ref[...].astype(F32) + b_ref[...].astype(F32)).astype(BF16)

    return pl.pallas_call(
        body, name=name,
        grid_spec=pltpu.PrefetchScalarGridSpec(num_scalar_prefetch=1, grid=(nrb,), in_specs=[mine, other], out_specs=other),
        out_shape=SDS((hr, hc), BF16), compiler_params=_cp(("parallel",), 32 * 2 ** 20),
    )(c_arr, dw, land)


def _sum_slots(name, slots, out_dtype=F32):
    n, R, C = slots.shape
    tr = _row_tile(R, C * n * jnp.dtype(slots.dtype).itemsize, 2 ** 21)

    def body(s_ref, o_ref):
        acc = s_ref[0].astype(F32)
        for k in range(1, n):
            acc = acc + s_ref[k].astype(F32)
        o_ref[...] = acc.astype(out_dtype)

    return pl.pallas_call(
        body, name=name, grid=(R // tr,),
        in_specs=[pl.BlockSpec((n, tr, C), lambda i: (0, i, 0))], out_specs=pl.BlockSpec((tr, C), lambda i: (i, 0)),
        out_shape=SDS((R, C), out_dtype), compiler_params=_cp(("parallel",), 32 * 2 ** 20),
    )(slots)


def _sum_into(name, part, slots, buf, l, sa, where):
    n, pr, pc = slots.shape
    tr = _row_tile(pr, pc * (n + 1) * jnp.dtype(slots.dtype).itemsize, 2 ** 21)
    nrb = pr // tr
    if sa == 1:
        dst = pl.BlockSpec((None, tr, pc), lambda i, c, j: (l, c[0] * nrb + i, 0))
        own = pl.BlockSpec((tr, pc), lambda i, c, j: (i, j[0]))
    else:
        dst = pl.BlockSpec((None, tr, pc), lambda i, c, j: (l, i, c[0]))
        own = pl.BlockSpec((tr, pc), lambda i, c, j: (j[0] * nrb + i, 0))

    def body(c_ref, j_ref, p_ref, s_ref, b_ref, o_ref):
        acc = p_ref[...].astype(F32)
        for k in range(n):
            acc = acc + s_ref[k].astype(F32)
        o_ref[...] = acc

    return pl.pallas_call(
        body, name=name,
        grid_spec=pltpu.PrefetchScalarGridSpec(
            num_scalar_prefetch=2, grid=(nrb,),
            in_specs=[own, pl.BlockSpec((n, tr, pc), lambda i, c, j: (0, i, 0)), ANY], out_specs=dst),
        out_shape=SDS(buf.shape, buf.dtype), input_output_aliases={4: 0},
        compiler_params=_cp(("parallel",), 32 * 2 ** 20),
    )(where[0], where[1], part, slots, buf)


def _adamw_body(g_ref, w_ref, m_ref, v_ref, go_ref, d_ref, mo_ref, vo_ref):
    bc1 = 1.0 - ADAM_B1 ** ADAM_STEP
    bc2 = 1.0 - ADAM_B2 ** ADAM_STEP
    gv = g_ref[...]
    mn = ADAM_B1 * m_ref[...] + (1.0 - ADAM_B1) * gv
    vn = ADAM_B2 * v_ref[...] + (1.0 - ADAM_B2) * (gv * gv)
    go_ref[...] = gv
    mo_ref[...] = mn
    vo_ref[...] = vn
    d_ref[...] = -ADAM_LR * ((mn / bc1) / (jnp.sqrt(vn / bc2) + ADAM_EPS) + ADAM_WD * w_ref[...])


def _adamw_layer(name, g, w, m, v, l, outs):
    L, r, c = g.shape
    tr = _row_tile(r, c * 4, 2 ** 21)
    spec = pl.BlockSpec((None, tr, c), lambda i: (l, i, 0))

    def body(g_ref, w_ref, m_ref, v_ref, a0, a1, a2, a3, go_ref, d_ref, mo_ref, vo_ref):
        _adamw_body(g_ref, w_ref, m_ref, v_ref, go_ref, d_ref, mo_ref, vo_ref)

    return list(pl.pallas_call(
        body, name=name, grid=(r // tr,), in_specs=[spec] * 4 + [ANY] * 4, out_specs=[spec] * 4,
        out_shape=[SDS((L, r, c), F32)] * 4, input_output_aliases={4 + k: k for k in range(4)},
        compiler_params=_cp(("parallel",), 48 * 2 ** 20),
    )(g, w, m, v, *outs))


def _adamw(name, g, w, m, v):
    R, C = g.shape
    tr = _row_tile(R, C * 4, 2 ** 19)

    def body(g_ref, w_ref, m_ref, v_ref, go_ref, d_ref, mo_ref, vo_ref):
        _adamw_body(g_ref, w_ref, m_ref, v_ref, go_ref, d_ref, mo_ref, vo_ref)

    spec = pl.BlockSpec((tr, C), lambda i: (i, 0))
    return pl.pallas_call(
        body, name=name, grid=(R // tr,), in_specs=[spec] * 4, out_specs=[spec] * 4,
        out_shape=[SDS((R, C), F32)] * 4, compiler_params=_cp(("parallel",), 32 * 2 ** 20),
    )(g, w, m, v)


ANY = pl.BlockSpec(memory_space=pl.ANY)


def _coords():
    return lax.axis_index("x"), lax.axis_index("y"), lax.axis_index("c")


def _other_chips(x, y):
    return [((1 - x, y), 2 * (1 - x) + y), ((x, 1 - y), 2 * x + (1 - y)), ((1 - x, 1 - y), 2 * (1 - x) + (1 - y))]


def _win(ref, axis, start, size):
    if not isinstance(start, int):
        start = pl.multiple_of(start, LANE if axis == 1 else BF16_ROWS)
    if axis == 0:
        return ref.at[pl.ds(start, size), :]
    return ref.at[:, pl.ds(start, size)]


def _rcopy(src, dst, ssem, rsem, dev):
    return pltpu.make_async_remote_copy(src_ref=src, dst_ref=dst, send_sem=ssem, recv_sem=rsem,
                                        device_id=dev, device_id_type=MESH)


HBM = pl.BlockSpec(memory_space=pltpu.HBM)
SEM = pl.BlockSpec(memory_space=pltpu.SEMAPHORE)
TOKEN = pl.BlockSpec(memory_space=pltpu.VMEM)
EFFECT = pltpu.SideEffectType.DATAFLOW_SIDE_EFFECTING


def _in_hbm(a):
    return pltpu.with_memory_space_constraint(a, pltpu.HBM)


def _ag_geometry(mats, sas, o):
    sa = sas[o]
    return sa, 1 - sa, mats[o].shape[sa] // N_CHIPS, mats[o].shape[1 - sa] // 2


def _ag_ici_copy(mats, sas, refs, o, k, sems, x, y, c):
    sa, ha, wl, hl = _ag_geometry(mats, sas, o)
    chip, jk = _other_chips(x, y)[k]
    mine = _win(_win(refs[o], sa, (2 * x + y) * wl, wl), ha, c * hl, hl)
    landed = _win(_win(refs[o], sa, jk * wl, wl), ha, c * hl, hl)
    return mine, landed, (*chip, c)


def _ag_start(name, mats, sas, after):
    n = len(mats)

    def body(*refs):
        ins = refs[:n]
        s_sem, r_sem = refs[n + 1], refs[n + 2]
        token = refs[2 * n + 3]
        x, y, c = _coords()
        for o in range(n):
            for k in range(3):
                mine, _, dev = _ag_ici_copy(mats, sas, ins, o, k, None, x, y, c)
                _rcopy(mine, mine, s_sem.at[3 * o + k], r_sem.at[3 * o + k], dev).start()
        token[...] = jnp.zeros_like(token)

    out = pl.pallas_call(
        body, name=name,
        out_shape=(pltpu.SemaphoreType.DMA((3 * n,)), pltpu.SemaphoreType.DMA((3 * n,)),
                   *[pltpu.HBM(m.shape, m.dtype) for m in mats], SDS((8, LANE), F32)),
        in_specs=[HBM] * n + [ANY], out_specs=(SEM, SEM, *[HBM] * n, TOKEN),
        input_output_aliases={k: 2 + k for k in range(n)},
        compiler_params=pltpu.CompilerParams(has_side_effects=EFFECT),
    )(*[_in_hbm(m) for m in mats], after)
    return out[0], out[1], list(out[2:2 + n]), out[2 + n]


def _ag_wait(name, mats, sas, s_sem, r_sem, after):
    n = len(mats)

    def body(*refs):
        ins = refs[:n]
        s_ref, r_ref = refs[n], refs[n + 1]
        x, y, c = _coords()
        for o in range(n):
            for k in range(3):
                mine, landed, dev = _ag_ici_copy(mats, sas, ins, o, k, None, x, y, c)
                cp = _rcopy(mine, landed, s_ref.at[3 * o + k], r_ref.at[3 * o + k], dev)
                cp.wait_send()
                cp.wait_recv()

    after = list(after) if isinstance(after, (list, tuple)) else [after]
    return list(pl.pallas_call(
        body, name=name, out_shape=[pltpu.HBM(m.shape, m.dtype) for m in mats],
        in_specs=[HBM] * n + [SEM, SEM] + [ANY] * len(after), out_specs=[HBM] * n,
        input_output_aliases={k: k for k in range(n)},
        compiler_params=pltpu.CompilerParams(has_side_effects=EFFECT),
    )(*mats, s_sem, r_sem, *after))


def _ag_forward(name, mats, sas):
    n = len(mats)

    def body(*refs):
        outs = refs[n:2 * n]
        s_fwd, r_fwd = refs[2 * n:]
        x, y, c = _coords()
        sibling = (x, y, 1 - c)
        sends = []
        for o in range(n):
            sa, ha, wl, hl = _ag_geometry(mats, sas, o)
            for k, (chip, jk) in enumerate(_other_chips(x, y)):
                landed = _win(_win(outs[o], sa, jk * wl, wl), ha, c * hl, hl)
                fwd = _rcopy(landed, landed, s_fwd.at[3 * o + k], r_fwd.at[3 * o + k], sibling)
                fwd.start()
                sends.append(fwd)
        for o in range(n):
            sa, ha, wl, hl = _ag_geometry(mats, sas, o)
            for k, (chip, jk) in enumerate(_other_chips(x, y)):
                got = _win(_win(outs[o], sa, jk * wl, wl), ha, (1 - c) * hl, hl)
                _rcopy(got, got, s_fwd.at[3 * o + k], r_fwd.at[3 * o + k], sibling).wait_recv()
        for cp in sends:
            cp.wait_send()

    return list(pl.pallas_call(
        body, name=name, in_specs=[ANY] * n, out_specs=[ANY] * n,
        out_shape=[SDS(m.shape, m.dtype) for m in mats], input_output_aliases={k: k for k in range(n)},
        scratch_shapes=[pltpu.SemaphoreType.DMA((3 * n,)), pltpu.SemaphoreType.DMA((3 * n,))],
        compiler_params=pltpu.CompilerParams(has_side_effects=True),
    )(*mats))


def _all_gather_vec(name, v):
    Lv, cv = v.shape

    def body(v_ref, o_ref, loc_sem, s_sem, r_sem):
        x, y, c = _coords()
        jme = 2 * x + y
        chips = _other_chips(x, y)
        mine = _win(o_ref, 1, jme * cv, cv)
        loc = pltpu.make_async_copy(v_ref, mine, loc_sem)
        loc.start()
        sends = []
        for k, (chip, _) in enumerate(chips):
            cp = _rcopy(v_ref, mine, s_sem.at[k], r_sem.at[k], (*chip, c))
            cp.start()
            sends.append(cp)
        for k, (chip, jk) in enumerate(chips):
            got = _win(o_ref, 1, jk * cv, cv)
            _rcopy(got, got, s_sem.at[k], r_sem.at[k], (*chip, c)).wait_recv()
        for cp in sends:
            cp.wait_send()
        loc.wait()

    return pl.pallas_call(
        body, name=name, in_specs=[ANY], out_specs=ANY, out_shape=SDS((Lv, 4 * cv), v.dtype),
        scratch_shapes=[pltpu.SemaphoreType.DMA, pltpu.SemaphoreType.DMA((3,)), pltpu.SemaphoreType.DMA((3,))],
        compiler_params=pltpu.CompilerParams(has_side_effects=True),
    )(v)


def _half_shape(shape, sa):
    R, C = shape
    return (R // 2, C) if sa == 1 else (R, C // 2)


def _rs_sibling_copy(dws, sas, d_refs, land_refs, o, x, y, c):
    ha = 1 - sas[o]
    hl = dws[o].shape[ha] // 2
    return _win(d_refs[o], ha, (1 - c) * hl, hl), land_refs[o], (x, y, 1 - c)


def _rs_sibling_start(name, dws, sas, after):
    n = len(dws)

    def body(*refs):
        ins = refs[:n]
        s_sem, r_sem = refs[n + 1], refs[n + 2]
        lands = refs[2 * n + 3:3 * n + 3]
        token = refs[3 * n + 3]
        x, y, c = _coords()
        for o in range(n):
            src, dst, dev = _rs_sibling_copy(dws, sas, ins, lands, o, x, y, c)
            _rcopy(src, dst, s_sem.at[o], r_sem.at[o], dev).start()
        token[...] = jnp.zeros_like(token)

    out = pl.pallas_call(
        body, name=name,
        out_shape=(pltpu.SemaphoreType.DMA((n,)), pltpu.SemaphoreType.DMA((n,)),
                   *[pltpu.HBM(d.shape, d.dtype) for d in dws],
                   *[pltpu.HBM(_half_shape(d.shape, sa), d.dtype) for d, sa in zip(dws, sas)],
                   SDS((8, LANE), F32)),
        in_specs=[HBM] * n + [ANY], out_specs=(SEM, SEM, *[HBM] * (2 * n), TOKEN),
        input_output_aliases={k: 2 + k for k in range(n)},
        compiler_params=pltpu.CompilerParams(has_side_effects=EFFECT),
    )(*[_in_hbm(d) for d in dws], after)
    return out[0], out[1], list(out[2:2 + n]), list(out[2 + n:2 + 2 * n]), out[2 + 2 * n]


def _rs_sibling_wait(name, dws, lands, sas, s_sem, r_sem, after):
    n = len(dws)

    def body(*refs):
        d_refs, land_refs = refs[:n], refs[n:2 * n]
        s_ref, r_ref = refs[2 * n], refs[2 * n + 1]
        x, y, c = _coords()
        for o in range(n):
            src, dst, dev = _rs_sibling_copy(dws, sas, d_refs, land_refs, o, x, y, c)
            cp = _rcopy(src, dst, s_ref.at[o], r_ref.at[o], dev)
            cp.wait_send()
            cp.wait_recv()

    after = list(after) if isinstance(after, (list, tuple)) else [after]
    out = pl.pallas_call(
        body, name=name,
        out_shape=[pltpu.HBM(a.shape, a.dtype) for a in (*dws, *lands)],
        in_specs=[HBM] * (2 * n) + [SEM, SEM] + [ANY] * len(after), out_specs=[HBM] * (2 * n),
        input_output_aliases={k: k for k in range(2 * n)},
        compiler_params=pltpu.CompilerParams(has_side_effects=EFFECT),
    )(*dws, *lands, s_sem, r_sem, *after)
    return list(out[:n]), list(out[n:])


def _rs_piece_shape(p, sa):
    hr, hc = p.shape
    return (hr // N_CHIPS, hc) if sa == 0 else (hr, hc // N_CHIPS)


def _rs_ici_copy(parts, sas, p_refs, slot_refs, o, k, x, y, c):
    sa = sas[o]
    pl_ = parts[o].shape[sa] // N_CHIPS
    chip, jk = _other_chips(x, y)[k]
    return _win(p_refs[o], sa, jk * pl_, pl_), slot_refs[o].at[k], (*chip, c)


def _rs_start(name, parts, sas, after):
    n = len(parts)

    def body(*refs):
        ins = refs[:n]
        s_sem, r_sem = refs[n + 1], refs[n + 2]
        slots = refs[2 * n + 3:3 * n + 3]
        token = refs[3 * n + 3]
        x, y, c = _coords()
        for o in range(n):
            for k in range(3):
                src, dst, dev = _rs_ici_copy(parts, sas, ins, slots, o, k, x, y, c)
                _rcopy(src, dst, s_sem.at[3 * o + k], r_sem.at[3 * o + k], dev).start()
        token[...] = jnp.zeros_like(token)

    out = pl.pallas_call(
        body, name=name,
        out_shape=(pltpu.SemaphoreType.DMA((3 * n,)), pltpu.SemaphoreType.DMA((3 * n,)),
                   *[pltpu.HBM(p.shape, p.dtype) for p in parts],
                   *[pltpu.HBM((3,) + _rs_piece_shape(p, sa), p.dtype) for p, sa in zip(parts, sas)],
                   SDS((8, LANE), F32)),
        in_specs=[HBM] * n + [ANY], out_specs=(SEM, SEM, *[HBM] * (2 * n), TOKEN),
        input_output_aliases={k: 2 + k for k in range(n)},
        compiler_params=pltpu.CompilerParams(has_side_effects=EFFECT),
    )(*[_in_hbm(p) for p in parts], after)
    return out[0], out[1], list(out[2:2 + n]), list(out[2 + n:2 + 2 * n]), out[2 + 2 * n]


def _rs_wait(name, parts, slots, sas, s_sem, r_sem, after):
    n = len(parts)

    def body(*refs):
        p_refs, slot_refs = refs[:n], refs[n:2 * n]
        s_ref, r_ref = refs[2 * n], refs[2 * n + 1]
        x, y, c = _coords()
        for o in range(n):
            for k in range(3):
                src, dst, dev = _rs_ici_copy(parts, sas, p_refs, slot_refs, o, k, x, y, c)
                cp = _rcopy(src, dst, s_ref.at[3 * o + k], r_ref.at[3 * o + k], dev)
                cp.wait_send()
                cp.wait_recv()

    after = list(after) if isinstance(after, (list, tuple)) else [after]
    out = pl.pallas_call(
        body, name=name,
        out_shape=[pltpu.HBM(a.shape, a.dtype) for a in (*parts, *slots)],
        in_specs=[HBM] * (2 * n) + [SEM, SEM] + [ANY] * len(after), out_specs=[HBM] * (2 * n),
        input_output_aliases={k: k for k in range(2 * n)},
        compiler_params=pltpu.CompilerParams(has_side_effects=EFFECT),
    )(*parts, *slots, s_sem, r_sem, *after)
    return list(out[:n]), list(out[n:])


def _rs_share_copy(sas, layers, buf_idx, bufs, refs, o, x, y, c):
    ha = 1 - sas[o]
    hl = bufs[buf_idx[o]].shape[1 + ha] // 2
    layer = refs[buf_idx[o]].at[layers[o]]
    return _win(layer, ha, c * hl, hl), _win(layer, ha, (1 - c) * hl, hl), (x, y, 1 - c)


def _rs_share_start(name, sas, layers, buf_idx, bufs, after):
    n, nbuf = len(sas), len(bufs)

    def body(*refs):
        ins = refs[:nbuf]
        s_sem, r_sem = refs[nbuf + 1], refs[nbuf + 2]
        token = refs[2 * nbuf + 3]
        x, y, c = _coords()
        for o in range(n):
            mine, _, dev = _rs_share_copy(sas, layers, buf_idx, bufs, ins, o, x, y, c)
            _rcopy(mine, mine, s_sem.at[o], r_sem.at[o], dev).start()
        token[...] = jnp.zeros_like(token)

    out = pl.pallas_call(
        body, name=name,
        out_shape=(pltpu.SemaphoreType.DMA((n,)), pltpu.SemaphoreType.DMA((n,)),
                   *[pltpu.HBM(b.shape, b.dtype) for b in bufs], SDS((8, LANE), F32)),
        in_specs=[HBM] * nbuf + [ANY], out_specs=(SEM, SEM, *[HBM] * nbuf, TOKEN),
        input_output_aliases={k: 2 + k for k in range(nbuf)},
        compiler_params=pltpu.CompilerParams(has_side_effects=EFFECT),
    )(*[_in_hbm(b) for b in bufs], after)
    return out[0], out[1], list(out[2:2 + nbuf]), out[2 + nbuf]


def _rs_share_wait(name, sas, layers, buf_idx, bufs, s_sem, r_sem, after):
    n, nbuf = len(sas), len(bufs)

    def body(*refs):
        ins = refs[:nbuf]
        s_ref, r_ref = refs[nbuf], refs[nbuf + 1]
        x, y, c = _coords()
        for o in range(n):
            mine, got, dev = _rs_share_copy(sas, layers, buf_idx, bufs, ins, o, x, y, c)
            cp = _rcopy(mine, got, s_ref.at[o], r_ref.at[o], dev)
            cp.wait_send()
            cp.wait_recv()

    after = list(after) if isinstance(after, (list, tuple)) else [after]
    return list(pl.pallas_call(
        body, name=name, out_shape=[pltpu.HBM(b.shape, b.dtype) for b in bufs],
        in_specs=[HBM] * nbuf + [SEM, SEM] + [ANY] * len(after), out_specs=[HBM] * nbuf,
        input_output_aliases={k: k for k in range(nbuf)},
        compiler_params=pltpu.CompilerParams(has_side_effects=EFFECT),
    )(*bufs, s_sem, r_sem, *after))


def _place_own(name, packed, me_arr):
    R, C = packed.shape
    tr = _row_tile(R, C * 4)

    def body(me_ref, p_ref, o_ref):
        o_ref[...] = p_ref[...]

    return pl.pallas_call(
        body, name=name,
        grid_spec=pltpu.PrefetchScalarGridSpec(
            num_scalar_prefetch=1, grid=(R // tr,),
            in_specs=[pl.BlockSpec((tr, C), lambda i, me: (i, 0))],
            out_specs=pl.BlockSpec((None, tr, C), lambda i, me: (me[0], i, 0))),
        out_shape=SDS((N_DEV, R, C), packed.dtype), compiler_params=_cp(("parallel",), 32 * 2 ** 20),
    )(me_arr, packed)


def _exchange_copy(s_refs, k, x, y, c):
    px = 1 - x if k & 4 else x
    py = 1 - y if k & 2 else y
    pc = 1 - c if k & 1 else c
    return s_refs.at[4 * x + 2 * y + c], s_refs.at[4 * px + 2 * py + pc], (px, py, pc)


def _exchange_start(name, slots, after):
    def body(s_ref, after_ref, s_sem, r_sem, out_ref, token):
        x, y, c = _coords()
        for k in range(1, N_DEV):
            mine, _, dev = _exchange_copy(s_ref, k, x, y, c)
            _rcopy(mine, mine, s_sem.at[k - 1], r_sem.at[k - 1], dev).start()
        token[...] = jnp.zeros_like(token)

    out = pl.pallas_call(
        body, name=name,
        out_shape=(pltpu.SemaphoreType.DMA((N_DEV - 1,)), pltpu.SemaphoreType.DMA((N_DEV - 1,)),
                   pltpu.HBM(slots.shape, slots.dtype), SDS((8, LANE), F32)),
        in_specs=[HBM, ANY], out_specs=(SEM, SEM, HBM, TOKEN), input_output_aliases={0: 2},
        compiler_params=pltpu.CompilerParams(has_side_effects=EFFECT),
    )(_in_hbm(slots), after)
    return out


def _exchange_wait(name, slots, s_sem, r_sem, after):
    def body(s_ref, s_sem_ref, r_sem_ref, *rest):
        x, y, c = _coords()
        for k in range(1, N_DEV):
            mine, theirs, dev = _exchange_copy(s_ref, k, x, y, c)
            cp = _rcopy(mine, theirs, s_sem_ref.at[k - 1], r_sem_ref.at[k - 1], dev)
            cp.wait_send()
            cp.wait_recv()

    after = list(after) if isinstance(after, (list, tuple)) else [after]
    return pl.pallas_call(
        body, name=name, out_shape=pltpu.HBM(slots.shape, slots.dtype),
        in_specs=[HBM, SEM, SEM] + [ANY] * len(after), out_specs=HBM, input_output_aliases={0: 0},
        compiler_params=pltpu.CompilerParams(has_side_effects=EFFECT),
    )(slots, s_sem, r_sem, *after)


def _pack(arrays):
    rows = []
    for a in arrays:
        flat = a.reshape(-1).astype(F32)
        pad = (-flat.size) % PACK_TILE
        rows.append(jnp.pad(flat, (0, pad)).reshape(-1, LANE))
    return jnp.concatenate(rows, axis=0)


def _unpack(packed, shapes):
    out, row = [], 0
    for s in shapes:
        size = int(np.prod(s)) if len(s) else 1
        nrows = -(-size // PACK_TILE) * (PACK_TILE // LANE)
        out.append(packed[row:row + nrows].reshape(-1)[:size].reshape(s))
        row += nrows
    return out


BIG_WEIGHTS = {
    "ffn1_w_gate": 1, "ffn1_w_up": 1, "ffn1_w_down": 0, "ffn2_w_gate": 1, "ffn2_w_up": 1, "ffn2_w_down": 0,
    "gmlp_w_in": 1, "gmlp_w_out": 0, "w_kv": 1, "attn_w_q": 1, "attn_w_o": 0,
}
SMALL_WEIGHTS = ("ffn1_norm", "mix_norm", "ffn2_norm", "gmlp_w_s", "gmlp_b_s", "kv_norm", "k_norm", "attn_q_norm")
WEIGHT_ORDER = ("ffn1_norm", "ffn1_w_gate", "ffn1_w_up", "ffn1_w_down", "mix_norm", "ffn2_norm", "ffn2_w_gate",
                "ffn2_w_up", "ffn2_w_down", "gmlp_w_in", "gmlp_v_norm", "gmlp_w_s", "gmlp_b_s", "gmlp_w_out",
                "kv_norm", "w_kv", "k_norm", "attn_w_q", "attn_q_norm", "attn_w_o")


def _ep_all(accs, ex):
    return list(accs)


def _as3d(w):
    return w if w.ndim == 3 else w.reshape((1,) + w.shape)


def kernel(x, ffn1_norm, ffn1_w_gate, ffn1_w_up, ffn1_w_down, mix_norm, ffn2_norm, ffn2_w_gate, ffn2_w_up, ffn2_w_down, gmlp_w_in, gmlp_v_norm, gmlp_w_s, gmlp_b_s, gmlp_w_out, kv_norm, w_kv, k_norm, attn_w_q, attn_q_norm, attn_w_o, loss_target, m_ffn1_norm, m_ffn1_w_gate, m_ffn1_w_up, m_ffn1_w_down, m_mix_norm, m_ffn2_norm, m_ffn2_w_gate, m_ffn2_w_up, m_ffn2_w_down, m_gmlp_w_in, m_gmlp_v_norm, m_gmlp_w_s, m_gmlp_b_s, m_gmlp_w_out, m_kv_norm, m_w_kv, m_k_norm, m_attn_w_q, m_attn_q_norm, m_attn_w_o, v_ffn1_norm, v_ffn1_w_gate, v_ffn1_w_up, v_ffn1_w_down, v_mix_norm, v_ffn2_norm, v_ffn2_w_gate, v_ffn2_w_up, v_ffn2_w_down, v_gmlp_w_in, v_gmlp_v_norm, v_gmlp_w_s, v_gmlp_b_s, v_gmlp_w_out, v_kv_norm, v_w_kv, v_k_norm, v_attn_w_q, v_attn_q_norm, v_attn_w_o):
    P = dict(locals())
    assert x.shape[0] == 1, "one sample per device"
    S, D = x.shape[1], x.shape[2]
    NL = ffn1_norm.shape[0]
    NG = len(DILATIONS)
    HD = attn_w_o.shape[1] * N_CHIPS
    H = HD // HEAD_DIM
    DG = gmlp_w_out.shape[1] * N_CHIPS
    G = DG // GMLP_GROUP_WIDTH
    assert all((S // d) % ATTN_BLOCK == 0 for d in DILATIONS) and S % GMLP_CHUNK == 0
    xs = x.reshape(S, D)
    tgt = loss_target.reshape(S, D)
    c_arr = lax.axis_index("c").astype(jnp.int32).reshape(1)
    chip = 2 * lax.axis_index("x") + lax.axis_index("y")
    chip_arr = chip.astype(jnp.int32).reshape(1)
    kv_layer = N_A_LAYERS - 1


    def layer_weights(l):
        names = [("ffn1_w_gate", l), ("ffn1_w_up", l), ("ffn1_w_down", l), ("ffn2_w_gate", l), ("ffn2_w_up", l), ("ffn2_w_down", l)]
        if l < N_A_LAYERS:
            names += [("gmlp_w_in", l), ("gmlp_w_out", l)]
        else:
            names += [("attn_w_q", l - N_A_LAYERS), ("attn_w_o", l - N_A_LAYERS)]
        if l == kv_layer:
            names += [("w_kv", 0)]
        return names

    W = {}
    ag_open = {}
    n_first = 3
    ag_units = {}
    for l in range(NL):
        ag_units[f"{l}a"], ag_units[f"{l}b"] = layer_weights(l)[:n_first], layer_weights(l)[n_first:]

    def cast_unit(u):
        return [_cast_into_gathered("cast_shard", _as3d(P[n]), li, BIG_WEIGHTS[n], chip_arr) for n, li in ag_units[u]]

    def ag_begin(u, after, mats):
        sas = [BIG_WEIGHTS[n] for n, _ in ag_units[u]]
        s_sem, r_sem, mats, token = _ag_start(f"ag_start_l{u}", mats, sas, after)
        ag_open[u] = (sas, s_sem, r_sem, mats)
        return token

    def ag_finish(u, after):
        sas, s_sem, r_sem, mats = ag_open.pop(u)
        mats = _ag_wait(f"ag_wait_l{u}", mats, sas, s_sem, r_sem, after)
        W.update(dict(zip(ag_units[u], _ag_forward(f"ag_forward_l{u}", mats, sas))))

    vnorm_full = _all_gather_vec("ag_vnorm", gmlp_v_norm)
    ag_token = vnorm_full
    for u in [f"{l}{h}" for l in range(min(2, NL)) for h in "ab"]:
        ag_token = ag_begin(u, ag_token, cast_unit(u))
    cast_ahead = {f"{l}{h}": cast_unit(f"{l}{h}") for l in range(2, NL) for h in "ab"}
    ag_finish("0a", [ag_token] + [m for u in cast_ahead for m in cast_ahead[u]])

    kgain = jnp.tile(k_norm[:, None, :], (1, H, 1)).reshape(1, NG * HD)
    qgain = [jnp.tile(attn_q_norm[j][:, None, :], (1, H, 1)).reshape(1, NG * HD) for j in range(NL - N_A_LAYERS)]
    q_scale = HEAD_DIM ** -0.5
    one = [(0, 0, 0)]

    def ffn_fwd(xc, gamma, wg, wu, wd, dep=None):
        n = _rms_fwd("ffn_norm", xc, gamma, dep)
        g, u, act = _mm("ffn_up", [n], [wg, wu], [(0, 0, 0), (0, 1, 1)], _ep_swiglu, [BF16] * 3)
        (x2,) = _mm("ffn_down", [act], [wd], one, _ep_residual(0.5), [F32], extras=[xc])
        return x2, (xc, n, g, u, act)

    saved = {}
    xc = xs
    for l in range(NL):
        if l > 0:
            ag_finish(f"{l}a", xc)
        if l + 2 < NL:
            for h in "ab":
                ag_token = ag_begin(f"{l + 2}{h}", ag_token, cast_ahead.pop(f"{l + 2}{h}"))
        xc, saved["f1", l] = ffn_fwd(xc, ffn1_norm[l], W["ffn1_w_gate", l], W["ffn1_w_up", l], W["ffn1_w_down", l], ag_token)
        ag_finish(f"{l}b", xc)
        h = _rms_fwd("mix_norm", xc, mix_norm[l])
        if l < N_A_LAYERS:
            zpre, z = _mm("gmlp_in", [h], [W["gmlp_w_in", l]], one, _ep_gelu, [BF16, F32])
            bias_full = jnp.repeat(gmlp_b_s[l].T, GMLP_GROUP_WIDTH, axis=1)
            gated = _gmlp_gate_fwd("gmlp_gate", z, vnorm_full[l:l + 1], gmlp_w_s[l], bias_full)
            (x2,) = _mm("gmlp_out", [gated], [W["gmlp_w_out", l]], one, _ep_residual(1.0), [F32], extras=[xc])
            saved["mix", l] = (xc, h, zpre, z, gated, bias_full)
        else:
            j = l - N_A_LAYERS
            (q_raw,) = _mm("attn_q", [h], [W["attn_w_q", j]], one, _ep_plain, [F32])
            qn = _head_norm_fwd("q_norm", q_raw, 0, 1, qgain[j], q_scale, False)
            os_, lses = [], []
            for gi, dil in enumerate(DILATIONS):
                o, lse = _attn_fwd(f"attn_fwd_d{dil}", qn, kn, vb, gi, dil)
                os_.append(o)
                lses.append(lse)
            ob = _attn_combine("attn_mix", os_, lses)
            (x2,) = _mm("attn_o", [ob], [W["attn_w_o", j]], one, _ep_residual(1.0), [F32], extras=[xc])
            saved["mix", l] = (xc, h, q_raw, qn, os_, lses, ob)
        xc = x2
        xc, saved["f2", l] = ffn_fwd(xc, ffn2_norm[l], W["ffn2_w_gate", l], W["ffn2_w_up", l], W["ffn2_w_down", l])
        if l == kv_layer:
            kvn = _rms_fwd("kv_norm", xc, kv_norm)
            (kv_raw,) = _mm("kv_proj", [kvn], [W["w_kv", 0]], one, _ep_plain, [F32])
            kn, vb = _head_norm_fwd("k_norm", kv_raw, 0, 2, kgain, 1.0, True)
            saved["kv"] = (xc, kvn, kv_raw)

    dx, dxb, loss_rows = _loss_grad("loss", xc, tgt, 0.5)
    dW = {}
    dsmall = {n: [None] * P[n].shape[0] for n in ("ffn1_norm", "mix_norm", "ffn2_norm")}
    dsmall.update(gmlp_w_s=[None] * N_A_LAYERS, gmlp_b_s=[None] * N_A_LAYERS, gmlp_v_norm=[None] * N_A_LAYERS,
                  attn_q_norm=[None] * (NL - N_A_LAYERS))
    dks = [[] for _ in DILATIONS]
    dvs = [[] for _ in DILATIONS]

    names_big = list(BIG_WEIGHTS)
    gbuf = [lax.empty(_as3d(P[n]).shape, F32) for n in names_big]
    where = (c_arr, chip_arr)
    sib_open = {}
    rs_open = {}

    def sib_begin(l, after):
        names = layer_weights(l)
        sas = [BIG_WEIGHTS[n] for n, _ in names]
        s_sem, r_sem, dws, lands, token = _rs_sibling_start(f"rs_sibling_start_l{l}", [dW[k] for k in names], sas, after)
        sib_open[l] = (names, sas, s_sem, r_sem, dws, lands)
        return token

    def rs_begin(l, after):
        names, sas, s_sem, r_sem, dws, lands = sib_open.pop(l)
        dws, lands = _rs_sibling_wait(f"rs_sibling_wait_l{l}", dws, lands, sas, s_sem, r_sem, after)
        parts = [_add_half("rs_add_half", d, ln, sa, c_arr) for d, ln, sa in zip(dws, lands, sas)]
        s_sem, r_sem, parts, slots, token = _rs_start(f"rs_start_l{l}", parts, sas, dws[0])
        rs_open[l] = (names, sas, s_sem, r_sem, parts, slots)
        return token

    share_open = {}
    rs_landed = {}

    def rs_collect(l, after):
        names, sas, s_sem, r_sem, parts, slots = rs_open.pop(l)
        parts, slots = _rs_wait(f"rs_wait_l{l}", parts, slots, sas, s_sem, r_sem, after)
        rs_landed[l] = (names, sas, parts, slots)

    def rs_reduce(l):
        names, sas, parts, slots = rs_landed.pop(l)
        for p, s, sa, (n, li) in zip(parts, slots, sas, names):
            bi = names_big.index(n)
            gbuf[bi] = _sum_into("rs_sum_chips", p, s, gbuf[bi], li, sa, where)
        layers, bidx = [li for _, li in names], [names_big.index(n) for n, _ in names]
        s_sem, r_sem, bufs, _ = _rs_share_start(f"rs_share_start_l{l}", sas, layers, bidx, gbuf, parts[0])
        gbuf[:] = bufs
        share_open[l] = (sas, layers, bidx, s_sem, r_sem)

    def share_done(l, after):
        if l in share_open:
            sas, layers, bidx, s_sem, r_sem = share_open.pop(l)
            gbuf[:] = _rs_share_wait(f"rs_share_wait_l{l}", sas, layers, bidx, gbuf, s_sem, r_sem, after)

    def ffn_bwd(dx, dxb, sv, gamma, wg, wu, wd, key, l, next_scale, dep=None):
        xin, n, g, u, act = sv
        dg, du = _mm("ffn_dact", [dxb], [wd], one, _ep_swiglu_bwd, [BF16, BF16], tb=True, extras=[g, u], dep=dep)
        (dW[key + "_w_down", l],) = _mm("ffn_dwd", [act], [dxb], one, _ep_plain, [BF16], ta=True)
        dW[key + "_w_gate", l], dW[key + "_w_up", l] = _mm("ffn_dwgu", [n], [dg, du], [(0, 0, 0), (0, 1, 1)], _ep_all, [BF16, BF16], ta=True)
        (dn,) = _mm("ffn_dn", [dg, du], [wg, wu], [(0, 0, 0), (1, 1, 0)], _ep_plain, [F32], tb=True)
        dx, dxb, dsmall[key + "_norm"][l] = _rms_bwd("ffn_norm_bwd", xin, gamma, dn, dx, next_scale)
        return dx, dxb

    dep = None
    for l in reversed(range(NL)):
        if l == kv_layer:
            x_kv, kvn, kv_raw = saved["kv"]
            dkv_raw, dkgain = _head_norm_bwd("k_norm_bwd", kv_raw, 0, 2, kgain, 1.0, dks, dvs)
            (dW["w_kv", 0],) = _mm("kv_dw", [kvn], [dkv_raw], one, _ep_plain, [BF16], ta=True, dep=dep)
            (dkvn,) = _mm("kv_dn", [dkv_raw], [W["w_kv", 0]], one, _ep_plain, [F32], tb=True)
            dx, dxb, dkvnorm = _rms_bwd("kv_norm_bwd", x_kv, kv_norm, dkvn, dx, 0.5)
        dx, dxb = ffn_bwd(dx, dxb, saved["f2", l], ffn2_norm[l], W["ffn2_w_gate", l], W["ffn2_w_up", l], W["ffn2_w_down", l], "ffn2", l, 1.0, dep)
        if l + 1 < NL:
            dep = rs_begin(l + 1, dx)
        if l < N_A_LAYERS:
            xin, h, zpre, z, gated, bias_full = saved["mix", l]
            (dW["gmlp_w_out", l],) = _mm("gmlp_dwout", [gated], [dxb], one, _ep_plain, [BF16], ta=True, dep=dep)
            (dgated,) = _mm("gmlp_dgated", [dxb], [W["gmlp_w_out", l]], one, _ep_plain, [F32], tb=True)
            dzpre, dws, dbacc, dvn = _gmlp_gate_bwd("gmlp_gate_bwd", z, zpre, dgated, vnorm_full[l:l + 1], gmlp_w_s[l],
                                                    jnp.swapaxes(gmlp_w_s[l], 1, 2), bias_full)
            (dW["gmlp_w_in", l],) = _mm("gmlp_dwin", [h], [dzpre], one, _ep_plain, [BF16], ta=True)
            (dh,) = _mm("gmlp_dh", [dzpre], [W["gmlp_w_in", l]], one, _ep_plain, [F32], tb=True)
            dsmall["gmlp_w_s"][l] = dws
            dsmall["gmlp_b_s"][l] = dbacc.reshape(GMLP_CHUNK, G, GMLP_GROUP_WIDTH).sum(-1).T
            dsmall["gmlp_v_norm"][l] = dvn.reshape(DG)
        else:
            j = l - N_A_LAYERS
            xin, h, q_raw, qn, os_, lses, ob = saved["mix", l]
            (dW["attn_w_o", j],) = _mm("attn_dwo", [ob], [dxb], one, _ep_plain, [BF16], ta=True, dep=dep)
            (d_ob,) = _mm("attn_dob", [dxb], [W["attn_w_o", j]], one, _ep_plain, [F32], tb=True)
            lse_t, dl_rows, dob = _attn_combine("attn_mix_bwd", os_, lses, d_o=d_ob)
            dqs = []
            for gi, dil in enumerate(DILATIONS):
                dq, dk, dv = _attn_bwd(f"attn_bwd_d{dil}", qn, kn, vb, dob, lse_t, dl_rows, gi, dil)
                dqs.append([dq])
                dks[gi].append(dk)
                dvs[gi].append(dv)
            dq_raw, dqgain = _head_norm_bwd("q_norm_bwd", q_raw, 0, 1, qgain[j], q_scale, dqs, None)
            (dW["attn_w_q", j],) = _mm("attn_dwq", [h], [dq_raw], one, _ep_plain, [BF16], ta=True)
            (dh,) = _mm("attn_dh", [dq_raw], [W["attn_w_q", j]], one, _ep_plain, [F32], tb=True)
            dsmall["attn_q_norm"][j] = dqgain.reshape(NG, H, HEAD_DIM).sum(1)
        dx, dxb, dsmall["mix_norm"][l] = _rms_bwd("mix_norm_bwd", xin, mix_norm[l], dh, dx, 0.5)
        dx, dxb = ffn_bwd(dx, dxb, saved["f1", l], ffn1_norm[l], W["ffn1_w_gate", l], W["ffn1_w_up", l], W["ffn1_w_down", l], "ffn1", l, 0.5)
        dep = sib_begin(l, dx)
        if l + 2 < NL:
            share_done(l + 3, dep)
            rs_collect(l + 2, dep)
            rs_reduce(l + 2)
    grad_x = dx.reshape(x.shape)

    loss_part = (0.5 / D) * jnp.sum(loss_rows)
    small_grads = [jnp.stack([g.reshape(P[n].shape[1:]) for g in dsmall[n]]) for n in ("ffn1_norm", "mix_norm", "ffn2_norm", "gmlp_w_s", "gmlp_b_s")]
    small_grads += [dkvnorm.reshape(kv_norm.shape), dkgain.reshape(NG, H, HEAD_DIM).sum(1), jnp.stack(dsmall["attn_q_norm"])]
    vn_grad_full = jnp.stack(dsmall["gmlp_v_norm"])
    me_arr = (2 * chip + lax.axis_index("c")).astype(jnp.int32).reshape(1)
    small_slots = _place_own("place_small", _pack(small_grads + [vn_grad_full, loss_part.reshape(1)]), me_arr)

    adam_out = {n: [lax.empty(_as3d(P[n]).shape, F32) for _ in range(4)] for n in names_big}

    def adam_layer(l):
        for n, li in layer_weights(l):
            adam_out[n] = _adamw_layer("adamw", gbuf[names_big.index(n)], _as3d(P[n]), _as3d(P["m_" + n]),
                                       _as3d(P["v_" + n]), li, adam_out[n])

    def updated():
        return [adam_out[n][0] for n in names_big]

    dep = rs_begin(0, dep)
    ex_s, ex_r, small_slots, dep = _exchange_start("exchange_small_start", small_slots, dep)
    for l in range(2, NL):
        share_done(l, dep)
    for l in reversed(range(2, NL)):
        adam_layer(l)
    if NL > 1:
        rs_collect(1, [dep] + updated())
        rs_reduce(1)
    rs_collect(0, [dep] + updated())
    if NL > 1:
        share_done(1, dep)
    rs_reduce(0)
    share_done(0, dep)
    for l in reversed(range(min(2, NL))):
        adam_layer(l)
    big_out = {n: [o.reshape(P[n].shape) for o in adam_out[n]] for n in names_big}

    total = _sum_slots("sum_devices", _exchange_wait("exchange_small_wait", small_slots, ex_s, ex_r, updated()))
    shapes = [P[n].shape for n in SMALL_WEIGHTS] + [vn_grad_full.shape, (1,)]
    red = _unpack(total, shapes)
    loss = red[-1].reshape(())
    cv = gmlp_v_norm.shape[1]
    vn_grad = lax.dynamic_slice_in_dim(red[-2], chip * cv, cv, axis=1)
    names_small = list(SMALL_WEIGHTS) + ["gmlp_v_norm"]
    g_small = red[:len(SMALL_WEIGHTS)] + [vn_grad]
    outs = _adamw("adamw_small", _pack(g_small), _pack([P[n] for n in names_small]),
                  _pack([P["m_" + n] for n in names_small]), _pack([P["v_" + n] for n in names_small]))
    small_shapes = [P[n].shape for n in names_small]
    small_out = {n: [] for n in names_small}
    for o in outs:
        for n, a in zip(names_small, _unpack(o, small_shapes)):
            small_out[n].append(a)

    res = {**big_out, **small_out}
    return (loss, grad_x, *[res[n][0] for n in WEIGHT_ORDER], *[res[n][1] for n in WEIGHT_ORDER],
            *[res[n][2] for n in WEIGHT_ORDER], *[res[n][3] for n in WEIGHT_ORDER])
```
